```python
import jax, jax.numpy as jnp
from jax import lax
import numpy as np

D_MODEL = 1024
BATCH = 32
SEQ = 256
DEPTH = 2
DEC_BATCH = 8
DEC_SEQ = 1024
PAST_LEN = 512

GRID_W = 64
HEAD_DIM = 64
NA_HEADS = 4
NA_WIN_R = 8
NA_WIN_C = 16
NA_QCOLS = 16
GQA_Q_HEADS = 8
GQA_KV_HEADS = 2
GQA_GROUP = GQA_Q_HEADS // GQA_KV_HEADS
CM_GROUPS = 4
CM_GROUP_DIM = 64
CHUNK = 128
ROPE_THETA = 10000.0
N_EXPERTS = 32
TOP_K = 4
D_FF = D_MODEL
SWIGLU_LIMIT = 7.0
SWIGLU_ALPHA = 1.702
EPS = 1e-6
Q_BLOCK = 128
MOE_BLOCK = 128
NEG_INF = -1e30

NA_WIDTH = NA_HEADS * HEAD_DIM
GQA_Q_WIDTH = GQA_Q_HEADS * HEAD_DIM
GQA_KV_WIDTH = GQA_KV_HEADS * HEAD_DIM
CM_WIDTH = CM_GROUPS * CM_GROUP_DIM
MIX_WIDTH = NA_WIDTH + GQA_Q_WIDTH + CM_WIDTH
SPLIT_SIZES = [NA_WIDTH, NA_WIDTH, NA_WIDTH, GQA_Q_WIDTH, GQA_KV_WIDTH, GQA_KV_WIDTH, CM_WIDTH, CM_WIDTH]
IN_WIDTH = sum(SPLIT_SIZES)

kernel_name = 'hybrid_flow_prefix_trunk_step'


def rms_norm(x, g):
    xf = x.astype(jnp.float32)
    y = xf * lax.rsqrt(jnp.mean(xf * xf, axis=-1, keepdims=True) + EPS)
    return (y * g.astype(jnp.float32)).astype(x.dtype)


def layer_norm(x, g, b):
    xf = x.astype(jnp.float32)
    mu = jnp.mean(xf, axis=-1, keepdims=True)
    var = jnp.mean(jnp.square(xf - mu), axis=-1, keepdims=True)
    y = (xf - mu) * lax.rsqrt(var + EPS)
    return (y * g.astype(jnp.float32) + b.astype(jnp.float32)).astype(x.dtype)


def split_projection(z):
    offs = [int(o) for o in np.cumsum(SPLIT_SIZES)[:-1]]
    return jnp.split(z, offs, axis=-1)


def heads(x, n):
    B, S, _ = x.shape
    return x.reshape(B, S, n, HEAD_DIM).transpose(0, 2, 1, 3)


def merge_heads(x):
    B, H, S, dh = x.shape
    return x.transpose(0, 2, 1, 3).reshape(B, S, H * dh)


def modulation(cond, w, b):
    m = (jax.nn.silu(cond) @ w + b)[:, None, :]
    return jnp.split(m, 6, axis=-1)


def modulate(x, shift, scale):
    return x * (1.0 + scale) + shift


def axial_rope_tables(n_tokens):
    t = jnp.arange(n_tokens)
    row = (t // GRID_W).astype(jnp.float32)
    col = (t % GRID_W).astype(jnp.float32)
    half = HEAD_DIM // 2
    inv = ROPE_THETA ** (-jnp.arange(0, half, 2, dtype=jnp.float32) / half)
    ang_r = row[:, None] * inv
    ang_c = col[:, None] * inv
    return (jnp.cos(ang_r), jnp.sin(ang_r), jnp.cos(ang_c), jnp.sin(ang_c))


def rotate(x, cos, sin):
    x1, x2 = jnp.split(x, 2, axis=-1)
    return jnp.concatenate([x1 * cos - x2 * sin, x1 * sin + x2 * cos], axis=-1)


def apply_axial_rope(x, tabs):
    cr, sr, cc, sc = tabs
    xr, xc = jnp.split(x.astype(jnp.float32), 2, axis=-1)
    return jnp.concatenate([rotate(xr, cr, sr), rotate(xc, cc, sc)], axis=-1).astype(x.dtype)


def dense_attention(q, k, v):
    B, Hk, G, Sq, dh = q.shape
    nb = Sq // Q_BLOCK
    qb = jnp.moveaxis(q.reshape(B, Hk, G, nb, Q_BLOCK, dh), 3, 0)
    scale = dh ** -0.5

    def one_block(qblk):
        s = jnp.einsum('bhgqd,bhkd->bhgqk', qblk, k).astype(jnp.float32) * scale
        p = jax.nn.softmax(s, axis=-1).astype(v.dtype)
        return jnp.einsum('bhgqk,bhkd->bhgqd', p, v)

    out = lax.map(one_block, qb)
    return jnp.moveaxis(out, 0, 3).reshape(B, Hk, G, Sq, dh)


def neighbourhood_attention(q, k, v, k_ctx, v_ctx, rpb):
    B, H, N, dh = q.shape
    R = N // GRID_W
    kr = min(NA_WIN_R, R)
    kb = NA_QCOLS + NA_WIN_C
    ncb = GRID_W // NA_QCOLS
    r = jnp.arange(R)
    row_idx = jnp.clip(r - kr // 2, 0, R - kr)[:, None] + jnp.arange(kr)
    j = jnp.arange(ncb)
    col_idx = jnp.clip(j * NA_QCOLS - NA_WIN_C // 2, 0, GRID_W - kb)[:, None] + jnp.arange(kb)
    qcol = j[:, None] * NA_QCOLS + jnp.arange(NA_QCOLS)
    win0 = jnp.clip(qcol - NA_WIN_C // 2, 0, GRID_W - NA_WIN_C)[..., None]
    kc = col_idx[:, None, :]
    col_ok = (kc >= win0) & (kc < win0 + NA_WIN_C)
    dr = row_idx - r[:, None] + (NA_WIN_R - 1)
    dc = jnp.clip(kc - qcol[..., None] + (NA_WIN_C - 1), 0, 2 * NA_WIN_C - 2)
    bias = rpb.astype(jnp.float32)[:, dr[:, None, None, :, None], dc[None, :, :, None, :]]
    bias = jnp.where(col_ok[None, None, :, :, None, :], bias, NEG_INF)

    kg = k.reshape(B, H, R, GRID_W, dh)
    vg = v.reshape(B, H, R, GRID_W, dh)
    gi_r = row_idx[:, None, :, None]
    gi_c = col_idx[None, :, None, :]
    k_blk = kg[:, :, gi_r, gi_c]
    v_blk = vg[:, :, gi_r, gi_c]
    qb = q.reshape(B, H, R, ncb, NA_QCOLS, dh)
    scale = dh ** -0.5
    s_loc = jnp.einsum('bhrjqd,bhrjakd->bhrjqak', qb, k_blk).astype(jnp.float32) * scale + bias[None]
    s_ctx = jnp.einsum('bhrjqd,bhpd->bhrjqp', qb, k_ctx).astype(jnp.float32) * scale
    n_loc = kr * kb
    s = jnp.concatenate([s_loc.reshape(s_loc.shape[:5] + (n_loc,)), s_ctx], axis=-1)
    p = jax.nn.softmax(s, axis=-1).astype(v.dtype)
    p_loc = p[..., :n_loc].reshape(s_loc.shape)
    p_ctx = p[..., n_loc:]
    out = (jnp.einsum('bhrjqak,bhrjakd->bhrjqd', p_loc, v_blk)
           + jnp.einsum('bhrjqp,bhpd->bhrjqd', p_ctx, v_ctx))
    return out.reshape(B, H, N, dh)


def chunk_mlp(u, v, ln_g, ln_b, ws, bs):
    B, S, _ = v.shape
    v = layer_norm(v, ln_g, ln_b)
    vc = v.reshape(B, S // CHUNK, CHUNK, CM_GROUPS, CM_GROUP_DIM)
    mixed = jnp.einsum('gpq,bnqgd->bnpgd', ws, vc) + bs.T[None, None, :, :, None]
    return u * mixed.reshape(B, S, CM_WIDTH)


def context_mixers(h, w_in, q_norm_g, k_norm_g, cm_ln_g, cm_ln_b, cm_ws, cm_bs, w_out):
    B, S, _ = h.shape
    qa, ka, va, qb, kb, vb, cu, cv = split_projection(h @ w_in)
    qa, ka, va = heads(qa, NA_HEADS), heads(ka, NA_HEADS), heads(va, NA_HEADS)
    oa = dense_attention(qa[:, :, None], ka, va)[:, :, 0]
    qb = rms_norm(heads(qb, GQA_Q_HEADS), q_norm_g)
    kb = rms_norm(heads(kb, GQA_KV_HEADS), k_norm_g)
    vb = heads(vb, GQA_KV_HEADS)
    ob = dense_attention(qb.reshape(B, GQA_KV_HEADS, GQA_GROUP, S, HEAD_DIM), kb, vb)
    ob = ob.reshape(B, GQA_Q_HEADS, S, HEAD_DIM)
    oc = chunk_mlp(jax.nn.gelu(cu), jax.nn.gelu(cv), cm_ln_g, cm_ln_b, cm_ws, cm_bs)
    out = jnp.concatenate([merge_heads(oa), merge_heads(ob), oc], axis=-1) @ w_out
    return out, ka, va, kb, vb


def latent_mixers(h, na_k_ctx, na_v_ctx, gqa_k_ctx, gqa_v_ctx, tabs, w_in, rpb, q_norm_g, k_norm_g,
                  cm_ln_g, cm_ln_b, cm_ws, cm_bs, w_out):
    B, N, _ = h.shape
    qa, ka, va, qb, kb, vb, cu, cv = split_projection(h @ w_in)
    oa = neighbourhood_attention(heads(qa, NA_HEADS), heads(ka, NA_HEADS), heads(va, NA_HEADS),
                                 na_k_ctx, na_v_ctx, rpb)
    qb = apply_axial_rope(rms_norm(heads(qb, GQA_Q_HEADS), q_norm_g), tabs)
    kb = apply_axial_rope(rms_norm(heads(kb, GQA_KV_HEADS), k_norm_g), tabs)
    k_all = jnp.concatenate([kb, gqa_k_ctx], axis=2)
    v_all = jnp.concatenate([heads(vb, GQA_KV_HEADS), gqa_v_ctx], axis=2)
    ob = dense_attention(qb.reshape(B, GQA_KV_HEADS, GQA_GROUP, N, HEAD_DIM), k_all, v_all)
    ob = ob.reshape(B, GQA_Q_HEADS, N, HEAD_DIM)
    oc = chunk_mlp(jax.nn.gelu(cu), jax.nn.gelu(cv), cm_ln_g, cm_ln_b, cm_ws, cm_bs)
    return jnp.concatenate([merge_heads(oa), merge_heads(ob), oc], axis=-1) @ w_out


def moe_ffn(h, rw, rb, wgu, bgu, wd, bd):
    T, D = h.shape
    logits = (h @ rw + rb).astype(jnp.float32)
    top_v, top_i = lax.top_k(logits, TOP_K)
    gates = jax.nn.softmax(top_v, axis=-1)
    n_assign = T * TOP_K
    flat_e = top_i.reshape(-1)
    flat_t = jnp.arange(n_assign, dtype=jnp.int32) // TOP_K
    flat_g = gates.reshape(-1)
    order = jnp.argsort(flat_e)
    se, st, sg = flat_e[order], flat_t[order], flat_g[order]
    counts = jnp.bincount(flat_e, length=N_EXPERTS)
    padded = (counts + MOE_BLOCK - 1) // MOE_BLOCK * MOE_BLOCK
    pad_end = jnp.cumsum(padded)
    pad_start = pad_end - padded
    start = jnp.cumsum(counts) - counts
    dest = pad_start[se] + jnp.arange(n_assign) - start[se]
    n_blocks = -(-n_assign // MOE_BLOCK) + N_EXPERTS
    n_rows = n_blocks * MOE_BLOCK
    row_tok = jnp.zeros((n_rows,), jnp.int32).at[dest].set(st)
    row_gate = jnp.zeros((n_rows,), jnp.float32).at[dest].set(sg)
    blk_e = jnp.minimum(jnp.searchsorted(pad_end, jnp.arange(n_blocks) * MOE_BLOCK, side='right'), N_EXPERTS - 1)
    xb = h[row_tok].reshape(n_blocks, MOE_BLOCK, D)

    def expert_block(args):
        xblk, e = args
        gu = xblk @ wgu[e] + bgu[e]
        g, u = gu[..., ::2], gu[..., 1::2]
        g = jnp.minimum(g, SWIGLU_LIMIT)
        u = jnp.clip(u, -SWIGLU_LIMIT, SWIGLU_LIMIT)
        act = (u + 1.0) * (g * jax.nn.sigmoid(SWIGLU_ALPHA * g))
        return act @ wd[e] + bd[e]

    yb = lax.map(expert_block, (xb, blk_e)).reshape(n_rows, D)
    yb = yb * row_gate[:, None].astype(yb.dtype)
    return jax.ops.segment_sum(yb, row_tok, num_segments=T)


def channel_sublayer(x, shift, scale, gate, norm_g, rw, rb, wgu, bgu, wd, bd):
    B, S, D = x.shape
    h = modulate(rms_norm(x, norm_g), shift, scale)
    return x + gate * moe_ffn(h.reshape(B * S, D), rw, rb, wgu, bgu, wd, bd).reshape(B, S, D)


def setup_inputs(seed: int = 0) -> dict:
    key = jax.random.key(seed)
    ks = jax.random.split(key, 32)
    f32 = jnp.float32
    nrm = lambda k, shape, s: jax.random.normal(k, shape, f32) * s
    D = D_MODEL
    return {
        'x_prompt': nrm(ks[0], (BATCH, SEQ, D), 1.0),
        'x_sample': nrm(ks[1], (DEC_BATCH, DEC_SEQ, D), 1.0),
        'cache_na_k': nrm(ks[2], (DEC_BATCH, DEPTH, NA_HEADS, PAST_LEN, HEAD_DIM), 1.0),
        'cache_na_v': nrm(ks[3], (DEC_BATCH, DEPTH, NA_HEADS, PAST_LEN, HEAD_DIM), 1.0),
        'cache_gqa_k': nrm(ks[4], (DEC_BATCH, DEPTH, GQA_KV_HEADS, PAST_LEN, HEAD_DIM), 1.0),
        'cache_gqa_v': nrm(ks[5], (DEC_BATCH, DEPTH, GQA_KV_HEADS, PAST_LEN, HEAD_DIM), 1.0),
        'c': nrm(ks[6], (DEC_BATCH, D), 1.0),
        'c_ctx': nrm(ks[7], (D,), 1.0),
        'w_mod': nrm(ks[8], (DEPTH, D, 6 * D), 0.5 * D ** -0.5),
        'b_mod': nrm(ks[9], (DEPTH, 6 * D), 0.01),
        'norm1_g': 1.0 + nrm(ks[10], (DEPTH, D), 0.02),
        'norm2_g': 1.0 + nrm(ks[11], (DEPTH, D), 0.02),
        'w_in': nrm(ks[12], (DEPTH, D, IN_WIDTH), D ** -0.5),
        'na_rpb': nrm(ks[13], (DEPTH, NA_HEADS, 2 * NA_WIN_R - 1, 2 * NA_WIN_C - 1), 0.1),
        'q_norm_g': 1.0 + nrm(ks[14], (DEPTH, HEAD_DIM), 0.02),
        'k_norm_g': 1.0 + nrm(ks[15], (DEPTH, HEAD_DIM), 0.02),
        'cm_ln_g': 1.0 + nrm(ks[16], (DEPTH, CM_WIDTH), 0.02),
        'cm_ln_b': nrm(ks[17], (DEPTH, CM_WIDTH), 0.01),
        'cm_ws': nrm(ks[18], (DEPTH, CM_GROUPS, CHUNK, CHUNK), CHUNK ** -0.5),
        'cm_bs': 1.0 + nrm(ks[19], (DEPTH, CM_GROUPS, CHUNK), 0.02),
        'w_out': nrm(ks[20], (DEPTH, MIX_WIDTH, D), MIX_WIDTH ** -0.5),
        'router_w': nrm(ks[21], (DEPTH, D, N_EXPERTS), D ** -0.5),
        'router_b': nrm(ks[22], (DEPTH, N_EXPERTS), 0.01),
        'w_gate_up': nrm(ks[23], (DEPTH, N_EXPERTS, D, 2 * D_FF), D ** -0.5),
        'b_gate_up': nrm(ks[24], (DEPTH, N_EXPERTS, 2 * D_FF), 0.01),
        'w_down': nrm(ks[25], (DEPTH, N_EXPERTS, D_FF, D), D_FF ** -0.5),
        'b_down': nrm(ks[26], (DEPTH, N_EXPERTS, D), 0.01),
        'final_norm_g': 1.0 + nrm(ks[27], (D,), 0.02),
    }


def reference(x_prompt, x_sample, cache_na_k, cache_na_v, cache_gqa_k, cache_gqa_v, c, c_ctx,
              w_mod, b_mod, norm1_g, norm2_g, w_in, na_rpb, q_norm_g, k_norm_g, cm_ln_g, cm_ln_b,
              cm_ws, cm_bs, w_out, router_w, router_b, w_gate_up, b_gate_up, w_down, b_down, final_norm_g):
    tabs = axial_rope_tables(x_sample.shape[1])
    xp, xs = x_prompt, x_sample
    na_k_list, na_v_list, gqa_k_list, gqa_v_list = [], [], [], []
    for l in range(DEPTH):
        sh1, sc1, g1, sh2, sc2, g2 = modulation(c_ctx[None, :], w_mod[l], b_mod[l])
        h = modulate(rms_norm(xp, norm1_g[l]), sh1, sc1)
        mix, ak, av, bk, bv = context_mixers(h, w_in[l], q_norm_g[l], k_norm_g[l], cm_ln_g[l], cm_ln_b[l],
                                             cm_ws[l], cm_bs[l], w_out[l])
        na_k_list.append(ak)
        na_v_list.append(av)
        gqa_k_list.append(bk)
        gqa_v_list.append(bv)
        xp = xp + g1 * mix
        xp = channel_sublayer(xp, sh2, sc2, g2, norm2_g[l], router_w[l], router_b[l],
                              w_gate_up[l], b_gate_up[l], w_down[l], b_down[l])
        sh1, sc1, g1, sh2, sc2, g2 = modulation(c, w_mod[l], b_mod[l])
        h = modulate(rms_norm(xs, norm1_g[l]), sh1, sc1)
        mix = latent_mixers(h, cache_na_k[:, l], cache_na_v[:, l], cache_gqa_k[:, l], cache_gqa_v[:, l], tabs,
                            w_in[l], na_rpb[l], q_norm_g[l], k_norm_g[l], cm_ln_g[l], cm_ln_b[l],
                            cm_ws[l], cm_bs[l], w_out[l])
        xs = xs + g1 * mix
        xs = channel_sublayer(xs, sh2, sc2, g2, norm2_g[l], router_w[l], router_b[l],
                              w_gate_up[l], b_gate_up[l], w_down[l], b_down[l])
    y_prompt = rms_norm(xp, final_norm_g)
    y_sample = rms_norm(xs, final_norm_g)
    new_na_k = jnp.stack(na_k_list, axis=1)
    new_na_v = jnp.stack(na_v_list, axis=1)
    new_gqa_k = jnp.stack(gqa_k_list, axis=1)
    new_gqa_v = jnp.stack(gqa_v_list, axis=1)
    return (y_prompt, y_sample, new_na_k, new_na_v, new_gqa_k, new_gqa_v)
```

```python
import functools

import jax
import jax.numpy as jnp
import numpy as np
from jax import lax
from jax.experimental import pallas as pl
from jax.experimental.pallas import tpu as pltpu

D = 1024
DEPTH = 2
GRID_W = 64
HEAD_DIM = 64
NA_WIN_R = 8
NA_WIN_C = 16
CHUNK = 128
ROPE_THETA = 10000.0
N_EXPERTS = 32
TOP_K = 4
SWIGLU_LIMIT = 7.0
SWIGLU_ALPHA = 1.702
EPS = 1e-6
MOE_BLOCK = 128
NEG_INF = -1e30
IN_WIDTH = 2048

QA, KA, VA, QB, KB, VB, CU, CV = 0, 256, 512, 768, 1280, 1408, 1536, 1792
OA, OB, OC = 0, 256, 768

LANES = 128
MOD_ROWS = 16
CTX_ROW = 8
VMEM_LIMIT = 56 * 1024 * 1024

F32 = jnp.float32
BF16 = jnp.bfloat16
HI = lax.Precision.HIGHEST


def _cparams(sem):
    return pltpu.CompilerParams(dimension_semantics=sem, vmem_limit_bytes=VMEM_LIMIT)


def _dot(a, b):
    return jnp.dot(a, b, preferred_element_type=F32)


def _dot_nt(a, b):
    return lax.dot_general(a, b, (((1,), (1,)), ((), ())), preferred_element_type=F32)


def _lane_lo():
    return lax.broadcasted_iota(jnp.int32, (1, LANES), 1) < HEAD_DIM


def _mod_kernel(c_ref, w_ref, b_ref, o_ref):
    c = c_ref[...]
    s = c * jax.nn.sigmoid(c)
    o_ref[0] = jnp.dot(s, w_ref[0], preferred_element_type=F32, precision=HI) + b_ref[0]


def _modulation(cond, w_mod, b_mod):
    tn = 1536
    n = w_mod.shape[-1]
    return pl.pallas_call(
        _mod_kernel,
        grid=(DEPTH, n // tn),
        in_specs=[
            pl.BlockSpec((MOD_ROWS, D), lambda l, j: (0, 0)),
            pl.BlockSpec((1, D, tn), lambda l, j: (l, 0, j)),
            pl.BlockSpec((1, 1, tn), lambda l, j: (l, 0, j)),
        ],
        out_specs=pl.BlockSpec((1, MOD_ROWS, tn), lambda l, j: (l, 0, j)),
        out_shape=jax.ShapeDtypeStruct((DEPTH, MOD_ROWS, n), F32),
        compiler_params=_cparams(("arbitrary", "arbitrary")),
        name="modulation",
    )(cond, w_mod, b_mod.reshape(DEPTH, 1, n))


def _mod_row(i, tm, tp, ts_per_batch):
    start = i * tm
    return jnp.where(start < tp, CTX_ROW, (start - tp) // ts_per_batch)


def _mod_spec(k, tm, tp, ts_per_batch):
    return pl.BlockSpec((1, 1, D), lambda i: (_mod_row(i, tm, tp, ts_per_batch) * 6 + k, 0, 0))


def _inproj_kernel(x_ref, shift_ref, scale_ref, g_ref, w_ref, z_ref):
    x = x_ref[...]
    ms = jnp.mean(x * x, axis=-1, keepdims=True)
    y = x * lax.rsqrt(ms + EPS) * g_ref[...]
    h = y * (1.0 + scale_ref[0]) + shift_ref[0]
    z_ref[...] = _dot(h.astype(BF16), w_ref[...])


def _inproj(x, modr, g, w, tp, ts_per_batch):
    tm = 512
    t = x.shape[0]
    return pl.pallas_call(
        _inproj_kernel,
        grid=(t // tm,),
        in_specs=[
            pl.BlockSpec((tm, D), lambda i: (i, 0)),
            _mod_spec(0, tm, tp, ts_per_batch),
            _mod_spec(1, tm, tp, ts_per_batch),
            pl.BlockSpec((1, D), lambda i: (0, 0)),
            pl.BlockSpec((D, IN_WIDTH), lambda i: (0, 0)),
        ],
        out_specs=pl.BlockSpec((tm, IN_WIDTH), lambda i: (i, 0)),
        out_shape=jax.ShapeDtypeStruct((t, IN_WIDTH), F32),
        compiler_params=_cparams(("parallel",)),
        name="inproj",
    )(x, modr, modr, g, w)


def _head_rms(x, g2):
    lo = _lane_lo()
    x2 = x * x
    s_lo = jnp.sum(jnp.where(lo, x2, 0.0), axis=-1, keepdims=True)
    s_hi = jnp.sum(jnp.where(lo, 0.0, x2), axis=-1, keepdims=True)
    ms = jnp.where(lo, s_lo, s_hi) * (1.0 / HEAD_DIM)
    return x * lax.rsqrt(ms + EPS) * g2


def _softmax_pv(q16, ks, vs, biases):
    ss = []
    for k, b in zip(ks, biases):
        s = _dot_nt(q16, k)
        if b is not None:
            s = s + b
        ss.append(s)
    m = ss[0].max(axis=-1, keepdims=True)
    for s in ss[1:]:
        m = jnp.maximum(m, s.max(axis=-1, keepdims=True))
    den = None
    acc = None
    for s, v in zip(ss, vs):
        e = jnp.exp(s - m)
        d = e.sum(axis=-1, keepdims=True)
        o = _dot(e.astype(BF16), v)
        den = d if den is None else den + d
        acc = o if acc is None else acc + o
    return acc / den


def _attend_pair(q, ks_lo, vs_lo, ks_hi, vs_hi, b_lo, b_hi):
    lo = _lane_lo()
    o_lo = _softmax_pv(jnp.where(lo, q, 0.0).astype(BF16), ks_lo, vs_lo, b_lo)
    o_hi = _softmax_pv(jnp.where(lo, 0.0, q).astype(BF16), ks_hi, vs_hi, b_hi)
    return jnp.where(lo, o_lo, o_hi)


def _gelu(x):
    c = np.sqrt(2.0 / np.pi).astype(np.float32)
    return x * (0.5 * (1.0 + jnp.tanh(c * (x + 0.044715 * (x * x * x)))))


def _layer_norm(x, g, b):
    mu = jnp.mean(x, axis=-1, keepdims=True)
    xc = x - mu
    var = jnp.mean(xc * xc, axis=-1, keepdims=True)
    return xc * lax.rsqrt(var + EPS) * g + b


def _chunk_mlp(cu, cv, lng_ref, lnb_ref, ws_ref, bsf_ref, out_ref, col0):
    lo = _lane_lo()
    u = _gelu(cu)
    v = _layer_norm(_gelu(cv), lng_ref[...], lnb_ref[...]).astype(BF16)
    s = cu.shape[0]
    for n in range(s // CHUNK):
        rows = slice(n * CHUNK, (n + 1) * CHUNK)
        for jb in range(2):
            cols = slice(jb * LANES, (jb + 1) * LANES)
            vb = v[rows, cols]
            m_lo = _dot(ws_ref[2 * jb].astype(BF16), vb)
            m_hi = _dot(ws_ref[2 * jb + 1].astype(BF16), vb)
            mixed = jnp.where(lo, m_lo, m_hi) + bsf_ref[:, cols]
            out_ref[rows, col0 + jb * LANES:col0 + (jb + 1) * LANES] = (u[rows, cols] * mixed).astype(out_ref.dtype)


def _ctx_mixer_kernel(z_ref, qg_ref, kg_ref, lng_ref, lnb_ref, ws_ref, bsf_ref, mix_ref, kbn_ref):
    scale = HEAD_DIM ** -0.5
    for j in range(2):
        q = z_ref[:, QA + LANES * j:QA + LANES * (j + 1)] * scale
        k = [z_ref[:, KA + LANES * j:KA + LANES * (j + 1)].astype(BF16)]
        v = [z_ref[:, VA + LANES * j:VA + LANES * (j + 1)].astype(BF16)]
        o = _attend_pair(q, k, v, k, v, [None], [None])
        mix_ref[:, OA + LANES * j:OA + LANES * (j + 1)] = o.astype(mix_ref.dtype)

    kb = _head_rms(z_ref[:, KB:KB + LANES], kg_ref[...])
    kbn_ref[...] = kb
    vb = z_ref[:, VB:VB + LANES]
    k_same = [kb.astype(BF16)]
    k_swap = [pltpu.roll(kb, HEAD_DIM, 1).astype(BF16)]
    v_same = [vb.astype(BF16)]
    v_swap = [pltpu.roll(vb, HEAD_DIM, 1).astype(BF16)]
    for j in range(4):
        q = _head_rms(z_ref[:, QB + LANES * j:QB + LANES * (j + 1)], qg_ref[...]) * scale
        if j // 2 == 0:
            o = _attend_pair(q, k_same, v_same, k_swap, v_swap, [None], [None])
        else:
            o = _attend_pair(q, k_swap, v_swap, k_same, v_same, [None], [None])
        mix_ref[:, OB + LANES * j:OB + LANES * (j + 1)] = o.astype(mix_ref.dtype)

    _chunk_mlp(z_ref[:, CU:CU + 256], z_ref[:, CV:CV + 256], lng_ref, lnb_ref, ws_ref, bsf_ref, mix_ref, OC)


def _ctx_mixer(z, n_batch, seq, qg2, kg2, lng, lnb, ws, bsf):
    small = lambda shape: pl.BlockSpec(shape, lambda b: (0,) * len(shape))
    return pl.pallas_call(
        _ctx_mixer_kernel,
        grid=(n_batch,),
        in_specs=[
            pl.BlockSpec((seq, IN_WIDTH), lambda b: (b, 0)),
            small((1, LANES)), small((1, LANES)), small((1, 256)), small((1, 256)),
            small((4, CHUNK, CHUNK)), small((CHUNK, 256)),
        ],
        out_specs=[
            pl.BlockSpec((seq, D), lambda b: (b, 0)),
            pl.BlockSpec((seq, LANES), lambda b: (b, 0)),
        ],
        out_shape=[
            jax.ShapeDtypeStruct((n_batch * seq, D), BF16),
            jax.ShapeDtypeStruct((n_batch * seq, LANES), F32),
        ],
        compiler_params=_cparams(("parallel",)),
        name="ctx_mixer",
    )(z, qg2, kg2, lng, lnb, ws, bsf)


def _rope(x, cos, sin):
    first = (lax.broadcasted_iota(jnp.int32, (1, LANES), 1) % 32) < 16
    partner = jnp.where(first, pltpu.roll(x, LANES - 16, 1), pltpu.roll(x, 16, 1))
    return x * cos + partner * sin


def _lat_mixer_kernel(zq_ref, zkv_ref, nakc_ref, navc_ref, gkc_ref, gvc_ref, bias_ref,
                      cosq_ref, sinq_ref, cosk_ref, sink_ref,
                      qg_ref, kg_ref, lng_ref, lnb_ref, ws_ref, bsf_ref,
                      mix_ref,
                      kl_ref, kls_ref, vl_ref, vls_ref, kc_ref, kcs_ref, vc_ref, vcs_ref):
    scale = HEAD_DIM ** -0.5

    @pl.when(pl.program_id(1) == 0)
    def _():
        kb = _rope(_head_rms(zkv_ref[:, KB:KB + LANES], kg_ref[...]), cosk_ref[...], sink_ref[...])
        vb = zkv_ref[:, VB:VB + LANES]
        kl_ref[...] = kb.astype(BF16)
        kls_ref[...] = pltpu.roll(kb, HEAD_DIM, 1).astype(BF16)
        vl_ref[...] = vb.astype(BF16)
        vls_ref[...] = pltpu.roll(vb, HEAD_DIM, 1).astype(BF16)
        kc = gkc_ref[0]
        vc = gvc_ref[0]
        kc_ref[...] = kc.astype(BF16)
        kcs_ref[...] = pltpu.roll(kc, HEAD_DIM, 1).astype(BF16)
        vc_ref[...] = vc.astype(BF16)
        vcs_ref[...] = pltpu.roll(vc, HEAD_DIM, 1).astype(BF16)

    for j in range(2):
        cols = slice(LANES * j, LANES * (j + 1))
        q = zq_ref[:, QA + LANES * j:QA + LANES * (j + 1)] * scale
        ks = [zkv_ref[:, KA + LANES * j:KA + LANES * (j + 1)].astype(BF16), nakc_ref[0, :, cols].astype(BF16)]
        vs = [zkv_ref[:, VA + LANES * j:VA + LANES * (j + 1)].astype(BF16), navc_ref[0, :, cols].astype(BF16)]
        o = _attend_pair(q, ks, vs, ks, vs, [bias_ref[2 * j], None], [bias_ref[2 * j + 1], None])
        mix_ref[:, OA + LANES * j:OA + LANES * (j + 1)] = o.astype(mix_ref.dtype)

    same = ([kl_ref[...], kc_ref[...]], [vl_ref[...], vc_ref[...]])
    swap = ([kls_ref[...], kcs_ref[...]], [vls_ref[...], vcs_ref[...]])
    for j in range(4):
        q = _head_rms(zq_ref[:, QB + LANES * j:QB + LANES * (j + 1)], qg_ref[...])
        q = _rope(q, cosq_ref[...], sinq_ref[...]) * scale
        lo_kv, hi_kv = (same, swap) if j // 2 == 0 else (swap, same)
        o = _attend_pair(q, lo_kv[0], lo_kv[1], hi_kv[0], hi_kv[1], [None, None], [None, None])
        mix_ref[:, OB + LANES * j:OB + LANES * (j + 1)] = o.astype(mix_ref.dtype)

    _chunk_mlp(zq_ref[:, CU:CU + 256], zq_ref[:, CV:CV + 256], lng_ref, lnb_ref, ws_ref, bsf_ref, mix_ref, OC)


def _lat_mixer(z, row0, n_batch, n_tok, past, nakc, navc, gkc, gvc, bias, cos, sin, qg2, kg2, lng, lnb, ws, bsf):
    tq = 256
    nq = n_tok // tq
    small = lambda shape: pl.BlockSpec(shape, lambda b, t: (0,) * len(shape))
    qblk0 = row0 // tq
    kvblk0 = row0 // n_tok
    return pl.pallas_call(
        _lat_mixer_kernel,
        grid=(n_batch, nq),
        in_specs=[
            pl.BlockSpec((tq, IN_WIDTH), lambda b, t: (qblk0 + b * nq + t, 0)),
            pl.BlockSpec((n_tok, IN_WIDTH), lambda b, t: (kvblk0 + b, 0)),
            pl.BlockSpec((1, past, 256), lambda b, t: (b, 0, 0)),
            pl.BlockSpec((1, past, 256), lambda b, t: (b, 0, 0)),
            pl.BlockSpec((1, past, LANES), lambda b, t: (b, 0, 0)),
            pl.BlockSpec((1, past, LANES), lambda b, t: (b, 0, 0)),
            pl.BlockSpec((4, tq, n_tok), lambda b, t: (0, t, 0)),
            pl.BlockSpec((tq, LANES), lambda b, t: (t, 0)),
            pl.BlockSpec((tq, LANES), lambda b, t: (t, 0)),
            small((n_tok, LANES)), small((n_tok, LANES)),
            small((1, LANES)), small((1, LANES)), small((1, 256)), small((1, 256)),
            small((4, CHUNK, CHUNK)), small((CHUNK, 256)),
        ],
        out_specs=pl.BlockSpec((tq, D), lambda b, t: (b * nq + t, 0)),
        out_shape=jax.ShapeDtypeStruct((n_batch * n_tok, D), BF16),
        scratch_shapes=[pltpu.VMEM((n_tok, LANES), BF16)] * 4 + [pltpu.VMEM((past, LANES), BF16)] * 4,
        compiler_params=_cparams(("parallel", "arbitrary")),
        name="lat_mixer",
    )(z, z, nakc, navc, gkc, gvc, bias, cos, sin, cos, sin, qg2, kg2, lng, lnb, ws, bsf)


def _outproj_kernel(mix_ref, x_ref, g1_ref, shift_ref, scale_ref, g_ref, w_ref, rw_ref, rb_ref,
                    xn_ref, hp_ref, lg_ref):
    x = x_ref[...] + g1_ref[0] * _dot(mix_ref[...], w_ref[...])
    xn_ref[...] = x
    ms = jnp.mean(x * x, axis=-1, keepdims=True)
    h = x * lax.rsqrt(ms + EPS) * g_ref[...]
    h = h * (1.0 + scale_ref[0]) + shift_ref[0]
    lg_ref[...] = jnp.dot(h, rw_ref[...], preferred_element_type=F32, precision=HI) + rb_ref[...]
    half = D // 2
    lo = pltpu.bitcast(h[:, :half].astype(BF16).astype(F32), jnp.uint32) >> 16
    hi = pltpu.bitcast(h[:, half:].astype(BF16).astype(F32), jnp.uint32) & jnp.uint32(0xFFFF0000)
    hp_ref[...] = hi | lo


def _outproj(mix, x, modr, g, w, rw, rb, tp, ts_per_batch):
    tm = 512
    t = x.shape[0]
    return pl.pallas_call(
        _outproj_kernel,
        grid=(t // tm,),
        in_specs=[
            pl.BlockSpec((tm, D), lambda i: (i, 0)),
            pl.BlockSpec((tm, D), lambda i: (i, 0)),
            _mod_spec(2, tm, tp, ts_per_batch),
            _mod_spec(3, tm, tp, ts_per_batch),
            _mod_spec(4, tm, tp, ts_per_batch),
            pl.BlockSpec((1, D), lambda i: (0, 0)),
            pl.BlockSpec((D, D), lambda i: (0, 0)),
            pl.BlockSpec((D, N_EXPERTS), lambda i: (0, 0)),
            pl.BlockSpec((1, N_EXPERTS), lambda i: (0, 0)),
        ],
        out_specs=[
            pl.BlockSpec((tm, D), lambda i: (i, 0)),
            pl.BlockSpec((tm, D // 2), lambda i: (i, 0)),
            pl.BlockSpec((tm, N_EXPERTS), lambda i: (i, 0)),
        ],
        out_shape=[
            jax.ShapeDtypeStruct((t, D), F32),
            jax.ShapeDtypeStruct((t, D // 2), jnp.uint32),
            jax.ShapeDtypeStruct((t, N_EXPERTS), F32),
        ],
        compiler_params=_cparams(("parallel",)),
        name="outproj",
    )(mix, x, modr, modr, modr, g, w, rw, rb)


def _moe_kernel(blk_e_ref, row_tok_ref, hp_ref, wg_ref, wu_ref, wd_ref, bg_ref, bu_ref, bd_ref, gate_ref,
                y_ref, xg_ref):
    base = pl.program_id(0) * MOE_BLOCK

    def gather(r, carry):
        tok = row_tok_ref[base + r]
        xg_ref[pl.ds(r, 1), :] = hp_ref[pl.ds(tok, 1), :]
        return carry

    lax.fori_loop(0, MOE_BLOCK, gather, 0, unroll=8)
    xp = xg_ref[...]
    half = D // 2
    x_lo = pltpu.bitcast(xp << 16, F32).astype(BF16)
    x_hi = pltpu.bitcast(xp & jnp.uint32(0xFFFF0000), F32).astype(BF16)
    g = _dot(x_lo, wg_ref[0, :half, :]) + _dot(x_hi, wg_ref[0, half:, :]) + bg_ref[0]
    u = _dot(x_lo, wu_ref[0, :half, :]) + _dot(x_hi, wu_ref[0, half:, :]) + bu_ref[0]
    g = jnp.minimum(g, SWIGLU_LIMIT)
    u = jnp.clip(u, -SWIGLU_LIMIT, SWIGLU_LIMIT)
    act = (u + 1.0) * (g * jax.nn.sigmoid(SWIGLU_ALPHA * g))
    y = _dot(act.astype(BF16), wd_ref[0]) + bd_ref[0]
    y_ref[...] = y * gate_ref[...]


def _moe(blk_e, row_tok, row_gate, hp, wg, wu, wd, bg, bu, bd):
    n_blocks = blk_e.shape[0]
    t = hp.shape[0]
    ew = lambda i, be, rt: (be[i], 0, 0)
    return pl.pallas_call(
        _moe_kernel,
        grid_spec=pltpu.PrefetchScalarGridSpec(
            num_scalar_prefetch=2,
            grid=(n_blocks,),
            in_specs=[
                pl.BlockSpec((t, D // 2), lambda i, be, rt: (0, 0)),
                pl.BlockSpec((1, D, D), ew),
                pl.BlockSpec((1, D, D), ew),
                pl.BlockSpec((1, D, D), ew),
                pl.BlockSpec((1, 1, D), ew),
                pl.BlockSpec((1, 1, D), ew),
                pl.BlockSpec((1, 1, D), ew),
                pl.BlockSpec((MOE_BLOCK, 1), lambda i, be, rt: (i, 0)),
            ],
            out_specs=pl.BlockSpec((MOE_BLOCK, D), lambda i, be, rt: (i, 0)),
            scratch_shapes=[pltpu.VMEM((MOE_BLOCK, D // 2), jnp.uint32)],
        ),
        out_shape=jax.ShapeDtypeStruct((n_blocks * MOE_BLOCK, D), F32),
        compiler_params=_cparams(("arbitrary",)),
        name="moe_experts",
    )(blk_e, row_tok, hp, wg, wu, wd, bg, bu, bd, row_gate)


def _route(logits):
    t = logits.shape[0]
    top_v, top_i = lax.top_k(logits, TOP_K)
    gates = jax.nn.softmax(top_v, axis=-1)
    n_assign = t * TOP_K
    flat_e = top_i.reshape(-1)
    flat_t = jnp.arange(n_assign, dtype=jnp.int32) // TOP_K
    flat_g = gates.reshape(-1)
    order = jnp.argsort(flat_e)
    se, st, sg = flat_e[order], flat_t[order], flat_g[order]
    counts = jnp.bincount(flat_e, length=N_EXPERTS)
    padded = (counts + MOE_BLOCK - 1) // MOE_BLOCK * MOE_BLOCK
    pad_end = jnp.cumsum(padded)
    pad_start = pad_end - padded
    start = jnp.cumsum(counts) - counts
    dest = (pad_start[se] + jnp.arange(n_assign) - start[se]).astype(jnp.int32)
    n_blocks = -(-n_assign // MOE_BLOCK) + N_EXPERTS
    n_rows = n_blocks * MOE_BLOCK
    row_tok = jnp.zeros((n_rows,), jnp.int32).at[dest].set(st)
    row_gate = jnp.zeros((n_rows,), F32).at[dest].set(sg)
    blk_e = jnp.minimum(jnp.searchsorted(pad_end, jnp.arange(n_blocks) * MOE_BLOCK, side='right'),
                        N_EXPERTS - 1).astype(jnp.int32)
    pos = jnp.zeros((n_assign,), jnp.int32).at[order].set(dest).reshape(t, TOP_K)
    return blk_e, row_tok, row_gate.reshape(n_rows, 1), pos


def _final_norm_kernel(x_ref, g_ref, o_ref):
    x = x_ref[...]
    ms = jnp.mean(x * x, axis=-1, keepdims=True)
    o_ref[...] = x * lax.rsqrt(ms + EPS) * g_ref[...]


def _final_norm(x, g):
    tm = 1024
    t = x.shape[0]
    return pl.pallas_call(
        _final_norm_kernel,
        grid=(t // tm,),
        in_specs=[pl.BlockSpec((tm, D), lambda i: (i, 0)), pl.BlockSpec((1, D), lambda i: (0, 0))],
        out_specs=pl.BlockSpec((tm, D), lambda i: (i, 0)),
        out_shape=jax.ShapeDtypeStruct((t, D), F32),
        compiler_params=_cparams(("parallel",)),
        name="final_norm",
    )(x, g)


def _na_bias(rpb, rows):
    kr = min(NA_WIN_R, rows)
    r = np.arange(rows)
    r0 = np.clip(r - kr // 2, 0, rows - kr)
    row_ok = (r[None, :] >= r0[:, None]) & (r[None, :] < r0[:, None] + kr)
    dr = np.clip(r[None, :] - r[:, None] + (NA_WIN_R - 1), 0, 2 * NA_WIN_R - 2)
    c = np.arange(GRID_W)
    w0 = np.clip(c - NA_WIN_C // 2, 0, GRID_W - NA_WIN_C)
    col_ok = (c[None, :] >= w0[:, None]) & (c[None, :] < w0[:, None] + NA_WIN_C)
    dc = np.clip(c[None, :] - c[:, None] + (NA_WIN_C - 1), 0, 2 * NA_WIN_C - 2)
    vals = rpb.astype(F32)[:, dr[:, None, :, None], dc[None, :, None, :]]
    ok = row_ok[:, None, :, None] & col_ok[None, :, None, :]
    n = rows * GRID_W
    return jnp.where(ok[None], vals, NEG_INF).reshape(rpb.shape[0], n, n)


def _rope_tables(n_tokens):
    t = np.arange(n_tokens)
    row = (t // GRID_W).astype(np.float32)
    col = (t % GRID_W).astype(np.float32)
    half = HEAD_DIM // 2
    inv = jnp.asarray(ROPE_THETA, F32) ** (-jnp.arange(0, half, 2, dtype=F32) / half)
    ang_r = jnp.asarray(row)[:, None] * inv
    ang_c = jnp.asarray(col)[:, None] * inv
    cr, sr, cc, sc = jnp.cos(ang_r), jnp.sin(ang_r), jnp.cos(ang_c), jnp.sin(ang_c)
    cos = jnp.concatenate([cr, cr, cc, cc] * 2, axis=-1)
    sin = jnp.concatenate([-sr, sr, -sc, sc] * 2, axis=-1)
    return cos, sin


def _to_heads(x, n_batch, seq, n_heads):
    return x.reshape(n_batch, seq, n_heads, HEAD_DIM).transpose(0, 2, 1, 3)


def _ctx_lanes(cache_l):
    b, h, p, dh = cache_l.shape
    return cache_l.transpose(0, 2, 1, 3).reshape(b, p, h * dh)


def kernel(x_prompt, x_sample, cache_na_k, cache_na_v, cache_gqa_k, cache_gqa_v, c, c_ctx, w_mod, b_mod, norm1_g, norm2_g, w_in, na_rpb, q_norm_g, k_norm_g, cm_ln_g, cm_ln_b, cm_ws, cm_bs, w_out, router_w, router_b, w_gate_up, b_gate_up, w_down, b_down, final_norm_g):
    nb, seq, _ = x_prompt.shape
    db, n_tok, _ = x_sample.shape
    past = cache_na_k.shape[3]
    tp, ts = nb * seq, db * n_tok
    assert db <= CTX_ROW and n_tok % GRID_W == 0

    cond = jnp.zeros((MOD_ROWS, D), F32).at[:db].set(c).at[CTX_ROW].set(c_ctx)
    mod = _modulation(cond, w_mod, b_mod)
    cos, sin = _rope_tables(n_tok)
    x = jnp.concatenate([x_prompt.reshape(tp, D), x_sample.reshape(ts, D)], axis=0)

    w_in16 = w_in.astype(BF16)
    w_out16 = w_out.astype(BF16)
    wg16 = w_gate_up[..., 0::2].astype(BF16)
    wu16 = w_gate_up[..., 1::2].astype(BF16)
    wd16 = w_down.astype(BF16)
    bg = b_gate_up[..., 0::2].reshape(DEPTH, N_EXPERTS, 1, D)
    bu = b_gate_up[..., 1::2].reshape(DEPTH, N_EXPERTS, 1, D)
    bd = b_down.reshape(DEPTH, N_EXPERTS, 1, D)

    na_k, na_v, gqa_k, gqa_v = [], [], [], []
    for l in range(DEPTH):
        modr = mod[l].reshape(MOD_ROWS * 6, 1, D)
        qg2 = jnp.tile(q_norm_g[l], 2).reshape(1, LANES)
        kg2 = jnp.tile(k_norm_g[l], 2).reshape(1, LANES)
        lng = cm_ln_g[l].reshape(1, 256)
        lnb = cm_ln_b[l].reshape(1, 256)
        bsf = jnp.repeat(cm_bs[l].T, HEAD_DIM, axis=1)

        z = _inproj(x, modr, norm1_g[l].reshape(1, D), w_in16[l], tp, n_tok)
        mix_p, kbn = _ctx_mixer(z, nb, seq, qg2, kg2, lng, lnb, cm_ws[l], bsf)
        mix_s = _lat_mixer(z, tp, db, n_tok, past,
                           _ctx_lanes(cache_na_k[:, l]), _ctx_lanes(cache_na_v[:, l]),
                           _ctx_lanes(cache_gqa_k[:, l]), _ctx_lanes(cache_gqa_v[:, l]),
                           _na_bias(na_rpb[l], n_tok // GRID_W), cos, sin, qg2, kg2, lng, lnb, cm_ws[l], bsf)
        na_k.append(_to_heads(z[:tp, KA:KA + 256], nb, seq, 4))
        na_v.append(_to_heads(z[:tp, VA:VA + 256], nb, seq, 4))
        gqa_k.append(_to_heads(kbn, nb, seq, 2))
        gqa_v.append(_to_heads(z[:tp, VB:VB + LANES], nb, seq, 2))

        mix = jnp.concatenate([mix_p, mix_s], axis=0)
        xn, hp, logits = _outproj(mix, x, modr, norm2_g[l].reshape(1, D), w_out16[l],
                                  router_w[l], router_b[l].reshape(1, N_EXPERTS), tp, n_tok)

        outs = []
        for r0, r1 in ((0, tp), (tp, tp + ts)):
            blk_e, row_tok, row_gate, pos = _route(logits[r0:r1])
            yb = _moe(blk_e, row_tok, row_gate, hp[r0:r1], wg16[l], wu16[l], wd16[l], bg[l], bu[l], bd[l])
            outs.append(yb[pos].sum(axis=1))
        g2_p = mod[l, CTX_ROW, 5 * D:6 * D][None, :]
        g2_s = mod[l, :db, 5 * D:6 * D][:, None, :]
        x_p = xn[:tp] + g2_p * outs[0]
        x_s = (xn[tp:].reshape(db, n_tok, D) + g2_s * outs[1].reshape(db, n_tok, D)).reshape(ts, D)
        x = jnp.concatenate([x_p, x_s], axis=0)

    y = _final_norm(x, final_norm_g.reshape(1, D))
    return (y[:tp].reshape(nb, seq, D), y[tp:].reshape(db, n_tok, D),
            jnp.stack(na_k, axis=1), jnp.stack(na_v, axis=1), jnp.stack(gqa_k, axis=1), jnp.stack(gqa_v, axis=1))
```

```python
import functools

import jax
import jax.numpy as jnp
import numpy as np
from jax import lax
from jax.experimental import pallas as pl
from jax.experimental.pallas import tpu as pltpu

D = 1024
DEPTH = 2
GRID_W = 64
HEAD_DIM = 64
NA_WIN_R = 8
NA_WIN_C = 16
CHUNK = 128
ROPE_THETA = 10000.0
N_EXPERTS = 32
TOP_K = 4
SWIGLU_LIMIT = 7.0
SWIGLU_ALPHA = 1.702
EPS = 1e-6
MOE_BLOCK = 128
NEG_INF = -1e30
IN_WIDTH = 2048

QA, KA, VA, QB, KB, VB, CU, CV = 0, 256, 512, 768, 1280, 1408, 1536, 1792
OA, OB, OC = 0, 256, 768

LANES = 128
MOD_ROWS = 16
CTX_ROW = 8
VMEM_LIMIT = 56 * 1024 * 1024

F32 = jnp.float32
BF16 = jnp.bfloat16
HI = lax.Precision.HIGHEST


def _cparams(sem):
    return pltpu.CompilerParams(dimension_semantics=sem, vmem_limit_bytes=VMEM_LIMIT)


def _dot(a, b):
    return jnp.dot(a, b, preferred_element_type=F32)


def _dot_nt(a, b):
    return lax.dot_general(a, b, (((1,), (1,)), ((), ())), preferred_element_type=F32)


def _lane_lo():
    return lax.broadcasted_iota(jnp.int32, (1, LANES), 1) < HEAD_DIM


def _mod_kernel(c_ref, w_ref, b_ref, o_ref):
    c = c_ref[...]
    s = c * jax.nn.sigmoid(c)
    o_ref[0] = jnp.dot(s, w_ref[0], preferred_element_type=F32, precision=HI) + b_ref[0]


def _modulation(cond, w_mod, b_mod):
    tn = 1536
    n = w_mod.shape[-1]
    return pl.pallas_call(
        _mod_kernel,
        grid=(DEPTH, n // tn),
        in_specs=[
            pl.BlockSpec((MOD_ROWS, D), lambda l, j: (0, 0)),
            pl.BlockSpec((1, D, tn), lambda l, j: (l, 0, j)),
            pl.BlockSpec((1, 1, tn), lambda l, j: (l, 0, j)),
        ],
        out_specs=pl.BlockSpec((1, MOD_ROWS, tn), lambda l, j: (l, 0, j)),
        out_shape=jax.ShapeDtypeStruct((DEPTH, MOD_ROWS, n), F32),
        compiler_params=_cparams(("arbitrary", "arbitrary")),
        name="modulation",
    )(cond, w_mod, b_mod.reshape(DEPTH, 1, n))


def _mod_row(i, tm, tp, ts_per_batch):
    start = i * tm
    return jnp.where(start < tp, CTX_ROW, (start - tp) // ts_per_batch)


def _mod_spec(k, tm, tp, ts_per_batch):
    return pl.BlockSpec((1, 1, D), lambda i: (_mod_row(i, tm, tp, ts_per_batch) * 6 + k, 0, 0))


def _inproj_kernel(x_ref, shift_ref, scale_ref, g_ref, w_ref, z_ref):
    x = x_ref[...]
    ms = jnp.mean(x * x, axis=-1, keepdims=True)
    y = x * lax.rsqrt(ms + EPS) * g_ref[...]
    h = y * (1.0 + scale_ref[0]) + shift_ref[0]
    z_ref[...] = _dot(h.astype(BF16), w_ref[...])


def _inproj(x, modr, g, w, tp, ts_per_batch):
    tm = 512
    t = x.shape[0]
    return pl.pallas_call(
        _inproj_kernel,
        grid=(t // tm,),
        in_specs=[
            pl.BlockSpec((tm, D), lambda i: (i, 0)),
            _mod_spec(0, tm, tp, ts_per_batch),
            _mod_spec(1, tm, tp, ts_per_batch),
            pl.BlockSpec((1, D), lambda i: (0, 0)),
            pl.BlockSpec((D, IN_WIDTH), lambda i: (0, 0)),
        ],
        out_specs=pl.BlockSpec((tm, IN_WIDTH), lambda i: (i, 0)),
        out_shape=jax.ShapeDtypeStruct((t, IN_WIDTH), F32),
        compiler_params=_cparams(("parallel",)),
        name="inproj",
    )(x, modr, modr, g, w)


def _head_rms(x, g2):
    lo = _lane_lo()
    x2 = x * x
    s_lo = jnp.sum(jnp.where(lo, x2, 0.0), axis=-1, keepdims=True)
    s_hi = jnp.sum(jnp.where(lo, 0.0, x2), axis=-1, keepdims=True)
    ms = jnp.where(lo, s_lo, s_hi) * (1.0 / HEAD_DIM)
    return x * lax.rsqrt(ms + EPS) * g2


def _softmax_pv(q16, ks, vs, biases):
    ss = []
    for k, b in zip(ks, biases):
        s = _dot_nt(q16, k)
        if b is not None:
            s = s + b
        ss.append(s)
    m = ss[0].max(axis=-1, keepdims=True)
    for s in ss[1:]:
        m = jnp.maximum(m, s.max(axis=-1, keepdims=True))
    den = None
    acc = None
    for s, v in zip(ss, vs):
        e = jnp.exp(s - m)
        d = e.sum(axis=-1, keepdims=True)
        o = _dot(e.astype(BF16), v)
        den = d if den is None else den + d
        acc = o if acc is None else acc + o
    return acc / den


def _attend_pair(q, ks_lo, vs_lo, ks_hi, vs_hi, b_lo, b_hi):
    lo = _lane_lo()
    o_lo = _softmax_pv(jnp.where(lo, q, 0.0).astype(BF16), ks_lo, vs_lo, b_lo)
    o_hi = _softmax_pv(jnp.where(lo, 0.0, q).astype(BF16), ks_hi, vs_hi, b_hi)
    return jnp.where(lo, o_lo, o_hi)


def _gelu(x):
    c = np.sqrt(2.0 / np.pi).astype(np.float32)
    return x * (0.5 * (1.0 + jnp.tanh(c * (x + 0.044715 * (x * x * x)))))


def _layer_norm(x, g, b):
    mu = jnp.mean(x, axis=-1, keepdims=True)
    xc = x - mu
    var = jnp.mean(xc * xc, axis=-1, keepdims=True)
    return xc * lax.rsqrt(var + EPS) * g + b


def _chunk_mlp(cu, cv, lng_ref, lnb_ref, ws_ref, bsf_ref, out_ref, col0):
    lo = _lane_lo()
    u = _gelu(cu)
    v = _layer_norm(_gelu(cv), lng_ref[...], lnb_ref[...]).astype(BF16)
    s = cu.shape[0]
    for n in range(s // CHUNK):
        rows = slice(n * CHUNK, (n + 1) * CHUNK)
        for jb in range(2):
            cols = slice(jb * LANES, (jb + 1) * LANES)
            vb = v[rows, cols]
            m_lo = _dot(ws_ref[2 * jb].astype(BF16), vb)
            m_hi = _dot(ws_ref[2 * jb + 1].astype(BF16), vb)
            mixed = jnp.where(lo, m_lo, m_hi) + bsf_ref[:, cols]
            out_ref[rows, col0 + jb * LANES:col0 + (jb + 1) * LANES] = (u[rows, cols] * mixed).astype(out_ref.dtype)


def _ctx_mixer_kernel(z_ref, qg_ref, kg_ref, lng_ref, lnb_ref, ws_ref, bsf_ref, mix_ref, kbn_ref):
    scale = HEAD_DIM ** -0.5
    for j in range(2):
        q = z_ref[:, QA + LANES * j:QA + LANES * (j + 1)] * scale
        k = [z_ref[:, KA + LANES * j:KA + LANES * (j + 1)].astype(BF16)]
        v = [z_ref[:, VA + LANES * j:VA + LANES * (j + 1)].astype(BF16)]
        o = _attend_pair(q, k, v, k, v, [None], [None])
        mix_ref[:, OA + LANES * j:OA + LANES * (j + 1)] = o.astype(mix_ref.dtype)

    kb = _head_rms(z_ref[:, KB:KB + LANES], kg_ref[...])
    kbn_ref[...] = kb
    vb = z_ref[:, VB:VB + LANES]
    k_same = [kb.astype(BF16)]
    k_swap = [pltpu.roll(kb, HEAD_DIM, 1).astype(BF16)]
    v_same = [vb.astype(BF16)]
    v_swap = [pltpu.roll(vb, HEAD_DIM, 1).astype(BF16)]
    for j in range(4):
        q = _head_rms(z_ref[:, QB + LANES * j:QB + LANES * (j + 1)], qg_ref[...]) * scale
        if j // 2 == 0:
            o = _attend_pair(q, k_same, v_same, k_swap, v_swap, [None], [None])
        else:
            o = _attend_pair(q, k_swap, v_swap, k_same, v_same, [None], [None])
        mix_ref[:, OB + LANES * j:OB + LANES * (j + 1)] = o.astype(mix_ref.dtype)

    _chunk_mlp(z_ref[:, CU:CU + 256], z_ref[:, CV:CV + 256], lng_ref, lnb_ref, ws_ref, bsf_ref, mix_ref, OC)


def _ctx_mixer(z, n_batch, seq, qg2, kg2, lng, lnb, ws, bsf):
    small = lambda shape: pl.BlockSpec(shape, lambda b: (0,) * len(shape))
    return pl.pallas_call(
        _ctx_mixer_kernel,
        grid=(n_batch,),
        in_specs=[
            pl.BlockSpec((seq, IN_WIDTH), lambda b: (b, 0)),
            small((1, LANES)), small((1, LANES)), small((1, 256)), small((1, 256)),
            small((4, CHUNK, CHUNK)), small((CHUNK, 256)),
        ],
        out_specs=[
            pl.BlockSpec((seq, D), lambda b: (b, 0)),
            pl.BlockSpec((seq, LANES), lambda b: (b, 0)),
        ],
        out_shape=[
            jax.ShapeDtypeStruct((n_batch * seq, D), BF16),
            jax.ShapeDtypeStruct((n_batch * seq, LANES), F32),
        ],
        compiler_params=_cparams(("parallel",)),
        name="ctx_mixer",
    )(z, qg2, kg2, lng, lnb, ws, bsf)


def _rope(x, cos, sin):
    first = (lax.broadcasted_iota(jnp.int32, (1, LANES), 1) % 32) < 16
    partner = jnp.where(first, pltpu.roll(x, LANES - 16, 1), pltpu.roll(x, 16, 1))
    return x * cos + partner * sin


def _lat_mixer_kernel(zq_ref, zkv_ref, nakc_ref, navc_ref, gkc_ref, gvc_ref, bias_ref,
                      cosq_ref, sinq_ref, cosk_ref, sink_ref,
                      qg_ref, kg_ref, lng_ref, lnb_ref, ws_ref, bsf_ref,
                      mix_ref,
                      kl_ref, kls_ref, vl_ref, vls_ref, kc_ref, kcs_ref, vc_ref, vcs_ref):
    scale = HEAD_DIM ** -0.5

    @pl.when(pl.program_id(1) == 0)
    def _():
        kb = _rope(_head_rms(zkv_ref[:, KB:KB + LANES], kg_ref[...]), cosk_ref[...], sink_ref[...])
        vb = zkv_ref[:, VB:VB + LANES]
        kl_ref[...] = kb.astype(BF16)
        kls_ref[...] = pltpu.roll(kb, HEAD_DIM, 1).astype(BF16)
        vl_ref[...] = vb.astype(BF16)
        vls_ref[...] = pltpu.roll(vb, HEAD_DIM, 1).astype(BF16)
        kc = gkc_ref[0]
        vc = gvc_ref[0]
        kc_ref[...] = kc.astype(BF16)
        kcs_ref[...] = pltpu.roll(kc, HEAD_DIM, 1).astype(BF16)
        vc_ref[...] = vc.astype(BF16)
        vcs_ref[...] = pltpu.roll(vc, HEAD_DIM, 1).astype(BF16)

    for j in range(2):
        cols = slice(LANES * j, LANES * (j + 1))
        q = zq_ref[:, QA + LANES * j:QA + LANES * (j + 1)] * scale
        ks = [zkv_ref[:, KA + LANES * j:KA + LANES * (j + 1)].astype(BF16), nakc_ref[0, :, cols].astype(BF16)]
        vs = [zkv_ref[:, VA + LANES * j:VA + LANES * (j + 1)].astype(BF16), navc_ref[0, :, cols].astype(BF16)]
        o = _attend_pair(q, ks, vs, ks, vs, [bias_ref[2 * j], None], [bias_ref[2 * j + 1], None])
        mix_ref[:, OA + LANES * j:OA + LANES * (j + 1)] = o.astype(mix_ref.dtype)

    same = ([kl_ref[...], kc_ref[...]], [vl_ref[...], vc_ref[...]])
    swap = ([kls_ref[...], kcs_ref[...]], [vls_ref[...], vcs_ref[...]])
    for j in range(4):
        q = _head_rms(zq_ref[:, QB + LANES * j:QB + LANES * (j + 1)], qg_ref[...])
        q = _rope(q, cosq_ref[...], sinq_ref[...]) * scale
        lo_kv, hi_kv = (same, swap) if j // 2 == 0 else (swap, same)
        o = _attend_pair(q, lo_kv[0], lo_kv[1], hi_kv[0], hi_kv[1], [None, None], [None, None])
        mix_ref[:, OB + LANES * j:OB + LANES * (j + 1)] = o.astype(mix_ref.dtype)

    _chunk_mlp(zq_ref[:, CU:CU + 256], zq_ref[:, CV:CV + 256], lng_ref, lnb_ref, ws_ref, bsf_ref, mix_ref, OC)


def _lat_mixer(z, row0, n_batch, n_tok, past, nakc, navc, gkc, gvc, bias, cos, sin, qg2, kg2, lng, lnb, ws, bsf):
    tq = 256
    nq = n_tok // tq
    small = lambda shape: pl.BlockSpec(shape, lambda b, t: (0,) * len(shape))
    qblk0 = row0 // tq
    kvblk0 = row0 // n_tok
    return pl.pallas_call(
        _lat_mixer_kernel,
        grid=(n_batch, nq),
        in_specs=[
            pl.BlockSpec((tq, IN_WIDTH), lambda b, t: (qblk0 + b * nq + t, 0)),
            pl.BlockSpec((n_tok, IN_WIDTH), lambda b, t: (kvblk0 + b, 0)),
            pl.BlockSpec((1, past, 256), lambda b, t: (b, 0, 0)),
            pl.BlockSpec((1, past, 256), lambda b, t: (b, 0, 0)),
            pl.BlockSpec((1, past, LANES), lambda b, t: (b, 0, 0)),
            pl.BlockSpec((1, past, LANES), lambda b, t: (b, 0, 0)),
            pl.BlockSpec((4, tq, n_tok), lambda b, t: (0, t, 0)),
            pl.BlockSpec((tq, LANES), lambda b, t: (t, 0)),
            pl.BlockSpec((tq, LANES), lambda b, t: (t, 0)),
            small((n_tok, LANES)), small((n_tok, LANES)),
            small((1, LANES)), small((1, LANES)), small((1, 256)), small((1, 256)),
            small((4, CHUNK, CHUNK)), small((CHUNK, 256)),
        ],
        out_specs=pl.BlockSpec((tq, D), lambda b, t: (b * nq + t, 0)),
        out_shape=jax.ShapeDtypeStruct((n_batch * n_tok, D), BF16),
        scratch_shapes=[pltpu.VMEM((n_tok, LANES), BF16)] * 4 + [pltpu.VMEM((past, LANES), BF16)] * 4,
        compiler_params=_cparams(("parallel", "arbitrary")),
        name="lat_mixer",
    )(z, z, nakc, navc, gkc, gvc, bias, cos, sin, cos, sin, qg2, kg2, lng, lnb, ws, bsf)


def _outproj_kernel(mix_ref, x_ref, g1_ref, shift_ref, scale_ref, g_ref, w_ref, rw_ref, rb_ref,
                    xn_ref, hp_ref, lg_ref):
    x = x_ref[...] + g1_ref[0] * _dot(mix_ref[...], w_ref[...])
    xn_ref[...] = x
    ms = jnp.mean(x * x, axis=-1, keepdims=True)
    h = x * lax.rsqrt(ms + EPS) * g_ref[...]
    h = h * (1.0 + scale_ref[0]) + shift_ref[0]
    lg_ref[...] = jnp.dot(h, rw_ref[...], preferred_element_type=F32, precision=HI) + rb_ref[...]
    half = D // 2
    lo = pltpu.bitcast(h[:, :half].astype(BF16).astype(F32), jnp.uint32) >> 16
    hi = pltpu.bitcast(h[:, half:].astype(BF16).astype(F32), jnp.uint32) & jnp.uint32(0xFFFF0000)
    hp_ref[...] = hi | lo


def _outproj(mix, x, modr, g, w, rw, rb, tp, ts_per_batch):
    tm = 512
    t = x.shape[0]
    return pl.pallas_call(
        _outproj_kernel,
        grid=(t // tm,),
        in_specs=[
            pl.BlockSpec((tm, D), lambda i: (i, 0)),
            pl.BlockSpec((tm, D), lambda i: (i, 0)),
            _mod_spec(2, tm, tp, ts_per_batch),
            _mod_spec(3, tm, tp, ts_per_batch),
            _mod_spec(4, tm, tp, ts_per_batch),
            pl.BlockSpec((1, D), lambda i: (0, 0)),
            pl.BlockSpec((D, D), lambda i: (0, 0)),
            pl.BlockSpec((D, N_EXPERTS), lambda i: (0, 0)),
            pl.BlockSpec((1, N_EXPERTS), lambda i: (0, 0)),
        ],
        out_specs=[
            pl.BlockSpec((tm, D), lambda i: (i, 0)),
            pl.BlockSpec((tm, D // 2), lambda i: (i, 0)),
            pl.BlockSpec((tm, N_EXPERTS), lambda i: (i, 0)),
        ],
        out_shape=[
            jax.ShapeDtypeStruct((t, D), F32),
            jax.ShapeDtypeStruct((t, D // 2), jnp.uint32),
            jax.ShapeDtypeStruct((t, N_EXPERTS), F32),
        ],
        compiler_params=_cparams(("parallel",)),
        name="outproj",
    )(mix, x, modr, modr, modr, g, w, rw, rb)


def _prep_wgu_kernel(w_ref, p_ref, wg_ref, wu_ref):
    for j in range(w_ref.shape[2] // 256):
        w = w_ref[0, :, 256 * j:256 * (j + 1)].astype(BF16)
        sel = _dot(w, p_ref[...])
        wg_ref[0, :, LANES * j:LANES * (j + 1)] = sel[:, :LANES].astype(BF16)
        wu_ref[0, :, LANES * j:LANES * (j + 1)] = sel[:, LANES:].astype(BF16)


def _prep_wgu(w):
    e, d, f2 = w.shape
    perm = np.zeros((256, 256), np.float32)
    perm[2 * np.arange(LANES), np.arange(LANES)] = 1.0
    perm[2 * np.arange(LANES) + 1, LANES + np.arange(LANES)] = 1.0
    out = jax.ShapeDtypeStruct((e, d, f2 // 2), BF16)
    return pl.pallas_call(
        _prep_wgu_kernel,
        grid=(e,),
        in_specs=[pl.BlockSpec((1, d, f2), lambda i: (i, 0, 0)), pl.BlockSpec((256, 256), lambda i: (0, 0))],
        out_specs=[pl.BlockSpec((1, d, f2 // 2), lambda i: (i, 0, 0))] * 2,
        out_shape=[out, out],
        compiler_params=_cparams(("parallel",)),
        name="prep_wgu",
    )(w, jnp.asarray(perm, BF16))


def _moe_kernel(blk_e_ref, row_tok_ref, row_dst_ref, hp_ref, wg_ref, wu_ref, wd_ref, bg_ref, bu_ref, bd_ref,
                gate_ref, out_ref, xg_ref, ybuf_ref, sem):
    i = pl.program_id(0)
    n = pl.num_programs(0)
    slot = i % 2
    base = i * MOE_BLOCK

    def slot_copy(s):
        return pltpu.make_async_copy(ybuf_ref.at[s], out_ref.at[pl.ds(0, MOE_BLOCK)], sem.at[s])

    @pl.when(i >= 2)
    def _():
        slot_copy(slot).wait()

    def gather(r, carry):
        tok = row_tok_ref[base + r]
        xg_ref[pl.ds(r, 1), :] = hp_ref[pl.ds(tok, 1), :]
        return carry

    lax.fori_loop(0, MOE_BLOCK, gather, 0, unroll=8)
    xp = xg_ref[...]
    half = D // 2
    x_lo = pltpu.bitcast(xp << 16, F32).astype(BF16)
    x_hi = pltpu.bitcast(xp & jnp.uint32(0xFFFF0000), F32).astype(BF16)
    g = _dot(x_lo, wg_ref[0, :half, :]) + _dot(x_hi, wg_ref[0, half:, :]) + bg_ref[0]
    u = _dot(x_lo, wu_ref[0, :half, :]) + _dot(x_hi, wu_ref[0, half:, :]) + bu_ref[0]
    g = jnp.minimum(g, SWIGLU_LIMIT)
    u = jnp.clip(u, -SWIGLU_LIMIT, SWIGLU_LIMIT)
    act = (u + 1.0) * (g * jax.nn.sigmoid(SWIGLU_ALPHA * g))
    y = _dot(act.astype(BF16), wd_ref[0]) + bd_ref[0]
    ybuf_ref[slot] = y * gate_ref[...]

    def scatter(r, carry):
        dst = row_dst_ref[base + r]
        pltpu.make_async_copy(ybuf_ref.at[slot, pl.ds(r, 1)], out_ref.at[pl.ds(dst, 1)], sem.at[slot]).start()
        return carry

    lax.fori_loop(0, MOE_BLOCK, scatter, 0, unroll=8)

    @pl.when(i == n - 1)
    def _():
        slot_copy(1 - slot).wait()
        slot_copy(slot).wait()


def _moe(blk_e, row_tok, row_dst, row_gate, hp, wg, wu, wd, bg, bu, bd):
    n_blocks = blk_e.shape[0]
    assert n_blocks >= 2
    t = hp.shape[0]
    ew = lambda i, be, rt, rd: (be[i], 0, 0)
    return pl.pallas_call(
        _moe_kernel,
        grid_spec=pltpu.PrefetchScalarGridSpec(
            num_scalar_prefetch=3,
            grid=(n_blocks,),
            in_specs=[
                pl.BlockSpec((t, D // 2), lambda i, be, rt, rd: (0, 0), pipeline_mode=pl.Buffered(1)),
                pl.BlockSpec((1, D, D), ew),
                pl.BlockSpec((1, D, D), ew),
                pl.BlockSpec((1, D, D), ew),
                pl.BlockSpec((1, 1, D), ew),
                pl.BlockSpec((1, 1, D), ew),
                pl.BlockSpec((1, 1, D), ew),
                pl.BlockSpec((MOE_BLOCK, 1), lambda i, be, rt, rd: (i, 0)),
            ],
            out_specs=pl.BlockSpec(memory_space=pl.ANY),
            scratch_shapes=[
                pltpu.VMEM((MOE_BLOCK, D // 2), jnp.uint32),
                pltpu.VMEM((2, MOE_BLOCK, D), F32),
                pltpu.SemaphoreType.DMA((2,)),
            ],
        ),
        out_shape=jax.ShapeDtypeStruct((n_blocks * MOE_BLOCK, D), F32),
        compiler_params=_cparams(("arbitrary",)),
        name="moe_experts",
    )(blk_e, row_tok, row_dst, hp, wg, wu, wd, bg, bu, bd, row_gate)


def _route(logits):
    t = logits.shape[0]
    top_v, top_i = lax.top_k(logits, TOP_K)
    gates = jax.nn.softmax(top_v, axis=-1)
    n_assign = t * TOP_K
    flat_e = top_i.reshape(-1)
    order = jnp.argsort(flat_e).astype(jnp.int32)
    se = flat_e[order]
    counts = jnp.bincount(flat_e, length=N_EXPERTS)
    padded = (counts + MOE_BLOCK - 1) // MOE_BLOCK * MOE_BLOCK
    pad_end = jnp.cumsum(padded)
    pad_start = pad_end - padded
    start = jnp.cumsum(counts) - counts
    dest = (pad_start[se] + jnp.arange(n_assign) - start[se]).astype(jnp.int32)
    n_blocks = -(-n_assign // MOE_BLOCK) + N_EXPERTS
    n_rows = n_blocks * MOE_BLOCK
    slot_row = (order % TOP_K) * t + order // TOP_K
    marked = jnp.zeros((n_rows,), jnp.int32).at[dest].set(slot_row + 1)
    is_real = marked > 0
    pad_rank = jnp.cumsum(jnp.where(is_real, 0, 1)) - 1
    row_dst = jnp.where(is_real, marked - 1, n_assign + pad_rank).astype(jnp.int32)
    row_tok = jnp.where(is_real, (marked - 1) % t, 0).astype(jnp.int32)
    row_gate = jnp.zeros((n_rows,), F32).at[dest].set(gates.reshape(-1)[order])
    blk_e = jnp.minimum(jnp.searchsorted(pad_end, jnp.arange(n_blocks) * MOE_BLOCK, side='right'),
                        N_EXPERTS - 1).astype(jnp.int32)
    return blk_e, row_tok, row_dst, row_gate.reshape(n_rows, 1)


def _combine_kernel(y0_ref, y1_ref, y2_ref, y3_ref, x_ref, g2_ref, o_ref):
    y = (y0_ref[...] + y1_ref[...]) + (y2_ref[...] + y3_ref[...])
    o_ref[...] = x_ref[...] + g2_ref[0] * y


def _combine(ys, x, modr, tp, ts_per_batch):
    tm = 512
    t = x.shape[0]
    nt = t // tm
    yspec = lambda k: pl.BlockSpec((tm, D), lambda i: (k * nt + i, 0))
    return pl.pallas_call(
        _combine_kernel,
        grid=(nt,),
        in_specs=[yspec(0), yspec(1), yspec(2), yspec(3),
                  pl.BlockSpec((tm, D), lambda i: (i, 0)), _mod_spec(5, tm, tp, ts_per_batch)],
        out_specs=pl.BlockSpec((tm, D), lambda i: (i, 0)),
        out_shape=jax.ShapeDtypeStruct((t, D), F32),
        compiler_params=_cparams(("parallel",)),
        name="moe_combine",
    )(ys, ys, ys, ys, x, modr)


def _final_norm_kernel(x_ref, g_ref, o_ref):
    x = x_ref[...]
    ms = jnp.mean(x * x, axis=-1, keepdims=True)
    o_ref[...] = x * lax.rsqrt(ms + EPS) * g_ref[...]


def _final_norm(x, g):
    tm = 1024
    t = x.shape[0]
    return pl.pallas_call(
        _final_norm_kernel,
        grid=(t // tm,),
        in_specs=[pl.BlockSpec((tm, D), lambda i: (i, 0)), pl.BlockSpec((1, D), lambda i: (0, 0))],
        out_specs=pl.BlockSpec((tm, D), lambda i: (i, 0)),
        out_shape=jax.ShapeDtypeStruct((t, D), F32),
        compiler_params=_cparams(("parallel",)),
        name="final_norm",
    )(x, g)


def _na_bias(rpb, rows):
    kr = min(NA_WIN_R, rows)
    r = np.arange(rows)
    r0 = np.clip(r - kr // 2, 0, rows - kr)
    row_ok = (r[None, :] >= r0[:, None]) & (r[None, :] < r0[:, None] + kr)
    dr = np.clip(r[None, :] - r[:, None] + (NA_WIN_R - 1), 0, 2 * NA_WIN_R - 2)
    c = np.arange(GRID_W)
    w0 = np.clip(c - NA_WIN_C // 2, 0, GRID_W - NA_WIN_C)
    col_ok = (c[None, :] >= w0[:, None]) & (c[None, :] < w0[:, None] + NA_WIN_C)
    dc = np.clip(c[None, :] - c[:, None] + (NA_WIN_C - 1), 0, 2 * NA_WIN_C - 2)
    vals = rpb.astype(F32)[:, dr[:, None, :, None], dc[None, :, None, :]]
    ok = row_ok[:, None, :, None] & col_ok[None, :, None, :]
    n = rows * GRID_W
    return jnp.where(ok[None], vals, NEG_INF).reshape(rpb.shape[0], n, n)


def _rope_tables(n_tokens):
    t = np.arange(n_tokens)
    row = (t // GRID_W).astype(np.float32)
    col = (t % GRID_W).astype(np.float32)
    half = HEAD_DIM // 2
    inv = jnp.asarray(ROPE_THETA, F32) ** (-jnp.arange(0, half, 2, dtype=F32) / half)
    ang_r = jnp.asarray(row)[:, None] * inv
    ang_c = jnp.asarray(col)[:, None] * inv
    cr, sr, cc, sc = jnp.cos(ang_r), jnp.sin(ang_r), jnp.cos(ang_c), jnp.sin(ang_c)
    cos = jnp.concatenate([cr, cr, cc, cc] * 2, axis=-1)
    sin = jnp.concatenate([-sr, sr, -sc, sc] * 2, axis=-1)
    return cos, sin


def _to_heads(x, n_batch, seq, n_heads):
    return x.reshape(n_batch, seq, n_heads, HEAD_DIM).transpose(0, 2, 1, 3)


def _ctx_lanes(cache_l):
    b, h, p, dh = cache_l.shape
    return cache_l.transpose(0, 2, 1, 3).reshape(b, p, h * dh)


def kernel(x_prompt, x_sample, cache_na_k, cache_na_v, cache_gqa_k, cache_gqa_v, c, c_ctx, w_mod, b_mod, norm1_g, norm2_g, w_in, na_rpb, q_norm_g, k_norm_g, cm_ln_g, cm_ln_b, cm_ws, cm_bs, w_out, router_w, router_b, w_gate_up, b_gate_up, w_down, b_down, final_norm_g):
    nb, seq, _ = x_prompt.shape
    db, n_tok, _ = x_sample.shape
    past = cache_na_k.shape[3]
    tp, ts = nb * seq, db * n_tok
    assert db <= CTX_ROW and n_tok % GRID_W == 0

    cond = jnp.zeros((MOD_ROWS, D), F32).at[:db].set(c).at[CTX_ROW].set(c_ctx)
    mod = _modulation(cond, w_mod, b_mod)
    cos, sin = _rope_tables(n_tok)
    x = jnp.concatenate([x_prompt.reshape(tp, D), x_sample.reshape(ts, D)], axis=0)

    w_in16 = w_in.astype(BF16)
    w_out16 = w_out.astype(BF16)
    wg16, wu16 = _prep_wgu(w_gate_up.reshape(DEPTH * N_EXPERTS, D, 2 * D))
    wg16 = wg16.reshape(DEPTH, N_EXPERTS, D, D)
    wu16 = wu16.reshape(DEPTH, N_EXPERTS, D, D)
    wd16 = w_down.astype(BF16)
    bg = b_gate_up[..., 0::2].reshape(DEPTH, N_EXPERTS, 1, D)
    bu = b_gate_up[..., 1::2].reshape(DEPTH, N_EXPERTS, 1, D)
    bd = b_down.reshape(DEPTH, N_EXPERTS, 1, D)

    na_k, na_v, gqa_k, gqa_v = [], [], [], []
    for l in range(DEPTH):
        modr = mod[l].reshape(MOD_ROWS * 6, 1, D)
        qg2 = jnp.tile(q_norm_g[l], 2).reshape(1, LANES)
        kg2 = jnp.tile(k_norm_g[l], 2).reshape(1, LANES)
        lng = cm_ln_g[l].reshape(1, 256)
        lnb = cm_ln_b[l].reshape(1, 256)
        bsf = jnp.repeat(cm_bs[l].T, HEAD_DIM, axis=1)

        z = _inproj(x, modr, norm1_g[l].reshape(1, D), w_in16[l], tp, n_tok)
        mix_p, kbn = _ctx_mixer(z, nb, seq, qg2, kg2, lng, lnb, cm_ws[l], bsf)
        mix_s = _lat_mixer(z, tp, db, n_tok, past,
                           _ctx_lanes(cache_na_k[:, l]), _ctx_lanes(cache_na_v[:, l]),
                           _ctx_lanes(cache_gqa_k[:, l]), _ctx_lanes(cache_gqa_v[:, l]),
                           _na_bias(na_rpb[l], n_tok // GRID_W), cos, sin, qg2, kg2, lng, lnb, cm_ws[l], bsf)
        na_k.append(_to_heads(z[:tp, KA:KA + 256], nb, seq, 4))
        na_v.append(_to_heads(z[:tp, VA:VA + 256], nb, seq, 4))
        gqa_k.append(_to_heads(kbn, nb, seq, 2))
        gqa_v.append(_to_heads(z[:tp, VB:VB + LANES], nb, seq, 2))

        mix = jnp.concatenate([mix_p, mix_s], axis=0)
        xn, hp, logits = _outproj(mix, x, modr, norm2_g[l].reshape(1, D), w_out16[l],
                                  router_w[l], router_b[l].reshape(1, N_EXPERTS), tp, n_tok)

        blk_e, row_tok, row_dst, row_gate = _route(logits)
        ys = _moe(blk_e, row_tok, row_dst, row_gate, hp, wg16[l], wu16[l], wd16[l], bg[l], bu[l], bd[l])
        x = _combine(ys, xn, modr, tp, n_tok)

    y = _final_norm(x, final_norm_g.reshape(1, D))
    return (y[:tp].reshape(nb, seq, D), y[tp:].reshape(db, n_tok, D),
            jnp.stack(na_k, axis=1), jnp.stack(na_v, axis=1), jnp.stack(gqa_k, axis=1), jnp.stack(gqa_v, axis=1))
```

```python
import functools

import jax
import jax.numpy as jnp
import numpy as np
from jax import lax
from jax.experimental import pallas as pl
from jax.experimental.pallas import tpu as pltpu

D = 1024
DEPTH = 2
GRID_W = 64
HEAD_DIM = 64
NA_WIN_R = 8
NA_WIN_C = 16
CHUNK = 128
ROPE_THETA = 10000.0
N_EXPERTS = 32
TOP_K = 4
SWIGLU_LIMIT = 7.0
SWIGLU_ALPHA = 1.702
EPS = 1e-6
MOE_BLOCK = 128
NEG_INF = -1e30
IN_WIDTH = 2048

QA, KA, VA, QB, KB, VB, CU, CV = 0, 256, 512, 768, 1280, 1408, 1536, 1792
OA, OB, OC = 0, 256, 768

LANES = 128
MOD_ROWS = 16
CTX_ROW = 8
VMEM_LIMIT = 56 * 1024 * 1024

F32 = jnp.float32
BF16 = jnp.bfloat16
HI = lax.Precision.HIGHEST


def _cparams(sem):
    return pltpu.CompilerParams(dimension_semantics=sem, vmem_limit_bytes=VMEM_LIMIT)


def _dot(a, b):
    return jnp.dot(a, b, preferred_element_type=F32)


def _dot_nt(a, b):
    return lax.dot_general(a, b, (((1,), (1,)), ((), ())), preferred_element_type=F32)


def _lane_lo():
    return lax.broadcasted_iota(jnp.int32, (1, LANES), 1) < HEAD_DIM


def _mod_kernel(c_ref, w_ref, b_ref, o_ref):
    c = c_ref[...]
    s = c * jax.nn.sigmoid(c)
    o_ref[0] = jnp.dot(s, w_ref[0], preferred_element_type=F32, precision=HI) + b_ref[0]


def _modulation(cond, w_mod, b_mod):
    tn = 1536
    n = w_mod.shape[-1]
    return pl.pallas_call(
        _mod_kernel,
        grid=(DEPTH, n // tn),
        in_specs=[
            pl.BlockSpec((MOD_ROWS, D), lambda l, j: (0, 0)),
            pl.BlockSpec((1, D, tn), lambda l, j: (l, 0, j)),
            pl.BlockSpec((1, 1, tn), lambda l, j: (l, 0, j)),
        ],
        out_specs=pl.BlockSpec((1, MOD_ROWS, tn), lambda l, j: (l, 0, j)),
        out_shape=jax.ShapeDtypeStruct((DEPTH, MOD_ROWS, n), F32),
        compiler_params=_cparams(("arbitrary", "arbitrary")),
        name="modulation",
    )(cond, w_mod, b_mod.reshape(DEPTH, 1, n))


def _mod_row(i, tm, tp, ts_per_batch):
    start = i * tm
    return jnp.where(start < tp, CTX_ROW, (start - tp) // ts_per_batch)


def _mod_spec(k, tm, tp, ts_per_batch):
    return pl.BlockSpec((1, 1, D), lambda i: (_mod_row(i, tm, tp, ts_per_batch) * 6 + k, 0, 0))


def _inproj_kernel(x_ref, shift_ref, scale_ref, g_ref, w_ref, z_ref):
    x = x_ref[...]
    ms = jnp.mean(x * x, axis=-1, keepdims=True)
    y = x * lax.rsqrt(ms + EPS) * g_ref[...]
    h = y * (1.0 + scale_ref[0]) + shift_ref[0]
    z_ref[...] = _dot(h.astype(BF16), w_ref[...])


def _inproj(x, modr, g, w, tp, ts_per_batch):
    tm = 512
    t = x.shape[0]
    return pl.pallas_call(
        _inproj_kernel,
        grid=(t // tm,),
        in_specs=[
            pl.BlockSpec((tm, D), lambda i: (i, 0)),
            _mod_spec(0, tm, tp, ts_per_batch),
            _mod_spec(1, tm, tp, ts_per_batch),
            pl.BlockSpec((1, D), lambda i: (0, 0)),
            pl.BlockSpec((D, IN_WIDTH), lambda i: (0, 0)),
        ],
        out_specs=pl.BlockSpec((tm, IN_WIDTH), lambda i: (i, 0)),
        out_shape=jax.ShapeDtypeStruct((t, IN_WIDTH), F32),
        compiler_params=_cparams(("parallel",)),
        name="inproj",
    )(x, modr, modr, g, w)


def _head_rms(x, g2):
    lo = _lane_lo()
    x2 = x * x
    s_lo = jnp.sum(jnp.where(lo, x2, 0.0), axis=-1, keepdims=True)
    s_hi = jnp.sum(jnp.where(lo, 0.0, x2), axis=-1, keepdims=True)
    ms = jnp.where(lo, s_lo, s_hi) * (1.0 / HEAD_DIM)
    return x * lax.rsqrt(ms + EPS) * g2


def _softmax_pv(q16, ks, vs, biases):
    ss = []
    for k, b in zip(ks, biases):
        s = _dot_nt(q16, k)
        if b is not None:
            s = s + b
        ss.append(s)
    m = ss[0].max(axis=-1, keepdims=True)
    for s in ss[1:]:
        m = jnp.maximum(m, s.max(axis=-1, keepdims=True))
    den = None
    acc = None
    for s, v in zip(ss, vs):
        e = jnp.exp(s - m)
        d = e.sum(axis=-1, keepdims=True)
        o = _dot(e.astype(BF16), v)
        den = d if den is None else den + d
        acc = o if acc is None else acc + o
    return acc / den


def _attend_pair(q, ks_lo, vs_lo, ks_hi, vs_hi, b_lo, b_hi):
    lo = _lane_lo()
    o_lo = _softmax_pv(jnp.where(lo, q, 0.0).astype(BF16), ks_lo, vs_lo, b_lo)
    o_hi = _softmax_pv(jnp.where(lo, 0.0, q).astype(BF16), ks_hi, vs_hi, b_hi)
    return jnp.where(lo, o_lo, o_hi)


def _gelu(x):
    c = np.sqrt(2.0 / np.pi).astype(np.float32)
    return x * (0.5 * (1.0 + jnp.tanh(c * (x + 0.044715 * (x * x * x)))))


def _layer_norm(x, g, b):
    mu = jnp.mean(x, axis=-1, keepdims=True)
    xc = x - mu
    var = jnp.mean(xc * xc, axis=-1, keepdims=True)
    return xc * lax.rsqrt(var + EPS) * g + b


def _chunk_mlp(cu, cv, lng_ref, lnb_ref, ws_ref, bsf_ref, out_ref, col0):
    lo = _lane_lo()
    u = _gelu(cu)
    v = _layer_norm(_gelu(cv), lng_ref[...], lnb_ref[...]).astype(BF16)
    s = cu.shape[0]
    for n in range(s // CHUNK):
        rows = slice(n * CHUNK, (n + 1) * CHUNK)
        for jb in range(2):
            cols = slice(jb * LANES, (jb + 1) * LANES)
            vb = v[rows, cols]
            m_lo = _dot(ws_ref[2 * jb].astype(BF16), vb)
            m_hi = _dot(ws_ref[2 * jb + 1].astype(BF16), vb)
            mixed = jnp.where(lo, m_lo, m_hi) + bsf_ref[:, cols]
            out_ref[rows, col0 + jb * LANES:col0 + (jb + 1) * LANES] = (u[rows, cols] * mixed).astype(out_ref.dtype)


def _ctx_mixer_kernel(z_ref, qg_ref, kg_ref, lng_ref, lnb_ref, ws_ref, bsf_ref, mix_ref, kbn_ref):
    scale = HEAD_DIM ** -0.5
    for j in range(2):
        q = z_ref[:, QA + LANES * j:QA + LANES * (j + 1)] * scale
        k = [z_ref[:, KA + LANES * j:KA + LANES * (j + 1)].astype(BF16)]
        v = [z_ref[:, VA + LANES * j:VA + LANES * (j + 1)].astype(BF16)]
        o = _attend_pair(q, k, v, k, v, [None], [None])
        mix_ref[:, OA + LANES * j:OA + LANES * (j + 1)] = o.astype(mix_ref.dtype)

    kb = _head_rms(z_ref[:, KB:KB + LANES], kg_ref[...])
    kbn_ref[...] = kb
    vb = z_ref[:, VB:VB + LANES]
    k_same = [kb.astype(BF16)]
    k_swap = [pltpu.roll(kb, HEAD_DIM, 1).astype(BF16)]
    v_same = [vb.astype(BF16)]
    v_swap = [pltpu.roll(vb, HEAD_DIM, 1).astype(BF16)]
    for j in range(4):
        q = _head_rms(z_ref[:, QB + LANES * j:QB + LANES * (j + 1)], qg_ref[...]) * scale
        if j // 2 == 0:
            o = _attend_pair(q, k_same, v_same, k_swap, v_swap, [None], [None])
        else:
            o = _attend_pair(q, k_swap, v_swap, k_same, v_same, [None], [None])
        mix_ref[:, OB + LANES * j:OB + LANES * (j + 1)] = o.astype(mix_ref.dtype)

    _chunk_mlp(z_ref[:, CU:CU + 256], z_ref[:, CV:CV + 256], lng_ref, lnb_ref, ws_ref, bsf_ref, mix_ref, OC)


def _ctx_mixer(z, n_batch, seq, qg2, kg2, lng, lnb, ws, bsf):
    small = lambda shape: pl.BlockSpec(shape, lambda b: (0,) * len(shape))
    return pl.pallas_call(
        _ctx_mixer_kernel,
        grid=(n_batch,),
        in_specs=[
            pl.BlockSpec((seq, IN_WIDTH), lambda b: (b, 0)),
            small((1, LANES)), small((1, LANES)), small((1, 256)), small((1, 256)),
            small((4, CHUNK, CHUNK)), small((CHUNK, 256)),
        ],
        out_specs=[
            pl.BlockSpec((seq, D), lambda b: (b, 0)),
            pl.BlockSpec((seq, LANES), lambda b: (b, 0)),
        ],
        out_shape=[
            jax.ShapeDtypeStruct((n_batch * seq, D), BF16),
            jax.ShapeDtypeStruct((n_batch * seq, LANES), F32),
        ],
        compiler_params=_cparams(("parallel",)),
        name="ctx_mixer",
    )(z, qg2, kg2, lng, lnb, ws, bsf)


def _rope(x, cos, sin):
    first = (lax.broadcasted_iota(jnp.int32, (1, LANES), 1) % 32) < 16
    partner = jnp.where(first, pltpu.roll(x, LANES - 16, 1), pltpu.roll(x, 16, 1))
    return x * cos + partner * sin


def _lat_mixer_kernel(zq_ref, zkv_ref, nakc_ref, navc_ref, gkc_ref, gvc_ref, bias_ref,
                      cosq_ref, sinq_ref, cosk_ref, sink_ref,
                      qg_ref, kg_ref, lng_ref, lnb_ref, ws_ref, bsf_ref,
                      mix_ref,
                      kl_ref, kls_ref, vl_ref, vls_ref, kc_ref, kcs_ref, vc_ref, vcs_ref):
    scale = HEAD_DIM ** -0.5

    @pl.when(pl.program_id(1) == 0)
    def _():
        kb = _rope(_head_rms(zkv_ref[:, KB:KB + LANES], kg_ref[...]), cosk_ref[...], sink_ref[...])
        vb = zkv_ref[:, VB:VB + LANES]
        kl_ref[...] = kb.astype(BF16)
        kls_ref[...] = pltpu.roll(kb, HEAD_DIM, 1).astype(BF16)
        vl_ref[...] = vb.astype(BF16)
        vls_ref[...] = pltpu.roll(vb, HEAD_DIM, 1).astype(BF16)
        kc = gkc_ref[0]
        vc = gvc_ref[0]
        kc_ref[...] = kc.astype(BF16)
        kcs_ref[...] = pltpu.roll(kc, HEAD_DIM, 1).astype(BF16)
        vc_ref[...] = vc.astype(BF16)
        vcs_ref[...] = pltpu.roll(vc, HEAD_DIM, 1).astype(BF16)

    for j in range(2):
        cols = slice(LANES * j, LANES * (j + 1))
        q = zq_ref[:, QA + LANES * j:QA + LANES * (j + 1)] * scale
        ks = [zkv_ref[:, KA + LANES * j:KA + LANES * (j + 1)].astype(BF16), nakc_ref[0, :, cols].astype(BF16)]
        vs = [zkv_ref[:, VA + LANES * j:VA + LANES * (j + 1)].astype(BF16), navc_ref[0, :, cols].astype(BF16)]
        o = _attend_pair(q, ks, vs, ks, vs, [bias_ref[2 * j], None], [bias_ref[2 * j + 1], None])
        mix_ref[:, OA + LANES * j:OA + LANES * (j + 1)] = o.astype(mix_ref.dtype)

    same = ([kl_ref[...], kc_ref[...]], [vl_ref[...], vc_ref[...]])
    swap = ([kls_ref[...], kcs_ref[...]], [vls_ref[...], vcs_ref[...]])
    for j in range(4):
        q = _head_rms(zq_ref[:, QB + LANES * j:QB + LANES * (j + 1)], qg_ref[...])
        q = _rope(q, cosq_ref[...], sinq_ref[...]) * scale
        lo_kv, hi_kv = (same, swap) if j // 2 == 0 else (swap, same)
        o = _attend_pair(q, lo_kv[0], lo_kv[1], hi_kv[0], hi_kv[1], [None, None], [None, None])
        mix_ref[:, OB + LANES * j:OB + LANES * (j + 1)] = o.astype(mix_ref.dtype)

    _chunk_mlp(zq_ref[:, CU:CU + 256], zq_ref[:, CV:CV + 256], lng_ref, lnb_ref, ws_ref, bsf_ref, mix_ref, OC)


def _lat_mixer(z, row0, n_batch, n_tok, past, nakc, navc, gkc, gvc, bias, cos, sin, qg2, kg2, lng, lnb, ws, bsf):
    tq = 256
    nq = n_tok // tq
    small = lambda shape: pl.BlockSpec(shape, lambda b, t: (0,) * len(shape))
    qblk0 = row0 // tq
    kvblk0 = row0 // n_tok
    return pl.pallas_call(
        _lat_mixer_kernel,
        grid=(n_batch, nq),
        in_specs=[
            pl.BlockSpec((tq, IN_WIDTH), lambda b, t: (qblk0 + b * nq + t, 0)),
            pl.BlockSpec((n_tok, IN_WIDTH), lambda b, t: (kvblk0 + b, 0)),
            pl.BlockSpec((1, past, 256), lambda b, t: (b, 0, 0)),
            pl.BlockSpec((1, past, 256), lambda b, t: (b, 0, 0)),
            pl.BlockSpec((1, past, LANES), lambda b, t: (b, 0, 0)),
            pl.BlockSpec((1, past, LANES), lambda b, t: (b, 0, 0)),
            pl.BlockSpec((4, tq, n_tok), lambda b, t: (0, t, 0)),
            pl.BlockSpec((tq, LANES), lambda b, t: (t, 0)),
            pl.BlockSpec((tq, LANES), lambda b, t: (t, 0)),
            small((n_tok, LANES)), small((n_tok, LANES)),
            small((1, LANES)), small((1, LANES)), small((1, 256)), small((1, 256)),
            small((4, CHUNK, CHUNK)), small((CHUNK, 256)),
        ],
        out_specs=pl.BlockSpec((tq, D), lambda b, t: (b * nq + t, 0)),
        out_shape=jax.ShapeDtypeStruct((n_batch * n_tok, D), BF16),
        scratch_shapes=[pltpu.VMEM((n_tok, LANES), BF16)] * 4 + [pltpu.VMEM((past, LANES), BF16)] * 4,
        compiler_params=_cparams(("parallel", "arbitrary")),
        name="lat_mixer",
    )(z, z, nakc, navc, gkc, gvc, bias, cos, sin, cos, sin, qg2, kg2, lng, lnb, ws, bsf)


def _outproj_kernel(mix_ref, x_ref, g1_ref, shift_ref, scale_ref, g_ref, w_ref, rw_ref, rb_ref,
                    xn_ref, hp_ref, lg_ref):
    x = x_ref[...] + g1_ref[0] * _dot(mix_ref[...], w_ref[...])
    xn_ref[...] = x
    ms = jnp.mean(x * x, axis=-1, keepdims=True)
    h = x * lax.rsqrt(ms + EPS) * g_ref[...]
    h = h * (1.0 + scale_ref[0]) + shift_ref[0]
    lg_ref[...] = jnp.dot(h, rw_ref[...], preferred_element_type=F32, precision=HI) + rb_ref[...]
    half = D // 2
    lo = pltpu.bitcast(h[:, :half].astype(BF16).astype(F32), jnp.uint32) >> 16
    hi = pltpu.bitcast(h[:, half:].astype(BF16).astype(F32), jnp.uint32) & jnp.uint32(0xFFFF0000)
    hp_ref[...] = hi | lo


def _outproj(mix, x, modr, g, w, rw, rb, tp, ts_per_batch):
    tm = 512
    t = x.shape[0]
    return pl.pallas_call(
        _outproj_kernel,
        grid=(t // tm,),
        in_specs=[
            pl.BlockSpec((tm, D), lambda i: (i, 0)),
            pl.BlockSpec((tm, D), lambda i: (i, 0)),
            _mod_spec(2, tm, tp, ts_per_batch),
            _mod_spec(3, tm, tp, ts_per_batch),
            _mod_spec(4, tm, tp, ts_per_batch),
            pl.BlockSpec((1, D), lambda i: (0, 0)),
            pl.BlockSpec((D, D), lambda i: (0, 0)),
            pl.BlockSpec((D, N_EXPERTS), lambda i: (0, 0)),
            pl.BlockSpec((1, N_EXPERTS), lambda i: (0, 0)),
        ],
        out_specs=[
            pl.BlockSpec((tm, D), lambda i: (i, 0)),
            pl.BlockSpec((tm, D // 2), lambda i: (i, 0)),
            pl.BlockSpec((tm, N_EXPERTS), lambda i: (i, 0)),
        ],
        out_shape=[
            jax.ShapeDtypeStruct((t, D), F32),
            jax.ShapeDtypeStruct((t, D // 2), jnp.uint32),
            jax.ShapeDtypeStruct((t, N_EXPERTS), F32),
        ],
        compiler_params=_cparams(("parallel",)),
        name="outproj",
    )(mix, x, modr, modr, modr, g, w, rw, rb)


def _prep_wgu_kernel(w_ref, p_ref, wg_ref, wu_ref):
    for j in range(w_ref.shape[2] // 256):
        w = w_ref[0, :, 256 * j:256 * (j + 1)].astype(BF16)
        sel = _dot(w, p_ref[...])
        wg_ref[0, :, LANES * j:LANES * (j + 1)] = sel[:, :LANES].astype(BF16)
        wu_ref[0, :, LANES * j:LANES * (j + 1)] = sel[:, LANES:].astype(BF16)


def _prep_wgu(w):
    e, d, f2 = w.shape
    perm = np.zeros((256, 256), np.float32)
    perm[2 * np.arange(LANES), np.arange(LANES)] = 1.0
    perm[2 * np.arange(LANES) + 1, LANES + np.arange(LANES)] = 1.0
    out = jax.ShapeDtypeStruct((e, d, f2 // 2), BF16)
    return pl.pallas_call(
        _prep_wgu_kernel,
        grid=(e,),
        in_specs=[pl.BlockSpec((1, d, f2), lambda i: (i, 0, 0)), pl.BlockSpec((256, 256), lambda i: (0, 0))],
        out_specs=[pl.BlockSpec((1, d, f2 // 2), lambda i: (i, 0, 0))] * 2,
        out_shape=[out, out],
        compiler_params=_cparams(("parallel",)),
        name="prep_wgu",
    )(w, jnp.asarray(perm, BF16))


def _moe_kernel(blk_e_ref, row_tok_ref, row_dst_ref, hp_ref, wg_ref, wu_ref, wd_ref, bg_ref, bu_ref, bd_ref,
                out_ref, xg_ref, ybuf_ref, sem):
    i = pl.program_id(0)
    n = pl.num_programs(0)
    slot = i % 2
    base = i * MOE_BLOCK

    def slot_copy(s):
        return pltpu.make_async_copy(ybuf_ref.at[s], out_ref.at[pl.ds(0, MOE_BLOCK)], sem.at[s])

    @pl.when(i >= 2)
    def _():
        slot_copy(slot).wait()

    def gather(r, carry):
        tok = row_tok_ref[base + r]
        xg_ref[pl.ds(r, 1), :] = hp_ref[pl.ds(tok, 1), :]
        return carry

    lax.fori_loop(0, MOE_BLOCK, gather, 0, unroll=8)
    xp = xg_ref[...]
    half = D // 2
    x_lo = pltpu.bitcast(xp << 16, F32).astype(BF16)
    x_hi = pltpu.bitcast(xp & jnp.uint32(0xFFFF0000), F32).astype(BF16)
    g = _dot(x_lo, wg_ref[0, :half, :]) + _dot(x_hi, wg_ref[0, half:, :]) + bg_ref[0]
    u = _dot(x_lo, wu_ref[0, :half, :]) + _dot(x_hi, wu_ref[0, half:, :]) + bu_ref[0]
    g = jnp.minimum(g, SWIGLU_LIMIT)
    u = jnp.clip(u, -SWIGLU_LIMIT, SWIGLU_LIMIT)
    act = (u + 1.0) * (g * jax.nn.sigmoid(SWIGLU_ALPHA * g))
    y = _dot(act.astype(BF16), wd_ref[0]) + bd_ref[0]
    ybuf_ref[slot] = y

    def scatter(r, carry):
        dst = row_dst_ref[base + r]
        pltpu.make_async_copy(ybuf_ref.at[slot, pl.ds(r, 1)], out_ref.at[pl.ds(dst, 1)], sem.at[slot]).start()
        return carry

    lax.fori_loop(0, MOE_BLOCK, scatter, 0, unroll=8)

    @pl.when(i == n - 1)
    def _():
        slot_copy(1 - slot).wait()
        slot_copy(slot).wait()


def _moe(blk_e, row_tok, row_dst, hp, wg, wu, wd, bg, bu, bd):
    n_blocks = blk_e.shape[0]
    assert n_blocks >= 2
    t = hp.shape[0]
    ew = lambda i, be, rt, rd: (be[i], 0, 0)
    return pl.pallas_call(
        _moe_kernel,
        grid_spec=pltpu.PrefetchScalarGridSpec(
            num_scalar_prefetch=3,
            grid=(n_blocks,),
            in_specs=[
                pl.BlockSpec((t, D // 2), lambda i, be, rt, rd: (0, 0), pipeline_mode=pl.Buffered(1)),
                pl.BlockSpec((1, D, D), ew),
                pl.BlockSpec((1, D, D), ew),
                pl.BlockSpec((1, D, D), ew),
                pl.BlockSpec((1, 1, D), ew),
                pl.BlockSpec((1, 1, D), ew),
                pl.BlockSpec((1, 1, D), ew),
            ],
            out_specs=pl.BlockSpec(memory_space=pl.ANY),
            scratch_shapes=[
                pltpu.VMEM((MOE_BLOCK, D // 2), jnp.uint32),
                pltpu.VMEM((2, MOE_BLOCK, D), F32),
                pltpu.SemaphoreType.DMA((2,)),
            ],
        ),
        out_shape=jax.ShapeDtypeStruct((n_blocks * MOE_BLOCK, D), F32),
        compiler_params=_cparams(("arbitrary",)),
        name="moe_experts",
    )(blk_e, row_tok, row_dst, hp, wg, wu, wd, bg, bu, bd)


def _route(logits):
    t = logits.shape[0]
    n_assign = t * TOP_K
    assert n_assign <= 1 << 16 and n_assign % MOE_BLOCK == 0
    top_v, top_i = lax.top_k(logits, TOP_K)
    gates = jax.nn.softmax(top_v, axis=-1)
    flat_e = top_i.reshape(-1).astype(jnp.int32)
    experts = jnp.arange(N_EXPERTS, dtype=jnp.int32)
    counts = jnp.sum((flat_e[:, None] == experts[None, :]).astype(jnp.int32), axis=0)
    padded = (counts + MOE_BLOCK - 1) // MOE_BLOCK * MOE_BLOCK
    pad_end = jnp.cumsum(padded)
    n_blocks = n_assign // MOE_BLOCK + N_EXPERTS
    real_keys = (flat_e << 17) | jnp.arange(n_assign, dtype=jnp.int32)
    pad_id = experts[:, None] * MOE_BLOCK + jnp.arange(MOE_BLOCK, dtype=jnp.int32)[None, :]
    used = jnp.arange(MOE_BLOCK, dtype=jnp.int32)[None, :] < (padded - counts)[:, None]
    pad_keys = (jnp.where(used, experts[:, None], 63) << 17) | (1 << 16) | pad_id
    keys = jnp.sort(jnp.concatenate([real_keys, pad_keys.reshape(-1)]))
    is_pad = ((keys >> 16) & 1) == 1
    payload = keys & 0xFFFF
    row_tok = jnp.where(is_pad, 0, payload // TOP_K)
    row_dst = jnp.where(is_pad, n_assign + payload, (payload % TOP_K) * t + payload // TOP_K)
    blk_e = jnp.minimum(jnp.searchsorted(pad_end, jnp.arange(n_blocks) * MOE_BLOCK, side='right'),
                        N_EXPERTS - 1).astype(jnp.int32)
    return blk_e, row_tok, row_dst, gates


def _combine_kernel(y0_ref, y1_ref, y2_ref, y3_ref, gates_ref, x_ref, g2_ref, o_ref):
    g = gates_ref[...]
    y = ((y0_ref[...] * g[:, 0:1] + y1_ref[...] * g[:, 1:2])
         + (y2_ref[...] * g[:, 2:3] + y3_ref[...] * g[:, 3:4]))
    o_ref[...] = x_ref[...] + g2_ref[0] * y


def _combine(ys, gates, x, modr, tp, ts_per_batch):
    tm = 512
    t = x.shape[0]
    nt = t // tm
    yspec = lambda k: pl.BlockSpec((tm, D), lambda i: (k * nt + i, 0))
    return pl.pallas_call(
        _combine_kernel,
        grid=(nt,),
        in_specs=[yspec(0), yspec(1), yspec(2), yspec(3), pl.BlockSpec((tm, TOP_K), lambda i: (i, 0)),
                  pl.BlockSpec((tm, D), lambda i: (i, 0)), _mod_spec(5, tm, tp, ts_per_batch)],
        out_specs=pl.BlockSpec((tm, D), lambda i: (i, 0)),
        out_shape=jax.ShapeDtypeStruct((t, D), F32),
        compiler_params=_cparams(("parallel",)),
        name="moe_combine",
    )(ys, ys, ys, ys, gates, x, modr)


def _final_norm_kernel(x_ref, g_ref, o_ref):
    x = x_ref[...]
    ms = jnp.mean(x * x, axis=-1, keepdims=True)
    o_ref[...] = x * lax.rsqrt(ms + EPS) * g_ref[...]


def _final_norm(x, g):
    tm = 1024
    t = x.shape[0]
    return pl.pallas_call(
        _final_norm_kernel,
        grid=(t // tm,),
        in_specs=[pl.BlockSpec((tm, D), lambda i: (i, 0)), pl.BlockSpec((1, D), lambda i: (0, 0))],
        out_specs=pl.BlockSpec((tm, D), lambda i: (i, 0)),
        out_shape=jax.ShapeDtypeStruct((t, D), F32),
        compiler_params=_cparams(("parallel",)),
        name="final_norm",
    )(x, g)


def _na_bias(rpb, rows):
    kr = min(NA_WIN_R, rows)
    r = np.arange(rows)
    r0 = np.clip(r - kr // 2, 0, rows - kr)
    row_ok = (r[None, :] >= r0[:, None]) & (r[None, :] < r0[:, None] + kr)
    dr = np.clip(r[None, :] - r[:, None] + (NA_WIN_R - 1), 0, 2 * NA_WIN_R - 2)
    c = np.arange(GRID_W)
    w0 = np.clip(c - NA_WIN_C // 2, 0, GRID_W - NA_WIN_C)
    col_ok = (c[None, :] >= w0[:, None]) & (c[None, :] < w0[:, None] + NA_WIN_C)
    dc = np.clip(c[None, :] - c[:, None] + (NA_WIN_C - 1), 0, 2 * NA_WIN_C - 2)
    oh_r = np.eye(2 * NA_WIN_R - 1, dtype=np.float32)[dr]
    oh_c = np.eye(2 * NA_WIN_C - 1, dtype=np.float32)[dc]
    by_row = jnp.einsum('rkd,hde->hrke', oh_r, rpb.astype(F32), precision=HI)
    vals = jnp.einsum('hrke,qce->hrqkc', by_row, oh_c, precision=HI)
    ok = row_ok[:, None, :, None] & col_ok[None, :, None, :]
    n = rows * GRID_W
    return jnp.where(ok[None], vals, NEG_INF).reshape(rpb.shape[0], n, n)


def _rope_tables(n_tokens):
    t = np.arange(n_tokens)
    row = (t // GRID_W).astype(np.float32)
    col = (t % GRID_W).astype(np.float32)
    half = HEAD_DIM // 2
    inv = jnp.asarray(ROPE_THETA, F32) ** (-jnp.arange(0, half, 2, dtype=F32) / half)
    ang_r = jnp.asarray(row)[:, None] * inv
    ang_c = jnp.asarray(col)[:, None] * inv
    cr, sr, cc, sc = jnp.cos(ang_r), jnp.sin(ang_r), jnp.cos(ang_c), jnp.sin(ang_c)
    cos = jnp.concatenate([cr, cr, cc, cc] * 2, axis=-1)
    sin = jnp.concatenate([-sr, sr, -sc, sc] * 2, axis=-1)
    return cos, sin


def _to_heads(x, n_batch, seq, n_heads):
    return x.reshape(n_batch, seq, n_heads, HEAD_DIM).transpose(0, 2, 1, 3)


def _ctx_lanes(cache_l):
    b, h, p, dh = cache_l.shape
    return cache_l.transpose(0, 2, 1, 3).reshape(b, p, h * dh)


def kernel(x_prompt, x_sample, cache_na_k, cache_na_v, cache_gqa_k, cache_gqa_v, c, c_ctx, w_mod, b_mod, norm1_g, norm2_g, w_in, na_rpb, q_norm_g, k_norm_g, cm_ln_g, cm_ln_b, cm_ws, cm_bs, w_out, router_w, router_b, w_gate_up, b_gate_up, w_down, b_down, final_norm_g):
    nb, seq, _ = x_prompt.shape
    db, n_tok, _ = x_sample.shape
    past = cache_na_k.shape[3]
    tp, ts = nb * seq, db * n_tok
    assert db <= CTX_ROW and n_tok % GRID_W == 0

    cond = jnp.zeros((MOD_ROWS, D), F32).at[:db].set(c).at[CTX_ROW].set(c_ctx)
    mod = _modulation(cond, w_mod, b_mod)
    cos, sin = _rope_tables(n_tok)
    x = jnp.concatenate([x_prompt.reshape(tp, D), x_sample.reshape(ts, D)], axis=0)

    w_in16 = w_in.astype(BF16)
    w_out16 = w_out.astype(BF16)
    wg16, wu16 = _prep_wgu(w_gate_up.reshape(DEPTH * N_EXPERTS, D, 2 * D))
    wg16 = wg16.reshape(DEPTH, N_EXPERTS, D, D)
    wu16 = wu16.reshape(DEPTH, N_EXPERTS, D, D)
    wd16 = w_down.astype(BF16)
    bg = b_gate_up[..., 0::2].reshape(DEPTH, N_EXPERTS, 1, D)
    bu = b_gate_up[..., 1::2].reshape(DEPTH, N_EXPERTS, 1, D)
    bd = b_down.reshape(DEPTH, N_EXPERTS, 1, D)

    na_k, na_v, gqa_k, gqa_v = [], [], [], []
    for l in range(DEPTH):
        modr = mod[l].reshape(MOD_ROWS * 6, 1, D)
        qg2 = jnp.tile(q_norm_g[l], 2).reshape(1, LANES)
        kg2 = jnp.tile(k_norm_g[l], 2).reshape(1, LANES)
        lng = cm_ln_g[l].reshape(1, 256)
        lnb = cm_ln_b[l].reshape(1, 256)
        bsf = jnp.repeat(cm_bs[l].T, HEAD_DIM, axis=1)

        z = _inproj(x, modr, norm1_g[l].reshape(1, D), w_in16[l], tp, n_tok)
        mix_p, kbn = _ctx_mixer(z, nb, seq, qg2, kg2, lng, lnb, cm_ws[l], bsf)
        mix_s = _lat_mixer(z, tp, db, n_tok, past,
                           _ctx_lanes(cache_na_k[:, l]), _ctx_lanes(cache_na_v[:, l]),
                           _ctx_lanes(cache_gqa_k[:, l]), _ctx_lanes(cache_gqa_v[:, l]),
                           _na_bias(na_rpb[l], n_tok // GRID_W), cos, sin, qg2, kg2, lng, lnb, cm_ws[l], bsf)
        na_k.append(_to_heads(z[:tp, KA:KA + 256], nb, seq, 4))
        na_v.append(_to_heads(z[:tp, VA:VA + 256], nb, seq, 4))
        gqa_k.append(_to_heads(kbn, nb, seq, 2))
        gqa_v.append(_to_heads(z[:tp, VB:VB + LANES], nb, seq, 2))

        mix = jnp.concatenate([mix_p, mix_s], axis=0)
        xn, hp, logits = _outproj(mix, x, modr, norm2_g[l].reshape(1, D), w_out16[l],
                                  router_w[l], router_b[l].reshape(1, N_EXPERTS), tp, n_tok)

        blk_e, row_tok, row_dst, gates = _route(logits)
        ys = _moe(blk_e, row_tok, row_dst, hp, wg16[l], wu16[l], wd16[l], bg[l], bu[l], bd[l])
        x = _combine(ys, gates, xn, modr, tp, n_tok)

    y = _final_norm(x, final_norm_g.reshape(1, D))
    return (y[:tp].reshape(nb, seq, D), y[tp:].reshape(db, n_tok, D),
            jnp.stack(na_k, axis=1), jnp.stack(na_v, axis=1), jnp.stack(gqa_k, axis=1), jnp.stack(gqa_v, axis=1))
```

```python
import functools

import jax
import jax.numpy as jnp
import numpy as np
from jax import lax
from jax.experimental import pallas as pl
from jax.experimental.pallas import tpu as pltpu

D = 1024
DEPTH = 2
GRID_W = 64
HEAD_DIM = 64
NA_WIN_R = 8
NA_WIN_C = 16
CHUNK = 128
ROPE_THETA = 10000.0
N_EXPERTS = 32
TOP_K = 4
SWIGLU_LIMIT = 7.0
SWIGLU_ALPHA = 1.702
EPS = 1e-6
MOE_BLOCK = 128
NEG_INF = -1e30
IN_WIDTH = 2048

QA, KA, VA, QB, KB, VB, CU, CV = 0, 256, 512, 768, 1280, 1408, 1536, 1792
OA, OB, OC = 0, 256, 768

LANES = 128
MOD_ROWS = 16
CTX_ROW = 8
VMEM_LIMIT = 56 * 1024 * 1024

F32 = jnp.float32
BF16 = jnp.bfloat16
HI = lax.Precision.HIGHEST


def _cparams(sem):
    return pltpu.CompilerParams(dimension_semantics=sem, vmem_limit_bytes=VMEM_LIMIT)


def _dot(a, b):
    return jnp.dot(a, b, preferred_element_type=F32)


def _dot_nt(a, b):
    return lax.dot_general(a, b, (((1,), (1,)), ((), ())), preferred_element_type=F32)


def _lane_lo():
    return lax.broadcasted_iota(jnp.int32, (1, LANES), 1) < HEAD_DIM


def _mod_kernel(c_ref, w_ref, b_ref, o_ref):
    c = c_ref[...]
    s = c * jax.nn.sigmoid(c)
    o_ref[0] = jnp.dot(s, w_ref[0], preferred_element_type=F32, precision=HI) + b_ref[0]


def _modulation(cond, w_mod, b_mod):
    tn = 1536
    n = w_mod.shape[-1]
    return pl.pallas_call(
        _mod_kernel,
        grid=(DEPTH, n // tn),
        in_specs=[
            pl.BlockSpec((MOD_ROWS, D), lambda l, j: (0, 0)),
            pl.BlockSpec((1, D, tn), lambda l, j: (l, 0, j)),
            pl.BlockSpec((1, 1, tn), lambda l, j: (l, 0, j)),
        ],
        out_specs=pl.BlockSpec((1, MOD_ROWS, tn), lambda l, j: (l, 0, j)),
        out_shape=jax.ShapeDtypeStruct((DEPTH, MOD_ROWS, n), F32),
        compiler_params=_cparams(("arbitrary", "arbitrary")),
        name="modulation",
    )(cond, w_mod, b_mod.reshape(DEPTH, 1, n))


def _mod_row(i, tm, tp, ts_per_batch):
    start = i * tm
    return jnp.where(start < tp, CTX_ROW, (start - tp) // ts_per_batch)


def _mod_spec(k, tm, tp, ts_per_batch):
    return pl.BlockSpec((1, 1, D), lambda i: (_mod_row(i, tm, tp, ts_per_batch) * 6 + k, 0, 0))


def _inproj_kernel(x_ref, shift_ref, scale_ref, g_ref, w_ref, z_ref):
    x = x_ref[...]
    ms = jnp.mean(x * x, axis=-1, keepdims=True)
    y = x * lax.rsqrt(ms + EPS) * g_ref[...]
    h = y * (1.0 + scale_ref[0]) + shift_ref[0]
    z_ref[...] = _dot(h.astype(BF16), w_ref[...])


def _inproj(x, modr, g, w, tp, ts_per_batch):
    tm = 512
    t = x.shape[0]
    return pl.pallas_call(
        _inproj_kernel,
        grid=(t // tm,),
        in_specs=[
            pl.BlockSpec((tm, D), lambda i: (i, 0)),
            _mod_spec(0, tm, tp, ts_per_batch),
            _mod_spec(1, tm, tp, ts_per_batch),
            pl.BlockSpec((1, D), lambda i: (0, 0)),
            pl.BlockSpec((D, IN_WIDTH), lambda i: (0, 0)),
        ],
        out_specs=pl.BlockSpec((tm, IN_WIDTH), lambda i: (i, 0)),
        out_shape=jax.ShapeDtypeStruct((t, IN_WIDTH), F32),
        compiler_params=_cparams(("parallel",)),
        name="inproj",
    )(x, modr, modr, g, w)


def _head_rms(x, g2):
    lo = _lane_lo()
    x2 = x * x
    s_lo = jnp.sum(jnp.where(lo, x2, 0.0), axis=-1, keepdims=True)
    s_hi = jnp.sum(jnp.where(lo, 0.0, x2), axis=-1, keepdims=True)
    ms = jnp.where(lo, s_lo, s_hi) * (1.0 / HEAD_DIM)
    return x * lax.rsqrt(ms + EPS) * g2


def _softmax_pv(q16, ks, vs, biases):
    ss = []
    for k, b in zip(ks, biases):
        s = _dot_nt(q16, k)
        if b is not None:
            s = s + b
        ss.append(s)
    m = ss[0].max(axis=-1, keepdims=True)
    for s in ss[1:]:
        m = jnp.maximum(m, s.max(axis=-1, keepdims=True))
    den = None
    acc = None
    for s, v in zip(ss, vs):
        e = jnp.exp(s - m)
        d = e.sum(axis=-1, keepdims=True)
        o = _dot(e.astype(BF16), v)
        den = d if den is None else den + d
        acc = o if acc is None else acc + o
    return acc / den


def _attend_pair(q, ks_lo, vs_lo, ks_hi, vs_hi, b_lo, b_hi):
    lo = _lane_lo()
    o_lo = _softmax_pv(jnp.where(lo, q, 0.0).astype(BF16), ks_lo, vs_lo, b_lo)
    o_hi = _softmax_pv(jnp.where(lo, 0.0, q).astype(BF16), ks_hi, vs_hi, b_hi)
    return jnp.where(lo, o_lo, o_hi)


def _gelu(x):
    c = np.sqrt(2.0 / np.pi).astype(np.float32)
    return x * (0.5 * (1.0 + jnp.tanh(c * (x + 0.044715 * (x * x * x)))))


def _layer_norm(x, g, b):
    mu = jnp.mean(x, axis=-1, keepdims=True)
    xc = x - mu
    var = jnp.mean(xc * xc, axis=-1, keepdims=True)
    return xc * lax.rsqrt(var + EPS) * g + b


def _chunk_mlp(cu, cv, lng_ref, lnb_ref, ws_ref, bsf_ref, out_ref, col0):
    lo = _lane_lo()
    u = _gelu(cu)
    v = _layer_norm(_gelu(cv), lng_ref[...], lnb_ref[...]).astype(BF16)
    s = cu.shape[0]
    for n in range(s // CHUNK):
        rows = slice(n * CHUNK, (n + 1) * CHUNK)
        for jb in range(2):
            cols = slice(jb * LANES, (jb + 1) * LANES)
            vb = v[rows, cols]
            m_lo = _dot(ws_ref[2 * jb].astype(BF16), vb)
            m_hi = _dot(ws_ref[2 * jb + 1].astype(BF16), vb)
            mixed = jnp.where(lo, m_lo, m_hi) + bsf_ref[:, cols]
            out_ref[rows, col0 + jb * LANES:col0 + (jb + 1) * LANES] = (u[rows, cols] * mixed).astype(out_ref.dtype)


def _ctx_mixer_kernel(z_ref, qg_ref, kg_ref, lng_ref, lnb_ref, ws_ref, bsf_ref, mix_ref, kbn_ref):
    scale = HEAD_DIM ** -0.5
    for j in range(2):
        q = z_ref[:, QA + LANES * j:QA + LANES * (j + 1)] * scale
        k = [z_ref[:, KA + LANES * j:KA + LANES * (j + 1)].astype(BF16)]
        v = [z_ref[:, VA + LANES * j:VA + LANES * (j + 1)].astype(BF16)]
        o = _attend_pair(q, k, v, k, v, [None], [None])
        mix_ref[:, OA + LANES * j:OA + LANES * (j + 1)] = o.astype(mix_ref.dtype)

    kb = _head_rms(z_ref[:, KB:KB + LANES], kg_ref[...])
    kbn_ref[...] = kb
    vb = z_ref[:, VB:VB + LANES]
    k_same = [kb.astype(BF16)]
    k_swap = [pltpu.roll(kb, HEAD_DIM, 1).astype(BF16)]
    v_same = [vb.astype(BF16)]
    v_swap = [pltpu.roll(vb, HEAD_DIM, 1).astype(BF16)]
    for j in range(4):
        q = _head_rms(z_ref[:, QB + LANES * j:QB + LANES * (j + 1)], qg_ref[...]) * scale
        if j // 2 == 0:
            o = _attend_pair(q, k_same, v_same, k_swap, v_swap, [None], [None])
        else:
            o = _attend_pair(q, k_swap, v_swap, k_same, v_same, [None], [None])
        mix_ref[:, OB + LANES * j:OB + LANES * (j + 1)] = o.astype(mix_ref.dtype)

    _chunk_mlp(z_ref[:, CU:CU + 256], z_ref[:, CV:CV + 256], lng_ref, lnb_ref, ws_ref, bsf_ref, mix_ref, OC)


def _ctx_mixer(z, n_batch, seq, qg2, kg2, lng, lnb, ws, bsf):
    small = lambda shape: pl.BlockSpec(shape, lambda b: (0,) * len(shape))
    return pl.pallas_call(
        _ctx_mixer_kernel,
        grid=(n_batch,),
        in_specs=[
            pl.BlockSpec((seq, IN_WIDTH), lambda b: (b, 0)),
            small((1, LANES)), small((1, LANES)), small((1, 256)), small((1, 256)),
            small((4, CHUNK, CHUNK)), small((CHUNK, 256)),
        ],
        out_specs=[
            pl.BlockSpec((seq, D), lambda b: (b, 0)),
            pl.BlockSpec((seq, LANES), lambda b: (b, 0)),
        ],
        out_shape=[
            jax.ShapeDtypeStruct((n_batch * seq, D), BF16),
            jax.ShapeDtypeStruct((n_batch * seq, LANES), F32),
        ],
        compiler_params=_cparams(("parallel",)),
        name="ctx_mixer",
    )(z, qg2, kg2, lng, lnb, ws, bsf)


def _rope(x, cos, sin):
    first = (lax.broadcasted_iota(jnp.int32, (1, LANES), 1) % 32) < 16
    partner = jnp.where(first, pltpu.roll(x, LANES - 16, 1), pltpu.roll(x, 16, 1))
    return x * cos + partner * sin


def _lat_mixer_kernel(zq_ref, zkv_ref, nakc_ref, navc_ref, gkc_ref, gvc_ref, bias_ref,
                      cosq_ref, sinq_ref, cosk_ref, sink_ref,
                      qg_ref, kg_ref, lng_ref, lnb_ref, ws_ref, bsf_ref,
                      mix_ref,
                      kl_ref, kls_ref, vl_ref, vls_ref, kc_ref, kcs_ref, vc_ref, vcs_ref):
    scale = HEAD_DIM ** -0.5

    @pl.when(pl.program_id(1) == 0)
    def _():
        kb = _rope(_head_rms(zkv_ref[:, KB:KB + LANES], kg_ref[...]), cosk_ref[...], sink_ref[...])
        vb = zkv_ref[:, VB:VB + LANES]
        kl_ref[...] = kb.astype(BF16)
        kls_ref[...] = pltpu.roll(kb, HEAD_DIM, 1).astype(BF16)
        vl_ref[...] = vb.astype(BF16)
        vls_ref[...] = pltpu.roll(vb, HEAD_DIM, 1).astype(BF16)
        kc = gkc_ref[0]
        vc = gvc_ref[0]
        kc_ref[...] = kc.astype(BF16)
        kcs_ref[...] = pltpu.roll(kc, HEAD_DIM, 1).astype(BF16)
        vc_ref[...] = vc.astype(BF16)
        vcs_ref[...] = pltpu.roll(vc, HEAD_DIM, 1).astype(BF16)

    for j in range(2):
        cols = slice(LANES * j, LANES * (j + 1))
        q = zq_ref[:, QA + LANES * j:QA + LANES * (j + 1)] * scale
        ks = [zkv_ref[:, KA + LANES * j:KA + LANES * (j + 1)].astype(BF16), nakc_ref[0, :, cols].astype(BF16)]
        vs = [zkv_ref[:, VA + LANES * j:VA + LANES * (j + 1)].astype(BF16), navc_ref[0, :, cols].astype(BF16)]
        o = _attend_pair(q, ks, vs, ks, vs, [bias_ref[2 * j], None], [bias_ref[2 * j + 1], None])
        mix_ref[:, OA + LANES * j:OA + LANES * (j + 1)] = o.astype(mix_ref.dtype)

    same = ([kl_ref[...], kc_ref[...]], [vl_ref[...], vc_ref[...]])
    swap = ([kls_ref[...], kcs_ref[...]], [vls_ref[...], vcs_ref[...]])
    for j in range(4):
        q = _head_rms(zq_ref[:, QB + LANES * j:QB + LANES * (j + 1)], qg_ref[...])
        q = _rope(q, cosq_ref[...], sinq_ref[...]) * scale
        lo_kv, hi_kv = (same, swap) if j // 2 == 0 else (swap, same)
        o = _attend_pair(q, lo_kv[0], lo_kv[1], hi_kv[0], hi_kv[1], [None, None], [None, None])
        mix_ref[:, OB + LANES * j:OB + LANES * (j + 1)] = o.astype(mix_ref.dtype)

    _chunk_mlp(zq_ref[:, CU:CU + 256], zq_ref[:, CV:CV + 256], lng_ref, lnb_ref, ws_ref, bsf_ref, mix_ref, OC)


def _lat_mixer(z, row0, n_batch, n_tok, past, nakc, navc, gkc, gvc, bias, cos, sin, qg2, kg2, lng, lnb, ws, bsf):
    tq = 256
    nq = n_tok // tq
    small = lambda shape: pl.BlockSpec(shape, lambda b, t: (0,) * len(shape))
    qblk0 = row0 // tq
    kvblk0 = row0 // n_tok
    return pl.pallas_call(
        _lat_mixer_kernel,
        grid=(n_batch, nq),
        in_specs=[
            pl.BlockSpec((tq, IN_WIDTH), lambda b, t: (qblk0 + b * nq + t, 0)),
            pl.BlockSpec((n_tok, IN_WIDTH), lambda b, t: (kvblk0 + b, 0)),
            pl.BlockSpec((1, past, 256), lambda b, t: (b, 0, 0)),
            pl.BlockSpec((1, past, 256), lambda b, t: (b, 0, 0)),
            pl.BlockSpec((1, past, LANES), lambda b, t: (b, 0, 0)),
            pl.BlockSpec((1, past, LANES), lambda b, t: (b, 0, 0)),
            pl.BlockSpec((4, tq, n_tok), lambda b, t: (0, t, 0)),
            pl.BlockSpec((tq, LANES), lambda b, t: (t, 0)),
            pl.BlockSpec((tq, LANES), lambda b, t: (t, 0)),
            small((n_tok, LANES)), small((n_tok, LANES)),
            small((1, LANES)), small((1, LANES)), small((1, 256)), small((1, 256)),
            small((4, CHUNK, CHUNK)), small((CHUNK, 256)),
        ],
        out_specs=pl.BlockSpec((tq, D), lambda b, t: (b * nq + t, 0)),
        out_shape=jax.ShapeDtypeStruct((n_batch * n_tok, D), BF16),
        scratch_shapes=[pltpu.VMEM((n_tok, LANES), BF16)] * 4 + [pltpu.VMEM((past, LANES), BF16)] * 4,
        compiler_params=_cparams(("parallel", "arbitrary")),
        name="lat_mixer",
    )(z, z, nakc, navc, gkc, gvc, bias, cos, sin, cos, sin, qg2, kg2, lng, lnb, ws, bsf)


def _outproj_kernel(mix_ref, x_ref, g1_ref, shift_ref, scale_ref, g_ref, w_ref, rw_ref, rb_ref,
                    xn_ref, hp_ref, ti_ref, gt_ref):
    x = x_ref[...] + g1_ref[0] * _dot(mix_ref[...], w_ref[...])
    xn_ref[...] = x
    ms = jnp.mean(x * x, axis=-1, keepdims=True)
    h = x * lax.rsqrt(ms + EPS) * g_ref[...]
    h = h * (1.0 + scale_ref[0]) + shift_ref[0]
    logits = jnp.dot(h, rw_ref[...], preferred_element_type=F32, precision=HI) + rb_ref[...]

    lane = lax.broadcasted_iota(jnp.int32, logits.shape, 1)
    top_i = jnp.zeros_like(lane)
    top_v = []
    for k in range(TOP_K):
        m = jnp.max(logits, axis=-1, keepdims=True)
        idx = jnp.min(jnp.where(logits == m, lane, N_EXPERTS), axis=-1, keepdims=True)
        logits = jnp.where(lane == idx, -jnp.inf, logits)
        top_i = jnp.where(lane == k, idx, top_i)
        top_v.append(m)
    es = [jnp.exp(v - top_v[0]) for v in top_v]
    den = (es[0] + es[1]) + (es[2] + es[3])
    gates = jnp.zeros(logits.shape, F32)
    for k in range(TOP_K):
        gates = jnp.where(lane == k, es[k] / den, gates)
    ti_ref[...] = top_i
    gt_ref[...] = gates

    half = D // 2
    lo = pltpu.bitcast(h[:, :half].astype(BF16).astype(F32), jnp.uint32) >> 16
    hi = pltpu.bitcast(h[:, half:].astype(BF16).astype(F32), jnp.uint32) & jnp.uint32(0xFFFF0000)
    hp_ref[...] = hi | lo


def _outproj(mix, x, modr, g, w, rw, rb, tp, ts_per_batch):
    tm = 512
    t = x.shape[0]
    return pl.pallas_call(
        _outproj_kernel,
        grid=(t // tm,),
        in_specs=[
            pl.BlockSpec((tm, D), lambda i: (i, 0)),
            pl.BlockSpec((tm, D), lambda i: (i, 0)),
            _mod_spec(2, tm, tp, ts_per_batch),
            _mod_spec(3, tm, tp, ts_per_batch),
            _mod_spec(4, tm, tp, ts_per_batch),
            pl.BlockSpec((1, D), lambda i: (0, 0)),
            pl.BlockSpec((D, D), lambda i: (0, 0)),
            pl.BlockSpec((D, N_EXPERTS), lambda i: (0, 0)),
            pl.BlockSpec((1, N_EXPERTS), lambda i: (0, 0)),
        ],
        out_specs=[
            pl.BlockSpec((tm, D), lambda i: (i, 0)),
            pl.BlockSpec((tm, D // 2), lambda i: (i, 0)),
            pl.BlockSpec((tm, N_EXPERTS), lambda i: (i, 0)),
            pl.BlockSpec((tm, N_EXPERTS), lambda i: (i, 0)),
        ],
        out_shape=[
            jax.ShapeDtypeStruct((t, D), F32),
            jax.ShapeDtypeStruct((t, D // 2), jnp.uint32),
            jax.ShapeDtypeStruct((t, N_EXPERTS), jnp.int32),
            jax.ShapeDtypeStruct((t, N_EXPERTS), F32),
        ],
        compiler_params=_cparams(("parallel",)),
        name="outproj",
    )(mix, x, modr, modr, modr, g, w, rw, rb)


def _prep_wgu_kernel(w_ref, p_ref, wg_ref, wu_ref):
    for j in range(w_ref.shape[2] // 256):
        w = w_ref[0, :, 256 * j:256 * (j + 1)].astype(BF16)
        sel = _dot(w, p_ref[...])
        wg_ref[0, :, LANES * j:LANES * (j + 1)] = sel[:, :LANES].astype(BF16)
        wu_ref[0, :, LANES * j:LANES * (j + 1)] = sel[:, LANES:].astype(BF16)


def _prep_wgu(w):
    e, d, f2 = w.shape
    perm = np.zeros((256, 256), np.float32)
    perm[2 * np.arange(LANES), np.arange(LANES)] = 1.0
    perm[2 * np.arange(LANES) + 1, LANES + np.arange(LANES)] = 1.0
    out = jax.ShapeDtypeStruct((e, d, f2 // 2), BF16)
    return pl.pallas_call(
        _prep_wgu_kernel,
        grid=(e,),
        in_specs=[pl.BlockSpec((1, d, f2), lambda i: (i, 0, 0)), pl.BlockSpec((256, 256), lambda i: (0, 0))],
        out_specs=[pl.BlockSpec((1, d, f2 // 2), lambda i: (i, 0, 0))] * 2,
        out_shape=[out, out],
        compiler_params=_cparams(("parallel",)),
        name="prep_wgu",
    )(w, jnp.asarray(perm, BF16))


def _moe_kernel(blk_e_ref, row_tok_ref, row_dst_ref, hp_ref, wg_ref, wu_ref, wd_ref, bg_ref, bu_ref, bd_ref,
                out_ref, xg_ref, ybuf_ref, sem):
    i = pl.program_id(0)
    n = pl.num_programs(0)
    cur = i % 2
    nxt = 1 - cur

    def slot_copy(s):
        return pltpu.make_async_copy(ybuf_ref.at[s], out_ref.at[pl.ds(0, MOE_BLOCK)], sem.at[s])

    def gather_row(base, r, s):
        tok = row_tok_ref[base + r]
        xg_ref[s, pl.ds(r, 1), :] = hp_ref[pl.ds(tok, 1), :]

    def send_row(base, r, s):
        dst = row_dst_ref[base + r]
        pltpu.make_async_copy(ybuf_ref.at[s, pl.ds(r, 1)], out_ref.at[pl.ds(dst, 1)], sem.at[s]).start()

    @pl.when(i == 0)
    def _():
        ybuf_ref[...] = jnp.zeros(ybuf_ref.shape, F32)

        def first(r, carry):
            gather_row(0, r, 0)
            return carry

        lax.fori_loop(0, MOE_BLOCK, first, 0, unroll=8)
        spare = (n + 1) * MOE_BLOCK
        pltpu.make_async_copy(ybuf_ref.at[0], out_ref.at[pl.ds(spare, MOE_BLOCK)], sem.at[0]).start()

    slot_copy(cur).wait()

    xp = xg_ref[cur]
    half = D // 2
    x_lo = pltpu.bitcast(xp << 16, F32).astype(BF16)
    x_hi = pltpu.bitcast(xp & jnp.uint32(0xFFFF0000), F32).astype(BF16)
    next_base = jnp.minimum(i + 1, n - 1) * MOE_BLOCK
    for r in range(MOE_BLOCK):
        gather_row(next_base, r, nxt)
    for r in range(MOE_BLOCK):
        send_row(i * MOE_BLOCK, r, nxt)
    g = _dot(x_lo, wg_ref[0, :half, :]) + _dot(x_hi, wg_ref[0, half:, :]) + bg_ref[0]
    u = _dot(x_lo, wu_ref[0, :half, :]) + _dot(x_hi, wu_ref[0, half:, :]) + bu_ref[0]
    g = jnp.minimum(g, SWIGLU_LIMIT)
    u = jnp.clip(u, -SWIGLU_LIMIT, SWIGLU_LIMIT)
    act = (u + 1.0) * (g * jax.nn.sigmoid(SWIGLU_ALPHA * g))
    ybuf_ref[cur] = _dot(act.astype(BF16), wd_ref[0]) + bd_ref[0]

    @pl.when(i == n - 1)
    def _():
        def last(r, carry):
            send_row(n * MOE_BLOCK, r, cur)
            return carry

        lax.fori_loop(0, MOE_BLOCK, last, 0, unroll=8)
        slot_copy(nxt).wait()
        slot_copy(cur).wait()


def _moe(blk_e, row_tok, row_dst, hp, wg, wu, wd, bg, bu, bd):
    n_blocks = blk_e.shape[0]
    assert n_blocks >= 2
    t = hp.shape[0]
    ew = lambda i, be, rt, rd: (be[i], 0, 0)
    return pl.pallas_call(
        _moe_kernel,
        grid_spec=pltpu.PrefetchScalarGridSpec(
            num_scalar_prefetch=3,
            grid=(n_blocks,),
            in_specs=[
                pl.BlockSpec((t, D // 2), lambda i, be, rt, rd: (0, 0), pipeline_mode=pl.Buffered(1)),
                pl.BlockSpec((1, D, D), ew),
                pl.BlockSpec((1, D, D), ew),
                pl.BlockSpec((1, D, D), ew),
                pl.BlockSpec((1, 1, D), ew),
                pl.BlockSpec((1, 1, D), ew),
                pl.BlockSpec((1, 1, D), ew),
            ],
            out_specs=pl.BlockSpec(memory_space=pl.ANY),
            scratch_shapes=[
                pltpu.VMEM((2, MOE_BLOCK, D // 2), jnp.uint32),
                pltpu.VMEM((2, MOE_BLOCK, D), F32),
                pltpu.SemaphoreType.DMA((2,)),
            ],
        ),
        out_shape=jax.ShapeDtypeStruct(((n_blocks + 2) * MOE_BLOCK, D), F32),
        compiler_params=_cparams(("arbitrary",)),
        name="moe_experts",
    )(blk_e, row_tok, row_dst, hp, wg, wu, wd, bg, bu, bd)


def _route(top_i):
    t = top_i.shape[0]
    n_assign = t * TOP_K
    assert n_assign <= 1 << 16 and n_assign % MOE_BLOCK == 0
    flat_e = top_i.reshape(-1).astype(jnp.int32)
    experts = jnp.arange(N_EXPERTS, dtype=jnp.int32)
    counts = jnp.sum((flat_e[:, None] == experts[None, :]).astype(jnp.int32), axis=0)
    padded = (counts + MOE_BLOCK - 1) // MOE_BLOCK * MOE_BLOCK
    pad_end = jnp.cumsum(padded)
    n_blocks = n_assign // MOE_BLOCK + N_EXPERTS
    real_keys = (flat_e << 17) | jnp.arange(n_assign, dtype=jnp.int32)
    pad_id = experts[:, None] * MOE_BLOCK + jnp.arange(MOE_BLOCK, dtype=jnp.int32)[None, :]
    used = jnp.arange(MOE_BLOCK, dtype=jnp.int32)[None, :] < (padded - counts)[:, None]
    pad_keys = (jnp.where(used, experts[:, None], 63) << 17) | (1 << 16) | pad_id
    keys = jnp.sort(jnp.concatenate([real_keys, pad_keys.reshape(-1)]))
    is_pad = ((keys >> 16) & 1) == 1
    payload = keys & 0xFFFF
    row_tok = jnp.where(is_pad, 0, payload // TOP_K)
    row_dst = jnp.where(is_pad, n_assign + payload, (payload % TOP_K) * t + payload // TOP_K)
    spare = n_blocks * MOE_BLOCK + jnp.arange(MOE_BLOCK, dtype=jnp.int32)
    row_dst = jnp.concatenate([spare, row_dst])
    blk_e = jnp.minimum(jnp.searchsorted(pad_end, jnp.arange(n_blocks) * MOE_BLOCK, side='right'),
                        N_EXPERTS - 1).astype(jnp.int32)
    return blk_e, row_tok, row_dst


def _combine_kernel(y0_ref, y1_ref, y2_ref, y3_ref, gates_ref, x_ref, g2_ref, o_ref):
    g = gates_ref[...]
    y = ((y0_ref[...] * g[:, 0:1] + y1_ref[...] * g[:, 1:2])
         + (y2_ref[...] * g[:, 2:3] + y3_ref[...] * g[:, 3:4]))
    o_ref[...] = x_ref[...] + g2_ref[0] * y


def _combine(ys, gates, x, modr, tp, ts_per_batch):
    tm = 512
    t = x.shape[0]
    nt = t // tm
    yspec = lambda k: pl.BlockSpec((tm, D), lambda i: (k * nt + i, 0))
    return pl.pallas_call(
        _combine_kernel,
        grid=(nt,),
        in_specs=[yspec(0), yspec(1), yspec(2), yspec(3), pl.BlockSpec((tm, TOP_K), lambda i: (i, 0)),
                  pl.BlockSpec((tm, D), lambda i: (i, 0)), _mod_spec(5, tm, tp, ts_per_batch)],
        out_specs=pl.BlockSpec((tm, D), lambda i: (i, 0)),
        out_shape=jax.ShapeDtypeStruct((t, D), F32),
        compiler_params=_cparams(("parallel",)),
        name="moe_combine",
    )(ys, ys, ys, ys, gates, x, modr)


def _final_norm_kernel(x_ref, g_ref, o_ref):
    x = x_ref[...]
    ms = jnp.mean(x * x, axis=-1, keepdims=True)
    o_ref[...] = x * lax.rsqrt(ms + EPS) * g_ref[...]


def _final_norm(x, g):
    tm = 1024
    t = x.shape[0]
    return pl.pallas_call(
        _final_norm_kernel,
        grid=(t // tm,),
        in_specs=[pl.BlockSpec((tm, D), lambda i: (i, 0)), pl.BlockSpec((1, D), lambda i: (0, 0))],
        out_specs=pl.BlockSpec((tm, D), lambda i: (i, 0)),
        out_shape=jax.ShapeDtypeStruct((t, D), F32),
        compiler_params=_cparams(("parallel",)),
        name="final_norm",
    )(x, g)


def _na_bias(rpb, rows):
    kr = min(NA_WIN_R, rows)
    r = np.arange(rows)
    r0 = np.clip(r - kr // 2, 0, rows - kr)
    row_ok = (r[None, :] >= r0[:, None]) & (r[None, :] < r0[:, None] + kr)
    dr = np.clip(r[None, :] - r[:, None] + (NA_WIN_R - 1), 0, 2 * NA_WIN_R - 2)
    c = np.arange(GRID_W)
    w0 = np.clip(c - NA_WIN_C // 2, 0, GRID_W - NA_WIN_C)
    col_ok = (c[None, :] >= w0[:, None]) & (c[None, :] < w0[:, None] + NA_WIN_C)
    dc = np.clip(c[None, :] - c[:, None] + (NA_WIN_C - 1), 0, 2 * NA_WIN_C - 2)
    oh_r = np.eye(2 * NA_WIN_R - 1, dtype=np.float32)[dr]
    oh_c = np.eye(2 * NA_WIN_C - 1, dtype=np.float32)[dc]
    by_row = jnp.einsum('rkd,hde->hrke', oh_r, rpb.astype(F32), precision=HI)
    vals = jnp.einsum('hrke,qce->hrqkc', by_row, oh_c, precision=HI)
    ok = row_ok[:, None, :, None] & col_ok[None, :, None, :]
    n = rows * GRID_W
    return jnp.where(ok[None], vals, NEG_INF).reshape(rpb.shape[0], n, n)


def _rope_tables(n_tokens):
    t = np.arange(n_tokens)
    row = (t // GRID_W).astype(np.float32)
    col = (t % GRID_W).astype(np.float32)
    half = HEAD_DIM // 2
    inv = jnp.asarray(ROPE_THETA, F32) ** (-jnp.arange(0, half, 2, dtype=F32) / half)
    ang_r = jnp.asarray(row)[:, None] * inv
    ang_c = jnp.asarray(col)[:, None] * inv
    cr, sr, cc, sc = jnp.cos(ang_r), jnp.sin(ang_r), jnp.cos(ang_c), jnp.sin(ang_c)
    cos = jnp.concatenate([cr, cr, cc, cc] * 2, axis=-1)
    sin = jnp.concatenate([-sr, sr, -sc, sc] * 2, axis=-1)
    return cos, sin


def _to_heads(x, n_batch, seq, n_heads):
    return x.reshape(n_batch, seq, n_heads, HEAD_DIM).transpose(0, 2, 1, 3)


def _ctx_lanes(cache_l):
    b, h, p, dh = cache_l.shape
    return cache_l.transpose(0, 2, 1, 3).reshape(b, p, h * dh)


def kernel(x_prompt, x_sample, cache_na_k, cache_na_v, cache_gqa_k, cache_gqa_v, c, c_ctx, w_mod, b_mod, norm1_g, norm2_g, w_in, na_rpb, q_norm_g, k_norm_g, cm_ln_g, cm_ln_b, cm_ws, cm_bs, w_out, router_w, router_b, w_gate_up, b_gate_up, w_down, b_down, final_norm_g):
    nb, seq, _ = x_prompt.shape
    db, n_tok, _ = x_sample.shape
    past = cache_na_k.shape[3]
    tp, ts = nb * seq, db * n_tok
    assert db <= CTX_ROW and n_tok % GRID_W == 0

    cond = jnp.zeros((MOD_ROWS, D), F32).at[:db].set(c).at[CTX_ROW].set(c_ctx)
    mod = _modulation(cond, w_mod, b_mod)
    cos, sin = _rope_tables(n_tok)
    x = jnp.concatenate([x_prompt.reshape(tp, D), x_sample.reshape(ts, D)], axis=0)

    w_in16 = w_in.astype(BF16)
    w_out16 = w_out.astype(BF16)
    wg16, wu16 = _prep_wgu(w_gate_up.reshape(DEPTH * N_EXPERTS, D, 2 * D))
    wg16 = wg16.reshape(DEPTH, N_EXPERTS, D, D)
    wu16 = wu16.reshape(DEPTH, N_EXPERTS, D, D)
    wd16 = w_down.astype(BF16)
    bg = b_gate_up[..., 0::2].reshape(DEPTH, N_EXPERTS, 1, D)
    bu = b_gate_up[..., 1::2].reshape(DEPTH, N_EXPERTS, 1, D)
    bd = b_down.reshape(DEPTH, N_EXPERTS, 1, D)

    na_k, na_v, gqa_k, gqa_v = [], [], [], []
    for l in range(DEPTH):
        modr = mod[l].reshape(MOD_ROWS * 6, 1, D)
        qg2 = jnp.tile(q_norm_g[l], 2).reshape(1, LANES)
        kg2 = jnp.tile(k_norm_g[l], 2).reshape(1, LANES)
        lng = cm_ln_g[l].reshape(1, 256)
        lnb = cm_ln_b[l].reshape(1, 256)
        bsf = jnp.repeat(cm_bs[l].T, HEAD_DIM, axis=1)

        z = _inproj(x, modr, norm1_g[l].reshape(1, D), w_in16[l], tp, n_tok)
        mix_p, kbn = _ctx_mixer(z, nb, seq, qg2, kg2, lng, lnb, cm_ws[l], bsf)
        mix_s = _lat_mixer(z, tp, db, n_tok, past,
                           _ctx_lanes(cache_na_k[:, l]), _ctx_lanes(cache_na_v[:, l]),
                           _ctx_lanes(cache_gqa_k[:, l]), _ctx_lanes(cache_gqa_v[:, l]),
                           _na_bias(na_rpb[l], n_tok // GRID_W), cos, sin, qg2, kg2, lng, lnb, cm_ws[l], bsf)
        na_k.append(_to_heads(z[:tp, KA:KA + 256], nb, seq, 4))
        na_v.append(_to_heads(z[:tp, VA:VA + 256], nb, seq, 4))
        gqa_k.append(_to_heads(kbn, nb, seq, 2))
        gqa_v.append(_to_heads(z[:tp, VB:VB + LANES], nb, seq, 2))

        mix = jnp.concatenate([mix_p, mix_s], axis=0)
        xn, hp, top_i, gates = _outproj(mix, x, modr, norm2_g[l].reshape(1, D), w_out16[l],
                                  router_w[l], router_b[l].reshape(1, N_EXPERTS), tp, n_tok)

        blk_e, row_tok, row_dst = _route(top_i[:, :TOP_K])
        gates = gates[:, :TOP_K]
        ys = _moe(blk_e, row_tok, row_dst, hp, wg16[l], wu16[l], wd16[l], bg[l], bu[l], bd[l])
        x = _combine(ys, gates, xn, modr, tp, n_tok)

    y = _final_norm(x, final_norm_g.reshape(1, D))
    return (y[:tp].reshape(nb, seq, D), y[tp:].reshape(db, n_tok, D),
            jnp.stack(na_k, axis=1), jnp.stack(na_v, axis=1), jnp.stack(gqa_k, axis=1), jnp.stack(gqa_v, axis=1))
```

```python
import functools

import jax
import jax.numpy as jnp
import numpy as np
from jax import lax
from jax.experimental import pallas as pl
from jax.experimental.pallas import tpu as pltpu

D = 1024
DEPTH = 2
GRID_W = 64
HEAD_DIM = 64
NA_WIN_R = 8
NA_WIN_C = 16
CHUNK = 128
ROPE_THETA = 10000.0
N_EXPERTS = 32
TOP_K = 4
SWIGLU_LIMIT = 7.0
SWIGLU_ALPHA = 1.702
EPS = 1e-6
MOE_BLOCK = 128
Y_SLOTS = 3
NEG_INF = -1e30
IN_WIDTH = 2048

QA, KA, VA, QB, KB, VB, CU, CV = 0, 256, 512, 768, 1280, 1408, 1536, 1792
OA, OB, OC = 0, 256, 768

LANES = 128
ROW_TILE = D // LANES
MOD_ROWS = 16
CTX_ROW = 8
VMEM_LIMIT = 56 * 1024 * 1024

F32 = jnp.float32
BF16 = jnp.bfloat16
HI = lax.Precision.HIGHEST


def _cparams(sem):
    return pltpu.CompilerParams(dimension_semantics=sem, vmem_limit_bytes=VMEM_LIMIT)


def _dot(a, b):
    return jnp.dot(a, b, preferred_element_type=F32)


def _dot_nt(a, b):
    return lax.dot_general(a, b, (((1,), (1,)), ((), ())), preferred_element_type=F32)


def _lane_lo():
    return lax.broadcasted_iota(jnp.int32, (1, LANES), 1) < HEAD_DIM


def _mod_kernel(c_ref, w_ref, b_ref, o_ref):
    c = c_ref[...]
    s = c * jax.nn.sigmoid(c)
    o_ref[0] = jnp.dot(s, w_ref[0], preferred_element_type=F32, precision=HI) + b_ref[0]


def _modulation(cond, w_mod, b_mod):
    tn = 1536
    n = w_mod.shape[-1]
    return pl.pallas_call(
        _mod_kernel,
        grid=(DEPTH, n // tn),
        in_specs=[
            pl.BlockSpec((MOD_ROWS, D), lambda l, j: (0, 0)),
            pl.BlockSpec((1, D, tn), lambda l, j: (l, 0, j)),
            pl.BlockSpec((1, 1, tn), lambda l, j: (l, 0, j)),
        ],
        out_specs=pl.BlockSpec((1, MOD_ROWS, tn), lambda l, j: (l, 0, j)),
        out_shape=jax.ShapeDtypeStruct((DEPTH, MOD_ROWS, n), F32),
        compiler_params=_cparams(("arbitrary", "arbitrary")),
        name="modulation",
    )(cond, w_mod, b_mod.reshape(DEPTH, 1, n))


def _mod_row(i, tm, tp, ts_per_batch):
    start = i * tm
    return jnp.where(start < tp, CTX_ROW, (start - tp) // ts_per_batch)


def _mod_spec(k, tm, tp, ts_per_batch):
    return pl.BlockSpec((1, 1, D), lambda i: (_mod_row(i, tm, tp, ts_per_batch) * 6 + k, 0, 0))


def _inproj_kernel(x_ref, shift_ref, scale_ref, g_ref, w_ref, z_ref):
    x = x_ref[...]
    ms = jnp.mean(x * x, axis=-1, keepdims=True)
    y = x * lax.rsqrt(ms + EPS) * g_ref[...]
    h = y * (1.0 + scale_ref[0]) + shift_ref[0]
    z_ref[...] = _dot(h.astype(BF16), w_ref[...])


def _inproj(x, modr, g, w, tp, ts_per_batch):
    tm = 512
    t = x.shape[0]
    return pl.pallas_call(
        _inproj_kernel,
        grid=(t // tm,),
        in_specs=[
            pl.BlockSpec((tm, D), lambda i: (i, 0)),
            _mod_spec(0, tm, tp, ts_per_batch),
            _mod_spec(1, tm, tp, ts_per_batch),
            pl.BlockSpec((1, D), lambda i: (0, 0)),
            pl.BlockSpec((D, IN_WIDTH), lambda i: (0, 0)),
        ],
        out_specs=pl.BlockSpec((tm, IN_WIDTH), lambda i: (i, 0)),
        out_shape=jax.ShapeDtypeStruct((t, IN_WIDTH), F32),
        compiler_params=_cparams(("parallel",)),
        name="inproj",
    )(x, modr, modr, g, w)


def _head_rms(x, g2):
    lo = _lane_lo()
    x2 = x * x
    s_lo = jnp.sum(jnp.where(lo, x2, 0.0), axis=-1, keepdims=True)
    s_hi = jnp.sum(jnp.where(lo, 0.0, x2), axis=-1, keepdims=True)
    ms = jnp.where(lo, s_lo, s_hi) * (1.0 / HEAD_DIM)
    return x * lax.rsqrt(ms + EPS) * g2


def _softmax_pv(q16, ks, vs, biases):
    ss = []
    for k, b in zip(ks, biases):
        s = _dot_nt(q16, k)
        if b is not None:
            s = s + b
        ss.append(s)
    m = ss[0].max(axis=-1, keepdims=True)
    for s in ss[1:]:
        m = jnp.maximum(m, s.max(axis=-1, keepdims=True))
    den = None
    acc = None
    for s, v in zip(ss, vs):
        e = jnp.exp(s - m)
        d = e.sum(axis=-1, keepdims=True)
        o = _dot(e.astype(BF16), v)
        den = d if den is None else den + d
        acc = o if acc is None else acc + o
    return acc / den


def _attend_pair(q, ks_lo, vs_lo, ks_hi, vs_hi, b_lo, b_hi):
    lo = _lane_lo()
    o_lo = _softmax_pv(jnp.where(lo, q, 0.0).astype(BF16), ks_lo, vs_lo, b_lo)
    o_hi = _softmax_pv(jnp.where(lo, 0.0, q).astype(BF16), ks_hi, vs_hi, b_hi)
    return jnp.where(lo, o_lo, o_hi)


def _gelu(x):
    c = np.sqrt(2.0 / np.pi).astype(np.float32)
    return x * (0.5 * (1.0 + jnp.tanh(c * (x + 0.044715 * (x * x * x)))))


def _layer_norm(x, g, b):
    mu = jnp.mean(x, axis=-1, keepdims=True)
    xc = x - mu
    var = jnp.mean(xc * xc, axis=-1, keepdims=True)
    return xc * lax.rsqrt(var + EPS) * g + b


def _chunk_mlp(cu, cv, lng_ref, lnb_ref, ws_ref, bsf_ref, out_ref, col0):
    lo = _lane_lo()
    u = _gelu(cu)
    v = _layer_norm(_gelu(cv), lng_ref[...], lnb_ref[...]).astype(BF16)
    s = cu.shape[0]
    for n in range(s // CHUNK):
        rows = slice(n * CHUNK, (n + 1) * CHUNK)
        for jb in range(2):
            cols = slice(jb * LANES, (jb + 1) * LANES)
            vb = v[rows, cols]
            m_lo = _dot(ws_ref[2 * jb].astype(BF16), vb)
            m_hi = _dot(ws_ref[2 * jb + 1].astype(BF16), vb)
            mixed = jnp.where(lo, m_lo, m_hi) + bsf_ref[:, cols]
            out_ref[rows, col0 + jb * LANES:col0 + (jb + 1) * LANES] = (u[rows, cols] * mixed).astype(out_ref.dtype)


def _ctx_mixer_kernel(z_ref, qg_ref, kg_ref, lng_ref, lnb_ref, ws_ref, bsf_ref, mix_ref, kbn_ref):
    scale = HEAD_DIM ** -0.5
    for j in range(2):
        q = z_ref[:, QA + LANES * j:QA + LANES * (j + 1)] * scale
        k = [z_ref[:, KA + LANES * j:KA + LANES * (j + 1)].astype(BF16)]
        v = [z_ref[:, VA + LANES * j:VA + LANES * (j + 1)].astype(BF16)]
        o = _attend_pair(q, k, v, k, v, [None], [None])
        mix_ref[:, OA + LANES * j:OA + LANES * (j + 1)] = o.astype(mix_ref.dtype)

    kb = _head_rms(z_ref[:, KB:KB + LANES], kg_ref[...])
    kbn_ref[...] = kb
    vb = z_ref[:, VB:VB + LANES]
    k_same = [kb.astype(BF16)]
    k_swap = [pltpu.roll(kb, HEAD_DIM, 1).astype(BF16)]
    v_same = [vb.astype(BF16)]
    v_swap = [pltpu.roll(vb, HEAD_DIM, 1).astype(BF16)]
    for j in range(4):
        q = _head_rms(z_ref[:, QB + LANES * j:QB + LANES * (j + 1)], qg_ref[...]) * scale
        if j // 2 == 0:
            o = _attend_pair(q, k_same, v_same, k_swap, v_swap, [None], [None])
        else:
            o = _attend_pair(q, k_swap, v_swap, k_same, v_same, [None], [None])
        mix_ref[:, OB + LANES * j:OB + LANES * (j + 1)] = o.astype(mix_ref.dtype)

    _chunk_mlp(z_ref[:, CU:CU + 256], z_ref[:, CV:CV + 256], lng_ref, lnb_ref, ws_ref, bsf_ref, mix_ref, OC)


def _ctx_mixer(z, n_batch, seq, qg2, kg2, lng, lnb, ws, bsf):
    small = lambda shape: pl.BlockSpec(shape, lambda b: (0,) * len(shape))
    return pl.pallas_call(
        _ctx_mixer_kernel,
        grid=(n_batch,),
        in_specs=[
            pl.BlockSpec((seq, IN_WIDTH), lambda b: (b, 0)),
            small((1, LANES)), small((1, LANES)), small((1, 256)), small((1, 256)),
            small((4, CHUNK, CHUNK)), small((CHUNK, 256)),
        ],
        out_specs=[
            pl.BlockSpec((seq, D), lambda b: (b, 0)),
            pl.BlockSpec((seq, LANES), lambda b: (b, 0)),
        ],
        out_shape=[
            jax.ShapeDtypeStruct((n_batch * seq, D), BF16),
            jax.ShapeDtypeStruct((n_batch * seq, LANES), F32),
        ],
        compiler_params=_cparams(("parallel",)),
        name="ctx_mixer",
    )(z, qg2, kg2, lng, lnb, ws, bsf)


def _rope(x, cos, sin):
    first = (lax.broadcasted_iota(jnp.int32, (1, LANES), 1) % 32) < 16
    partner = jnp.where(first, pltpu.roll(x, LANES - 16, 1), pltpu.roll(x, 16, 1))
    return x * cos + partner * sin


def _lat_mixer_kernel(zq_ref, zkv_ref, nakc_ref, navc_ref, gkc_ref, gvc_ref, bias_ref,
                      cosq_ref, sinq_ref, cosk_ref, sink_ref,
                      qg_ref, kg_ref, lng_ref, lnb_ref, ws_ref, bsf_ref,
                      mix_ref,
                      kl_ref, kls_ref, vl_ref, vls_ref, kc_ref, kcs_ref, vc_ref, vcs_ref):
    scale = HEAD_DIM ** -0.5

    @pl.when(pl.program_id(1) == 0)
    def _():
        kb = _rope(_head_rms(zkv_ref[:, KB:KB + LANES], kg_ref[...]), cosk_ref[...], sink_ref[...])
        vb = zkv_ref[:, VB:VB + LANES]
        kl_ref[...] = kb.astype(BF16)
        kls_ref[...] = pltpu.roll(kb, HEAD_DIM, 1).astype(BF16)
        vl_ref[...] = vb.astype(BF16)
        vls_ref[...] = pltpu.roll(vb, HEAD_DIM, 1).astype(BF16)
        kc = gkc_ref[0]
        vc = gvc_ref[0]
        kc_ref[...] = kc.astype(BF16)
        kcs_ref[...] = pltpu.roll(kc, HEAD_DIM, 1).astype(BF16)
        vc_ref[...] = vc.astype(BF16)
        vcs_ref[...] = pltpu.roll(vc, HEAD_DIM, 1).astype(BF16)

    for j in range(2):
        cols = slice(LANES * j, LANES * (j + 1))
        q = zq_ref[:, QA + LANES * j:QA + LANES * (j + 1)] * scale
        ks = [zkv_ref[:, KA + LANES * j:KA + LANES * (j + 1)].astype(BF16), nakc_ref[0, :, cols].astype(BF16)]
        vs = [zkv_ref[:, VA + LANES * j:VA + LANES * (j + 1)].astype(BF16), navc_ref[0, :, cols].astype(BF16)]
        o = _attend_pair(q, ks, vs, ks, vs, [bias_ref[2 * j], None], [bias_ref[2 * j + 1], None])
        mix_ref[:, OA + LANES * j:OA + LANES * (j + 1)] = o.astype(mix_ref.dtype)

    same = ([kl_ref[...], kc_ref[...]], [vl_ref[...], vc_ref[...]])
    swap = ([kls_ref[...], kcs_ref[...]], [vls_ref[...], vcs_ref[...]])
    for j in range(4):
        q = _head_rms(zq_ref[:, QB + LANES * j:QB + LANES * (j + 1)], qg_ref[...])
        q = _rope(q, cosq_ref[...], sinq_ref[...]) * scale
        lo_kv, hi_kv = (same, swap) if j // 2 == 0 else (swap, same)
        o = _attend_pair(q, lo_kv[0], lo_kv[1], hi_kv[0], hi_kv[1], [None, None], [None, None])
        mix_ref[:, OB + LANES * j:OB + LANES * (j + 1)] = o.astype(mix_ref.dtype)

    _chunk_mlp(zq_ref[:, CU:CU + 256], zq_ref[:, CV:CV + 256], lng_ref, lnb_ref, ws_ref, bsf_ref, mix_ref, OC)


def _lat_mixer(z, row0, n_batch, n_tok, past, nakc, navc, gkc, gvc, bias, cos, sin, qg2, kg2, lng, lnb, ws, bsf):
    tq = 256
    nq = n_tok // tq
    small = lambda shape: pl.BlockSpec(shape, lambda b, t: (0,) * len(shape))
    qblk0 = row0 // tq
    kvblk0 = row0 // n_tok
    return pl.pallas_call(
        _lat_mixer_kernel,
        grid=(n_batch, nq),
        in_specs=[
            pl.BlockSpec((tq, IN_WIDTH), lambda b, t: (qblk0 + b * nq + t, 0)),
            pl.BlockSpec((n_tok, IN_WIDTH), lambda b, t: (kvblk0 + b, 0)),
            pl.BlockSpec((1, past, 256), lambda b, t: (b, 0, 0)),
            pl.BlockSpec((1, past, 256), lambda b, t: (b, 0, 0)),
            pl.BlockSpec((1, past, LANES), lambda b, t: (b, 0, 0)),
            pl.BlockSpec((1, past, LANES), lambda b, t: (b, 0, 0)),
            pl.BlockSpec((4, tq, n_tok), lambda b, t: (0, t, 0)),
            pl.BlockSpec((tq, LANES), lambda b, t: (t, 0)),
            pl.BlockSpec((tq, LANES), lambda b, t: (t, 0)),
            small((n_tok, LANES)), small((n_tok, LANES)),
            small((1, LANES)), small((1, LANES)), small((1, 256)), small((1, 256)),
            small((4, CHUNK, CHUNK)), small((CHUNK, 256)),
        ],
        out_specs=pl.BlockSpec((tq, D), lambda b, t: (b * nq + t, 0)),
        out_shape=jax.ShapeDtypeStruct((n_batch * n_tok, D), BF16),
        scratch_shapes=[pltpu.VMEM((n_tok, LANES), BF16)] * 4 + [pltpu.VMEM((past, LANES), BF16)] * 4,
        compiler_params=_cparams(("parallel", "arbitrary")),
        name="lat_mixer",
    )(z, z, nakc, navc, gkc, gvc, bias, cos, sin, cos, sin, qg2, kg2, lng, lnb, ws, bsf)


def _outproj_kernel(mix_ref, x_ref, g1_ref, shift_ref, scale_ref, g_ref, w_ref, rw_ref, rb_ref,
                    xn_ref, ht_ref, ti_ref, gt_ref):
    x = x_ref[...] + g1_ref[0] * _dot(mix_ref[...], w_ref[...])
    xn_ref[...] = x
    ms = jnp.mean(x * x, axis=-1, keepdims=True)
    h = x * lax.rsqrt(ms + EPS) * g_ref[...]
    h = h * (1.0 + scale_ref[0]) + shift_ref[0]
    logits = jnp.dot(h, rw_ref[...], preferred_element_type=F32, precision=HI) + rb_ref[...]

    lane = lax.broadcasted_iota(jnp.int32, logits.shape, 1)
    top_i = jnp.zeros_like(lane)
    top_v = []
    for k in range(TOP_K):
        m = jnp.max(logits, axis=-1, keepdims=True)
        idx = jnp.min(jnp.where(logits == m, lane, N_EXPERTS), axis=-1, keepdims=True)
        logits = jnp.where(lane == idx, -jnp.inf, logits)
        top_i = jnp.where(lane == k, idx, top_i)
        top_v.append(m)
    es = [jnp.exp(v - top_v[0]) for v in top_v]
    den = (es[0] + es[1]) + (es[2] + es[3])
    gates = jnp.zeros(logits.shape, F32)
    for k in range(TOP_K):
        gates = jnp.where(lane == k, es[k] / den, gates)
    ti_ref[...] = top_i
    gt_ref[...] = gates

    for c in range(ROW_TILE):
        ht_ref[pl.ds(c, h.shape[0], stride=ROW_TILE), :] = h[:, LANES * c:LANES * (c + 1)]


def _outproj(mix, x, modr, g, w, rw, rb, tp, ts_per_batch):
    tm = 512
    t = x.shape[0]
    return pl.pallas_call(
        _outproj_kernel,
        grid=(t // tm,),
        in_specs=[
            pl.BlockSpec((tm, D), lambda i: (i, 0)),
            pl.BlockSpec((tm, D), lambda i: (i, 0)),
            _mod_spec(2, tm, tp, ts_per_batch),
            _mod_spec(3, tm, tp, ts_per_batch),
            _mod_spec(4, tm, tp, ts_per_batch),
            pl.BlockSpec((1, D), lambda i: (0, 0)),
            pl.BlockSpec((D, D), lambda i: (0, 0)),
            pl.BlockSpec((D, N_EXPERTS), lambda i: (0, 0)),
            pl.BlockSpec((1, N_EXPERTS), lambda i: (0, 0)),
        ],
        out_specs=[
            pl.BlockSpec((tm, D), lambda i: (i, 0)),
            pl.BlockSpec((tm * ROW_TILE, LANES), lambda i: (i, 0)),
            pl.BlockSpec((tm, N_EXPERTS), lambda i: (i, 0)),
            pl.BlockSpec((tm, N_EXPERTS), lambda i: (i, 0)),
        ],
        out_shape=[
            jax.ShapeDtypeStruct((t, D), F32),
            jax.ShapeDtypeStruct((t * ROW_TILE, LANES), F32),
            jax.ShapeDtypeStruct((t, N_EXPERTS), jnp.int32),
            jax.ShapeDtypeStruct((t, N_EXPERTS), F32),
        ],
        compiler_params=_cparams(("parallel",)),
        name="outproj",
    )(mix, x, modr, modr, modr, g, w, rw, rb)


def _prep_wgu_kernel(w_ref, p_ref, wg_ref, wu_ref):
    for j in range(w_ref.shape[2] // 256):
        w = w_ref[0, :, 256 * j:256 * (j + 1)].astype(BF16)
        sel = _dot(w, p_ref[...])
        wg_ref[0, :, LANES * j:LANES * (j + 1)] = sel[:, :LANES].astype(BF16)
        wu_ref[0, :, LANES * j:LANES * (j + 1)] = sel[:, LANES:].astype(BF16)


def _prep_wgu(w):
    e, d, f2 = w.shape
    perm = np.zeros((256, 256), np.float32)
    perm[2 * np.arange(LANES), np.arange(LANES)] = 1.0
    perm[2 * np.arange(LANES) + 1, LANES + np.arange(LANES)] = 1.0
    out = jax.ShapeDtypeStruct((e, d, f2 // 2), BF16)
    return pl.pallas_call(
        _prep_wgu_kernel,
        grid=(e,),
        in_specs=[pl.BlockSpec((1, d, f2), lambda i: (i, 0, 0)), pl.BlockSpec((256, 256), lambda i: (0, 0))],
        out_specs=[pl.BlockSpec((1, d, f2 // 2), lambda i: (i, 0, 0))] * 2,
        out_shape=[out, out],
        compiler_params=_cparams(("parallel",)),
        name="prep_wgu",
    )(w, jnp.asarray(perm, BF16))


def _moe_kernel(blk_e_ref, row_tok_ref, row_dst_ref, ht_ref, wg_ref, wu_ref, wd_ref, bg_ref, bu_ref, bd_ref,
                out_ref, xg_ref, ybuf_ref, sem):
    i = pl.program_id(1)
    nb = pl.num_programs(1)
    step = pl.program_id(0) * nb + i
    n = pl.num_programs(0) * nb
    cur = step % 2
    nxt = 1 - cur
    ycur = step % Y_SLOTS
    yprev = (step + Y_SLOTS - 1) % Y_SLOTS
    blk = MOE_BLOCK * ROW_TILE

    def slot_copy(s):
        return pltpu.make_async_copy(ybuf_ref.at[pl.ds(s * blk, blk)], out_ref.at[pl.ds(0, blk)], sem.at[s])

    def gather_row(base, r, s):
        tok = row_tok_ref[base + r]
        xg_ref[pl.ds(pl.multiple_of(s * blk + r * ROW_TILE, ROW_TILE), ROW_TILE), :] = (
            ht_ref[pl.ds(pl.multiple_of(tok * ROW_TILE, ROW_TILE), ROW_TILE), :])

    def send_row(base, r, s, priority=0):
        dst = row_dst_ref[base + r]
        pltpu.make_async_copy(
            ybuf_ref.at[pl.ds(pl.multiple_of(s * blk + r * ROW_TILE, ROW_TILE), ROW_TILE)],
            out_ref.at[pl.ds(pl.multiple_of(dst * ROW_TILE, ROW_TILE), ROW_TILE)], sem.at[s]).start(priority)

    @pl.when(step == 0)
    def _():
        ybuf_ref[...] = jnp.zeros(ybuf_ref.shape, F32)
        for s in range(Y_SLOTS - 1):
            pltpu.make_async_copy(ybuf_ref.at[pl.ds(s * blk, blk)], out_ref.at[pl.ds((n + 1 + s) * blk, blk)],
                                  sem.at[s]).start()

    @pl.when(i == 0)
    def _():
        def first(r, carry):
            gather_row(step * MOE_BLOCK, r, cur)
            return carry

        lax.fori_loop(0, MOE_BLOCK, first, 0, unroll=8)

    slot_copy(ycur).wait()

    x = jnp.concatenate(
        [xg_ref[pl.ds(cur * blk + c, MOE_BLOCK, stride=ROW_TILE), :].astype(BF16) for c in range(ROW_TILE)], axis=1)
    next_blk = jnp.where(i + 1 < nb, step + 1, step)
    for r in range(MOE_BLOCK):
        gather_row(next_blk * MOE_BLOCK, r, nxt)
    for r in range(MOE_BLOCK):
        send_row(step * MOE_BLOCK, r, yprev, priority=r % 2)
    g = _dot(x, wg_ref[0]) + bg_ref[0]
    u = _dot(x, wu_ref[0]) + bu_ref[0]
    g = jnp.minimum(g, SWIGLU_LIMIT)
    u = jnp.clip(u, -SWIGLU_LIMIT, SWIGLU_LIMIT)
    act = (u + 1.0) * (g * jax.nn.sigmoid(SWIGLU_ALPHA * g))
    y = _dot(act.astype(BF16), wd_ref[0]) + bd_ref[0]
    for c in range(ROW_TILE):
        ybuf_ref[pl.ds(ycur * blk + c, MOE_BLOCK, stride=ROW_TILE), :] = y[:, LANES * c:LANES * (c + 1)]

    @pl.when(step == n - 1)
    def _():
        def last(r, carry):
            send_row(n * MOE_BLOCK, r, ycur)
            return carry

        lax.fori_loop(0, MOE_BLOCK, last, 0, unroll=8)
        for s in range(Y_SLOTS):
            slot_copy(s).wait()


def _moe(blk_e, row_tok, row_dst, ht, n_groups, wg, wu, wd, bg, bu, bd):
    nb = blk_e.shape[0] // n_groups
    tg = ht.shape[0] // ROW_TILE // n_groups
    ew = lambda g, i, be, rt, rd: (be[g * nb + i], 0, 0)
    return pl.pallas_call(
        _moe_kernel,
        grid_spec=pltpu.PrefetchScalarGridSpec(
            num_scalar_prefetch=3,
            grid=(n_groups, nb),
            in_specs=[
                pl.BlockSpec((tg * ROW_TILE, LANES), lambda g, i, be, rt, rd: (g, 0), pipeline_mode=pl.Buffered(1)),
                pl.BlockSpec((1, D, D), ew),
                pl.BlockSpec((1, D, D), ew),
                pl.BlockSpec((1, D, D), ew),
                pl.BlockSpec((1, 1, D), ew),
                pl.BlockSpec((1, 1, D), ew),
                pl.BlockSpec((1, 1, D), ew),
            ],
            out_specs=pl.BlockSpec(memory_space=pl.ANY),
            scratch_shapes=[
                pltpu.VMEM((2 * MOE_BLOCK * ROW_TILE, LANES), F32),
                pltpu.VMEM((Y_SLOTS * MOE_BLOCK * ROW_TILE, LANES), F32),
                pltpu.SemaphoreType.DMA((Y_SLOTS,)),
            ],
        ),
        out_shape=jax.ShapeDtypeStruct(((n_groups * nb + Y_SLOTS) * MOE_BLOCK * ROW_TILE, LANES), F32),
        compiler_params=_cparams(("arbitrary", "arbitrary")),
        name="moe_experts",
    )(blk_e, row_tok, row_dst, ht, wg, wu, wd, bg, bu, bd)


def _route(top_i, group, n_groups):
    tg = top_i.shape[0]
    n_assign = tg * TOP_K
    assert n_assign <= 1 << 16 and n_assign % MOE_BLOCK == 0
    flat_e = top_i.reshape(-1).astype(jnp.int32)
    experts = jnp.arange(N_EXPERTS, dtype=jnp.int32)
    counts = jnp.sum((flat_e[:, None] == experts[None, :]).astype(jnp.int32), axis=0)
    padded = (counts + MOE_BLOCK - 1) // MOE_BLOCK * MOE_BLOCK
    pad_end = jnp.cumsum(padded)
    n_blocks = n_assign // MOE_BLOCK + N_EXPERTS
    real_keys = (flat_e << 17) | jnp.arange(n_assign, dtype=jnp.int32)
    pad_id = experts[:, None] * MOE_BLOCK + jnp.arange(MOE_BLOCK, dtype=jnp.int32)[None, :]
    used = jnp.arange(MOE_BLOCK, dtype=jnp.int32)[None, :] < (padded - counts)[:, None]
    pad_keys = (jnp.where(used, experts[:, None], 63) << 17) | (1 << 16) | pad_id
    keys = jnp.sort(jnp.concatenate([real_keys, pad_keys.reshape(-1)]))
    is_pad = ((keys >> 16) & 1) == 1
    payload = keys & 0xFFFF
    tok = payload // TOP_K
    t_all = tg * n_groups
    row_tok = jnp.where(is_pad, 0, tok)
    row_dst = jnp.where(is_pad, TOP_K * t_all + group * (N_EXPERTS * MOE_BLOCK) + payload,
                        (payload % TOP_K) * t_all + group * tg + tok)
    blk_start = jnp.arange(n_blocks, dtype=jnp.int32) * MOE_BLOCK
    blk_e = jnp.sum((pad_end[None, :] <= blk_start[:, None]).astype(jnp.int32), axis=1)
    blk_e = jnp.minimum(blk_e, N_EXPERTS - 1)
    return blk_e, row_tok, row_dst


def _route_groups(top_i, n_groups):
    tg = top_i.shape[0] // n_groups
    parts = [_route(top_i[g * tg:(g + 1) * tg], g, n_groups) for g in range(n_groups)]
    blk_e = jnp.concatenate([p[0] for p in parts])
    row_tok = jnp.concatenate([p[1] for p in parts])
    spare = row_tok.shape[0] + jnp.arange(MOE_BLOCK, dtype=jnp.int32)
    row_dst = jnp.concatenate([spare] + [p[2] for p in parts])
    return blk_e, row_tok, row_dst


def _combine_kernel(y0_ref, y1_ref, y2_ref, y3_ref, gates_ref, x_ref, g2_ref, o_ref):
    g = gates_ref[...]
    tm = x_ref.shape[0]
    for c in range(ROW_TILE):
        rows = pl.ds(c, tm, stride=ROW_TILE)
        cols = slice(LANES * c, LANES * (c + 1))
        y = ((y0_ref[rows, :] * g[:, 0:1] + y1_ref[rows, :] * g[:, 1:2])
             + (y2_ref[rows, :] * g[:, 2:3] + y3_ref[rows, :] * g[:, 3:4]))
        o_ref[:, cols] = x_ref[:, cols] + g2_ref[0][:, cols] * y


def _combine(ys, gates, x, modr, tp, ts_per_batch):
    tm = 512
    t = x.shape[0]
    nt = t // tm
    yspec = lambda k: pl.BlockSpec((tm * ROW_TILE, LANES), lambda i: (k * nt + i, 0))
    return pl.pallas_call(
        _combine_kernel,
        grid=(nt,),
        in_specs=[yspec(0), yspec(1), yspec(2), yspec(3), pl.BlockSpec((tm, TOP_K), lambda i: (i, 0)),
                  pl.BlockSpec((tm, D), lambda i: (i, 0)), _mod_spec(5, tm, tp, ts_per_batch)],
        out_specs=pl.BlockSpec((tm, D), lambda i: (i, 0)),
        out_shape=jax.ShapeDtypeStruct((t, D), F32),
        compiler_params=_cparams(("parallel",)),
        name="moe_combine",
    )(ys, ys, ys, ys, gates, x, modr)


def _final_norm_kernel(x_ref, g_ref, o_ref):
    x = x_ref[...]
    ms = jnp.mean(x * x, axis=-1, keepdims=True)
    o_ref[...] = x * lax.rsqrt(ms + EPS) * g_ref[...]


def _final_norm(x, g):
    tm = 1024
    t = x.shape[0]
    return pl.pallas_call(
        _final_norm_kernel,
        grid=(t // tm,),
        in_specs=[pl.BlockSpec((tm, D), lambda i: (i, 0)), pl.BlockSpec((1, D), lambda i: (0, 0))],
        out_specs=pl.BlockSpec((tm, D), lambda i: (i, 0)),
        out_shape=jax.ShapeDtypeStruct((t, D), F32),
        compiler_params=_cparams(("parallel",)),
        name="final_norm",
    )(x, g)


def _na_bias(rpb, rows):
    kr = min(NA_WIN_R, rows)
    r = np.arange(rows)
    r0 = np.clip(r - kr // 2, 0, rows - kr)
    row_ok = (r[None, :] >= r0[:, None]) & (r[None, :] < r0[:, None] + kr)
    dr = np.clip(r[None, :] - r[:, None] + (NA_WIN_R - 1), 0, 2 * NA_WIN_R - 2)
    c = np.arange(GRID_W)
    w0 = np.clip(c - NA_WIN_C // 2, 0, GRID_W - NA_WIN_C)
    col_ok = (c[None, :] >= w0[:, None]) & (c[None, :] < w0[:, None] + NA_WIN_C)
    dc = np.clip(c[None, :] - c[:, None] + (NA_WIN_C - 1), 0, 2 * NA_WIN_C - 2)
    oh_r = np.eye(2 * NA_WIN_R - 1, dtype=np.float32)[dr]
    oh_c = np.eye(2 * NA_WIN_C - 1, dtype=np.float32)[dc]
    by_row = jnp.einsum('rkd,hde->hrke', oh_r, rpb.astype(F32), precision=HI)
    vals = jnp.einsum('hrke,qce->hrqkc', by_row, oh_c, precision=HI)
    ok = row_ok[:, None, :, None] & col_ok[None, :, None, :]
    n = rows * GRID_W
    return jnp.where(ok[None], vals, NEG_INF).reshape(rpb.shape[0], n, n)


def _rope_tables(n_tokens):
    t = np.arange(n_tokens)
    row = (t // GRID_W).astype(np.float32)
    col = (t % GRID_W).astype(np.float32)
    half = HEAD_DIM // 2
    inv = jnp.asarray(ROPE_THETA, F32) ** (-jnp.arange(0, half, 2, dtype=F32) / half)
    ang_r = jnp.asarray(row)[:, None] * inv
    ang_c = jnp.asarray(col)[:, None] * inv
    cr, sr, cc, sc = jnp.cos(ang_r), jnp.sin(ang_r), jnp.cos(ang_c), jnp.sin(ang_c)
    cos = jnp.concatenate([cr, cr, cc, cc] * 2, axis=-1)
    sin = jnp.concatenate([-sr, sr, -sc, sc] * 2, axis=-1)
    return cos, sin


def _to_heads(x, n_batch, seq, n_heads):
    return x.reshape(n_batch, seq, n_heads, HEAD_DIM).transpose(0, 2, 1, 3)


def _ctx_lanes(cache_l):
    b, h, p, dh = cache_l.shape
    return cache_l.transpose(0, 2, 1, 3).reshape(b, p, h * dh)


def kernel(x_prompt, x_sample, cache_na_k, cache_na_v, cache_gqa_k, cache_gqa_v, c, c_ctx, w_mod, b_mod, norm1_g, norm2_g, w_in, na_rpb, q_norm_g, k_norm_g, cm_ln_g, cm_ln_b, cm_ws, cm_bs, w_out, router_w, router_b, w_gate_up, b_gate_up, w_down, b_down, final_norm_g):
    nb, seq, _ = x_prompt.shape
    db, n_tok, _ = x_sample.shape
    past = cache_na_k.shape[3]
    tp, ts = nb * seq, db * n_tok
    assert db <= CTX_ROW and n_tok % GRID_W == 0
    assert tp == ts

    cond = jnp.zeros((MOD_ROWS, D), F32).at[:db].set(c).at[CTX_ROW].set(c_ctx)
    mod = _modulation(cond, w_mod, b_mod)
    cos, sin = _rope_tables(n_tok)
    x = jnp.concatenate([x_prompt.reshape(tp, D), x_sample.reshape(ts, D)], axis=0)

    w_in16 = w_in.astype(BF16)
    w_out16 = w_out.astype(BF16)
    wg16, wu16 = _prep_wgu(w_gate_up.reshape(DEPTH * N_EXPERTS, D, 2 * D))
    wg16 = wg16.reshape(DEPTH, N_EXPERTS, D, D)
    wu16 = wu16.reshape(DEPTH, N_EXPERTS, D, D)
    wd16 = w_down.astype(BF16)
    bg = b_gate_up[..., 0::2].reshape(DEPTH, N_EXPERTS, 1, D)
    bu = b_gate_up[..., 1::2].reshape(DEPTH, N_EXPERTS, 1, D)
    bd = b_down.reshape(DEPTH, N_EXPERTS, 1, D)

    na_k, na_v, gqa_k, gqa_v = [], [], [], []
    for l in range(DEPTH):
        modr = mod[l].reshape(MOD_ROWS * 6, 1, D)
        qg2 = jnp.tile(q_norm_g[l], 2).reshape(1, LANES)
        kg2 = jnp.tile(k_norm_g[l], 2).reshape(1, LANES)
        lng = cm_ln_g[l].reshape(1, 256)
        lnb = cm_ln_b[l].reshape(1, 256)
        bsf = jnp.repeat(cm_bs[l].T, HEAD_DIM, axis=1)

        z = _inproj(x, modr, norm1_g[l].reshape(1, D), w_in16[l], tp, n_tok)
        mix_p, kbn = _ctx_mixer(z, nb, seq, qg2, kg2, lng, lnb, cm_ws[l], bsf)
        mix_s = _lat_mixer(z, tp, db, n_tok, past,
                           _ctx_lanes(cache_na_k[:, l]), _ctx_lanes(cache_na_v[:, l]),
                           _ctx_lanes(cache_gqa_k[:, l]), _ctx_lanes(cache_gqa_v[:, l]),
                           _na_bias(na_rpb[l], n_tok // GRID_W), cos, sin, qg2, kg2, lng, lnb, cm_ws[l], bsf)
        na_k.append(_to_heads(z[:tp, KA:KA + 256], nb, seq, 4))
        na_v.append(_to_heads(z[:tp, VA:VA + 256], nb, seq, 4))
        gqa_k.append(_to_heads(kbn, nb, seq, 2))
        gqa_v.append(_to_heads(z[:tp, VB:VB + LANES], nb, seq, 2))

        mix = jnp.concatenate([mix_p, mix_s], axis=0)
        xn, ht, top_i, gates = _outproj(mix, x, modr, norm2_g[l].reshape(1, D), w_out16[l],
                                  router_w[l], router_b[l].reshape(1, N_EXPERTS), tp, n_tok)

        blk_e, row_tok, row_dst = _route_groups(top_i[:, :TOP_K], 2)
        gates = gates[:, :TOP_K]
        ys = _moe(blk_e, row_tok, row_dst, ht, 2, wg16[l], wu16[l], wd16[l], bg[l], bu[l], bd[l])
        x = _combine(ys, gates, xn, modr, tp, n_tok)

    y = _final_norm(x, final_norm_g.reshape(1, D))
    return (y[:tp].reshape(nb, seq, D), y[tp:].reshape(db, n_tok, D),
            jnp.stack(na_k, axis=1), jnp.stack(na_v, axis=1), jnp.stack(gqa_k, axis=1), jnp.stack(gqa_v, axis=1))
```

```python
import functools

import jax
import jax.numpy as jnp
import numpy as np
from jax import lax
from jax.experimental import pallas as pl
from jax.experimental.pallas import tpu as pltpu

D = 1024
DEPTH = 2
GRID_W = 64
HEAD_DIM = 64
NA_WIN_R = 8
NA_WIN_C = 16
CHUNK = 128
ROPE_THETA = 10000.0
N_EXPERTS = 32
TOP_K = 4
SWIGLU_LIMIT = 7.0
SWIGLU_ALPHA = 1.702
EPS = 1e-6
MOE_BLOCK = 256
Y_SLOTS = 3
NEG_INF = -1e30
IN_WIDTH = 2048

QA, KA, VA, QB, KB, VB, CU, CV = 0, 256, 512, 768, 1280, 1408, 1536, 1792
OA, OB, OC = 0, 256, 768

LANES = 128
ROW_TILE = D // LANES
MOD_ROWS = 16
CTX_ROW = 8
VMEM_LIMIT = 56 * 1024 * 1024

F32 = jnp.float32
BF16 = jnp.bfloat16
HI = lax.Precision.HIGHEST


def _cparams(sem):
    return pltpu.CompilerParams(dimension_semantics=sem, vmem_limit_bytes=VMEM_LIMIT)


def _dot(a, b):
    return jnp.dot(a, b, preferred_element_type=F32)


def _dot_nt(a, b):
    return lax.dot_general(a, b, (((1,), (1,)), ((), ())), preferred_element_type=F32)


def _lane_lo():
    return lax.broadcasted_iota(jnp.int32, (1, LANES), 1) < HEAD_DIM


def _mod_kernel(c_ref, w_ref, b_ref, o_ref):
    c = c_ref[...]
    s = c * jax.nn.sigmoid(c)
    o_ref[0] = jnp.dot(s, w_ref[0], preferred_element_type=F32, precision=HI) + b_ref[0]


def _modulation(cond, w_mod, b_mod):
    tn = 1536
    n = w_mod.shape[-1]
    return pl.pallas_call(
        _mod_kernel,
        grid=(DEPTH, n // tn),
        in_specs=[
            pl.BlockSpec((MOD_ROWS, D), lambda l, j: (0, 0)),
            pl.BlockSpec((1, D, tn), lambda l, j: (l, 0, j)),
            pl.BlockSpec((1, 1, tn), lambda l, j: (l, 0, j)),
        ],
        out_specs=pl.BlockSpec((1, MOD_ROWS, tn), lambda l, j: (l, 0, j)),
        out_shape=jax.ShapeDtypeStruct((DEPTH, MOD_ROWS, n), F32),
        compiler_params=_cparams(("arbitrary", "arbitrary")),
        name="modulation",
    )(cond, w_mod, b_mod.reshape(DEPTH, 1, n))


def _mod_row(i, tm, tp, ts_per_batch):
    start = i * tm
    return jnp.where(start < tp, CTX_ROW, (start - tp) // ts_per_batch)


def _mod_spec(k, tm, tp, ts_per_batch):
    return pl.BlockSpec((1, 1, D), lambda i: (_mod_row(i, tm, tp, ts_per_batch) * 6 + k, 0, 0))


def _inproj_kernel(x_ref, shift_ref, scale_ref, g_ref, w_ref, z_ref):
    x = x_ref[...]
    ms = jnp.mean(x * x, axis=-1, keepdims=True)
    y = x * lax.rsqrt(ms + EPS) * g_ref[...]
    h = y * (1.0 + scale_ref[0]) + shift_ref[0]
    z_ref[...] = _dot(h.astype(BF16), w_ref[...])


def _inproj(x, modr, g, w, tp, ts_per_batch):
    tm = 512
    t = x.shape[0]
    return pl.pallas_call(
        _inproj_kernel,
        grid=(t // tm,),
        in_specs=[
            pl.BlockSpec((tm, D), lambda i: (i, 0)),
            _mod_spec(0, tm, tp, ts_per_batch),
            _mod_spec(1, tm, tp, ts_per_batch),
            pl.BlockSpec((1, D), lambda i: (0, 0)),
            pl.BlockSpec((D, IN_WIDTH), lambda i: (0, 0)),
        ],
        out_specs=pl.BlockSpec((tm, IN_WIDTH), lambda i: (i, 0)),
        out_shape=jax.ShapeDtypeStruct((t, IN_WIDTH), F32),
        compiler_params=_cparams(("parallel",)),
        name="inproj",
    )(x, modr, modr, g, w)


def _head_rms(x, g2):
    lo = _lane_lo()
    x2 = x * x
    s_lo = jnp.sum(jnp.where(lo, x2, 0.0), axis=-1, keepdims=True)
    s_hi = jnp.sum(jnp.where(lo, 0.0, x2), axis=-1, keepdims=True)
    ms = jnp.where(lo, s_lo, s_hi) * (1.0 / HEAD_DIM)
    return x * lax.rsqrt(ms + EPS) * g2


def _softmax_pv(q16, ks, vs, biases):
    ss = []
    for k, b in zip(ks, biases):
        s = _dot_nt(q16, k)
        if b is not None:
            s = s + b
        ss.append(s)
    m = ss[0].max(axis=-1, keepdims=True)
    for s in ss[1:]:
        m = jnp.maximum(m, s.max(axis=-1, keepdims=True))
    den = None
    acc = None
    for s, v in zip(ss, vs):
        e = jnp.exp(s - m)
        d = e.sum(axis=-1, keepdims=True)
        o = _dot(e.astype(BF16), v)
        den = d if den is None else den + d
        acc = o if acc is None else acc + o
    return acc / den


def _attend_pair(q, ks_lo, vs_lo, ks_hi, vs_hi, b_lo, b_hi):
    lo = _lane_lo()
    o_lo = _softmax_pv(jnp.where(lo, q, 0.0).astype(BF16), ks_lo, vs_lo, b_lo)
    o_hi = _softmax_pv(jnp.where(lo, 0.0, q).astype(BF16), ks_hi, vs_hi, b_hi)
    return jnp.where(lo, o_lo, o_hi)


def _gelu(x):
    c = np.sqrt(2.0 / np.pi).astype(np.float32)
    return x * (0.5 * (1.0 + jnp.tanh(c * (x + 0.044715 * (x * x * x)))))


def _layer_norm(x, g, b):
    mu = jnp.mean(x, axis=-1, keepdims=True)
    xc = x - mu
    var = jnp.mean(xc * xc, axis=-1, keepdims=True)
    return xc * lax.rsqrt(var + EPS) * g + b


def _chunk_mlp(cu, cv, lng_ref, lnb_ref, ws_ref, bsf_ref, out_ref, col0):
    lo = _lane_lo()
    u = _gelu(cu)
    v = _layer_norm(_gelu(cv), lng_ref[...], lnb_ref[...]).astype(BF16)
    s = cu.shape[0]
    for n in range(s // CHUNK):
        rows = slice(n * CHUNK, (n + 1) * CHUNK)
        for jb in range(2):
            cols = slice(jb * LANES, (jb + 1) * LANES)
            vb = v[rows, cols]
            m_lo = _dot(ws_ref[2 * jb].astype(BF16), vb)
            m_hi = _dot(ws_ref[2 * jb + 1].astype(BF16), vb)
            mixed = jnp.where(lo, m_lo, m_hi) + bsf_ref[:, cols]
            out_ref[rows, col0 + jb * LANES:col0 + (jb + 1) * LANES] = (u[rows, cols] * mixed).astype(out_ref.dtype)


def _ctx_mixer_kernel(z_ref, qg_ref, kg_ref, lng_ref, lnb_ref, ws_ref, bsf_ref, mix_ref, kbn_ref):
    scale = HEAD_DIM ** -0.5
    for j in range(2):
        q = z_ref[:, QA + LANES * j:QA + LANES * (j + 1)] * scale
        k = [z_ref[:, KA + LANES * j:KA + LANES * (j + 1)].astype(BF16)]
        v = [z_ref[:, VA + LANES * j:VA + LANES * (j + 1)].astype(BF16)]
        o = _attend_pair(q, k, v, k, v, [None], [None])
        mix_ref[:, OA + LANES * j:OA + LANES * (j + 1)] = o.astype(mix_ref.dtype)

    kb = _head_rms(z_ref[:, KB:KB + LANES], kg_ref[...])
    kbn_ref[...] = kb
    vb = z_ref[:, VB:VB + LANES]
    k_same = [kb.astype(BF16)]
    k_swap = [pltpu.roll(kb, HEAD_DIM, 1).astype(BF16)]
    v_same = [vb.astype(BF16)]
    v_swap = [pltpu.roll(vb, HEAD_DIM, 1).astype(BF16)]
    for j in range(4):
        q = _head_rms(z_ref[:, QB + LANES * j:QB + LANES * (j + 1)], qg_ref[...]) * scale
        if j // 2 == 0:
            o = _attend_pair(q, k_same, v_same, k_swap, v_swap, [None], [None])
        else:
            o = _attend_pair(q, k_swap, v_swap, k_same, v_same, [None], [None])
        mix_ref[:, OB + LANES * j:OB + LANES * (j + 1)] = o.astype(mix_ref.dtype)

    _chunk_mlp(z_ref[:, CU:CU + 256], z_ref[:, CV:CV + 256], lng_ref, lnb_ref, ws_ref, bsf_ref, mix_ref, OC)


def _ctx_mixer(z, n_batch, seq, qg2, kg2, lng, lnb, ws, bsf):
    small = lambda shape: pl.BlockSpec(shape, lambda b: (0,) * len(shape))
    return pl.pallas_call(
        _ctx_mixer_kernel,
        grid=(n_batch,),
        in_specs=[
            pl.BlockSpec((seq, IN_WIDTH), lambda b: (b, 0)),
            small((1, LANES)), small((1, LANES)), small((1, 256)), small((1, 256)),
            small((4, CHUNK, CHUNK)), small((CHUNK, 256)),
        ],
        out_specs=[
            pl.BlockSpec((seq, D), lambda b: (b, 0)),
            pl.BlockSpec((seq, LANES), lambda b: (b, 0)),
        ],
        out_shape=[
            jax.ShapeDtypeStruct((n_batch * seq, D), BF16),
            jax.ShapeDtypeStruct((n_batch * seq, LANES), F32),
        ],
        compiler_params=_cparams(("parallel",)),
        name="ctx_mixer",
    )(z, qg2, kg2, lng, lnb, ws, bsf)


def _rope(x, cos, sin):
    first = (lax.broadcasted_iota(jnp.int32, (1, LANES), 1) % 32) < 16
    partner = jnp.where(first, pltpu.roll(x, LANES - 16, 1), pltpu.roll(x, 16, 1))
    return x * cos + partner * sin


def _lat_mixer_kernel(zq_ref, zkv_ref, nakc_ref, navc_ref, gkc_ref, gvc_ref, bias_ref,
                      cosq_ref, sinq_ref, cosk_ref, sink_ref,
                      qg_ref, kg_ref, lng_ref, lnb_ref, ws_ref, bsf_ref,
                      mix_ref,
                      kl_ref, kls_ref, vl_ref, vls_ref, kc_ref, kcs_ref, vc_ref, vcs_ref):
    scale = HEAD_DIM ** -0.5

    @pl.when(pl.program_id(1) == 0)
    def _():
        kb = _rope(_head_rms(zkv_ref[:, KB:KB + LANES], kg_ref[...]), cosk_ref[...], sink_ref[...])
        vb = zkv_ref[:, VB:VB + LANES]
        kl_ref[...] = kb.astype(BF16)
        kls_ref[...] = pltpu.roll(kb, HEAD_DIM, 1).astype(BF16)
        vl_ref[...] = vb.astype(BF16)
        vls_ref[...] = pltpu.roll(vb, HEAD_DIM, 1).astype(BF16)
        kc = gkc_ref[0]
        vc = gvc_ref[0]
        kc_ref[...] = kc.astype(BF16)
        kcs_ref[...] = pltpu.roll(kc, HEAD_DIM, 1).astype(BF16)
        vc_ref[...] = vc.astype(BF16)
        vcs_ref[...] = pltpu.roll(vc, HEAD_DIM, 1).astype(BF16)

    for j in range(2):
        cols = slice(LANES * j, LANES * (j + 1))
        q = zq_ref[:, QA + LANES * j:QA + LANES * (j + 1)] * scale
        ks = [zkv_ref[:, KA + LANES * j:KA + LANES * (j + 1)].astype(BF16), nakc_ref[0, :, cols].astype(BF16)]
        vs = [zkv_ref[:, VA + LANES * j:VA + LANES * (j + 1)].astype(BF16), navc_ref[0, :, cols].astype(BF16)]
        o = _attend_pair(q, ks, vs, ks, vs, [bias_ref[2 * j], None], [bias_ref[2 * j + 1], None])
        mix_ref[:, OA + LANES * j:OA + LANES * (j + 1)] = o.astype(mix_ref.dtype)

    same = ([kl_ref[...], kc_ref[...]], [vl_ref[...], vc_ref[...]])
    swap = ([kls_ref[...], kcs_ref[...]], [vls_ref[...], vcs_ref[...]])
    for j in range(4):
        q = _head_rms(zq_ref[:, QB + LANES * j:QB + LANES * (j + 1)], qg_ref[...])
        q = _rope(q, cosq_ref[...], sinq_ref[...]) * scale
        lo_kv, hi_kv = (same, swap) if j // 2 == 0 else (swap, same)
        o = _attend_pair(q, lo_kv[0], lo_kv[1], hi_kv[0], hi_kv[1], [None, None], [None, None])
        mix_ref[:, OB + LANES * j:OB + LANES * (j + 1)] = o.astype(mix_ref.dtype)

    _chunk_mlp(zq_ref[:, CU:CU + 256], zq_ref[:, CV:CV + 256], lng_ref, lnb_ref, ws_ref, bsf_ref, mix_ref, OC)


def _lat_mixer(z, row0, n_batch, n_tok, past, nakc, navc, gkc, gvc, bias, cos, sin, qg2, kg2, lng, lnb, ws, bsf):
    tq = 256
    nq = n_tok // tq
    small = lambda shape: pl.BlockSpec(shape, lambda b, t: (0,) * len(shape))
    qblk0 = row0 // tq
    kvblk0 = row0 // n_tok
    return pl.pallas_call(
        _lat_mixer_kernel,
        grid=(n_batch, nq),
        in_specs=[
            pl.BlockSpec((tq, IN_WIDTH), lambda b, t: (qblk0 + b * nq + t, 0)),
            pl.BlockSpec((n_tok, IN_WIDTH), lambda b, t: (kvblk0 + b, 0)),
            pl.BlockSpec((1, past, 256), lambda b, t: (b, 0, 0)),
            pl.BlockSpec((1, past, 256), lambda b, t: (b, 0, 0)),
            pl.BlockSpec((1, past, LANES), lambda b, t: (b, 0, 0)),
            pl.BlockSpec((1, past, LANES), lambda b, t: (b, 0, 0)),
            pl.BlockSpec((4, tq, n_tok), lambda b, t: (0, t, 0)),
            pl.BlockSpec((tq, LANES), lambda b, t: (t, 0)),
            pl.BlockSpec((tq, LANES), lambda b, t: (t, 0)),
            small((n_tok, LANES)), small((n_tok, LANES)),
            small((1, LANES)), small((1, LANES)), small((1, 256)), small((1, 256)),
            small((4, CHUNK, CHUNK)), small((CHUNK, 256)),
        ],
        out_specs=pl.BlockSpec((tq, D), lambda b, t: (b * nq + t, 0)),
        out_shape=jax.ShapeDtypeStruct((n_batch * n_tok, D), BF16),
        scratch_shapes=[pltpu.VMEM((n_tok, LANES), BF16)] * 4 + [pltpu.VMEM((past, LANES), BF16)] * 4,
        compiler_params=_cparams(("parallel", "arbitrary")),
        name="lat_mixer",
    )(z, z, nakc, navc, gkc, gvc, bias, cos, sin, cos, sin, qg2, kg2, lng, lnb, ws, bsf)


def _outproj_kernel(mix_ref, x_ref, g1_ref, shift_ref, scale_ref, g_ref, w_ref, rw_ref, rb_ref,
                    xn_ref, ht_ref, ti_ref, gt_ref):
    x = x_ref[...] + g1_ref[0] * _dot(mix_ref[...], w_ref[...])
    xn_ref[...] = x
    ms = jnp.mean(x * x, axis=-1, keepdims=True)
    h = x * lax.rsqrt(ms + EPS) * g_ref[...]
    h = h * (1.0 + scale_ref[0]) + shift_ref[0]
    logits = jnp.dot(h, rw_ref[...], preferred_element_type=F32, precision=HI) + rb_ref[...]

    lane = lax.broadcasted_iota(jnp.int32, logits.shape, 1)
    top_i = jnp.zeros_like(lane)
    top_v = []
    for k in range(TOP_K):
        m = jnp.max(logits, axis=-1, keepdims=True)
        idx = jnp.min(jnp.where(logits == m, lane, N_EXPERTS), axis=-1, keepdims=True)
        logits = jnp.where(lane == idx, -jnp.inf, logits)
        top_i = jnp.where(lane == k, idx, top_i)
        top_v.append(m)
    es = [jnp.exp(v - top_v[0]) for v in top_v]
    den = (es[0] + es[1]) + (es[2] + es[3])
    gates = jnp.zeros(logits.shape, F32)
    for k in range(TOP_K):
        gates = jnp.where(lane == k, es[k] / den, gates)
    ti_ref[...] = top_i
    gt_ref[...] = gates

    for c in range(ROW_TILE):
        ht_ref[pl.ds(c, h.shape[0], stride=ROW_TILE), :] = h[:, LANES * c:LANES * (c + 1)]


def _outproj(mix, x, modr, g, w, rw, rb, tp, ts_per_batch):
    tm = 512
    t = x.shape[0]
    return pl.pallas_call(
        _outproj_kernel,
        grid=(t // tm,),
        in_specs=[
            pl.BlockSpec((tm, D), lambda i: (i, 0)),
            pl.BlockSpec((tm, D), lambda i: (i, 0)),
            _mod_spec(2, tm, tp, ts_per_batch),
            _mod_spec(3, tm, tp, ts_per_batch),
            _mod_spec(4, tm, tp, ts_per_batch),
            pl.BlockSpec((1, D), lambda i: (0, 0)),
            pl.BlockSpec((D, D), lambda i: (0, 0)),
            pl.BlockSpec((D, N_EXPERTS), lambda i: (0, 0)),
            pl.BlockSpec((1, N_EXPERTS), lambda i: (0, 0)),
        ],
        out_specs=[
            pl.BlockSpec((tm, D), lambda i: (i, 0)),
            pl.BlockSpec((tm * ROW_TILE, LANES), lambda i: (i, 0)),
            pl.BlockSpec((tm, N_EXPERTS), lambda i: (i, 0)),
            pl.BlockSpec((tm, N_EXPERTS), lambda i: (i, 0)),
        ],
        out_shape=[
            jax.ShapeDtypeStruct((t, D), F32),
            jax.ShapeDtypeStruct((t * ROW_TILE, LANES), F32),
            jax.ShapeDtypeStruct((t, N_EXPERTS), jnp.int32),
            jax.ShapeDtypeStruct((t, N_EXPERTS), F32),
        ],
        compiler_params=_cparams(("parallel",)),
        name="outproj",
    )(mix, x, modr, modr, modr, g, w, rw, rb)


def _prep_wgu_kernel(w_ref, p_ref, wg_ref, wu_ref):
    for j in range(w_ref.shape[2] // 256):
        w = w_ref[0, :, 256 * j:256 * (j + 1)].astype(BF16)
        sel = _dot(w, p_ref[...])
        wg_ref[0, :, LANES * j:LANES * (j + 1)] = sel[:, :LANES].astype(BF16)
        wu_ref[0, :, LANES * j:LANES * (j + 1)] = sel[:, LANES:].astype(BF16)


def _prep_wgu(w):
    e, d, f2 = w.shape
    perm = np.zeros((256, 256), np.float32)
    perm[2 * np.arange(LANES), np.arange(LANES)] = 1.0
    perm[2 * np.arange(LANES) + 1, LANES + np.arange(LANES)] = 1.0
    out = jax.ShapeDtypeStruct((e, d, f2 // 2), BF16)
    return pl.pallas_call(
        _prep_wgu_kernel,
        grid=(e,),
        in_specs=[pl.BlockSpec((1, d, f2), lambda i: (i, 0, 0)), pl.BlockSpec((256, 256), lambda i: (0, 0))],
        out_specs=[pl.BlockSpec((1, d, f2 // 2), lambda i: (i, 0, 0))] * 2,
        out_shape=[out, out],
        compiler_params=_cparams(("parallel",)),
        name="prep_wgu",
    )(w, jnp.asarray(perm, BF16))


def _moe_kernel(blk_e_ref, row_tok_ref, row_dst_ref, ht_ref, wg_ref, wu_ref, wd_ref, bg_ref, bu_ref, bd_ref,
                out_ref, xg_ref, ybuf_ref, sem):
    i = pl.program_id(1)
    nb = pl.num_programs(1)
    step = pl.program_id(0) * nb + i
    n = pl.num_programs(0) * nb
    cur = step % 2
    nxt = 1 - cur
    ycur = step % Y_SLOTS
    yprev = (step + Y_SLOTS - 1) % Y_SLOTS
    blk = MOE_BLOCK * ROW_TILE

    def slot_copy(s):
        return pltpu.make_async_copy(ybuf_ref.at[pl.ds(s * blk, blk)], out_ref.at[pl.ds(0, blk)], sem.at[s])

    def gather_row(base, r, s):
        tok = row_tok_ref[base + r]
        xg_ref[pl.ds(pl.multiple_of(s * blk + r * ROW_TILE, ROW_TILE), ROW_TILE), :] = (
            ht_ref[pl.ds(pl.multiple_of(tok * ROW_TILE, ROW_TILE), ROW_TILE), :])

    def send_row(base, r, s, priority=0):
        dst = row_dst_ref[base + r]
        pltpu.make_async_copy(
            ybuf_ref.at[pl.ds(pl.multiple_of(s * blk + r * ROW_TILE, ROW_TILE), ROW_TILE)],
            out_ref.at[pl.ds(pl.multiple_of(dst * ROW_TILE, ROW_TILE), ROW_TILE)], sem.at[s]).start(priority)

    @pl.when(step == 0)
    def _():
        ybuf_ref[...] = jnp.zeros(ybuf_ref.shape, F32)
        for s in range(Y_SLOTS - 1):
            pltpu.make_async_copy(ybuf_ref.at[pl.ds(s * blk, blk)], out_ref.at[pl.ds((n + 1 + s) * blk, blk)],
                                  sem.at[s]).start()

    @pl.when(i == 0)
    def _():
        def first(r, carry):
            gather_row(step * MOE_BLOCK, r, cur)
            return carry

        lax.fori_loop(0, MOE_BLOCK, first, 0, unroll=8)

    slot_copy(ycur).wait()

    x = jnp.concatenate(
        [xg_ref[pl.ds(cur * blk + c, MOE_BLOCK, stride=ROW_TILE), :].astype(BF16) for c in range(ROW_TILE)], axis=1)
    next_blk = jnp.where(i + 1 < nb, step + 1, step)
    for r in range(MOE_BLOCK):
        gather_row(next_blk * MOE_BLOCK, r, nxt)
    for r in range(MOE_BLOCK):
        send_row(step * MOE_BLOCK, r, yprev, priority=r % 2)
    g = _dot(x, wg_ref[0]) + bg_ref[0]
    u = _dot(x, wu_ref[0]) + bu_ref[0]
    g = jnp.minimum(g, SWIGLU_LIMIT)
    u = jnp.clip(u, -SWIGLU_LIMIT, SWIGLU_LIMIT)
    act = (u + 1.0) * (g * jax.nn.sigmoid(SWIGLU_ALPHA * g))
    y = _dot(act.astype(BF16), wd_ref[0]) + bd_ref[0]
    for c in range(ROW_TILE):
        ybuf_ref[pl.ds(ycur * blk + c, MOE_BLOCK, stride=ROW_TILE), :] = y[:, LANES * c:LANES * (c + 1)]

    @pl.when(step == n - 1)
    def _():
        def last(r, carry):
            send_row(n * MOE_BLOCK, r, ycur)
            return carry

        lax.fori_loop(0, MOE_BLOCK, last, 0, unroll=8)
        for s in range(Y_SLOTS):
            slot_copy(s).wait()


def _moe(blk_e, row_tok, row_dst, ht, n_groups, wg, wu, wd, bg, bu, bd):
    nb = blk_e.shape[0] // n_groups
    tg = ht.shape[0] // ROW_TILE // n_groups
    ew = lambda g, i, be, rt, rd: (be[g * nb + i], 0, 0)
    return pl.pallas_call(
        _moe_kernel,
        grid_spec=pltpu.PrefetchScalarGridSpec(
            num_scalar_prefetch=3,
            grid=(n_groups, nb),
            in_specs=[
                pl.BlockSpec((tg * ROW_TILE, LANES), lambda g, i, be, rt, rd: (g, 0), pipeline_mode=pl.Buffered(1)),
                pl.BlockSpec((1, D, D), ew),
                pl.BlockSpec((1, D, D), ew),
                pl.BlockSpec((1, D, D), ew),
                pl.BlockSpec((1, 1, D), ew),
                pl.BlockSpec((1, 1, D), ew),
                pl.BlockSpec((1, 1, D), ew),
            ],
            out_specs=pl.BlockSpec(memory_space=pl.ANY),
            scratch_shapes=[
                pltpu.VMEM((2 * MOE_BLOCK * ROW_TILE, LANES), F32),
                pltpu.VMEM((Y_SLOTS * MOE_BLOCK * ROW_TILE, LANES), F32),
                pltpu.SemaphoreType.DMA((Y_SLOTS,)),
            ],
        ),
        out_shape=jax.ShapeDtypeStruct(((n_groups * nb + Y_SLOTS) * MOE_BLOCK * ROW_TILE, LANES), F32),
        compiler_params=_cparams(("arbitrary", "arbitrary")),
        name="moe_experts",
    )(blk_e, row_tok, row_dst, ht, wg, wu, wd, bg, bu, bd)


def _route(top_i, group, n_groups):
    tg = top_i.shape[0]
    n_assign = tg * TOP_K
    assert n_assign <= 1 << 16 and n_assign % MOE_BLOCK == 0
    flat_e = top_i.reshape(-1).astype(jnp.int32)
    experts = jnp.arange(N_EXPERTS, dtype=jnp.int32)
    counts = jnp.sum((flat_e[:, None] == experts[None, :]).astype(jnp.int32), axis=0)
    padded = (counts + MOE_BLOCK - 1) // MOE_BLOCK * MOE_BLOCK
    pad_end = jnp.cumsum(padded)
    n_blocks = n_assign // MOE_BLOCK + N_EXPERTS
    real_keys = (flat_e << 17) | jnp.arange(n_assign, dtype=jnp.int32)
    pad_id = experts[:, None] * MOE_BLOCK + jnp.arange(MOE_BLOCK, dtype=jnp.int32)[None, :]
    used = jnp.arange(MOE_BLOCK, dtype=jnp.int32)[None, :] < (padded - counts)[:, None]
    pad_keys = (jnp.where(used, experts[:, None], 63) << 17) | (1 << 16) | pad_id
    keys = jnp.sort(jnp.concatenate([real_keys, pad_keys.reshape(-1)]))
    is_pad = ((keys >> 16) & 1) == 1
    payload = keys & 0xFFFF
    tok = payload // TOP_K
    t_all = tg * n_groups
    row_tok = jnp.where(is_pad, 0, tok)
    row_dst = jnp.where(is_pad, TOP_K * t_all + group * (N_EXPERTS * MOE_BLOCK) + payload,
                        (payload % TOP_K) * t_all + group * tg + tok)
    blk_start = jnp.arange(n_blocks, dtype=jnp.int32) * MOE_BLOCK
    blk_e = jnp.sum((pad_end[None, :] <= blk_start[:, None]).astype(jnp.int32), axis=1)
    blk_e = jnp.minimum(blk_e, N_EXPERTS - 1)
    return blk_e, row_tok, row_dst


def _route_groups(top_i, n_groups):
    tg = top_i.shape[0] // n_groups
    parts = [_route(top_i[g * tg:(g + 1) * tg], g, n_groups) for g in range(n_groups)]
    blk_e = jnp.concatenate([p[0] for p in parts])
    row_tok = jnp.concatenate([p[1] for p in parts])
    spare = row_tok.shape[0] + jnp.arange(MOE_BLOCK, dtype=jnp.int32)
    row_dst = jnp.concatenate([spare] + [p[2] for p in parts])
    return blk_e, row_tok, row_dst


def _combine_kernel(y0_ref, y1_ref, y2_ref, y3_ref, gates_ref, x_ref, g2_ref, o_ref):
    g = gates_ref[...]
    tm = x_ref.shape[0]
    for c in range(ROW_TILE):
        rows = pl.ds(c, tm, stride=ROW_TILE)
        cols = slice(LANES * c, LANES * (c + 1))
        y = ((y0_ref[rows, :] * g[:, 0:1] + y1_ref[rows, :] * g[:, 1:2])
             + (y2_ref[rows, :] * g[:, 2:3] + y3_ref[rows, :] * g[:, 3:4]))
        o_ref[:, cols] = x_ref[:, cols] + g2_ref[0][:, cols] * y


def _combine(ys, gates, x, modr, tp, ts_per_batch):
    tm = 512
    t = x.shape[0]
    nt = t // tm
    yspec = lambda k: pl.BlockSpec((tm * ROW_TILE, LANES), lambda i: (k * nt + i, 0))
    return pl.pallas_call(
        _combine_kernel,
        grid=(nt,),
        in_specs=[yspec(0), yspec(1), yspec(2), yspec(3), pl.BlockSpec((tm, TOP_K), lambda i: (i, 0)),
                  pl.BlockSpec((tm, D), lambda i: (i, 0)), _mod_spec(5, tm, tp, ts_per_batch)],
        out_specs=pl.BlockSpec((tm, D), lambda i: (i, 0)),
        out_shape=jax.ShapeDtypeStruct((t, D), F32),
        compiler_params=_cparams(("parallel",)),
        name="moe_combine",
    )(ys, ys, ys, ys, gates, x, modr)


def _final_norm_kernel(x_ref, g_ref, o_ref):
    x = x_ref[...]
    ms = jnp.mean(x * x, axis=-1, keepdims=True)
    o_ref[...] = x * lax.rsqrt(ms + EPS) * g_ref[...]


def _final_norm(x, g):
    tm = 1024
    t = x.shape[0]
    return pl.pallas_call(
        _final_norm_kernel,
        grid=(t // tm,),
        in_specs=[pl.BlockSpec((tm, D), lambda i: (i, 0)), pl.BlockSpec((1, D), lambda i: (0, 0))],
        out_specs=pl.BlockSpec((tm, D), lambda i: (i, 0)),
        out_shape=jax.ShapeDtypeStruct((t, D), F32),
        compiler_params=_cparams(("parallel",)),
        name="final_norm",
    )(x, g)


def _na_bias(rpb, rows):
    kr = min(NA_WIN_R, rows)
    r = np.arange(rows)
    r0 = np.clip(r - kr // 2, 0, rows - kr)
    row_ok = (r[None, :] >= r0[:, None]) & (r[None, :] < r0[:, None] + kr)
    dr = np.clip(r[None, :] - r[:, None] + (NA_WIN_R - 1), 0, 2 * NA_WIN_R - 2)
    c = np.arange(GRID_W)
    w0 = np.clip(c - NA_WIN_C // 2, 0, GRID_W - NA_WIN_C)
    col_ok = (c[None, :] >= w0[:, None]) & (c[None, :] < w0[:, None] + NA_WIN_C)
    dc = np.clip(c[None, :] - c[:, None] + (NA_WIN_C - 1), 0, 2 * NA_WIN_C - 2)
    oh_r = np.eye(2 * NA_WIN_R - 1, dtype=np.float32)[dr]
    oh_c = np.eye(2 * NA_WIN_C - 1, dtype=np.float32)[dc]
    by_row = jnp.einsum('rkd,hde->hrke', oh_r, rpb.astype(F32), precision=HI)
    vals = jnp.einsum('hrke,qce->hrqkc', by_row, oh_c, precision=HI)
    ok = row_ok[:, None, :, None] & col_ok[None, :, None, :]
    n = rows * GRID_W
    return jnp.where(ok[None], vals, NEG_INF).reshape(rpb.shape[0], n, n)


def _rope_tables(n_tokens):
    t = np.arange(n_tokens)
    row = (t // GRID_W).astype(np.float32)
    col = (t % GRID_W).astype(np.float32)
    half = HEAD_DIM // 2
    inv = jnp.asarray(ROPE_THETA, F32) ** (-jnp.arange(0, half, 2, dtype=F32) / half)
    ang_r = jnp.asarray(row)[:, None] * inv
    ang_c = jnp.asarray(col)[:, None] * inv
    cr, sr, cc, sc = jnp.cos(ang_r), jnp.sin(ang_r), jnp.cos(ang_c), jnp.sin(ang_c)
    cos = jnp.concatenate([cr, cr, cc, cc] * 2, axis=-1)
    sin = jnp.concatenate([-sr, sr, -sc, sc] * 2, axis=-1)
    return cos, sin


def _to_heads(x, n_batch, seq, n_heads):
    return x.reshape(n_batch, seq, n_heads, HEAD_DIM).transpose(0, 2, 1, 3)


def _ctx_lanes(cache_l):
    b, h, p, dh = cache_l.shape
    return cache_l.transpose(0, 2, 1, 3).reshape(b, p, h * dh)


def kernel(x_prompt, x_sample, cache_na_k, cache_na_v, cache_gqa_k, cache_gqa_v, c, c_ctx, w_mod, b_mod, norm1_g, norm2_g, w_in, na_rpb, q_norm_g, k_norm_g, cm_ln_g, cm_ln_b, cm_ws, cm_bs, w_out, router_w, router_b, w_gate_up, b_gate_up, w_down, b_down, final_norm_g):
    nb, seq, _ = x_prompt.shape
    db, n_tok, _ = x_sample.shape
    past = cache_na_k.shape[3]
    tp, ts = nb * seq, db * n_tok
    assert db <= CTX_ROW and n_tok % GRID_W == 0
    assert tp == ts

    cond = jnp.zeros((MOD_ROWS, D), F32).at[:db].set(c).at[CTX_ROW].set(c_ctx)
    mod = _modulation(cond, w_mod, b_mod)
    cos, sin = _rope_tables(n_tok)
    x = jnp.concatenate([x_prompt.reshape(tp, D), x_sample.reshape(ts, D)], axis=0)

    w_in16 = w_in.astype(BF16)
    w_out16 = w_out.astype(BF16)
    wg16, wu16 = _prep_wgu(w_gate_up.reshape(DEPTH * N_EXPERTS, D, 2 * D))
    wg16 = wg16.reshape(DEPTH, N_EXPERTS, D, D)
    wu16 = wu16.reshape(DEPTH, N_EXPERTS, D, D)
    wd16 = w_down.astype(BF16)
    bg = b_gate_up[..., 0::2].reshape(DEPTH, N_EXPERTS, 1, D)
    bu = b_gate_up[..., 1::2].reshape(DEPTH, N_EXPERTS, 1, D)
    bd = b_down.reshape(DEPTH, N_EXPERTS, 1, D)

    na_k, na_v, gqa_k, gqa_v = [], [], [], []
    for l in range(DEPTH):
        modr = mod[l].reshape(MOD_ROWS * 6, 1, D)
        qg2 = jnp.tile(q_norm_g[l], 2).reshape(1, LANES)
        kg2 = jnp.tile(k_norm_g[l], 2).reshape(1, LANES)
        lng = cm_ln_g[l].reshape(1, 256)
        lnb = cm_ln_b[l].reshape(1, 256)
        bsf = jnp.repeat(cm_bs[l].T, HEAD_DIM, axis=1)

        z = _inproj(x, modr, norm1_g[l].reshape(1, D), w_in16[l], tp, n_tok)
        mix_p, kbn = _ctx_mixer(z, nb, seq, qg2, kg2, lng, lnb, cm_ws[l], bsf)
        mix_s = _lat_mixer(z, tp, db, n_tok, past,
                           _ctx_lanes(cache_na_k[:, l]), _ctx_lanes(cache_na_v[:, l]),
                           _ctx_lanes(cache_gqa_k[:, l]), _ctx_lanes(cache_gqa_v[:, l]),
                           _na_bias(na_rpb[l], n_tok // GRID_W), cos, sin, qg2, kg2, lng, lnb, cm_ws[l], bsf)
        na_k.append(_to_heads(z[:tp, KA:KA + 256], nb, seq, 4))
        na_v.append(_to_heads(z[:tp, VA:VA + 256], nb, seq, 4))
        gqa_k.append(_to_heads(kbn, nb, seq, 2))
        gqa_v.append(_to_heads(z[:tp, VB:VB + LANES], nb, seq, 2))

        mix = jnp.concatenate([mix_p, mix_s], axis=0)
        xn, ht, top_i, gates = _outproj(mix, x, modr, norm2_g[l].reshape(1, D), w_out16[l],
                                  router_w[l], router_b[l].reshape(1, N_EXPERTS), tp, n_tok)

        blk_e, row_tok, row_dst = _route_groups(top_i[:, :TOP_K], 2)
        gates = gates[:, :TOP_K]
        ys = _moe(blk_e, row_tok, row_dst, ht, 2, wg16[l], wu16[l], wd16[l], bg[l], bu[l], bd[l])
        x = _combine(ys, gates, xn, modr, tp, n_tok)

    y = _final_norm(x, final_norm_g.reshape(1, D))
    return (y[:tp].reshape(nb, seq, D), y[tp:].reshape(db, n_tok, D),
            jnp.stack(na_k, axis=1), jnp.stack(na_v, axis=1), jnp.stack(gqa_k, axis=1), jnp.stack(gqa_v, axis=1))
```

```python
import functools

import jax
import jax.numpy as jnp
import numpy as np
from jax import lax
from jax.experimental import pallas as pl
from jax.experimental.pallas import tpu as pltpu

D = 1024
DEPTH = 2
GRID_W = 64
HEAD_DIM = 64
NA_WIN_R = 8
NA_WIN_C = 16
CHUNK = 128
ROPE_THETA = 10000.0
N_EXPERTS = 32
TOP_K = 4
SWIGLU_LIMIT = 7.0
SWIGLU_ALPHA = 1.702
EPS = 1e-6
MOE_BLOCK = 128
Y_SLOTS = 3
NEG_INF = -1e30
IN_WIDTH = 2048

QA, KA, VA, QB, KB, VB, CU, CV = 0, 256, 512, 768, 1280, 1408, 1536, 1792
OA, OB, OC = 0, 256, 768

LANES = 128
ROW_TILE = D // LANES
MOD_ROWS = 16
CTX_ROW = 8
VMEM_LIMIT = 56 * 1024 * 1024

F32 = jnp.float32
BF16 = jnp.bfloat16
HI = lax.Precision.HIGHEST


def _cparams(sem):
    return pltpu.CompilerParams(dimension_semantics=sem, vmem_limit_bytes=VMEM_LIMIT)


def _dot(a, b):
    return jnp.dot(a, b, preferred_element_type=F32)


def _dot_nt(a, b):
    return lax.dot_general(a, b, (((1,), (1,)), ((), ())), preferred_element_type=F32)


def _lane_lo():
    return lax.broadcasted_iota(jnp.int32, (1, LANES), 1) < HEAD_DIM


def _mod_kernel(c_ref, w_ref, b_ref, o_ref):
    c = c_ref[...]
    s = c * jax.nn.sigmoid(c)
    o_ref[0] = jnp.dot(s, w_ref[0], preferred_element_type=F32, precision=HI) + b_ref[0]


def _modulation(cond, w_mod, b_mod):
    tn = 1536
    n = w_mod.shape[-1]
    return pl.pallas_call(
        _mod_kernel,
        grid=(DEPTH, n // tn),
        in_specs=[
            pl.BlockSpec((MOD_ROWS, D), lambda l, j: (0, 0)),
            pl.BlockSpec((1, D, tn), lambda l, j: (l, 0, j)),
            pl.BlockSpec((1, 1, tn), lambda l, j: (l, 0, j)),
        ],
        out_specs=pl.BlockSpec((1, MOD_ROWS, tn), lambda l, j: (l, 0, j)),
        out_shape=jax.ShapeDtypeStruct((DEPTH, MOD_ROWS, n), F32),
        compiler_params=_cparams(("arbitrary", "arbitrary")),
        name="modulation",
    )(cond, w_mod, b_mod.reshape(DEPTH, 1, n))


def _mod_row(i, tm, tp, ts_per_batch):
    start = i * tm
    return jnp.where(start < tp, CTX_ROW, (start - tp) // ts_per_batch)


def _mod_spec(k, tm, tp, ts_per_batch):
    return pl.BlockSpec((1, 1, D), lambda i: (_mod_row(i, tm, tp, ts_per_batch) * 6 + k, 0, 0))


def _x_specs(tm, n_top, bot_off, width=D):
    return [pl.BlockSpec((tm, width), lambda i: (jnp.minimum(i, n_top - 1), 0)),
            pl.BlockSpec((tm, width), lambda i: (jnp.maximum(i - n_top, 0) + bot_off, 0))]


def _inproj_kernel(n_top, xa_ref, xb_ref, shift_ref, scale_ref, g_ref, w_ref, z_ref):
    x = jnp.where(pl.program_id(0) < n_top, xa_ref[...], xb_ref[...])
    ms = jnp.mean(x * x, axis=-1, keepdims=True)
    y = x * lax.rsqrt(ms + EPS) * g_ref[...]
    h = y * (1.0 + scale_ref[0]) + shift_ref[0]
    z_ref[...] = _dot(h.astype(BF16), w_ref[...])


def _inproj(xa, xb, bot_off_rows, t, modr, g, w, tp, ts_per_batch):
    tm = 512
    n_top = tp // tm
    return pl.pallas_call(
        functools.partial(_inproj_kernel, n_top),
        grid=(t // tm,),
        in_specs=_x_specs(tm, n_top, bot_off_rows // tm) + [
            _mod_spec(0, tm, tp, ts_per_batch),
            _mod_spec(1, tm, tp, ts_per_batch),
            pl.BlockSpec((1, D), lambda i: (0, 0)),
            pl.BlockSpec((D, IN_WIDTH), lambda i: (0, 0)),
        ],
        out_specs=pl.BlockSpec((tm, IN_WIDTH), lambda i: (i, 0)),
        out_shape=jax.ShapeDtypeStruct((t, IN_WIDTH), F32),
        compiler_params=_cparams(("parallel",)),
        name="inproj",
    )(xa, xb, modr, modr, g, w)


def _head_rms(x, g2):
    lo = _lane_lo()
    x2 = x * x
    s_lo = jnp.sum(jnp.where(lo, x2, 0.0), axis=-1, keepdims=True)
    s_hi = jnp.sum(jnp.where(lo, 0.0, x2), axis=-1, keepdims=True)
    ms = jnp.where(lo, s_lo, s_hi) * (1.0 / HEAD_DIM)
    return x * lax.rsqrt(ms + EPS) * g2


def _softmax_pv(q16, ks, vs, biases):
    ss = []
    for k, b in zip(ks, biases):
        s = _dot_nt(q16, k)
        if b is not None:
            s = s + b
        ss.append(s)
    m = ss[0].max(axis=-1, keepdims=True)
    for s in ss[1:]:
        m = jnp.maximum(m, s.max(axis=-1, keepdims=True))
    den = None
    acc = None
    for s, v in zip(ss, vs):
        e = jnp.exp(s - m)
        d = e.sum(axis=-1, keepdims=True)
        o = _dot(e.astype(BF16), v)
        den = d if den is None else den + d
        acc = o if acc is None else acc + o
    return acc / den


def _attend_pair(q, ks_lo, vs_lo, ks_hi, vs_hi, b_lo, b_hi):
    lo = _lane_lo()
    o_lo = _softmax_pv(jnp.where(lo, q, 0.0).astype(BF16), ks_lo, vs_lo, b_lo)
    o_hi = _softmax_pv(jnp.where(lo, 0.0, q).astype(BF16), ks_hi, vs_hi, b_hi)
    return jnp.where(lo, o_lo, o_hi)


def _gelu(x):
    c = np.sqrt(2.0 / np.pi).astype(np.float32)
    return x * (0.5 * (1.0 + jnp.tanh(c * (x + 0.044715 * (x * x * x)))))


def _layer_norm(x, g, b):
    mu = jnp.mean(x, axis=-1, keepdims=True)
    xc = x - mu
    var = jnp.mean(xc * xc, axis=-1, keepdims=True)
    return xc * lax.rsqrt(var + EPS) * g + b


def _chunk_mlp(cu, cv, lng_ref, lnb_ref, ws_ref, bsf_ref, out_ref, col0):
    lo = _lane_lo()
    u = _gelu(cu)
    v = _layer_norm(_gelu(cv), lng_ref[...], lnb_ref[...]).astype(BF16)
    s = cu.shape[0]
    for n in range(s // CHUNK):
        rows = slice(n * CHUNK, (n + 1) * CHUNK)
        for jb in range(2):
            cols = slice(jb * LANES, (jb + 1) * LANES)
            vb = v[rows, cols]
            m_lo = _dot(ws_ref[2 * jb].astype(BF16), vb)
            m_hi = _dot(ws_ref[2 * jb + 1].astype(BF16), vb)
            mixed = jnp.where(lo, m_lo, m_hi) + bsf_ref[:, cols]
            out_ref[rows, col0 + jb * LANES:col0 + (jb + 1) * LANES] = (u[rows, cols] * mixed).astype(out_ref.dtype)


def _ctx_mixer_kernel(z_ref, qg_ref, kg_ref, lng_ref, lnb_ref, ws_ref, bsf_ref, mix_ref, kbn_ref):
    scale = HEAD_DIM ** -0.5
    for j in range(2):
        q = z_ref[:, QA + LANES * j:QA + LANES * (j + 1)] * scale
        k = [z_ref[:, KA + LANES * j:KA + LANES * (j + 1)].astype(BF16)]
        v = [z_ref[:, VA + LANES * j:VA + LANES * (j + 1)].astype(BF16)]
        o = _attend_pair(q, k, v, k, v, [None], [None])
        mix_ref[:, OA + LANES * j:OA + LANES * (j + 1)] = o.astype(mix_ref.dtype)

    kb = _head_rms(z_ref[:, KB:KB + LANES], kg_ref[...])
    kbn_ref[...] = kb
    vb = z_ref[:, VB:VB + LANES]
    k_same = [kb.astype(BF16)]
    k_swap = [pltpu.roll(kb, HEAD_DIM, 1).astype(BF16)]
    v_same = [vb.astype(BF16)]
    v_swap = [pltpu.roll(vb, HEAD_DIM, 1).astype(BF16)]
    for j in range(4):
        q = _head_rms(z_ref[:, QB + LANES * j:QB + LANES * (j + 1)], qg_ref[...]) * scale
        if j // 2 == 0:
            o = _attend_pair(q, k_same, v_same, k_swap, v_swap, [None], [None])
        else:
            o = _attend_pair(q, k_swap, v_swap, k_same, v_same, [None], [None])
        mix_ref[:, OB + LANES * j:OB + LANES * (j + 1)] = o.astype(mix_ref.dtype)

    _chunk_mlp(z_ref[:, CU:CU + 256], z_ref[:, CV:CV + 256], lng_ref, lnb_ref, ws_ref, bsf_ref, mix_ref, OC)


def _ctx_mixer(z, n_batch, seq, qg2, kg2, lng, lnb, ws, bsf):
    small = lambda shape: pl.BlockSpec(shape, lambda b: (0,) * len(shape))
    return pl.pallas_call(
        _ctx_mixer_kernel,
        grid=(n_batch,),
        in_specs=[
            pl.BlockSpec((seq, IN_WIDTH), lambda b: (b, 0)),
            small((1, LANES)), small((1, LANES)), small((1, 256)), small((1, 256)),
            small((4, CHUNK, CHUNK)), small((CHUNK, 256)),
        ],
        out_specs=[
            pl.BlockSpec((seq, D), lambda b: (b, 0)),
            pl.BlockSpec((seq, LANES), lambda b: (b, 0)),
        ],
        out_shape=[
            jax.ShapeDtypeStruct((n_batch * seq, D), BF16),
            jax.ShapeDtypeStruct((n_batch * seq, LANES), F32),
        ],
        compiler_params=_cparams(("parallel",)),
        name="ctx_mixer",
    )(z, qg2, kg2, lng, lnb, ws, bsf)


def _rope(x, cos, sin):
    first = (lax.broadcasted_iota(jnp.int32, (1, LANES), 1) % 32) < 16
    partner = jnp.where(first, pltpu.roll(x, LANES - 16, 1), pltpu.roll(x, 16, 1))
    return x * cos + partner * sin


def _lat_mixer_kernel(zq_ref, zkv_ref, nakc_ref, navc_ref, gkc_ref, gvc_ref, bias_ref,
                      cosq_ref, sinq_ref, cosk_ref, sink_ref,
                      qg_ref, kg_ref, lng_ref, lnb_ref, ws_ref, bsf_ref,
                      mix_ref,
                      kl_ref, kls_ref, vl_ref, vls_ref, kc_ref, kcs_ref, vc_ref, vcs_ref):
    scale = HEAD_DIM ** -0.5

    @pl.when(pl.program_id(1) == 0)
    def _():
        kb = _rope(_head_rms(zkv_ref[:, KB:KB + LANES], kg_ref[...]), cosk_ref[...], sink_ref[...])
        vb = zkv_ref[:, VB:VB + LANES]
        kl_ref[...] = kb.astype(BF16)
        kls_ref[...] = pltpu.roll(kb, HEAD_DIM, 1).astype(BF16)
        vl_ref[...] = vb.astype(BF16)
        vls_ref[...] = pltpu.roll(vb, HEAD_DIM, 1).astype(BF16)
        kc = gkc_ref[0]
        vc = gvc_ref[0]
        kc_ref[...] = kc.astype(BF16)
        kcs_ref[...] = pltpu.roll(kc, HEAD_DIM, 1).astype(BF16)
        vc_ref[...] = vc.astype(BF16)
        vcs_ref[...] = pltpu.roll(vc, HEAD_DIM, 1).astype(BF16)

    for j in range(2):
        cols = slice(LANES * j, LANES * (j + 1))
        q = zq_ref[:, QA + LANES * j:QA + LANES * (j + 1)] * scale
        ks = [zkv_ref[:, KA + LANES * j:KA + LANES * (j + 1)].astype(BF16), nakc_ref[0, :, cols].astype(BF16)]
        vs = [zkv_ref[:, VA + LANES * j:VA + LANES * (j + 1)].astype(BF16), navc_ref[0, :, cols].astype(BF16)]
        o = _attend_pair(q, ks, vs, ks, vs, [bias_ref[2 * j], None], [bias_ref[2 * j + 1], None])
        mix_ref[:, OA + LANES * j:OA + LANES * (j + 1)] = o.astype(mix_ref.dtype)

    same = ([kl_ref[...], kc_ref[...]], [vl_ref[...], vc_ref[...]])
    swap = ([kls_ref[...], kcs_ref[...]], [vls_ref[...], vcs_ref[...]])
    for j in range(4):
        q = _head_rms(zq_ref[:, QB + LANES * j:QB + LANES * (j + 1)], qg_ref[...])
        q = _rope(q, cosq_ref[...], sinq_ref[...]) * scale
        lo_kv, hi_kv = (same, swap) if j // 2 == 0 else (swap, same)
        o = _attend_pair(q, lo_kv[0], lo_kv[1], hi_kv[0], hi_kv[1], [None, None], [None, None])
        mix_ref[:, OB + LANES * j:OB + LANES * (j + 1)] = o.astype(mix_ref.dtype)

    _chunk_mlp(zq_ref[:, CU:CU + 256], zq_ref[:, CV:CV + 256], lng_ref, lnb_ref, ws_ref, bsf_ref, mix_ref, OC)


def _lat_mixer(z, row0, n_batch, n_tok, past, nakc, navc, gkc, gvc, bias, cos, sin, qg2, kg2, lng, lnb, ws, bsf):
    tq = 256
    nq = n_tok // tq
    small = lambda shape: pl.BlockSpec(shape, lambda b, t: (0,) * len(shape))
    qblk0 = row0 // tq
    kvblk0 = row0 // n_tok
    return pl.pallas_call(
        _lat_mixer_kernel,
        grid=(n_batch, nq),
        in_specs=[
            pl.BlockSpec((tq, IN_WIDTH), lambda b, t: (qblk0 + b * nq + t, 0)),
            pl.BlockSpec((n_tok, IN_WIDTH), lambda b, t: (kvblk0 + b, 0)),
            pl.BlockSpec((1, past, 256), lambda b, t: (b, 0, 0)),
            pl.BlockSpec((1, past, 256), lambda b, t: (b, 0, 0)),
            pl.BlockSpec((1, past, LANES), lambda b, t: (b, 0, 0)),
            pl.BlockSpec((1, past, LANES), lambda b, t: (b, 0, 0)),
            pl.BlockSpec((4, tq, n_tok), lambda b, t: (0, t, 0)),
            pl.BlockSpec((tq, LANES), lambda b, t: (t, 0)),
            pl.BlockSpec((tq, LANES), lambda b, t: (t, 0)),
            small((n_tok, LANES)), small((n_tok, LANES)),
            small((1, LANES)), small((1, LANES)), small((1, 256)), small((1, 256)),
            small((4, CHUNK, CHUNK)), small((CHUNK, 256)),
        ],
        out_specs=pl.BlockSpec((tq, D), lambda b, t: (b * nq + t, 0)),
        out_shape=jax.ShapeDtypeStruct((n_batch * n_tok, D), BF16),
        scratch_shapes=[pltpu.VMEM((n_tok, LANES), BF16)] * 4 + [pltpu.VMEM((past, LANES), BF16)] * 4,
        compiler_params=_cparams(("parallel", "arbitrary")),
        name="lat_mixer",
    )(z, z, nakc, navc, gkc, gvc, bias, cos, sin, cos, sin, qg2, kg2, lng, lnb, ws, bsf)


def _outproj_kernel(n_top, mixa_ref, mixb_ref, xa_ref, xb_ref, g1_ref, shift_ref, scale_ref, g_ref, w_ref, rw_ref, rb_ref,
                    xn_ref, ht_ref, ti_ref, gt_ref):
    top = pl.program_id(0) < n_top
    x = jnp.where(top, xa_ref[...], xb_ref[...])
    x = x + g1_ref[0] * _dot(jnp.where(top, mixa_ref[...], mixb_ref[...]), w_ref[...])
    xn_ref[...] = x
    ms = jnp.mean(x * x, axis=-1, keepdims=True)
    h = x * lax.rsqrt(ms + EPS) * g_ref[...]
    h = h * (1.0 + scale_ref[0]) + shift_ref[0]
    logits = jnp.dot(h, rw_ref[...], preferred_element_type=F32, precision=HI) + rb_ref[...]

    lane = lax.broadcasted_iota(jnp.int32, logits.shape, 1)
    top_i = jnp.zeros_like(lane)
    top_v = []
    for k in range(TOP_K):
        m = jnp.max(logits, axis=-1, keepdims=True)
        idx = jnp.min(jnp.where(logits == m, lane, N_EXPERTS), axis=-1, keepdims=True)
        logits = jnp.where(lane == idx, -jnp.inf, logits)
        top_i = jnp.where(lane == k, idx, top_i)
        top_v.append(m)
    es = [jnp.exp(v - top_v[0]) for v in top_v]
    den = (es[0] + es[1]) + (es[2] + es[3])
    gates = jnp.zeros(logits.shape, F32)
    for k in range(TOP_K):
        gates = jnp.where(lane == k, es[k] / den, gates)
    ti_ref[...] = top_i
    gt_ref[...] = gates

    for c in range(ROW_TILE):
        ht_ref[pl.ds(c, h.shape[0], stride=ROW_TILE), :] = h[:, LANES * c:LANES * (c + 1)]


def _outproj(mix_a, mix_b, xa, xb, bot_off_rows, modr, g, w, rw, rb, tp, ts_per_batch):
    tm = 512
    t = mix_a.shape[0] + mix_b.shape[0]
    n_top = tp // tm
    return pl.pallas_call(
        functools.partial(_outproj_kernel, n_top),
        grid=(t // tm,),
        in_specs=_x_specs(tm, n_top, 0) + _x_specs(tm, n_top, bot_off_rows // tm) + [
            _mod_spec(2, tm, tp, ts_per_batch),
            _mod_spec(3, tm, tp, ts_per_batch),
            _mod_spec(4, tm, tp, ts_per_batch),
            pl.BlockSpec((1, D), lambda i: (0, 0)),
            pl.BlockSpec((D, D), lambda i: (0, 0)),
            pl.BlockSpec((D, N_EXPERTS), lambda i: (0, 0)),
            pl.BlockSpec((1, N_EXPERTS), lambda i: (0, 0)),
        ],
        out_specs=[
            pl.BlockSpec((tm, D), lambda i: (i, 0)),
            pl.BlockSpec((tm * ROW_TILE, LANES), lambda i: (i, 0)),
            pl.BlockSpec((tm, N_EXPERTS), lambda i: (i, 0)),
            pl.BlockSpec((tm, N_EXPERTS), lambda i: (i, 0)),
        ],
        out_shape=[
            jax.ShapeDtypeStruct((t, D), F32),
            jax.ShapeDtypeStruct((t * ROW_TILE, LANES), F32),
            jax.ShapeDtypeStruct((t, N_EXPERTS), jnp.int32),
            jax.ShapeDtypeStruct((t, N_EXPERTS), F32),
        ],
        compiler_params=_cparams(("parallel",)),
        name="outproj",
    )(mix_a, mix_b, xa, xb, modr, modr, modr, g, w, rw, rb)


def _prep_wgu_kernel(w_ref, p_ref, wg_ref, wu_ref):
    for j in range(w_ref.shape[2] // 256):
        w = w_ref[0, :, 256 * j:256 * (j + 1)].astype(BF16)
        sel = _dot(w, p_ref[...])
        wg_ref[0, :, LANES * j:LANES * (j + 1)] = sel[:, :LANES].astype(BF16)
        wu_ref[0, :, LANES * j:LANES * (j + 1)] = sel[:, LANES:].astype(BF16)


def _prep_wgu(w):
    e, d, f2 = w.shape
    perm = np.zeros((256, 256), np.float32)
    perm[2 * np.arange(LANES), np.arange(LANES)] = 1.0
    perm[2 * np.arange(LANES) + 1, LANES + np.arange(LANES)] = 1.0
    out = jax.ShapeDtypeStruct((e, d, f2 // 2), BF16)
    return pl.pallas_call(
        _prep_wgu_kernel,
        grid=(e,),
        in_specs=[pl.BlockSpec((1, d, f2), lambda i: (i, 0, 0)), pl.BlockSpec((256, 256), lambda i: (0, 0))],
        out_specs=[pl.BlockSpec((1, d, f2 // 2), lambda i: (i, 0, 0))] * 2,
        out_shape=[out, out],
        compiler_params=_cparams(("parallel",)),
        name="prep_wgu",
    )(w, jnp.asarray(perm, BF16))


def _moe_kernel(blk_e_ref, row_tok_ref, row_dst_ref, ht_ref, wg_ref, wu_ref, wd_ref, bg_ref, bu_ref, bd_ref,
                out_ref, xg_ref, ybuf_ref, sem):
    i = pl.program_id(1)
    nb = pl.num_programs(1)
    step = pl.program_id(0) * nb + i
    n = pl.num_programs(0) * nb
    cur = step % 2
    nxt = 1 - cur
    ycur = step % Y_SLOTS
    yprev = (step + Y_SLOTS - 1) % Y_SLOTS
    blk = MOE_BLOCK * ROW_TILE

    def slot_copy(s):
        return pltpu.make_async_copy(ybuf_ref.at[pl.ds(s * blk, blk)], out_ref.at[pl.ds(0, blk)], sem.at[s])

    def gather_row(base, r, s):
        tok = row_tok_ref[base + r]
        xg_ref[pl.ds(pl.multiple_of(s * blk + r * ROW_TILE, ROW_TILE), ROW_TILE), :] = (
            ht_ref[pl.ds(pl.multiple_of(tok * ROW_TILE, ROW_TILE), ROW_TILE), :])

    def send_row(base, r, s, priority=0):
        dst = row_dst_ref[base + r]
        pltpu.make_async_copy(
            ybuf_ref.at[pl.ds(pl.multiple_of(s * blk + r * ROW_TILE, ROW_TILE), ROW_TILE)],
            out_ref.at[pl.ds(pl.multiple_of(dst * ROW_TILE, ROW_TILE), ROW_TILE)], sem.at[s]).start(priority)

    @pl.when(step == 0)
    def _():
        ybuf_ref[...] = jnp.zeros(ybuf_ref.shape, F32)
        for s in range(Y_SLOTS - 1):
            pltpu.make_async_copy(ybuf_ref.at[pl.ds(s * blk, blk)], out_ref.at[pl.ds((n + 1 + s) * blk, blk)],
                                  sem.at[s]).start()

    @pl.when(i == 0)
    def _():
        def first(r, carry):
            gather_row(step * MOE_BLOCK, r, cur)
            return carry

        lax.fori_loop(0, MOE_BLOCK, first, 0, unroll=8)

    slot_copy(ycur).wait()

    x = jnp.concatenate(
        [xg_ref[pl.ds(cur * blk + c, MOE_BLOCK, stride=ROW_TILE), :].astype(BF16) for c in range(ROW_TILE)], axis=1)
    next_blk = jnp.where(i + 1 < nb, step + 1, step)
    for r in range(MOE_BLOCK):
        gather_row(next_blk * MOE_BLOCK, r, nxt)
    for r in range(MOE_BLOCK):
        send_row(step * MOE_BLOCK, r, yprev, priority=r % 2)
    g = _dot(x, wg_ref[0]) + bg_ref[0]
    u = _dot(x, wu_ref[0]) + bu_ref[0]
    g = jnp.minimum(g, SWIGLU_LIMIT)
    u = jnp.clip(u, -SWIGLU_LIMIT, SWIGLU_LIMIT)
    act = (u + 1.0) * (g * jax.nn.sigmoid(SWIGLU_ALPHA * g))
    y = _dot(act.astype(BF16), wd_ref[0]) + bd_ref[0]
    for c in range(ROW_TILE):
        ybuf_ref[pl.ds(ycur * blk + c, MOE_BLOCK, stride=ROW_TILE), :] = y[:, LANES * c:LANES * (c + 1)]

    @pl.when(step == n - 1)
    def _():
        def last(r, carry):
            send_row(n * MOE_BLOCK, r, ycur)
            return carry

        lax.fori_loop(0, MOE_BLOCK, last, 0, unroll=8)
        for s in range(Y_SLOTS):
            slot_copy(s).wait()


def _moe(blk_e, row_tok, row_dst, ht, n_groups, wg, wu, wd, bg, bu, bd):
    nb = blk_e.shape[0] // n_groups
    tg = ht.shape[0] // ROW_TILE // n_groups
    ew = lambda g, i, be, rt, rd: (be[g * nb + i], 0, 0)
    return pl.pallas_call(
        _moe_kernel,
        grid_spec=pltpu.PrefetchScalarGridSpec(
            num_scalar_prefetch=3,
            grid=(n_groups, nb),
            in_specs=[
                pl.BlockSpec((tg * ROW_TILE, LANES), lambda g, i, be, rt, rd: (g, 0), pipeline_mode=pl.Buffered(1)),
                pl.BlockSpec((1, D, D), ew),
                pl.BlockSpec((1, D, D), ew),
                pl.BlockSpec((1, D, D), ew),
                pl.BlockSpec((1, 1, D), ew),
                pl.BlockSpec((1, 1, D), ew),
                pl.BlockSpec((1, 1, D), ew),
            ],
            out_specs=pl.BlockSpec(memory_space=pl.ANY),
            scratch_shapes=[
                pltpu.VMEM((2 * MOE_BLOCK * ROW_TILE, LANES), F32),
                pltpu.VMEM((Y_SLOTS * MOE_BLOCK * ROW_TILE, LANES), F32),
                pltpu.SemaphoreType.DMA((Y_SLOTS,)),
            ],
        ),
        out_shape=jax.ShapeDtypeStruct(((n_groups * nb + Y_SLOTS) * MOE_BLOCK * ROW_TILE, LANES), F32),
        compiler_params=_cparams(("arbitrary", "arbitrary")),
        name="moe_experts",
    )(blk_e, row_tok, row_dst, ht, wg, wu, wd, bg, bu, bd)


def _route(top_i, group, n_groups):
    tg = top_i.shape[0]
    n_assign = tg * TOP_K
    assert n_assign <= 1 << 16 and n_assign % MOE_BLOCK == 0
    flat_e = top_i.reshape(-1).astype(jnp.int32)
    experts = jnp.arange(N_EXPERTS, dtype=jnp.int32)
    counts = jnp.sum((flat_e[:, None] == experts[None, :]).astype(jnp.int32), axis=0)
    padded = (counts + MOE_BLOCK - 1) // MOE_BLOCK * MOE_BLOCK
    pad_end = jnp.cumsum(padded)
    n_blocks = n_assign // MOE_BLOCK + N_EXPERTS
    real_keys = (flat_e << 17) | jnp.arange(n_assign, dtype=jnp.int32)
    pad_id = experts[:, None] * MOE_BLOCK + jnp.arange(MOE_BLOCK, dtype=jnp.int32)[None, :]
    used = jnp.arange(MOE_BLOCK, dtype=jnp.int32)[None, :] < (padded - counts)[:, None]
    pad_keys = (jnp.where(used, experts[:, None], 63) << 17) | (1 << 16) | pad_id
    keys = jnp.sort(jnp.concatenate([real_keys, pad_keys.reshape(-1)]))
    is_pad = ((keys >> 16) & 1) == 1
    payload = keys & 0xFFFF
    tok = payload // TOP_K
    t_all = tg * n_groups
    row_tok = jnp.where(is_pad, 0, tok)
    row_dst = jnp.where(is_pad, TOP_K * t_all + group * (N_EXPERTS * MOE_BLOCK) + payload,
                        (payload % TOP_K) * t_all + group * tg + tok)
    blk_start = jnp.arange(n_blocks, dtype=jnp.int32) * MOE_BLOCK
    blk_e = jnp.sum((pad_end[None, :] <= blk_start[:, None]).astype(jnp.int32), axis=1)
    blk_e = jnp.minimum(blk_e, N_EXPERTS - 1)
    return blk_e, row_tok, row_dst


def _route_groups(top_i, n_groups):
    tg = top_i.shape[0] // n_groups
    parts = [_route(top_i[g * tg:(g + 1) * tg], g, n_groups) for g in range(n_groups)]
    blk_e = jnp.concatenate([p[0] for p in parts])
    row_tok = jnp.concatenate([p[1] for p in parts])
    spare = row_tok.shape[0] + jnp.arange(MOE_BLOCK, dtype=jnp.int32)
    row_dst = jnp.concatenate([spare] + [p[2] for p in parts])
    return blk_e, row_tok, row_dst


def _combine_kernel(final, y0_ref, y1_ref, y2_ref, y3_ref, gates_ref, x_ref, g2_ref, gf_ref, o_ref):
    g = gates_ref[...]
    tm = x_ref.shape[0]
    for c in range(ROW_TILE):
        rows = pl.ds(c, tm, stride=ROW_TILE)
        cols = slice(LANES * c, LANES * (c + 1))
        y = ((y0_ref[rows, :] * g[:, 0:1] + y1_ref[rows, :] * g[:, 1:2])
             + (y2_ref[rows, :] * g[:, 2:3] + y3_ref[rows, :] * g[:, 3:4]))
        o_ref[:, cols] = x_ref[:, cols] + g2_ref[0][:, cols] * y
    if final:
        x = o_ref[...]
        ms = jnp.mean(x * x, axis=-1, keepdims=True)
        o_ref[...] = x * lax.rsqrt(ms + EPS) * gf_ref[...]


def _combine(ys, gates, x, modr, tp, ts_per_batch, row0, n_rows, final_g=None):
    tm = 512
    nt_all = x.shape[0] // tm
    i0 = row0 // tm
    yspec = lambda k: pl.BlockSpec((tm * ROW_TILE, LANES), lambda i: (k * nt_all + i0 + i, 0))
    mod = pl.BlockSpec((1, 1, D), lambda i: (_mod_row(i0 + i, tm, tp, ts_per_batch) * 6 + 5, 0, 0))
    gf = jnp.ones((1, D), F32) if final_g is None else final_g
    return pl.pallas_call(
        functools.partial(_combine_kernel, final_g is not None),
        grid=(n_rows // tm,),
        in_specs=[yspec(0), yspec(1), yspec(2), yspec(3), pl.BlockSpec((tm, TOP_K), lambda i: (i0 + i, 0)),
                  pl.BlockSpec((tm, D), lambda i: (i0 + i, 0)), mod, pl.BlockSpec((1, D), lambda i: (0, 0))],
        out_specs=pl.BlockSpec((tm, D), lambda i: (i, 0)),
        out_shape=jax.ShapeDtypeStruct((n_rows, D), F32),
        compiler_params=_cparams(("parallel",)),
        name="moe_combine",
    )(ys, ys, ys, ys, gates, x, modr, gf)


def _na_bias(rpb, rows):
    kr = min(NA_WIN_R, rows)
    r = np.arange(rows)
    r0 = np.clip(r - kr // 2, 0, rows - kr)
    row_ok = (r[None, :] >= r0[:, None]) & (r[None, :] < r0[:, None] + kr)
    dr = np.clip(r[None, :] - r[:, None] + (NA_WIN_R - 1), 0, 2 * NA_WIN_R - 2)
    c = np.arange(GRID_W)
    w0 = np.clip(c - NA_WIN_C // 2, 0, GRID_W - NA_WIN_C)
    col_ok = (c[None, :] >= w0[:, None]) & (c[None, :] < w0[:, None] + NA_WIN_C)
    dc = np.clip(c[None, :] - c[:, None] + (NA_WIN_C - 1), 0, 2 * NA_WIN_C - 2)
    oh_r = np.eye(2 * NA_WIN_R - 1, dtype=np.float32)[dr]
    oh_c = np.eye(2 * NA_WIN_C - 1, dtype=np.float32)[dc]
    by_row = jnp.einsum('rkd,hde->hrke', oh_r, rpb.astype(F32), precision=HI)
    vals = jnp.einsum('hrke,qce->hrqkc', by_row, oh_c, precision=HI)
    ok = row_ok[:, None, :, None] & col_ok[None, :, None, :]
    n = rows * GRID_W
    return jnp.where(ok[None], vals, NEG_INF).reshape(rpb.shape[0], n, n)


def _rope_tables(n_tokens):
    t = np.arange(n_tokens)
    row = (t // GRID_W).astype(np.float32)
    col = (t % GRID_W).astype(np.float32)
    half = HEAD_DIM // 2
    inv = jnp.asarray(ROPE_THETA, F32) ** (-jnp.arange(0, half, 2, dtype=F32) / half)
    ang_r = jnp.asarray(row)[:, None] * inv
    ang_c = jnp.asarray(col)[:, None] * inv
    cr, sr, cc, sc = jnp.cos(ang_r), jnp.sin(ang_r), jnp.cos(ang_c), jnp.sin(ang_c)
    cos = jnp.concatenate([cr, cr, cc, cc] * 2, axis=-1)
    sin = jnp.concatenate([-sr, sr, -sc, sc] * 2, axis=-1)
    return cos, sin


def _to_heads(x, n_batch, seq, n_heads):
    return x.reshape(n_batch, seq, n_heads, HEAD_DIM).transpose(0, 2, 1, 3)


def _ctx_lanes(cache_l):
    b, h, p, dh = cache_l.shape
    return cache_l.transpose(0, 2, 1, 3).reshape(b, p, h * dh)


def kernel(x_prompt, x_sample, cache_na_k, cache_na_v, cache_gqa_k, cache_gqa_v, c, c_ctx, w_mod, b_mod, norm1_g, norm2_g, w_in, na_rpb, q_norm_g, k_norm_g, cm_ln_g, cm_ln_b, cm_ws, cm_bs, w_out, router_w, router_b, w_gate_up, b_gate_up, w_down, b_down, final_norm_g):
    nb, seq, _ = x_prompt.shape
    db, n_tok, _ = x_sample.shape
    past = cache_na_k.shape[3]
    tp, ts = nb * seq, db * n_tok
    assert db <= CTX_ROW and n_tok % GRID_W == 0
    assert tp == ts

    cond = jnp.zeros((MOD_ROWS, D), F32).at[:db].set(c).at[CTX_ROW].set(c_ctx)
    mod = _modulation(cond, w_mod, b_mod)
    cos, sin = _rope_tables(n_tok)
    xa, xb, xb_row0 = x_prompt.reshape(tp, D), x_sample.reshape(ts, D), 0

    w_in16 = w_in.astype(BF16)
    w_out16 = w_out.astype(BF16)
    wg16, wu16 = _prep_wgu(w_gate_up.reshape(DEPTH * N_EXPERTS, D, 2 * D))
    wd16 = w_down.astype(BF16).reshape(DEPTH * N_EXPERTS, D, D)
    bg = b_gate_up[..., 0::2].reshape(DEPTH * N_EXPERTS, 1, D)
    bu = b_gate_up[..., 1::2].reshape(DEPTH * N_EXPERTS, 1, D)
    bd = b_down.reshape(DEPTH * N_EXPERTS, 1, D)

    na_k, na_v, gqa_k, gqa_v = [], [], [], []
    for l in range(DEPTH):
        modr = mod[l].reshape(MOD_ROWS * 6, 1, D)
        qg2 = jnp.tile(q_norm_g[l], 2).reshape(1, LANES)
        kg2 = jnp.tile(k_norm_g[l], 2).reshape(1, LANES)
        lng = cm_ln_g[l].reshape(1, 256)
        lnb = cm_ln_b[l].reshape(1, 256)
        bsf = jnp.repeat(cm_bs[l].T, HEAD_DIM, axis=1)

        z = _inproj(xa, xb, xb_row0, tp + ts, modr, norm1_g[l].reshape(1, D), w_in16[l], tp, n_tok)
        mix_p, kbn = _ctx_mixer(z, nb, seq, qg2, kg2, lng, lnb, cm_ws[l], bsf)
        mix_s = _lat_mixer(z, tp, db, n_tok, past,
                           _ctx_lanes(cache_na_k[:, l]), _ctx_lanes(cache_na_v[:, l]),
                           _ctx_lanes(cache_gqa_k[:, l]), _ctx_lanes(cache_gqa_v[:, l]),
                           _na_bias(na_rpb[l], n_tok // GRID_W), cos, sin, qg2, kg2, lng, lnb, cm_ws[l], bsf)
        na_k.append(_to_heads(z[:tp, KA:KA + 256], nb, seq, 4))
        na_v.append(_to_heads(z[:tp, VA:VA + 256], nb, seq, 4))
        gqa_k.append(_to_heads(kbn, nb, seq, 2))
        gqa_v.append(_to_heads(z[:tp, VB:VB + LANES], nb, seq, 2))

        xn, ht, top_i, gates = _outproj(mix_p, mix_s, xa, xb, xb_row0, modr, norm2_g[l].reshape(1, D), w_out16[l],
                                  router_w[l], router_b[l].reshape(1, N_EXPERTS), tp, n_tok)

        blk_e, row_tok, row_dst = _route_groups(top_i[:, :TOP_K], 2)
        gates = gates[:, :TOP_K]
        ys = _moe(blk_e + l * N_EXPERTS, row_tok, row_dst, ht, 2, wg16, wu16, wd16, bg, bu, bd)
        if l + 1 < DEPTH:
            x = _combine(ys, gates, xn, modr, tp, n_tok, 0, tp + ts)
            xa, xb, xb_row0 = x, x, tp

    gf = final_norm_g.reshape(1, D)
    y_p = _combine(ys, gates, xn, modr, tp, n_tok, 0, tp, gf)
    y_s = _combine(ys, gates, xn, modr, tp, n_tok, tp, ts, gf)
    return (y_p.reshape(nb, seq, D), y_s.reshape(db, n_tok, D),
            jnp.stack(na_k, axis=1), jnp.stack(na_v, axis=1), jnp.stack(gqa_k, axis=1), jnp.stack(gqa_v, axis=1))
```

```python
import functools

import jax
import jax.numpy as jnp
import numpy as np
from jax import lax
from jax.experimental import pallas as pl
from jax.experimental.pallas import tpu as pltpu

D = 1024
DEPTH = 2
GRID_W = 64
HEAD_DIM = 64
NA_WIN_R = 8
NA_WIN_C = 16
CHUNK = 128
ROPE_THETA = 10000.0
N_EXPERTS = 32
TOP_K = 4
SWIGLU_LIMIT = 7.0
SWIGLU_ALPHA = 1.702
EPS = 1e-6
MOE_BLOCK = 128
Y_SLOTS = 3
NEG_INF = -1e30
IN_WIDTH = 2048

QA, KA, VA, QB, KB, VB, CU, CV = 0, 256, 512, 768, 1280, 1408, 1536, 1792
OA, OB, OC = 0, 256, 768

LANES = 128
ROW_TILE = D // LANES
MOD_ROWS = 16
CTX_ROW = 8
VMEM_LIMIT = 56 * 1024 * 1024

F32 = jnp.float32
BF16 = jnp.bfloat16
HI = lax.Precision.HIGHEST


def _cparams(sem):
    return pltpu.CompilerParams(dimension_semantics=sem, vmem_limit_bytes=VMEM_LIMIT)


def _dot(a, b):
    return jnp.dot(a, b, preferred_element_type=F32)


def _dot_nt(a, b):
    return lax.dot_general(a, b, (((1,), (1,)), ((), ())), preferred_element_type=F32)


def _lane_lo():
    return lax.broadcasted_iota(jnp.int32, (1, LANES), 1) < HEAD_DIM


def _mod_kernel(c_ref, w_ref, b_ref, o_ref):
    c = c_ref[...]
    s = c * jax.nn.sigmoid(c)
    o_ref[0] = jnp.dot(s, w_ref[0], preferred_element_type=F32, precision=HI) + b_ref[0]


def _modulation(cond, w_mod, b_mod):
    tn = 1536
    n = w_mod.shape[-1]
    return pl.pallas_call(
        _mod_kernel,
        grid=(DEPTH, n // tn),
        in_specs=[
            pl.BlockSpec((MOD_ROWS, D), lambda l, j: (0, 0)),
            pl.BlockSpec((1, D, tn), lambda l, j: (l, 0, j)),
            pl.BlockSpec((1, 1, tn), lambda l, j: (l, 0, j)),
        ],
        out_specs=pl.BlockSpec((1, MOD_ROWS, tn), lambda l, j: (l, 0, j)),
        out_shape=jax.ShapeDtypeStruct((DEPTH, MOD_ROWS, n), F32),
        compiler_params=_cparams(("arbitrary", "arbitrary")),
        name="modulation",
    )(cond, w_mod, b_mod.reshape(DEPTH, 1, n))


def _mod_row(i, tm, tp, ts_per_batch):
    start = i * tm
    return jnp.where(start < tp, CTX_ROW, (start - tp) // ts_per_batch)


def _mod_spec(k, tm, tp, ts_per_batch):
    return pl.BlockSpec((1, 1, D), lambda i: (_mod_row(i, tm, tp, ts_per_batch) * 6 + k, 0, 0))


def _x_specs(tm, n_top, bot_off, width=D):
    return [pl.BlockSpec((tm, width), lambda i: (jnp.minimum(i, n_top - 1), 0)),
            pl.BlockSpec((tm, width), lambda i: (jnp.maximum(i - n_top, 0) + bot_off, 0))]


def _inproj_kernel(n_top, xa_ref, xb_ref, shift_ref, scale_ref, g_ref, w_ref, z_ref):
    x = jnp.where(pl.program_id(0) < n_top, xa_ref[...], xb_ref[...])
    ms = jnp.mean(x * x, axis=-1, keepdims=True)
    y = x * lax.rsqrt(ms + EPS) * g_ref[...]
    h = y * (1.0 + scale_ref[0]) + shift_ref[0]
    z_ref[...] = _dot(h.astype(BF16), w_ref[...])


def _inproj(xa, xb, bot_off_rows, t, modr, g, w, tp, ts_per_batch):
    tm = 512
    n_top = tp // tm
    return pl.pallas_call(
        functools.partial(_inproj_kernel, n_top),
        grid=(t // tm,),
        in_specs=_x_specs(tm, n_top, bot_off_rows // tm) + [
            _mod_spec(0, tm, tp, ts_per_batch),
            _mod_spec(1, tm, tp, ts_per_batch),
            pl.BlockSpec((1, D), lambda i: (0, 0)),
            pl.BlockSpec((D, IN_WIDTH), lambda i: (0, 0)),
        ],
        out_specs=pl.BlockSpec((tm, IN_WIDTH), lambda i: (i, 0)),
        out_shape=jax.ShapeDtypeStruct((t, IN_WIDTH), F32),
        compiler_params=_cparams(("parallel",)),
        name="inproj",
    )(xa, xb, modr, modr, g, w)


def _head_rms(x, g2):
    lo = _lane_lo()
    x2 = x * x
    s_lo = jnp.sum(jnp.where(lo, x2, 0.0), axis=-1, keepdims=True)
    s_hi = jnp.sum(jnp.where(lo, 0.0, x2), axis=-1, keepdims=True)
    ms = jnp.where(lo, s_lo, s_hi) * (1.0 / HEAD_DIM)
    return x * lax.rsqrt(ms + EPS) * g2


def _softmax_pv(q16, ks, vs, biases):
    ss = []
    for k, b in zip(ks, biases):
        s = _dot_nt(q16, k)
        if b is not None:
            s = s + b
        ss.append(s)
    m = ss[0].max(axis=-1, keepdims=True)
    for s in ss[1:]:
        m = jnp.maximum(m, s.max(axis=-1, keepdims=True))
    den = None
    acc = None
    for s, v in zip(ss, vs):
        e = jnp.exp(s - m)
        d = e.sum(axis=-1, keepdims=True)
        o = _dot(e.astype(BF16), v)
        den = d if den is None else den + d
        acc = o if acc is None else acc + o
    return acc / den


def _attend_pair(q, ks_lo, vs_lo, ks_hi, vs_hi, b_lo, b_hi):
    lo = _lane_lo()
    o_lo = _softmax_pv(jnp.where(lo, q, 0.0).astype(BF16), ks_lo, vs_lo, b_lo)
    o_hi = _softmax_pv(jnp.where(lo, 0.0, q).astype(BF16), ks_hi, vs_hi, b_hi)
    return jnp.where(lo, o_lo, o_hi)


def _gelu(x):
    c = np.sqrt(2.0 / np.pi).astype(np.float32)
    return x * (0.5 * (1.0 + jnp.tanh(c * (x + 0.044715 * (x * x * x)))))


def _layer_norm(x, g, b):
    mu = jnp.mean(x, axis=-1, keepdims=True)
    xc = x - mu
    var = jnp.mean(xc * xc, axis=-1, keepdims=True)
    return xc * lax.rsqrt(var + EPS) * g + b


def _chunk_mlp(cu, cv, lng_ref, lnb_ref, ws_ref, bsf_ref, out_ref, col0):
    lo = _lane_lo()
    u = _gelu(cu)
    v = _layer_norm(_gelu(cv), lng_ref[...], lnb_ref[...]).astype(BF16)
    s = cu.shape[0]
    for n in range(s // CHUNK):
        rows = slice(n * CHUNK, (n + 1) * CHUNK)
        for jb in range(2):
            cols = slice(jb * LANES, (jb + 1) * LANES)
            vb = v[rows, cols]
            m_lo = _dot(ws_ref[2 * jb].astype(BF16), vb)
            m_hi = _dot(ws_ref[2 * jb + 1].astype(BF16), vb)
            mixed = jnp.where(lo, m_lo, m_hi) + bsf_ref[:, cols]
            out_ref[rows, col0 + jb * LANES:col0 + (jb + 1) * LANES] = (u[rows, cols] * mixed).astype(out_ref.dtype)


def _ctx_mixer_kernel(z_ref, qg_ref, kg_ref, lng_ref, lnb_ref, ws_ref, bsf_ref, mix_ref, kbn_ref):
    scale = HEAD_DIM ** -0.5
    for j in range(2):
        q = z_ref[:, QA + LANES * j:QA + LANES * (j + 1)] * scale
        k = [z_ref[:, KA + LANES * j:KA + LANES * (j + 1)].astype(BF16)]
        v = [z_ref[:, VA + LANES * j:VA + LANES * (j + 1)].astype(BF16)]
        o = _attend_pair(q, k, v, k, v, [None], [None])
        mix_ref[:, OA + LANES * j:OA + LANES * (j + 1)] = o.astype(mix_ref.dtype)

    kb = _head_rms(z_ref[:, KB:KB + LANES], kg_ref[...])
    kbn_ref[...] = kb
    vb = z_ref[:, VB:VB + LANES]
    k_same = [kb.astype(BF16)]
    k_swap = [pltpu.roll(kb, HEAD_DIM, 1).astype(BF16)]
    v_same = [vb.astype(BF16)]
    v_swap = [pltpu.roll(vb, HEAD_DIM, 1).astype(BF16)]
    for j in range(4):
        q = _head_rms(z_ref[:, QB + LANES * j:QB + LANES * (j + 1)], qg_ref[...]) * scale
        if j // 2 == 0:
            o = _attend_pair(q, k_same, v_same, k_swap, v_swap, [None], [None])
        else:
            o = _attend_pair(q, k_swap, v_swap, k_same, v_same, [None], [None])
        mix_ref[:, OB + LANES * j:OB + LANES * (j + 1)] = o.astype(mix_ref.dtype)

    _chunk_mlp(z_ref[:, CU:CU + 256], z_ref[:, CV:CV + 256], lng_ref, lnb_ref, ws_ref, bsf_ref, mix_ref, OC)


def _ctx_mixer(z, n_batch, seq, qg2, kg2, lng, lnb, ws, bsf):
    small = lambda shape: pl.BlockSpec(shape, lambda b: (0,) * len(shape))
    return pl.pallas_call(
        _ctx_mixer_kernel,
        grid=(n_batch,),
        in_specs=[
            pl.BlockSpec((seq, IN_WIDTH), lambda b: (b, 0)),
            small((1, LANES)), small((1, LANES)), small((1, 256)), small((1, 256)),
            small((4, CHUNK, CHUNK)), small((CHUNK, 256)),
        ],
        out_specs=[
            pl.BlockSpec((seq, D), lambda b: (b, 0)),
            pl.BlockSpec((seq, LANES), lambda b: (b, 0)),
        ],
        out_shape=[
            jax.ShapeDtypeStruct((n_batch * seq, D), BF16),
            jax.ShapeDtypeStruct((n_batch * seq, LANES), F32),
        ],
        compiler_params=_cparams(("parallel",)),
        name="ctx_mixer",
    )(z, qg2, kg2, lng, lnb, ws, bsf)


def _rope(x, cos, sin):
    first = (lax.broadcasted_iota(jnp.int32, (1, LANES), 1) % 32) < 16
    partner = jnp.where(first, pltpu.roll(x, LANES - 16, 1), pltpu.roll(x, 16, 1))
    return x * cos + partner * sin


def _lat_mixer_kernel(zq_ref, zkv_ref, nakc_ref, navc_ref, gkc_ref, gvc_ref, bias_ref,
                      cosq_ref, sinq_ref, cosk_ref, sink_ref,
                      qg_ref, kg_ref, lng_ref, lnb_ref, ws_ref, bsf_ref,
                      mix_ref,
                      kl_ref, kls_ref, vl_ref, vls_ref, kc_ref, kcs_ref, vc_ref, vcs_ref):
    scale = HEAD_DIM ** -0.5

    @pl.when(pl.program_id(1) == 0)
    def _():
        kb = _rope(_head_rms(zkv_ref[:, KB:KB + LANES], kg_ref[...]), cosk_ref[...], sink_ref[...])
        vb = zkv_ref[:, VB:VB + LANES]
        kl_ref[...] = kb.astype(BF16)
        kls_ref[...] = pltpu.roll(kb, HEAD_DIM, 1).astype(BF16)
        vl_ref[...] = vb.astype(BF16)
        vls_ref[...] = pltpu.roll(vb, HEAD_DIM, 1).astype(BF16)
        kc = gkc_ref[0]
        vc = gvc_ref[0]
        kc_ref[...] = kc.astype(BF16)
        kcs_ref[...] = pltpu.roll(kc, HEAD_DIM, 1).astype(BF16)
        vc_ref[...] = vc.astype(BF16)
        vcs_ref[...] = pltpu.roll(vc, HEAD_DIM, 1).astype(BF16)

    for j in range(2):
        cols = slice(LANES * j, LANES * (j + 1))
        q = zq_ref[:, QA + LANES * j:QA + LANES * (j + 1)] * scale
        ks = [zkv_ref[:, KA + LANES * j:KA + LANES * (j + 1)].astype(BF16), nakc_ref[0, :, cols].astype(BF16)]
        vs = [zkv_ref[:, VA + LANES * j:VA + LANES * (j + 1)].astype(BF16), navc_ref[0, :, cols].astype(BF16)]
        o = _attend_pair(q, ks, vs, ks, vs, [bias_ref[2 * j], None], [bias_ref[2 * j + 1], None])
        mix_ref[:, OA + LANES * j:OA + LANES * (j + 1)] = o.astype(mix_ref.dtype)

    same = ([kl_ref[...], kc_ref[...]], [vl_ref[...], vc_ref[...]])
    swap = ([kls_ref[...], kcs_ref[...]], [vls_ref[...], vcs_ref[...]])
    for j in range(4):
        q = _head_rms(zq_ref[:, QB + LANES * j:QB + LANES * (j + 1)], qg_ref[...])
        q = _rope(q, cosq_ref[...], sinq_ref[...]) * scale
        lo_kv, hi_kv = (same, swap) if j // 2 == 0 else (swap, same)
        o = _attend_pair(q, lo_kv[0], lo_kv[1], hi_kv[0], hi_kv[1], [None, None], [None, None])
        mix_ref[:, OB + LANES * j:OB + LANES * (j + 1)] = o.astype(mix_ref.dtype)

    _chunk_mlp(zq_ref[:, CU:CU + 256], zq_ref[:, CV:CV + 256], lng_ref, lnb_ref, ws_ref, bsf_ref, mix_ref, OC)


def _lat_mixer(z, row0, n_batch, n_tok, past, nakc, navc, gkc, gvc, bias, cos, sin, qg2, kg2, lng, lnb, ws, bsf):
    tq = 256
    nq = n_tok // tq
    small = lambda shape: pl.BlockSpec(shape, lambda b, t: (0,) * len(shape))
    qblk0 = row0 // tq
    kvblk0 = row0 // n_tok
    return pl.pallas_call(
        _lat_mixer_kernel,
        grid=(n_batch, nq),
        in_specs=[
            pl.BlockSpec((tq, IN_WIDTH), lambda b, t: (qblk0 + b * nq + t, 0)),
            pl.BlockSpec((n_tok, IN_WIDTH), lambda b, t: (kvblk0 + b, 0)),
            pl.BlockSpec((1, past, 256), lambda b, t: (b, 0, 0)),
            pl.BlockSpec((1, past, 256), lambda b, t: (b, 0, 0)),
            pl.BlockSpec((1, past, LANES), lambda b, t: (b, 0, 0)),
            pl.BlockSpec((1, past, LANES), lambda b, t: (b, 0, 0)),
            pl.BlockSpec((4, tq, n_tok), lambda b, t: (0, t, 0)),
            pl.BlockSpec((tq, LANES), lambda b, t: (t, 0)),
            pl.BlockSpec((tq, LANES), lambda b, t: (t, 0)),
            small((n_tok, LANES)), small((n_tok, LANES)),
            small((1, LANES)), small((1, LANES)), small((1, 256)), small((1, 256)),
            small((4, CHUNK, CHUNK)), small((CHUNK, 256)),
        ],
        out_specs=pl.BlockSpec((tq, D), lambda b, t: (b * nq + t, 0)),
        out_shape=jax.ShapeDtypeStruct((n_batch * n_tok, D), BF16),
        scratch_shapes=[pltpu.VMEM((n_tok, LANES), BF16)] * 4 + [pltpu.VMEM((past, LANES), BF16)] * 4,
        compiler_params=_cparams(("parallel", "arbitrary")),
        name="lat_mixer",
    )(z, z, nakc, navc, gkc, gvc, bias, cos, sin, cos, sin, qg2, kg2, lng, lnb, ws, bsf)


def _split_bf16(x):
    hi = x.astype(BF16)
    return hi, (x - hi.astype(F32)).astype(BF16)


def _outproj_kernel(n_top, mixa_ref, mixb_ref, xa_ref, xb_ref, g1_ref, shift_ref, scale_ref, g_ref, w_ref, rwt_ref, rb_ref,
                    xn_ref, ht_ref, ti_ref, gt_ref):
    top = pl.program_id(0) < n_top
    x = jnp.where(top, xa_ref[...], xb_ref[...])
    x = x + g1_ref[0] * _dot(jnp.where(top, mixa_ref[...], mixb_ref[...]), w_ref[...])
    xn_ref[...] = x
    ms = jnp.mean(x * x, axis=-1, keepdims=True)
    h = x * lax.rsqrt(ms + EPS) * g_ref[...]
    h = h * (1.0 + scale_ref[0]) + shift_ref[0]
    h_hi, h_lo = _split_bf16(h)
    rw_hi, rw_lo = _split_bf16(rwt_ref[...])
    logits = (_dot_nt(rw_hi, h_hi) + _dot_nt(rw_hi, h_lo)) + _dot_nt(rw_lo, h_hi) + rb_ref[...]

    expert = lax.broadcasted_iota(jnp.int32, logits.shape, 0)
    out_row = lax.broadcasted_iota(jnp.int32, ti_ref.shape, 0)
    top_i = jnp.zeros(ti_ref.shape, jnp.int32)
    top_v = []
    for k in range(TOP_K):
        m = jnp.max(logits, axis=0, keepdims=True)
        idx = jnp.min(jnp.where(logits == m, expert, N_EXPERTS), axis=0, keepdims=True)
        logits = jnp.where(expert == idx, -jnp.inf, logits)
        top_i = jnp.where(out_row == k, idx, top_i)
        top_v.append(m)
    es = [jnp.exp(v - top_v[0]) for v in top_v]
    den = (es[0] + es[1]) + (es[2] + es[3])
    gates = jnp.zeros(gt_ref.shape, F32)
    for k in range(TOP_K):
        gates = jnp.where(out_row == k, es[k] / den, gates)
    ti_ref[...] = top_i
    gt_ref[...] = gates

    for c in range(ROW_TILE):
        ht_ref[pl.ds(c, h.shape[0], stride=ROW_TILE), :] = h[:, LANES * c:LANES * (c + 1)]


def _outproj(mix_a, mix_b, xa, xb, bot_off_rows, modr, g, w, rwt, rb, tp, ts_per_batch):
    tm = 512
    t = mix_a.shape[0] + mix_b.shape[0]
    n_top = tp // tm
    return pl.pallas_call(
        functools.partial(_outproj_kernel, n_top),
        grid=(t // tm,),
        in_specs=_x_specs(tm, n_top, 0) + _x_specs(tm, n_top, bot_off_rows // tm) + [
            _mod_spec(2, tm, tp, ts_per_batch),
            _mod_spec(3, tm, tp, ts_per_batch),
            _mod_spec(4, tm, tp, ts_per_batch),
            pl.BlockSpec((1, D), lambda i: (0, 0)),
            pl.BlockSpec((D, D), lambda i: (0, 0)),
            pl.BlockSpec((N_EXPERTS, D), lambda i: (0, 0)),
            pl.BlockSpec((N_EXPERTS, 1), lambda i: (0, 0)),
        ],
        out_specs=[
            pl.BlockSpec((tm, D), lambda i: (i, 0)),
            pl.BlockSpec((tm * ROW_TILE, LANES), lambda i: (i, 0)),
            pl.BlockSpec((8, tm), lambda i: (0, i)),
            pl.BlockSpec((8, tm), lambda i: (0, i)),
        ],
        out_shape=[
            jax.ShapeDtypeStruct((t, D), F32),
            jax.ShapeDtypeStruct((t * ROW_TILE, LANES), F32),
            jax.ShapeDtypeStruct((8, t), jnp.int32),
            jax.ShapeDtypeStruct((8, t), F32),
        ],
        compiler_params=_cparams(("parallel",)),
        name="outproj",
    )(mix_a, mix_b, xa, xb, modr, modr, modr, g, w, rwt, rb)


def _prep_wgu_kernel(w_ref, p_ref, wg_ref, wu_ref):
    for j in range(w_ref.shape[2] // 256):
        w = w_ref[0, :, 256 * j:256 * (j + 1)].astype(BF16)
        sel = _dot(w, p_ref[...])
        wg_ref[0, :, LANES * j:LANES * (j + 1)] = sel[:, :LANES].astype(BF16)
        wu_ref[0, :, LANES * j:LANES * (j + 1)] = sel[:, LANES:].astype(BF16)


def _prep_wgu(w):
    e, d, f2 = w.shape
    perm = np.zeros((256, 256), np.float32)
    perm[2 * np.arange(LANES), np.arange(LANES)] = 1.0
    perm[2 * np.arange(LANES) + 1, LANES + np.arange(LANES)] = 1.0
    out = jax.ShapeDtypeStruct((e, d, f2 // 2), BF16)
    return pl.pallas_call(
        _prep_wgu_kernel,
        grid=(e,),
        in_specs=[pl.BlockSpec((1, d, f2), lambda i: (i, 0, 0)), pl.BlockSpec((256, 256), lambda i: (0, 0))],
        out_specs=[pl.BlockSpec((1, d, f2 // 2), lambda i: (i, 0, 0))] * 2,
        out_shape=[out, out],
        compiler_params=_cparams(("parallel",)),
        name="prep_wgu",
    )(w, jnp.asarray(perm, BF16))


def _moe_kernel(blk_e_ref, row_tok_ref, row_dst_ref, n_live_ref, ht_ref, wg_ref, wu_ref, wd_ref, bg_ref, bu_ref, bd_ref,
                out_ref, xg_ref, ybuf_ref, sem):
    i = pl.program_id(1)
    nb = pl.num_programs(1)
    step = pl.program_id(0) * nb + i
    n = pl.num_programs(0) * nb
    cur = step % 2
    nxt = 1 - cur
    ycur = step % Y_SLOTS
    yprev = (step + Y_SLOTS - 1) % Y_SLOTS
    blk = MOE_BLOCK * ROW_TILE

    def slot_copy(s):
        return pltpu.make_async_copy(ybuf_ref.at[pl.ds(s * blk, blk)], out_ref.at[pl.ds(0, blk)], sem.at[s])

    def gather_row(base, r, s):
        tok = row_tok_ref[base + r]
        xg_ref[pl.ds(pl.multiple_of(s * blk + r * ROW_TILE, ROW_TILE), ROW_TILE), :] = (
            ht_ref[pl.ds(pl.multiple_of(tok * ROW_TILE, ROW_TILE), ROW_TILE), :])

    def send_row(base, r, s, priority=0):
        dst = row_dst_ref[base + r]
        pltpu.make_async_copy(
            ybuf_ref.at[pl.ds(pl.multiple_of(s * blk + r * ROW_TILE, ROW_TILE), ROW_TILE)],
            out_ref.at[pl.ds(pl.multiple_of(dst * ROW_TILE, ROW_TILE), ROW_TILE)], sem.at[s]).start(priority)

    @pl.when(step == 0)
    def _():
        ybuf_ref[...] = jnp.zeros(ybuf_ref.shape, F32)
        for s in range(Y_SLOTS - 1):
            pltpu.make_async_copy(ybuf_ref.at[pl.ds(s * blk, blk)], out_ref.at[pl.ds((n + 1 + s) * blk, blk)],
                                  sem.at[s]).start()

    @pl.when(i == 0)
    def _():
        def first(r, carry):
            gather_row(step * MOE_BLOCK, r, cur)
            return carry

        lax.fori_loop(0, MOE_BLOCK, first, 0, unroll=8)

    slot_copy(ycur).wait()
    live = i < n_live_ref[pl.program_id(0)]

    @pl.when(jnp.logical_not(live))
    def _():
        def prev(r, carry):
            send_row(step * MOE_BLOCK, r, yprev)
            return carry

        lax.fori_loop(0, MOE_BLOCK, prev, 0, unroll=8)

    @pl.when(live)
    def _():
        _moe_block(step, i, nb, cur, nxt, ycur, yprev, blk, gather_row, send_row,
                   xg_ref, ybuf_ref, wg_ref, wu_ref, wd_ref, bg_ref, bu_ref, bd_ref)

    @pl.when(step == n - 1)
    def _():
        def last(r, carry):
            send_row(n * MOE_BLOCK, r, ycur)
            return carry

        lax.fori_loop(0, MOE_BLOCK, last, 0, unroll=8)
        for s in range(Y_SLOTS):
            slot_copy(s).wait()


def _moe_block(step, i, nb, cur, nxt, ycur, yprev, blk, gather_row, send_row,
               xg_ref, ybuf_ref, wg_ref, wu_ref, wd_ref, bg_ref, bu_ref, bd_ref):
    x = jnp.concatenate(
        [xg_ref[pl.ds(cur * blk + c, MOE_BLOCK, stride=ROW_TILE), :].astype(BF16) for c in range(ROW_TILE)], axis=1)
    next_blk = jnp.where(i + 1 < nb, step + 1, step)
    for r in range(MOE_BLOCK):
        gather_row(next_blk * MOE_BLOCK, r, nxt)
    for r in range(MOE_BLOCK):
        send_row(step * MOE_BLOCK, r, yprev, priority=r % 2)
    g = _dot(x, wg_ref[0]) + bg_ref[0]
    u = _dot(x, wu_ref[0]) + bu_ref[0]
    g = jnp.minimum(g, SWIGLU_LIMIT)
    u = jnp.clip(u, -SWIGLU_LIMIT, SWIGLU_LIMIT)
    act = (u + 1.0) * (g * jax.nn.sigmoid(SWIGLU_ALPHA * g))
    y = _dot(act.astype(BF16), wd_ref[0]) + bd_ref[0]
    for c in range(ROW_TILE):
        ybuf_ref[pl.ds(ycur * blk + c, MOE_BLOCK, stride=ROW_TILE), :] = y[:, LANES * c:LANES * (c + 1)]


def _moe(blk_e, row_tok, row_dst, n_live, ht, n_groups, wg, wu, wd, bg, bu, bd):
    nb = blk_e.shape[0] // n_groups
    tg = ht.shape[0] // ROW_TILE // n_groups
    ew = lambda g, i, be, rt, rd, nl: (be[g * nb + i], 0, 0)
    wspec = pl.BlockSpec((1, D, D), ew)
    return pl.pallas_call(
        _moe_kernel,
        grid_spec=pltpu.PrefetchScalarGridSpec(
            num_scalar_prefetch=4,
            grid=(n_groups, nb),
            in_specs=[
                pl.BlockSpec((tg * ROW_TILE, LANES), lambda g, i, be, rt, rd, nl: (g, 0), pipeline_mode=pl.Buffered(1)),
                wspec, wspec, wspec,
                pl.BlockSpec((1, 1, D), ew),
                pl.BlockSpec((1, 1, D), ew),
                pl.BlockSpec((1, 1, D), ew),
            ],
            out_specs=pl.BlockSpec(memory_space=pl.ANY),
            scratch_shapes=[
                pltpu.VMEM((2 * MOE_BLOCK * ROW_TILE, LANES), F32),
                pltpu.VMEM((Y_SLOTS * MOE_BLOCK * ROW_TILE, LANES), F32),
                pltpu.SemaphoreType.DMA((Y_SLOTS,)),
            ],
        ),
        out_shape=jax.ShapeDtypeStruct(((n_groups * nb + Y_SLOTS) * MOE_BLOCK * ROW_TILE, LANES), F32),
        compiler_params=_cparams(("arbitrary", "arbitrary")),
        name="moe_experts",
    )(blk_e, row_tok, row_dst, n_live, ht, wg, wu, wd, bg, bu, bd)


def _route(top_i, group, n_groups):
    tg = top_i.shape[0]
    n_assign = tg * TOP_K
    assert n_assign <= 1 << 16 and n_assign % MOE_BLOCK == 0
    flat_e = top_i.reshape(-1).astype(jnp.int32)
    experts = jnp.arange(N_EXPERTS, dtype=jnp.int32)
    counts = jnp.sum((flat_e[:, None] == experts[None, :]).astype(jnp.int32), axis=0)
    padded = (counts + MOE_BLOCK - 1) // MOE_BLOCK * MOE_BLOCK
    pad_end = jnp.cumsum(padded)
    n_blocks = n_assign // MOE_BLOCK + N_EXPERTS
    real_keys = (flat_e << 17) | jnp.arange(n_assign, dtype=jnp.int32)
    pad_id = experts[:, None] * MOE_BLOCK + jnp.arange(MOE_BLOCK, dtype=jnp.int32)[None, :]
    used = jnp.arange(MOE_BLOCK, dtype=jnp.int32)[None, :] < (padded - counts)[:, None]
    pad_keys = (jnp.where(used, experts[:, None], 63) << 17) | (1 << 16) | pad_id
    keys = jnp.sort(jnp.concatenate([real_keys, pad_keys.reshape(-1)]))
    is_pad = ((keys >> 16) & 1) == 1
    payload = keys & 0xFFFF
    tok = payload // TOP_K
    t_all = tg * n_groups
    row_tok = jnp.where(is_pad, 0, tok)
    row_dst = jnp.where(is_pad, TOP_K * t_all + group * (N_EXPERTS * MOE_BLOCK) + payload,
                        (payload % TOP_K) * t_all + group * tg + tok)
    blk_start = jnp.arange(n_blocks, dtype=jnp.int32) * MOE_BLOCK
    blk_e = jnp.sum((pad_end[None, :] <= blk_start[:, None]).astype(jnp.int32), axis=1)
    blk_e = jnp.minimum(blk_e, N_EXPERTS - 1)
    n_live = pad_end[-1] // MOE_BLOCK
    return blk_e, row_tok, row_dst, n_live


def _route_groups(top_i, n_groups):
    tg = top_i.shape[0] // n_groups
    parts = [_route(top_i[g * tg:(g + 1) * tg], g, n_groups) for g in range(n_groups)]
    blk_e = jnp.concatenate([p[0] for p in parts])
    row_tok = jnp.concatenate([p[1] for p in parts])
    spare = row_tok.shape[0] + jnp.arange(MOE_BLOCK, dtype=jnp.int32)
    row_dst = jnp.concatenate([spare] + [p[2] for p in parts])
    n_live = jnp.stack([p[3] for p in parts]).astype(jnp.int32)
    return blk_e, row_tok, row_dst, n_live


def _combine_kernel(final, y0_ref, y1_ref, y2_ref, y3_ref, gates_ref, x_ref, g2_ref, gf_ref, o_ref):
    g = gates_ref[...]
    tm = x_ref.shape[0]
    for c in range(ROW_TILE):
        rows = pl.ds(c, tm, stride=ROW_TILE)
        cols = slice(LANES * c, LANES * (c + 1))
        y = ((y0_ref[rows, :] * g[:, 0:1] + y1_ref[rows, :] * g[:, 1:2])
             + (y2_ref[rows, :] * g[:, 2:3] + y3_ref[rows, :] * g[:, 3:4]))
        o_ref[:, cols] = x_ref[:, cols] + g2_ref[0][:, cols] * y
    if final:
        x = o_ref[...]
        ms = jnp.mean(x * x, axis=-1, keepdims=True)
        o_ref[...] = x * lax.rsqrt(ms + EPS) * gf_ref[...]


def _combine(ys, gates, x, modr, tp, ts_per_batch, row0, n_rows, final_g=None):
    tm = 512
    nt_all = x.shape[0] // tm
    i0 = row0 // tm
    yspec = lambda k: pl.BlockSpec((tm * ROW_TILE, LANES), lambda i: (k * nt_all + i0 + i, 0))
    mod = pl.BlockSpec((1, 1, D), lambda i: (_mod_row(i0 + i, tm, tp, ts_per_batch) * 6 + 5, 0, 0))
    gf = jnp.ones((1, D), F32) if final_g is None else final_g
    return pl.pallas_call(
        functools.partial(_combine_kernel, final_g is not None),
        grid=(n_rows // tm,),
        in_specs=[yspec(0), yspec(1), yspec(2), yspec(3), pl.BlockSpec((tm, TOP_K), lambda i: (i0 + i, 0)),
                  pl.BlockSpec((tm, D), lambda i: (i0 + i, 0)), mod, pl.BlockSpec((1, D), lambda i: (0, 0))],
        out_specs=pl.BlockSpec((tm, D), lambda i: (i, 0)),
        out_shape=jax.ShapeDtypeStruct((n_rows, D), F32),
        compiler_params=_cparams(("parallel",)),
        name="moe_combine",
    )(ys, ys, ys, ys, gates, x, modr, gf)


def _na_bias(rpb, rows):
    kr = min(NA_WIN_R, rows)
    r = np.arange(rows)
    r0 = np.clip(r - kr // 2, 0, rows - kr)
    row_ok = (r[None, :] >= r0[:, None]) & (r[None, :] < r0[:, None] + kr)
    dr = np.clip(r[None, :] - r[:, None] + (NA_WIN_R - 1), 0, 2 * NA_WIN_R - 2)
    c = np.arange(GRID_W)
    w0 = np.clip(c - NA_WIN_C // 2, 0, GRID_W - NA_WIN_C)
    col_ok = (c[None, :] >= w0[:, None]) & (c[None, :] < w0[:, None] + NA_WIN_C)
    dc = np.clip(c[None, :] - c[:, None] + (NA_WIN_C - 1), 0, 2 * NA_WIN_C - 2)
    oh_r = np.eye(2 * NA_WIN_R - 1, dtype=np.float32)[dr]
    oh_c = np.eye(2 * NA_WIN_C - 1, dtype=np.float32)[dc]
    by_row = jnp.einsum('rkd,hde->hrke', oh_r, rpb.astype(F32), precision=HI)
    vals = jnp.einsum('hrke,qce->hrqkc', by_row, oh_c, precision=HI)
    ok = row_ok[:, None, :, None] & col_ok[None, :, None, :]
    n = rows * GRID_W
    return jnp.where(ok[None], vals, NEG_INF).reshape(rpb.shape[0], n, n)


def _rope_tables(n_tokens):
    t = np.arange(n_tokens)
    row = (t // GRID_W).astype(np.float32)
    col = (t % GRID_W).astype(np.float32)
    half = HEAD_DIM // 2
    inv = jnp.asarray(ROPE_THETA, F32) ** (-jnp.arange(0, half, 2, dtype=F32) / half)
    ang_r = jnp.asarray(row)[:, None] * inv
    ang_c = jnp.asarray(col)[:, None] * inv
    cr, sr, cc, sc = jnp.cos(ang_r), jnp.sin(ang_r), jnp.cos(ang_c), jnp.sin(ang_c)
    cos = jnp.concatenate([cr, cr, cc, cc] * 2, axis=-1)
    sin = jnp.concatenate([-sr, sr, -sc, sc] * 2, axis=-1)
    return cos, sin


def _to_heads(x, n_batch, seq, n_heads):
    return x.reshape(n_batch, seq, n_heads, HEAD_DIM).transpose(0, 2, 1, 3)


def _ctx_lanes(cache_l):
    b, h, p, dh = cache_l.shape
    return cache_l.transpose(0, 2, 1, 3).reshape(b, p, h * dh)


def kernel(x_prompt, x_sample, cache_na_k, cache_na_v, cache_gqa_k, cache_gqa_v, c, c_ctx, w_mod, b_mod, norm1_g, norm2_g, w_in, na_rpb, q_norm_g, k_norm_g, cm_ln_g, cm_ln_b, cm_ws, cm_bs, w_out, router_w, router_b, w_gate_up, b_gate_up, w_down, b_down, final_norm_g):
    nb, seq, _ = x_prompt.shape
    db, n_tok, _ = x_sample.shape
    past = cache_na_k.shape[3]
    tp, ts = nb * seq, db * n_tok
    assert db <= CTX_ROW and n_tok % GRID_W == 0
    assert tp == ts

    cond = jnp.zeros((MOD_ROWS, D), F32).at[:db].set(c).at[CTX_ROW].set(c_ctx)
    mod = _modulation(cond, w_mod, b_mod)
    cos, sin = _rope_tables(n_tok)
    xa, xb, xb_row0 = x_prompt.reshape(tp, D), x_sample.reshape(ts, D), 0

    w_in16 = w_in.astype(BF16)
    w_out16 = w_out.astype(BF16)
    wg16, wu16 = _prep_wgu(w_gate_up.reshape(DEPTH * N_EXPERTS, D, 2 * D))
    wd16 = w_down.astype(BF16).reshape(DEPTH * N_EXPERTS, D, D)
    bg = b_gate_up[..., 0::2].reshape(DEPTH * N_EXPERTS, 1, D)
    bu = b_gate_up[..., 1::2].reshape(DEPTH * N_EXPERTS, 1, D)
    bd = b_down.reshape(DEPTH * N_EXPERTS, 1, D)

    na_k, na_v, gqa_k, gqa_v = [], [], [], []
    for l in range(DEPTH):
        modr = mod[l].reshape(MOD_ROWS * 6, 1, D)
        qg2 = jnp.tile(q_norm_g[l], 2).reshape(1, LANES)
        kg2 = jnp.tile(k_norm_g[l], 2).reshape(1, LANES)
        lng = cm_ln_g[l].reshape(1, 256)
        lnb = cm_ln_b[l].reshape(1, 256)
        bsf = jnp.repeat(cm_bs[l].T, HEAD_DIM, axis=1)

        z = _inproj(xa, xb, xb_row0, tp + ts, modr, norm1_g[l].reshape(1, D), w_in16[l], tp, n_tok)
        mix_p, kbn = _ctx_mixer(z, nb, seq, qg2, kg2, lng, lnb, cm_ws[l], bsf)
        mix_s = _lat_mixer(z, tp, db, n_tok, past,
                           _ctx_lanes(cache_na_k[:, l]), _ctx_lanes(cache_na_v[:, l]),
                           _ctx_lanes(cache_gqa_k[:, l]), _ctx_lanes(cache_gqa_v[:, l]),
                           _na_bias(na_rpb[l], n_tok // GRID_W), cos, sin, qg2, kg2, lng, lnb, cm_ws[l], bsf)
        na_k.append(_to_heads(z[:tp, KA:KA + 256], nb, seq, 4))
        na_v.append(_to_heads(z[:tp, VA:VA + 256], nb, seq, 4))
        gqa_k.append(_to_heads(kbn, nb, seq, 2))
        gqa_v.append(_to_heads(z[:tp, VB:VB + LANES], nb, seq, 2))

        xn, ht, top_i, gates = _outproj(mix_p, mix_s, xa, xb, xb_row0, modr, norm2_g[l].reshape(1, D), w_out16[l],
                                  router_w[l].T, router_b[l].reshape(N_EXPERTS, 1), tp, n_tok)

        blk_e, row_tok, row_dst, n_live = _route_groups(top_i[:TOP_K].T, 2)
        gates = gates[:TOP_K].T
        ys = _moe(blk_e + l * N_EXPERTS, row_tok, row_dst, n_live, ht, 2, wg16, wu16, wd16, bg, bu, bd)
        if l + 1 < DEPTH:
            x = _combine(ys, gates, xn, modr, tp, n_tok, 0, tp + ts)
            xa, xb, xb_row0 = x, x, tp

    gf = final_norm_g.reshape(1, D)
    y_p = _combine(ys, gates, xn, modr, tp, n_tok, 0, tp, gf)
    y_s = _combine(ys, gates, xn, modr, tp, n_tok, tp, ts, gf)
    return (y_p.reshape(nb, seq, D), y_s.reshape(db, n_tok, D),
            jnp.stack(na_k, axis=1), jnp.stack(na_v, axis=1), jnp.stack(gqa_k, axis=1), jnp.stack(gqa_v, axis=1))
```

```python
import functools

import jax
import jax.numpy as jnp
import numpy as np
from jax import lax
from jax.experimental import pallas as pl
from jax.experimental.pallas import tpu as pltpu

D = 1024
DEPTH = 2
GRID_W = 64
HEAD_DIM = 64
NA_WIN_R = 8
NA_WIN_C = 16
CHUNK = 128
ROPE_THETA = 10000.0
N_EXPERTS = 32
TOP_K = 4
SWIGLU_LIMIT = 7.0
SWIGLU_ALPHA = 1.702
EPS = 1e-6
MOE_BLOCK = 128
Y_SLOTS = 3
NEG_INF = -1e30
IN_WIDTH = 2048

QA, KA, VA, QB, KB, VB, CU, CV = 0, 256, 512, 768, 1280, 1408, 1536, 1792
OA, OB, OC = 0, 256, 768

LANES = 128
ROW_TILE = D // LANES
MOD_ROWS = 16
CTX_ROW = 8
VMEM_LIMIT = 56 * 1024 * 1024

F32 = jnp.float32
BF16 = jnp.bfloat16
HI = lax.Precision.HIGHEST


def _cparams(sem):
    return pltpu.CompilerParams(dimension_semantics=sem, vmem_limit_bytes=VMEM_LIMIT)


def _dot(a, b):
    return jnp.dot(a, b, preferred_element_type=F32)


def _dot_nt(a, b):
    return lax.dot_general(a, b, (((1,), (1,)), ((), ())), preferred_element_type=F32)


def _lane_lo():
    return lax.broadcasted_iota(jnp.int32, (1, LANES), 1) < HEAD_DIM


def _mod_kernel(c_ref, w_ref, b_ref, o_ref):
    c = c_ref[...]
    s = c * jax.nn.sigmoid(c)
    o_ref[0] = jnp.dot(s, w_ref[0], preferred_element_type=F32, precision=HI) + b_ref[0]


def _modulation(cond, w_mod, b_mod):
    tn = 1536
    n = w_mod.shape[-1]
    return pl.pallas_call(
        _mod_kernel,
        grid=(DEPTH, n // tn),
        in_specs=[
            pl.BlockSpec((MOD_ROWS, D), lambda l, j: (0, 0)),
            pl.BlockSpec((1, D, tn), lambda l, j: (l, 0, j)),
            pl.BlockSpec((1, 1, tn), lambda l, j: (l, 0, j)),
        ],
        out_specs=pl.BlockSpec((1, MOD_ROWS, tn), lambda l, j: (l, 0, j)),
        out_shape=jax.ShapeDtypeStruct((DEPTH, MOD_ROWS, n), F32),
        compiler_params=_cparams(("arbitrary", "arbitrary")),
        name="modulation",
    )(cond, w_mod, b_mod.reshape(DEPTH, 1, n))


def _mod_row(i, tm, tp, ts_per_batch):
    start = i * tm
    return jnp.where(start < tp, CTX_ROW, (start - tp) // ts_per_batch)


def _mod_spec(k, tm, tp, ts_per_batch):
    return pl.BlockSpec((1, 1, D), lambda i: (_mod_row(i, tm, tp, ts_per_batch) * 6 + k, 0, 0))


def _x_specs(tm, n_top, bot_off, width=D):
    return [pl.BlockSpec((tm, width), lambda i: (jnp.minimum(i, n_top - 1), 0)),
            pl.BlockSpec((tm, width), lambda i: (jnp.maximum(i - n_top, 0) + bot_off, 0))]


def _inproj_kernel(n_top, xa_ref, xb_ref, shift_ref, scale_ref, g_ref, w_ref, z_ref):
    x = jnp.where(pl.program_id(0) < n_top, xa_ref[...], xb_ref[...])
    ms = jnp.mean(x * x, axis=-1, keepdims=True)
    y = x * lax.rsqrt(ms + EPS) * g_ref[...]
    h = y * (1.0 + scale_ref[0]) + shift_ref[0]
    z_ref[...] = _dot(h.astype(BF16), w_ref[...])


def _inproj(xa, xb, bot_off_rows, t, modr, g, w, tp, ts_per_batch):
    tm = 512
    n_top = tp // tm
    return pl.pallas_call(
        functools.partial(_inproj_kernel, n_top),
        grid=(t // tm,),
        in_specs=_x_specs(tm, n_top, bot_off_rows // tm) + [
            _mod_spec(0, tm, tp, ts_per_batch),
            _mod_spec(1, tm, tp, ts_per_batch),
            pl.BlockSpec((1, D), lambda i: (0, 0)),
            pl.BlockSpec((D, IN_WIDTH), lambda i: (0, 0)),
        ],
        out_specs=pl.BlockSpec((tm, IN_WIDTH), lambda i: (i, 0)),
        out_shape=jax.ShapeDtypeStruct((t, IN_WIDTH), F32),
        compiler_params=_cparams(("parallel",)),
        name="inproj",
    )(xa, xb, modr, modr, g, w)


def _head_rms(x, g2):
    lo = _lane_lo()
    x2 = x * x
    s_lo = jnp.sum(jnp.where(lo, x2, 0.0), axis=-1, keepdims=True)
    s_hi = jnp.sum(jnp.where(lo, 0.0, x2), axis=-1, keepdims=True)
    ms = jnp.where(lo, s_lo, s_hi) * (1.0 / HEAD_DIM)
    return x * lax.rsqrt(ms + EPS) * g2


def _softmax_pv(q16, ks, vs, biases):
    ss = []
    for k, b in zip(ks, biases):
        s = _dot_nt(q16, k)
        if b is not None:
            s = s + b
        ss.append(s)
    m = ss[0].max(axis=-1, keepdims=True)
    for s in ss[1:]:
        m = jnp.maximum(m, s.max(axis=-1, keepdims=True))
    den = None
    acc = None
    for s, v in zip(ss, vs):
        e = jnp.exp(s - m)
        d = e.sum(axis=-1, keepdims=True)
        o = _dot(e.astype(BF16), v)
        den = d if den is None else den + d
        acc = o if acc is None else acc + o
    return acc / den


def _attend_pair(q, ks_lo, vs_lo, ks_hi, vs_hi, b_lo, b_hi):
    lo = _lane_lo()
    o_lo = _softmax_pv(jnp.where(lo, q, 0.0).astype(BF16), ks_lo, vs_lo, b_lo)
    o_hi = _softmax_pv(jnp.where(lo, 0.0, q).astype(BF16), ks_hi, vs_hi, b_hi)
    return jnp.where(lo, o_lo, o_hi)


def _gelu(x):
    c = np.sqrt(2.0 / np.pi).astype(np.float32)
    return x * (0.5 * (1.0 + jnp.tanh(c * (x + 0.044715 * (x * x * x)))))


def _layer_norm(x, g, b):
    mu = jnp.mean(x, axis=-1, keepdims=True)
    xc = x - mu
    var = jnp.mean(xc * xc, axis=-1, keepdims=True)
    return xc * lax.rsqrt(var + EPS) * g + b


def _chunk_mlp(cu, cv, lng_ref, lnb_ref, ws_ref, bsf_ref, out_ref, col0):
    lo = _lane_lo()
    u = _gelu(cu)
    v = _layer_norm(_gelu(cv), lng_ref[...], lnb_ref[...]).astype(BF16)
    s = cu.shape[0]
    for n in range(s // CHUNK):
        rows = slice(n * CHUNK, (n + 1) * CHUNK)
        for jb in range(2):
            cols = slice(jb * LANES, (jb + 1) * LANES)
            vb = v[rows, cols]
            m_lo = _dot(ws_ref[2 * jb].astype(BF16), vb)
            m_hi = _dot(ws_ref[2 * jb + 1].astype(BF16), vb)
            mixed = jnp.where(lo, m_lo, m_hi) + bsf_ref[:, cols]
            out_ref[rows, col0 + jb * LANES:col0 + (jb + 1) * LANES] = (u[rows, cols] * mixed).astype(out_ref.dtype)


def _ctx_mixer_kernel(z_ref, qg_ref, kg_ref, lng_ref, lnb_ref, ws_ref, bsf_ref, mix_ref, kbn_ref):
    scale = HEAD_DIM ** -0.5
    for j in range(2):
        q = z_ref[:, QA + LANES * j:QA + LANES * (j + 1)] * scale
        k = [z_ref[:, KA + LANES * j:KA + LANES * (j + 1)].astype(BF16)]
        v = [z_ref[:, VA + LANES * j:VA + LANES * (j + 1)].astype(BF16)]
        o = _attend_pair(q, k, v, k, v, [None], [None])
        mix_ref[:, OA + LANES * j:OA + LANES * (j + 1)] = o.astype(mix_ref.dtype)

    kb = _head_rms(z_ref[:, KB:KB + LANES], kg_ref[...])
    kbn_ref[...] = kb
    vb = z_ref[:, VB:VB + LANES]
    k_same = [kb.astype(BF16)]
    k_swap = [pltpu.roll(kb, HEAD_DIM, 1).astype(BF16)]
    v_same = [vb.astype(BF16)]
    v_swap = [pltpu.roll(vb, HEAD_DIM, 1).astype(BF16)]
    for j in range(4):
        q = _head_rms(z_ref[:, QB + LANES * j:QB + LANES * (j + 1)], qg_ref[...]) * scale
        if j // 2 == 0:
            o = _attend_pair(q, k_same, v_same, k_swap, v_swap, [None], [None])
        else:
            o = _attend_pair(q, k_swap, v_swap, k_same, v_same, [None], [None])
        mix_ref[:, OB + LANES * j:OB + LANES * (j + 1)] = o.astype(mix_ref.dtype)

    _chunk_mlp(z_ref[:, CU:CU + 256], z_ref[:, CV:CV + 256], lng_ref, lnb_ref, ws_ref, bsf_ref, mix_ref, OC)


def _ctx_mixer(z, n_batch, seq, qg2, kg2, lng, lnb, ws, bsf):
    small = lambda shape: pl.BlockSpec(shape, lambda b: (0,) * len(shape))
    return pl.pallas_call(
        _ctx_mixer_kernel,
        grid=(n_batch,),
        in_specs=[
            pl.BlockSpec((seq, IN_WIDTH), lambda b: (b, 0)),
            small((1, LANES)), small((1, LANES)), small((1, 256)), small((1, 256)),
            small((4, CHUNK, CHUNK)), small((CHUNK, 256)),
        ],
        out_specs=[
            pl.BlockSpec((seq, D), lambda b: (b, 0)),
            pl.BlockSpec((seq, LANES), lambda b: (b, 0)),
        ],
        out_shape=[
            jax.ShapeDtypeStruct((n_batch * seq, D), BF16),
            jax.ShapeDtypeStruct((n_batch * seq, LANES), F32),
        ],
        compiler_params=_cparams(("parallel",)),
        name="ctx_mixer",
    )(z, qg2, kg2, lng, lnb, ws, bsf)


def _rope(x, cos, sin):
    first = (lax.broadcasted_iota(jnp.int32, (1, LANES), 1) % 32) < 16
    partner = jnp.where(first, pltpu.roll(x, LANES - 16, 1), pltpu.roll(x, 16, 1))
    return x * cos + partner * sin


def _lat_mixer_kernel(zq_ref, zkv_ref, nakc_ref, navc_ref, gkc_ref, gvc_ref, bias_ref,
                      cosq_ref, sinq_ref, cosk_ref, sink_ref,
                      qg_ref, kg_ref, lng_ref, lnb_ref, ws_ref, bsf_ref,
                      mix_ref,
                      kl_ref, kls_ref, vl_ref, vls_ref, kc_ref, kcs_ref, vc_ref, vcs_ref):
    scale = HEAD_DIM ** -0.5

    @pl.when(pl.program_id(1) == 0)
    def _():
        kb = _rope(_head_rms(zkv_ref[:, KB:KB + LANES], kg_ref[...]), cosk_ref[...], sink_ref[...])
        vb = zkv_ref[:, VB:VB + LANES]
        kl_ref[...] = kb.astype(BF16)
        kls_ref[...] = pltpu.roll(kb, HEAD_DIM, 1).astype(BF16)
        vl_ref[...] = vb.astype(BF16)
        vls_ref[...] = pltpu.roll(vb, HEAD_DIM, 1).astype(BF16)
        kc = gkc_ref[0]
        vc = gvc_ref[0]
        kc_ref[...] = kc.astype(BF16)
        kcs_ref[...] = pltpu.roll(kc, HEAD_DIM, 1).astype(BF16)
        vc_ref[...] = vc.astype(BF16)
        vcs_ref[...] = pltpu.roll(vc, HEAD_DIM, 1).astype(BF16)

    for j in range(2):
        cols = slice(LANES * j, LANES * (j + 1))
        q = zq_ref[:, QA + LANES * j:QA + LANES * (j + 1)] * scale
        ks = [zkv_ref[:, KA + LANES * j:KA + LANES * (j + 1)].astype(BF16), nakc_ref[0, :, cols].astype(BF16)]
        vs = [zkv_ref[:, VA + LANES * j:VA + LANES * (j + 1)].astype(BF16), navc_ref[0, :, cols].astype(BF16)]
        o = _attend_pair(q, ks, vs, ks, vs, [bias_ref[2 * j], None], [bias_ref[2 * j + 1], None])
        mix_ref[:, OA + LANES * j:OA + LANES * (j + 1)] = o.astype(mix_ref.dtype)

    same = ([kl_ref[...], kc_ref[...]], [vl_ref[...], vc_ref[...]])
    swap = ([kls_ref[...], kcs_ref[...]], [vls_ref[...], vcs_ref[...]])
    for j in range(4):
        q = _head_rms(zq_ref[:, QB + LANES * j:QB + LANES * (j + 1)], qg_ref[...])
        q = _rope(q, cosq_ref[...], sinq_ref[...]) * scale
        lo_kv, hi_kv = (same, swap) if j // 2 == 0 else (swap, same)
        o = _attend_pair(q, lo_kv[0], lo_kv[1], hi_kv[0], hi_kv[1], [None, None], [None, None])
        mix_ref[:, OB + LANES * j:OB + LANES * (j + 1)] = o.astype(mix_ref.dtype)

    _chunk_mlp(zq_ref[:, CU:CU + 256], zq_ref[:, CV:CV + 256], lng_ref, lnb_ref, ws_ref, bsf_ref, mix_ref, OC)


def _lat_mixer(z, row0, n_batch, n_tok, past, nakc, navc, gkc, gvc, bias, cos, sin, qg2, kg2, lng, lnb, ws, bsf):
    tq = 256
    nq = n_tok // tq
    small = lambda shape: pl.BlockSpec(shape, lambda b, t: (0,) * len(shape))
    qblk0 = row0 // tq
    kvblk0 = row0 // n_tok
    return pl.pallas_call(
        _lat_mixer_kernel,
        grid=(n_batch, nq),
        in_specs=[
            pl.BlockSpec((tq, IN_WIDTH), lambda b, t: (qblk0 + b * nq + t, 0)),
            pl.BlockSpec((n_tok, IN_WIDTH), lambda b, t: (kvblk0 + b, 0)),
            pl.BlockSpec((1, past, 256), lambda b, t: (b, 0, 0)),
            pl.BlockSpec((1, past, 256), lambda b, t: (b, 0, 0)),
            pl.BlockSpec((1, past, LANES), lambda b, t: (b, 0, 0)),
            pl.BlockSpec((1, past, LANES), lambda b, t: (b, 0, 0)),
            pl.BlockSpec((4, tq, n_tok), lambda b, t: (0, t, 0)),
            pl.BlockSpec((tq, LANES), lambda b, t: (t, 0)),
            pl.BlockSpec((tq, LANES), lambda b, t: (t, 0)),
            small((n_tok, LANES)), small((n_tok, LANES)),
            small((1, LANES)), small((1, LANES)), small((1, 256)), small((1, 256)),
            small((4, CHUNK, CHUNK)), small((CHUNK, 256)),
        ],
        out_specs=pl.BlockSpec((tq, D), lambda b, t: (b * nq + t, 0)),
        out_shape=jax.ShapeDtypeStruct((n_batch * n_tok, D), BF16),
        scratch_shapes=[pltpu.VMEM((n_tok, LANES), BF16)] * 4 + [pltpu.VMEM((past, LANES), BF16)] * 4,
        compiler_params=_cparams(("parallel", "arbitrary")),
        name="lat_mixer",
    )(z, z, nakc, navc, gkc, gvc, bias, cos, sin, cos, sin, qg2, kg2, lng, lnb, ws, bsf)


def _split_bf16(x):
    hi = x.astype(BF16)
    return hi, (x - hi.astype(F32)).astype(BF16)


def _outproj_kernel(n_top, mixa_ref, mixb_ref, xa_ref, xb_ref, g1_ref, shift_ref, scale_ref, g_ref, w_ref, rwt_ref, rb_ref,
                    xn_ref, ht_ref, ti_ref, gt_ref):
    top = pl.program_id(0) < n_top
    x = jnp.where(top, xa_ref[...], xb_ref[...])
    x = x + g1_ref[0] * _dot(jnp.where(top, mixa_ref[...], mixb_ref[...]), w_ref[...])
    xn_ref[...] = x
    ms = jnp.mean(x * x, axis=-1, keepdims=True)
    h = x * lax.rsqrt(ms + EPS) * g_ref[...]
    h = h * (1.0 + scale_ref[0]) + shift_ref[0]
    h_hi, h_lo = _split_bf16(h)
    rw_hi, rw_lo = _split_bf16(rwt_ref[...])
    logits = (_dot_nt(rw_hi, h_hi) + _dot_nt(rw_hi, h_lo)) + _dot_nt(rw_lo, h_hi) + rb_ref[...]

    expert = lax.broadcasted_iota(jnp.int32, logits.shape, 0)
    out_row = lax.broadcasted_iota(jnp.int32, ti_ref.shape, 0)
    top_i = jnp.zeros(ti_ref.shape, jnp.int32)
    top_v = []
    for k in range(TOP_K):
        m = jnp.max(logits, axis=0, keepdims=True)
        idx = jnp.min(jnp.where(logits == m, expert, N_EXPERTS), axis=0, keepdims=True)
        logits = jnp.where(expert == idx, -jnp.inf, logits)
        top_i = jnp.where(out_row == k, idx, top_i)
        top_v.append(m)
    es = [jnp.exp(v - top_v[0]) for v in top_v]
    den = (es[0] + es[1]) + (es[2] + es[3])
    gates = jnp.zeros(gt_ref.shape, F32)
    for k in range(TOP_K):
        gates = jnp.where(out_row == k, es[k] / den, gates)
    ti_ref[...] = top_i
    gt_ref[...] = gates

    for c in range(ROW_TILE):
        ht_ref[pl.ds(c, h.shape[0], stride=ROW_TILE), :] = h[:, LANES * c:LANES * (c + 1)]


def _outproj(mix_a, mix_b, xa, xb, bot_off_rows, modr, g, w, rwt, rb, tp, ts_per_batch):
    tm = 512
    t = mix_a.shape[0] + mix_b.shape[0]
    n_top = tp // tm
    return pl.pallas_call(
        functools.partial(_outproj_kernel, n_top),
        grid=(t // tm,),
        in_specs=_x_specs(tm, n_top, 0) + _x_specs(tm, n_top, bot_off_rows // tm) + [
            _mod_spec(2, tm, tp, ts_per_batch),
            _mod_spec(3, tm, tp, ts_per_batch),
            _mod_spec(4, tm, tp, ts_per_batch),
            pl.BlockSpec((1, D), lambda i: (0, 0)),
            pl.BlockSpec((D, D), lambda i: (0, 0)),
            pl.BlockSpec((N_EXPERTS, D), lambda i: (0, 0)),
            pl.BlockSpec((N_EXPERTS, 1), lambda i: (0, 0)),
        ],
        out_specs=[
            pl.BlockSpec((tm, D), lambda i: (i, 0)),
            pl.BlockSpec((tm * ROW_TILE, LANES), lambda i: (i, 0)),
            pl.BlockSpec((8, tm), lambda i: (0, i)),
            pl.BlockSpec((8, tm), lambda i: (0, i)),
        ],
        out_shape=[
            jax.ShapeDtypeStruct((t, D), F32),
            jax.ShapeDtypeStruct((t * ROW_TILE, LANES), F32),
            jax.ShapeDtypeStruct((8, t), jnp.int32),
            jax.ShapeDtypeStruct((8, t), F32),
        ],
        compiler_params=_cparams(("parallel",)),
        name="outproj",
    )(mix_a, mix_b, xa, xb, modr, modr, modr, g, w, rwt, rb)


def _prep_wgu_kernel(w_ref, p_ref, wg_ref, wu_ref):
    for j in range(w_ref.shape[2] // 256):
        w = w_ref[0, :, 256 * j:256 * (j + 1)].astype(BF16)
        sel = _dot(w, p_ref[...])
        wg_ref[0, :, LANES * j:LANES * (j + 1)] = sel[:, :LANES].astype(BF16)
        wu_ref[0, :, LANES * j:LANES * (j + 1)] = sel[:, LANES:].astype(BF16)


def _prep_wgu(w):
    e, d, f2 = w.shape
    perm = np.zeros((256, 256), np.float32)
    perm[2 * np.arange(LANES), np.arange(LANES)] = 1.0
    perm[2 * np.arange(LANES) + 1, LANES + np.arange(LANES)] = 1.0
    out = jax.ShapeDtypeStruct((e, d, f2 // 2), BF16)
    return pl.pallas_call(
        _prep_wgu_kernel,
        grid=(e,),
        in_specs=[pl.BlockSpec((1, d, f2), lambda i: (i, 0, 0)), pl.BlockSpec((256, 256), lambda i: (0, 0))],
        out_specs=[pl.BlockSpec((1, d, f2 // 2), lambda i: (i, 0, 0))] * 2,
        out_shape=[out, out],
        compiler_params=_cparams(("parallel",)),
        name="prep_wgu",
    )(w, jnp.asarray(perm, BF16))


def _moe_kernel(blk_e_ref, row_tok_ref, row_dst_ref, n_live_ref, wctl_ref, ht_ref, wg_hbm, wu_hbm, wd_hbm,
                bg_ref, bu_ref, bd_ref, out_ref, xg_ref, ybuf_ref, wbuf_ref, sem, wsem):
    i = pl.program_id(1)
    nb = pl.num_programs(1)
    step = pl.program_id(0) * nb + i
    n = pl.num_programs(0) * nb
    cur = step % 2
    nxt = 1 - cur
    ycur = step % Y_SLOTS
    yprev = (step + Y_SLOTS - 1) % Y_SLOTS
    blk = MOE_BLOCK * ROW_TILE

    def slot_copy(s):
        return pltpu.make_async_copy(ybuf_ref.at[pl.ds(s * blk, blk)], out_ref.at[pl.ds(0, blk)], sem.at[s])

    def gather_row(base, r, s):
        tok = row_tok_ref[base + r]
        xg_ref[pl.ds(pl.multiple_of(s * blk + r * ROW_TILE, ROW_TILE), ROW_TILE), :] = (
            ht_ref[pl.ds(pl.multiple_of(tok * ROW_TILE, ROW_TILE), ROW_TILE), :])

    def send_row(base, r, s, priority=0):
        dst = row_dst_ref[base + r]
        pltpu.make_async_copy(
            ybuf_ref.at[pl.ds(pl.multiple_of(s * blk + r * ROW_TILE, ROW_TILE), ROW_TILE)],
            out_ref.at[pl.ds(pl.multiple_of(dst * ROW_TILE, ROW_TILE), ROW_TILE)], sem.at[s]).start(priority)

    @pl.when(step == 0)
    def _():
        ybuf_ref[...] = jnp.zeros(ybuf_ref.shape, F32)
        for s in range(Y_SLOTS - 1):
            pltpu.make_async_copy(ybuf_ref.at[pl.ds(s * blk, blk)], out_ref.at[pl.ds((n + 1 + s) * blk, blk)],
                                  sem.at[s]).start()

    @pl.when(i == 0)
    def _():
        def first(r, carry):
            gather_row(step * MOE_BLOCK, r, cur)
            return carry

        lax.fori_loop(0, MOE_BLOCK, first, 0, unroll=8)

    wslot = wctl_ref[2 * n + step]

    def weight_copies(e, s):
        return [pltpu.make_async_copy(w.at[e], wbuf_ref.at[s, j], wsem.at[s, j])
                for j, w in enumerate((wg_hbm, wu_hbm, wd_hbm))]

    @pl.when(step == 0)
    def _():
        for cp in weight_copies(blk_e_ref[0], 0):
            cp.start()

    @pl.when(wctl_ref[step] == 1)
    def _():
        for cp in weight_copies(blk_e_ref[step], wslot):
            cp.wait()
        next_e = wctl_ref[n + step]

        @pl.when(next_e >= 0)
        def _():
            for cp in weight_copies(next_e, 1 - wslot):
                cp.start()

    slot_copy(ycur).wait()
    live = i < n_live_ref[pl.program_id(0)]

    @pl.when(jnp.logical_not(live))
    def _():
        def prev(r, carry):
            send_row(step * MOE_BLOCK, r, yprev)
            return carry

        lax.fori_loop(0, MOE_BLOCK, prev, 0, unroll=8)

    @pl.when(live)
    def _():
        _moe_block(step, i, nb, cur, nxt, ycur, yprev, blk, gather_row, send_row, xg_ref, ybuf_ref,
                   wbuf_ref.at[wslot, 0], wbuf_ref.at[wslot, 1], wbuf_ref.at[wslot, 2], bg_ref, bu_ref, bd_ref)

    @pl.when(step == n - 1)
    def _():
        def last(r, carry):
            send_row(n * MOE_BLOCK, r, ycur)
            return carry

        lax.fori_loop(0, MOE_BLOCK, last, 0, unroll=8)
        for s in range(Y_SLOTS):
            slot_copy(s).wait()


def _moe_block(step, i, nb, cur, nxt, ycur, yprev, blk, gather_row, send_row,
               xg_ref, ybuf_ref, wg_ref, wu_ref, wd_ref, bg_ref, bu_ref, bd_ref):
    x = jnp.concatenate(
        [xg_ref[pl.ds(cur * blk + c, MOE_BLOCK, stride=ROW_TILE), :].astype(BF16) for c in range(ROW_TILE)], axis=1)
    next_blk = jnp.where(i + 1 < nb, step + 1, step)
    for r in range(MOE_BLOCK):
        gather_row(next_blk * MOE_BLOCK, r, nxt)
    for r in range(MOE_BLOCK):
        send_row(step * MOE_BLOCK, r, yprev, priority=r % 2)
    g = _dot(x, wg_ref[...]) + bg_ref[0]
    u = _dot(x, wu_ref[...]) + bu_ref[0]
    g = jnp.minimum(g, SWIGLU_LIMIT)
    u = jnp.clip(u, -SWIGLU_LIMIT, SWIGLU_LIMIT)
    act = (u + 1.0) * (g * jax.nn.sigmoid(SWIGLU_ALPHA * g))
    y = _dot(act.astype(BF16), wd_ref[...]) + bd_ref[0]
    for c in range(ROW_TILE):
        ybuf_ref[pl.ds(ycur * blk + c, MOE_BLOCK, stride=ROW_TILE), :] = y[:, LANES * c:LANES * (c + 1)]


def _weight_schedule(blk_e):
    n = blk_e.shape[0]
    start = jnp.concatenate([jnp.ones((1,), jnp.int32), (blk_e[1:] != blk_e[:-1]).astype(jnp.int32)])
    parity = (jnp.cumsum(start) - 1) % 2
    idx = jnp.arange(n, dtype=jnp.int32)
    later_start = (idx[None, :] > idx[:, None]) & (start[None, :] == 1)
    j_next = jnp.min(jnp.where(later_start, idx[None, :], n), axis=1)
    next_e = jnp.where(j_next < n, blk_e[jnp.minimum(j_next, n - 1)], -1)
    return jnp.concatenate([start, next_e, parity]).astype(jnp.int32)


def _moe(blk_e, row_tok, row_dst, n_live, ht, n_groups, wg, wu, wd, bg, bu, bd):
    nb = blk_e.shape[0] // n_groups
    tg = ht.shape[0] // ROW_TILE // n_groups
    ew = lambda g, i, be, rt, rd, nl, wc: (be[g * nb + i], 0, 0)
    hbm = pl.BlockSpec(memory_space=pl.ANY)
    return pl.pallas_call(
        _moe_kernel,
        grid_spec=pltpu.PrefetchScalarGridSpec(
            num_scalar_prefetch=5,
            grid=(n_groups, nb),
            in_specs=[
                pl.BlockSpec((tg * ROW_TILE, LANES), lambda g, i, be, rt, rd, nl, wc: (g, 0),
                             pipeline_mode=pl.Buffered(1)),
                hbm, hbm, hbm,
                pl.BlockSpec((1, 1, D), ew),
                pl.BlockSpec((1, 1, D), ew),
                pl.BlockSpec((1, 1, D), ew),
            ],
            out_specs=pl.BlockSpec(memory_space=pl.ANY),
            scratch_shapes=[
                pltpu.VMEM((2 * MOE_BLOCK * ROW_TILE, LANES), F32),
                pltpu.VMEM((Y_SLOTS * MOE_BLOCK * ROW_TILE, LANES), F32),
                pltpu.VMEM((2, 3, D, D), BF16),
                pltpu.SemaphoreType.DMA((Y_SLOTS,)),
                pltpu.SemaphoreType.DMA((2, 3)),
            ],
        ),
        out_shape=jax.ShapeDtypeStruct(((n_groups * nb + Y_SLOTS) * MOE_BLOCK * ROW_TILE, LANES), F32),
        compiler_params=_cparams(("arbitrary", "arbitrary")),
        name="moe_experts",
    )(blk_e, row_tok, row_dst, n_live, _weight_schedule(blk_e), ht, wg, wu, wd, bg, bu, bd)


def _route(top_i, group, n_groups):
    tg = top_i.shape[0]
    n_assign = tg * TOP_K
    assert n_assign <= 1 << 16 and n_assign % MOE_BLOCK == 0
    flat_e = top_i.reshape(-1).astype(jnp.int32)
    experts = jnp.arange(N_EXPERTS, dtype=jnp.int32)
    counts = jnp.sum((flat_e[:, None] == experts[None, :]).astype(jnp.int32), axis=0)
    padded = (counts + MOE_BLOCK - 1) // MOE_BLOCK * MOE_BLOCK
    pad_end = jnp.cumsum(padded)
    n_blocks = n_assign // MOE_BLOCK + N_EXPERTS
    real_keys = (flat_e << 17) | jnp.arange(n_assign, dtype=jnp.int32)
    pad_id = experts[:, None] * MOE_BLOCK + jnp.arange(MOE_BLOCK, dtype=jnp.int32)[None, :]
    used = jnp.arange(MOE_BLOCK, dtype=jnp.int32)[None, :] < (padded - counts)[:, None]
    pad_keys = (jnp.where(used, experts[:, None], 63) << 17) | (1 << 16) | pad_id
    keys = jnp.sort(jnp.concatenate([real_keys, pad_keys.reshape(-1)]))
    is_pad = ((keys >> 16) & 1) == 1
    payload = keys & 0xFFFF
    tok = payload // TOP_K
    t_all = tg * n_groups
    row_tok = jnp.where(is_pad, 0, tok)
    row_dst = jnp.where(is_pad, TOP_K * t_all + group * (N_EXPERTS * MOE_BLOCK) + payload,
                        (payload % TOP_K) * t_all + group * tg + tok)
    blk_start = jnp.arange(n_blocks, dtype=jnp.int32) * MOE_BLOCK
    blk_e = jnp.sum((pad_end[None, :] <= blk_start[:, None]).astype(jnp.int32), axis=1)
    blk_e = jnp.minimum(blk_e, N_EXPERTS - 1)
    n_live = pad_end[-1] // MOE_BLOCK
    return blk_e, row_tok, row_dst, n_live


def _route_groups(top_i, n_groups):
    tg = top_i.shape[0] // n_groups
    parts = [_route(top_i[g * tg:(g + 1) * tg], g, n_groups) for g in range(n_groups)]
    blk_e = jnp.concatenate([p[0] for p in parts])
    row_tok = jnp.concatenate([p[1] for p in parts])
    spare = row_tok.shape[0] + jnp.arange(MOE_BLOCK, dtype=jnp.int32)
    row_dst = jnp.concatenate([spare] + [p[2] for p in parts])
    n_live = jnp.stack([p[3] for p in parts]).astype(jnp.int32)
    return blk_e, row_tok, row_dst, n_live


def _combine_kernel(final, y0_ref, y1_ref, y2_ref, y3_ref, gates_ref, x_ref, g2_ref, gf_ref, o_ref):
    g = gates_ref[...]
    tm = x_ref.shape[0]
    for c in range(ROW_TILE):
        rows = pl.ds(c, tm, stride=ROW_TILE)
        cols = slice(LANES * c, LANES * (c + 1))
        y = ((y0_ref[rows, :] * g[:, 0:1] + y1_ref[rows, :] * g[:, 1:2])
             + (y2_ref[rows, :] * g[:, 2:3] + y3_ref[rows, :] * g[:, 3:4]))
        o_ref[:, cols] = x_ref[:, cols] + g2_ref[0][:, cols] * y
    if final:
        x = o_ref[...]
        ms = jnp.mean(x * x, axis=-1, keepdims=True)
        o_ref[...] = x * lax.rsqrt(ms + EPS) * gf_ref[...]


def _combine(ys, gates, x, modr, tp, ts_per_batch, row0, n_rows, final_g=None):
    tm = 512
    nt_all = x.shape[0] // tm
    i0 = row0 // tm
    yspec = lambda k: pl.BlockSpec((tm * ROW_TILE, LANES), lambda i: (k * nt_all + i0 + i, 0))
    mod = pl.BlockSpec((1, 1, D), lambda i: (_mod_row(i0 + i, tm, tp, ts_per_batch) * 6 + 5, 0, 0))
    gf = jnp.ones((1, D), F32) if final_g is None else final_g
    return pl.pallas_call(
        functools.partial(_combine_kernel, final_g is not None),
        grid=(n_rows // tm,),
        in_specs=[yspec(0), yspec(1), yspec(2), yspec(3), pl.BlockSpec((tm, TOP_K), lambda i: (i0 + i, 0)),
                  pl.BlockSpec((tm, D), lambda i: (i0 + i, 0)), mod, pl.BlockSpec((1, D), lambda i: (0, 0))],
        out_specs=pl.BlockSpec((tm, D), lambda i: (i, 0)),
        out_shape=jax.ShapeDtypeStruct((n_rows, D), F32),
        compiler_params=_cparams(("parallel",)),
        name="moe_combine",
    )(ys, ys, ys, ys, gates, x, modr, gf)


def _na_bias(rpb, rows):
    kr = min(NA_WIN_R, rows)
    r = np.arange(rows)
    r0 = np.clip(r - kr // 2, 0, rows - kr)
    row_ok = (r[None, :] >= r0[:, None]) & (r[None, :] < r0[:, None] + kr)
    dr = np.clip(r[None, :] - r[:, None] + (NA_WIN_R - 1), 0, 2 * NA_WIN_R - 2)
    c = np.arange(GRID_W)
    w0 = np.clip(c - NA_WIN_C // 2, 0, GRID_W - NA_WIN_C)
    col_ok = (c[None, :] >= w0[:, None]) & (c[None, :] < w0[:, None] + NA_WIN_C)
    dc = np.clip(c[None, :] - c[:, None] + (NA_WIN_C - 1), 0, 2 * NA_WIN_C - 2)
    oh_r = np.eye(2 * NA_WIN_R - 1, dtype=np.float32)[dr]
    oh_c = np.eye(2 * NA_WIN_C - 1, dtype=np.float32)[dc]
    by_row = jnp.einsum('rkd,hde->hrke', oh_r, rpb.astype(F32), precision=HI)
    vals = jnp.einsum('hrke,qce->hrqkc', by_row, oh_c, precision=HI)
    ok = row_ok[:, None, :, None] & col_ok[None, :, None, :]
    n = rows * GRID_W
    return jnp.where(ok[None], vals, NEG_INF).reshape(rpb.shape[0], n, n)


def _rope_tables(n_tokens):
    t = np.arange(n_tokens)
    row = (t // GRID_W).astype(np.float32)
    col = (t % GRID_W).astype(np.float32)
    half = HEAD_DIM // 2
    inv = jnp.asarray(ROPE_THETA, F32) ** (-jnp.arange(0, half, 2, dtype=F32) / half)
    ang_r = jnp.asarray(row)[:, None] * inv
    ang_c = jnp.asarray(col)[:, None] * inv
    cr, sr, cc, sc = jnp.cos(ang_r), jnp.sin(ang_r), jnp.cos(ang_c), jnp.sin(ang_c)
    cos = jnp.concatenate([cr, cr, cc, cc] * 2, axis=-1)
    sin = jnp.concatenate([-sr, sr, -sc, sc] * 2, axis=-1)
    return cos, sin


def _to_heads(x, n_batch, seq, n_heads):
    return x.reshape(n_batch, seq, n_heads, HEAD_DIM).transpose(0, 2, 1, 3)


def _ctx_lanes(cache_l):
    b, h, p, dh = cache_l.shape
    return cache_l.transpose(0, 2, 1, 3).reshape(b, p, h * dh)


def kernel(x_prompt, x_sample, cache_na_k, cache_na_v, cache_gqa_k, cache_gqa_v, c, c_ctx, w_mod, b_mod, norm1_g, norm2_g, w_in, na_rpb, q_norm_g, k_norm_g, cm_ln_g, cm_ln_b, cm_ws, cm_bs, w_out, router_w, router_b, w_gate_up, b_gate_up, w_down, b_down, final_norm_g):
    nb, seq, _ = x_prompt.shape
    db, n_tok, _ = x_sample.shape
    past = cache_na_k.shape[3]
    tp, ts = nb * seq, db * n_tok
    assert db <= CTX_ROW and n_tok % GRID_W == 0
    assert tp == ts

    cond = jnp.zeros((MOD_ROWS, D), F32).at[:db].set(c).at[CTX_ROW].set(c_ctx)
    mod = _modulation(cond, w_mod, b_mod)
    cos, sin = _rope_tables(n_tok)
    xa, xb, xb_row0 = x_prompt.reshape(tp, D), x_sample.reshape(ts, D), 0

    w_in16 = w_in.astype(BF16)
    w_out16 = w_out.astype(BF16)
    wg16, wu16 = _prep_wgu(w_gate_up.reshape(DEPTH * N_EXPERTS, D, 2 * D))
    wd16 = w_down.astype(BF16).reshape(DEPTH * N_EXPERTS, D, D)
    bg = b_gate_up[..., 0::2].reshape(DEPTH * N_EXPERTS, 1, D)
    bu = b_gate_up[..., 1::2].reshape(DEPTH * N_EXPERTS, 1, D)
    bd = b_down.reshape(DEPTH * N_EXPERTS, 1, D)

    na_k, na_v, gqa_k, gqa_v = [], [], [], []
    for l in range(DEPTH):
        modr = mod[l].reshape(MOD_ROWS * 6, 1, D)
        qg2 = jnp.tile(q_norm_g[l], 2).reshape(1, LANES)
        kg2 = jnp.tile(k_norm_g[l], 2).reshape(1, LANES)
        lng = cm_ln_g[l].reshape(1, 256)
        lnb = cm_ln_b[l].reshape(1, 256)
        bsf = jnp.repeat(cm_bs[l].T, HEAD_DIM, axis=1)

        z = _inproj(xa, xb, xb_row0, tp + ts, modr, norm1_g[l].reshape(1, D), w_in16[l], tp, n_tok)
        mix_p, kbn = _ctx_mixer(z, nb, seq, qg2, kg2, lng, lnb, cm_ws[l], bsf)
        mix_s = _lat_mixer(z, tp, db, n_tok, past,
                           _ctx_lanes(cache_na_k[:, l]), _ctx_lanes(cache_na_v[:, l]),
                           _ctx_lanes(cache_gqa_k[:, l]), _ctx_lanes(cache_gqa_v[:, l]),
                           _na_bias(na_rpb[l], n_tok // GRID_W), cos, sin, qg2, kg2, lng, lnb, cm_ws[l], bsf)
        na_k.append(_to_heads(z[:tp, KA:KA + 256], nb, seq, 4))
        na_v.append(_to_heads(z[:tp, VA:VA + 256], nb, seq, 4))
        gqa_k.append(_to_heads(kbn, nb, seq, 2))
        gqa_v.append(_to_heads(z[:tp, VB:VB + LANES], nb, seq, 2))

        xn, ht, top_i, gates = _outproj(mix_p, mix_s, xa, xb, xb_row0, modr, norm2_g[l].reshape(1, D), w_out16[l],
                                  router_w[l].T, router_b[l].reshape(N_EXPERTS, 1), tp, n_tok)

        blk_e, row_tok, row_dst, n_live = _route_groups(top_i[:TOP_K].T, 2)
        gates = gates[:TOP_K].T
        ys = _moe(blk_e + l * N_EXPERTS, row_tok, row_dst, n_live, ht, 2, wg16, wu16, wd16, bg, bu, bd)
        if l + 1 < DEPTH:
            x = _combine(ys, gates, xn, modr, tp, n_tok, 0, tp + ts)
            xa, xb, xb_row0 = x, x, tp

    gf = final_norm_g.reshape(1, D)
    y_p = _combine(ys, gates, xn, modr, tp, n_tok, 0, tp, gf)
    y_s = _combine(ys, gates, xn, modr, tp, n_tok, tp, ts, gf)
    return (y_p.reshape(nb, seq, D), y_s.reshape(db, n_tok, D),
            jnp.stack(na_k, axis=1), jnp.stack(na_v, axis=1), jnp.stack(gqa_k, axis=1), jnp.stack(gqa_v, axis=1))
```

```python
import functools

import jax
import jax.numpy as jnp
import numpy as np
from jax import lax
from jax.experimental import pallas as pl
from jax.experimental.pallas import tpu as pltpu

D = 1024
DEPTH = 2
GRID_W = 64
HEAD_DIM = 64
NA_WIN_R = 8
NA_WIN_C = 16
CHUNK = 128
ROPE_THETA = 10000.0
N_EXPERTS = 32
TOP_K = 4
SWIGLU_LIMIT = 7.0
SWIGLU_ALPHA = 1.702
EPS = 1e-6
MOE_BLOCK = 128
Y_SLOTS = 3
NEG_INF = -1e30
IN_WIDTH = 2048

QA, KA, VA, QB, KB, VB, CU, CV = 0, 256, 512, 768, 1280, 1408, 1536, 1792
OA, OB, OC = 0, 256, 768

LANES = 128
ROW_TILE = D // LANES
MOD_ROWS = 16
CTX_ROW = 8
VMEM_LIMIT = 56 * 1024 * 1024

F32 = jnp.float32
BF16 = jnp.bfloat16
HI = lax.Precision.HIGHEST


def _cparams(sem):
    return pltpu.CompilerParams(dimension_semantics=sem, vmem_limit_bytes=VMEM_LIMIT)


def _dot(a, b):
    return jnp.dot(a, b, preferred_element_type=F32)


def _dot_nt(a, b):
    return lax.dot_general(a, b, (((1,), (1,)), ((), ())), preferred_element_type=F32)


def _lane_lo():
    return lax.broadcasted_iota(jnp.int32, (1, LANES), 1) < HEAD_DIM


def _mod_kernel(c_ref, w_ref, b_ref, o_ref):
    c = c_ref[...]
    s = c * jax.nn.sigmoid(c)
    o_ref[0] = jnp.dot(s, w_ref[0], preferred_element_type=F32, precision=HI) + b_ref[0]


def _modulation(cond, w_mod, b_mod):
    tn = 1536
    n = w_mod.shape[-1]
    return pl.pallas_call(
        _mod_kernel,
        grid=(DEPTH, n // tn),
        in_specs=[
            pl.BlockSpec((MOD_ROWS, D), lambda l, j: (0, 0)),
            pl.BlockSpec((1, D, tn), lambda l, j: (l, 0, j)),
            pl.BlockSpec((1, 1, tn), lambda l, j: (l, 0, j)),
        ],
        out_specs=pl.BlockSpec((1, MOD_ROWS, tn), lambda l, j: (l, 0, j)),
        out_shape=jax.ShapeDtypeStruct((DEPTH, MOD_ROWS, n), F32),
        compiler_params=_cparams(("arbitrary", "arbitrary")),
        name="modulation",
    )(cond, w_mod, b_mod.reshape(DEPTH, 1, n))


def _mod_row(i, tm, tp, ts_per_batch):
    start = i * tm
    return jnp.where(start < tp, CTX_ROW, (start - tp) // ts_per_batch)


def _mod_spec(k, tm, tp, ts_per_batch):
    return pl.BlockSpec((1, 1, D), lambda i: (_mod_row(i, tm, tp, ts_per_batch) * 6 + k, 0, 0))


def _x_specs(tm, n_top, bot_off, width=D):
    return [pl.BlockSpec((tm, width), lambda i: (jnp.minimum(i, n_top - 1), 0)),
            pl.BlockSpec((tm, width), lambda i: (jnp.maximum(i - n_top, 0) + bot_off, 0))]


def _inproj_kernel(n_top, xa_ref, xb_ref, shift_ref, scale_ref, g_ref, w_ref, z_ref):
    x = jnp.where(pl.program_id(0) < n_top, xa_ref[...], xb_ref[...])
    ms = jnp.mean(x * x, axis=-1, keepdims=True)
    y = x * lax.rsqrt(ms + EPS) * g_ref[...]
    h = y * (1.0 + scale_ref[0]) + shift_ref[0]
    z_ref[...] = _dot(h.astype(BF16), w_ref[...])


def _inproj(xa, xb, bot_off_rows, t, modr, g, w, tp, ts_per_batch):
    tm = 512
    n_top = tp // tm
    return pl.pallas_call(
        functools.partial(_inproj_kernel, n_top),
        grid=(t // tm,),
        in_specs=_x_specs(tm, n_top, bot_off_rows // tm) + [
            _mod_spec(0, tm, tp, ts_per_batch),
            _mod_spec(1, tm, tp, ts_per_batch),
            pl.BlockSpec((1, D), lambda i: (0, 0)),
            pl.BlockSpec((D, IN_WIDTH), lambda i: (0, 0)),
        ],
        out_specs=pl.BlockSpec((tm, IN_WIDTH), lambda i: (i, 0)),
        out_shape=jax.ShapeDtypeStruct((t, IN_WIDTH), F32),
        compiler_params=_cparams(("parallel",)),
        name="inproj",
    )(xa, xb, modr, modr, g, w)


def _head_rms(x, g2):
    lo = _lane_lo()
    x2 = x * x
    s_lo = jnp.sum(jnp.where(lo, x2, 0.0), axis=-1, keepdims=True)
    s_hi = jnp.sum(jnp.where(lo, 0.0, x2), axis=-1, keepdims=True)
    ms = jnp.where(lo, s_lo, s_hi) * (1.0 / HEAD_DIM)
    return x * lax.rsqrt(ms + EPS) * g2


def _softmax_pv(q16, ks, vs, biases):
    ss = []
    for k, b in zip(ks, biases):
        s = _dot_nt(q16, k)
        if b is not None:
            s = s + b
        ss.append(s)
    m = ss[0].max(axis=-1, keepdims=True)
    for s in ss[1:]:
        m = jnp.maximum(m, s.max(axis=-1, keepdims=True))
    acc = None
    for s, v in zip(ss, vs):
        o = _dot(jnp.exp((s - m).astype(BF16)), v)
        acc = o if acc is None else acc + o
    return acc / pltpu.roll(acc, HEAD_DIM, 1)


def _attend_pair(q, ks_lo, vs_lo, ks_hi, vs_hi, b_lo, b_hi):
    lo = _lane_lo()
    ones = jnp.ones((1, LANES), BF16)
    o_lo = _softmax_pv(jnp.where(lo, q, 0.0).astype(BF16), ks_lo, [jnp.where(lo, v, ones) for v in vs_lo], b_lo)
    o_hi = _softmax_pv(jnp.where(lo, 0.0, q).astype(BF16), ks_hi, [jnp.where(lo, ones, v) for v in vs_hi], b_hi)
    return jnp.where(lo, o_lo, o_hi)


def _gelu(x):
    c = np.sqrt(2.0 / np.pi).astype(np.float32)
    return x * (0.5 * (1.0 + jnp.tanh(c * (x + 0.044715 * (x * x * x)))))


def _layer_norm(x, g, b):
    mu = jnp.mean(x, axis=-1, keepdims=True)
    xc = x - mu
    var = jnp.mean(xc * xc, axis=-1, keepdims=True)
    return xc * lax.rsqrt(var + EPS) * g + b


def _chunk_mlp(cu, cv, lng_ref, lnb_ref, ws_ref, bsf_ref, out_ref, col0):
    lo = _lane_lo()
    u = _gelu(cu)
    v = _layer_norm(_gelu(cv), lng_ref[...], lnb_ref[...]).astype(BF16)
    s = cu.shape[0]
    for n in range(s // CHUNK):
        rows = slice(n * CHUNK, (n + 1) * CHUNK)
        for jb in range(2):
            cols = slice(jb * LANES, (jb + 1) * LANES)
            vb = v[rows, cols]
            m_lo = _dot(ws_ref[2 * jb].astype(BF16), vb)
            m_hi = _dot(ws_ref[2 * jb + 1].astype(BF16), vb)
            mixed = jnp.where(lo, m_lo, m_hi) + bsf_ref[:, cols]
            out_ref[rows, col0 + jb * LANES:col0 + (jb + 1) * LANES] = (u[rows, cols] * mixed).astype(out_ref.dtype)


def _ctx_mixer_kernel(z_ref, qg_ref, kg_ref, lng_ref, lnb_ref, ws_ref, bsf_ref, mix_ref, kbn_ref):
    scale = HEAD_DIM ** -0.5
    for j in range(2):
        q = z_ref[:, QA + LANES * j:QA + LANES * (j + 1)] * scale
        k = [z_ref[:, KA + LANES * j:KA + LANES * (j + 1)].astype(BF16)]
        v = [z_ref[:, VA + LANES * j:VA + LANES * (j + 1)].astype(BF16)]
        o = _attend_pair(q, k, v, k, v, [None], [None])
        mix_ref[:, OA + LANES * j:OA + LANES * (j + 1)] = o.astype(mix_ref.dtype)

    kb = _head_rms(z_ref[:, KB:KB + LANES], kg_ref[...])
    kbn_ref[...] = kb
    vb = z_ref[:, VB:VB + LANES]
    k_same = [kb.astype(BF16)]
    k_swap = [pltpu.roll(kb, HEAD_DIM, 1).astype(BF16)]
    v_same = [vb.astype(BF16)]
    v_swap = [pltpu.roll(vb, HEAD_DIM, 1).astype(BF16)]
    for j in range(4):
        q = _head_rms(z_ref[:, QB + LANES * j:QB + LANES * (j + 1)], qg_ref[...]) * scale
        if j // 2 == 0:
            o = _attend_pair(q, k_same, v_same, k_swap, v_swap, [None], [None])
        else:
            o = _attend_pair(q, k_swap, v_swap, k_same, v_same, [None], [None])
        mix_ref[:, OB + LANES * j:OB + LANES * (j + 1)] = o.astype(mix_ref.dtype)

    _chunk_mlp(z_ref[:, CU:CU + 256], z_ref[:, CV:CV + 256], lng_ref, lnb_ref, ws_ref, bsf_ref, mix_ref, OC)


def _ctx_mixer(z, n_batch, seq, qg2, kg2, lng, lnb, ws, bsf):
    small = lambda shape: pl.BlockSpec(shape, lambda b: (0,) * len(shape))
    return pl.pallas_call(
        _ctx_mixer_kernel,
        grid=(n_batch,),
        in_specs=[
            pl.BlockSpec((seq, IN_WIDTH), lambda b: (b, 0)),
            small((1, LANES)), small((1, LANES)), small((1, 256)), small((1, 256)),
            small((4, CHUNK, CHUNK)), small((CHUNK, 256)),
        ],
        out_specs=[
            pl.BlockSpec((seq, D), lambda b: (b, 0)),
            pl.BlockSpec((seq, LANES), lambda b: (b, 0)),
        ],
        out_shape=[
            jax.ShapeDtypeStruct((n_batch * seq, D), BF16),
            jax.ShapeDtypeStruct((n_batch * seq, LANES), F32),
        ],
        compiler_params=_cparams(("parallel",)),
        name="ctx_mixer",
    )(z, qg2, kg2, lng, lnb, ws, bsf)


def _rope(x, cos, sin):
    first = (lax.broadcasted_iota(jnp.int32, (1, LANES), 1) % 32) < 16
    partner = jnp.where(first, pltpu.roll(x, LANES - 16, 1), pltpu.roll(x, 16, 1))
    return x * cos + partner * sin


def _lat_mixer_kernel(zq_ref, zkv_ref, nakc_ref, navc_ref, gkc_ref, gvc_ref, bias_ref,
                      cosq_ref, sinq_ref, cosk_ref, sink_ref,
                      qg_ref, kg_ref, lng_ref, lnb_ref, ws_ref, bsf_ref,
                      mix_ref,
                      kl_ref, kls_ref, vl_ref, vls_ref, kc_ref, kcs_ref, vc_ref, vcs_ref):
    scale = HEAD_DIM ** -0.5

    @pl.when(pl.program_id(1) == 0)
    def _():
        kb = _rope(_head_rms(zkv_ref[:, KB:KB + LANES], kg_ref[...]), cosk_ref[...], sink_ref[...])
        vb = zkv_ref[:, VB:VB + LANES]
        kl_ref[...] = kb.astype(BF16)
        kls_ref[...] = pltpu.roll(kb, HEAD_DIM, 1).astype(BF16)
        vl_ref[...] = vb.astype(BF16)
        vls_ref[...] = pltpu.roll(vb, HEAD_DIM, 1).astype(BF16)
        kc = gkc_ref[0]
        vc = gvc_ref[0]
        kc_ref[...] = kc.astype(BF16)
        kcs_ref[...] = pltpu.roll(kc, HEAD_DIM, 1).astype(BF16)
        vc_ref[...] = vc.astype(BF16)
        vcs_ref[...] = pltpu.roll(vc, HEAD_DIM, 1).astype(BF16)

    for j in range(2):
        cols = slice(LANES * j, LANES * (j + 1))
        q = zq_ref[:, QA + LANES * j:QA + LANES * (j + 1)] * scale
        ks = [zkv_ref[:, KA + LANES * j:KA + LANES * (j + 1)].astype(BF16), nakc_ref[0, :, cols].astype(BF16)]
        vs = [zkv_ref[:, VA + LANES * j:VA + LANES * (j + 1)].astype(BF16), navc_ref[0, :, cols].astype(BF16)]
        o = _attend_pair(q, ks, vs, ks, vs, [bias_ref[2 * j], None], [bias_ref[2 * j + 1], None])
        mix_ref[:, OA + LANES * j:OA + LANES * (j + 1)] = o.astype(mix_ref.dtype)

    same = ([kl_ref[...], kc_ref[...]], [vl_ref[...], vc_ref[...]])
    swap = ([kls_ref[...], kcs_ref[...]], [vls_ref[...], vcs_ref[...]])
    for j in range(4):
        q = _head_rms(zq_ref[:, QB + LANES * j:QB + LANES * (j + 1)], qg_ref[...])
        q = _rope(q, cosq_ref[...], sinq_ref[...]) * scale
        lo_kv, hi_kv = (same, swap) if j // 2 == 0 else (swap, same)
        o = _attend_pair(q, lo_kv[0], lo_kv[1], hi_kv[0], hi_kv[1], [None, None], [None, None])
        mix_ref[:, OB + LANES * j:OB + LANES * (j + 1)] = o.astype(mix_ref.dtype)

    _chunk_mlp(zq_ref[:, CU:CU + 256], zq_ref[:, CV:CV + 256], lng_ref, lnb_ref, ws_ref, bsf_ref, mix_ref, OC)


def _lat_mixer(z, row0, n_batch, n_tok, past, nakc, navc, gkc, gvc, bias, cos, sin, qg2, kg2, lng, lnb, ws, bsf):
    tq = 256
    nq = n_tok // tq
    small = lambda shape: pl.BlockSpec(shape, lambda b, t: (0,) * len(shape))
    qblk0 = row0 // tq
    kvblk0 = row0 // n_tok
    return pl.pallas_call(
        _lat_mixer_kernel,
        grid=(n_batch, nq),
        in_specs=[
            pl.BlockSpec((tq, IN_WIDTH), lambda b, t: (qblk0 + b * nq + t, 0)),
            pl.BlockSpec((n_tok, IN_WIDTH), lambda b, t: (kvblk0 + b, 0)),
            pl.BlockSpec((1, past, 256), lambda b, t: (b, 0, 0)),
            pl.BlockSpec((1, past, 256), lambda b, t: (b, 0, 0)),
            pl.BlockSpec((1, past, LANES), lambda b, t: (b, 0, 0)),
            pl.BlockSpec((1, past, LANES), lambda b, t: (b, 0, 0)),
            pl.BlockSpec((4, tq, n_tok), lambda b, t: (0, t, 0)),
            pl.BlockSpec((tq, LANES), lambda b, t: (t, 0)),
            pl.BlockSpec((tq, LANES), lambda b, t: (t, 0)),
            small((n_tok, LANES)), small((n_tok, LANES)),
            small((1, LANES)), small((1, LANES)), small((1, 256)), small((1, 256)),
            small((4, CHUNK, CHUNK)), small((CHUNK, 256)),
        ],
        out_specs=pl.BlockSpec((tq, D), lambda b, t: (b * nq + t, 0)),
        out_shape=jax.ShapeDtypeStruct((n_batch * n_tok, D), BF16),
        scratch_shapes=[pltpu.VMEM((n_tok, LANES), BF16)] * 4 + [pltpu.VMEM((past, LANES), BF16)] * 4,
        compiler_params=_cparams(("parallel", "arbitrary")),
        name="lat_mixer",
    )(z, z, nakc, navc, gkc, gvc, bias, cos, sin, cos, sin, qg2, kg2, lng, lnb, ws, bsf)


def _split_bf16(x):
    hi = x.astype(BF16)
    return hi, (x - hi.astype(F32)).astype(BF16)


def _outproj_kernel(n_top, mixa_ref, mixb_ref, xa_ref, xb_ref, g1_ref, shift_ref, scale_ref, g_ref, w_ref, rwt_ref, rb_ref,
                    xn_ref, ht_ref, ti_ref, gt_ref):
    top = pl.program_id(0) < n_top
    x = jnp.where(top, xa_ref[...], xb_ref[...])
    x = x + g1_ref[0] * _dot(jnp.where(top, mixa_ref[...], mixb_ref[...]), w_ref[...])
    xn_ref[...] = x
    ms = jnp.mean(x * x, axis=-1, keepdims=True)
    h = x * lax.rsqrt(ms + EPS) * g_ref[...]
    h = h * (1.0 + scale_ref[0]) + shift_ref[0]
    h_hi, h_lo = _split_bf16(h)
    rw_hi, rw_lo = _split_bf16(rwt_ref[...])
    logits = (_dot_nt(rw_hi, h_hi) + _dot_nt(rw_hi, h_lo)) + _dot_nt(rw_lo, h_hi) + rb_ref[...]

    expert = lax.broadcasted_iota(jnp.int32, logits.shape, 0)
    out_row = lax.broadcasted_iota(jnp.int32, ti_ref.shape, 0)
    top_i = jnp.zeros(ti_ref.shape, jnp.int32)
    top_v = []
    for k in range(TOP_K):
        m = jnp.max(logits, axis=0, keepdims=True)
        idx = jnp.min(jnp.where(logits == m, expert, N_EXPERTS), axis=0, keepdims=True)
        logits = jnp.where(expert == idx, -jnp.inf, logits)
        top_i = jnp.where(out_row == k, idx, top_i)
        top_v.append(m)
    es = [jnp.exp(v - top_v[0]) for v in top_v]
    den = (es[0] + es[1]) + (es[2] + es[3])
    gates = jnp.zeros(gt_ref.shape, F32)
    for k in range(TOP_K):
        gates = jnp.where(out_row == k, es[k] / den, gates)
    ti_ref[...] = top_i
    gt_ref[...] = gates

    for c in range(ROW_TILE):
        ht_ref[pl.ds(c, h.shape[0], stride=ROW_TILE), :] = h[:, LANES * c:LANES * (c + 1)]


def _outproj(mix_a, mix_b, xa, xb, bot_off_rows, modr, g, w, rwt, rb, tp, ts_per_batch):
    tm = 512
    t = mix_a.shape[0] + mix_b.shape[0]
    n_top = tp // tm
    return pl.pallas_call(
        functools.partial(_outproj_kernel, n_top),
        grid=(t // tm,),
        in_specs=_x_specs(tm, n_top, 0) + _x_specs(tm, n_top, bot_off_rows // tm) + [
            _mod_spec(2, tm, tp, ts_per_batch),
            _mod_spec(3, tm, tp, ts_per_batch),
            _mod_spec(4, tm, tp, ts_per_batch),
            pl.BlockSpec((1, D), lambda i: (0, 0)),
            pl.BlockSpec((D, D), lambda i: (0, 0)),
            pl.BlockSpec((N_EXPERTS, D), lambda i: (0, 0)),
            pl.BlockSpec((N_EXPERTS, 1), lambda i: (0, 0)),
        ],
        out_specs=[
            pl.BlockSpec((tm, D), lambda i: (i, 0)),
            pl.BlockSpec((tm * ROW_TILE, LANES), lambda i: (i, 0)),
            pl.BlockSpec((8, tm), lambda i: (0, i)),
            pl.BlockSpec((8, tm), lambda i: (0, i)),
        ],
        out_shape=[
            jax.ShapeDtypeStruct((t, D), F32),
            jax.ShapeDtypeStruct((t * ROW_TILE, LANES), F32),
            jax.ShapeDtypeStruct((8, t), jnp.int32),
            jax.ShapeDtypeStruct((8, t), F32),
        ],
        compiler_params=_cparams(("parallel",)),
        name="outproj",
    )(mix_a, mix_b, xa, xb, modr, modr, modr, g, w, rwt, rb)


def _prep_wgu_kernel(w_ref, p_ref, wg_ref, wu_ref):
    for j in range(w_ref.shape[2] // 256):
        w = w_ref[0, :, 256 * j:256 * (j + 1)].astype(BF16)
        sel = _dot(w, p_ref[...])
        wg_ref[0, :, LANES * j:LANES * (j + 1)] = sel[:, :LANES].astype(BF16)
        wu_ref[0, :, LANES * j:LANES * (j + 1)] = sel[:, LANES:].astype(BF16)


def _prep_wgu(w):
    e, d, f2 = w.shape
    perm = np.zeros((256, 256), np.float32)
    perm[2 * np.arange(LANES), np.arange(LANES)] = 1.0
    perm[2 * np.arange(LANES) + 1, LANES + np.arange(LANES)] = 1.0
    out = jax.ShapeDtypeStruct((e, d, f2 // 2), BF16)
    return pl.pallas_call(
        _prep_wgu_kernel,
        grid=(e,),
        in_specs=[pl.BlockSpec((1, d, f2), lambda i: (i, 0, 0)), pl.BlockSpec((256, 256), lambda i: (0, 0))],
        out_specs=[pl.BlockSpec((1, d, f2 // 2), lambda i: (i, 0, 0))] * 2,
        out_shape=[out, out],
        compiler_params=_cparams(("parallel",)),
        name="prep_wgu",
    )(w, jnp.asarray(perm, BF16))


def _moe_kernel(blk_e_ref, row_tok_ref, row_dst_ref, n_live_ref, wctl_ref, ht_ref, wg_hbm, wu_hbm, wd_hbm,
                bg_ref, bu_ref, bd_ref, out_ref, xg_ref, ybuf_ref, wbuf_ref, sem, wsem):
    i = pl.program_id(1)
    nb = pl.num_programs(1)
    step = pl.program_id(0) * nb + i
    n = pl.num_programs(0) * nb
    cur = step % 2
    nxt = 1 - cur
    ycur = step % Y_SLOTS
    yprev = (step + Y_SLOTS - 1) % Y_SLOTS
    blk = MOE_BLOCK * ROW_TILE

    def slot_copy(s):
        return pltpu.make_async_copy(ybuf_ref.at[pl.ds(s * blk, blk)], out_ref.at[pl.ds(0, blk)], sem.at[s])

    def gather_row(base, r, s):
        tok = row_tok_ref[base + r]
        xg_ref[pl.ds(pl.multiple_of(s * blk + r * ROW_TILE, ROW_TILE), ROW_TILE), :] = (
            ht_ref[pl.ds(pl.multiple_of(tok * ROW_TILE, ROW_TILE), ROW_TILE), :])

    def send_row(base, r, s, priority=0):
        dst = row_dst_ref[base + r]
        pltpu.make_async_copy(
            ybuf_ref.at[pl.ds(pl.multiple_of(s * blk + r * ROW_TILE, ROW_TILE), ROW_TILE)],
            out_ref.at[pl.ds(pl.multiple_of(dst * ROW_TILE, ROW_TILE), ROW_TILE)], sem.at[s]).start(priority)

    @pl.when(step == 0)
    def _():
        ybuf_ref[...] = jnp.zeros(ybuf_ref.shape, F32)
        for s in range(Y_SLOTS - 1):
            pltpu.make_async_copy(ybuf_ref.at[pl.ds(s * blk, blk)], out_ref.at[pl.ds((n + 1 + s) * blk, blk)],
                                  sem.at[s]).start()

    @pl.when(i == 0)
    def _():
        def first(r, carry):
            gather_row(step * MOE_BLOCK, r, cur)
            return carry

        lax.fori_loop(0, MOE_BLOCK, first, 0, unroll=8)

    wslot = wctl_ref[2 * n + step]

    def weight_copies(e, s):
        return [pltpu.make_async_copy(w.at[e], wbuf_ref.at[s, j], wsem.at[s, j])
                for j, w in enumerate((wg_hbm, wu_hbm, wd_hbm))]

    @pl.when(step == 0)
    def _():
        for cp in weight_copies(blk_e_ref[0], 0):
            cp.start()

    @pl.when(wctl_ref[step] == 1)
    def _():
        for cp in weight_copies(blk_e_ref[step], wslot):
            cp.wait()
        next_e = wctl_ref[n + step]

        @pl.when(next_e >= 0)
        def _():
            for cp in weight_copies(next_e, 1 - wslot):
                cp.start()

    slot_copy(ycur).wait()
    live = i < n_live_ref[pl.program_id(0)]

    @pl.when(jnp.logical_not(live))
    def _():
        def prev(r, carry):
            send_row(step * MOE_BLOCK, r, yprev)
            return carry

        lax.fori_loop(0, MOE_BLOCK, prev, 0, unroll=8)

    @pl.when(live)
    def _():
        _moe_block(step, i, nb, cur, nxt, ycur, yprev, blk, gather_row, send_row, xg_ref, ybuf_ref,
                   wbuf_ref.at[wslot, 0], wbuf_ref.at[wslot, 1], wbuf_ref.at[wslot, 2], bg_ref, bu_ref, bd_ref)

    @pl.when(step == n - 1)
    def _():
        def last(r, carry):
            send_row(n * MOE_BLOCK, r, ycur)
            return carry

        lax.fori_loop(0, MOE_BLOCK, last, 0, unroll=8)
        for s in range(Y_SLOTS):
            slot_copy(s).wait()


def _moe_block(step, i, nb, cur, nxt, ycur, yprev, blk, gather_row, send_row,
               xg_ref, ybuf_ref, wg_ref, wu_ref, wd_ref, bg_ref, bu_ref, bd_ref):
    x = jnp.concatenate(
        [xg_ref[pl.ds(cur * blk + c, MOE_BLOCK, stride=ROW_TILE), :].astype(BF16) for c in range(ROW_TILE)], axis=1)
    next_blk = jnp.where(i + 1 < nb, step + 1, step)
    for r in range(MOE_BLOCK):
        gather_row(next_blk * MOE_BLOCK, r, nxt)
    for r in range(MOE_BLOCK):
        send_row(step * MOE_BLOCK, r, yprev, priority=r % 2)
    g = _dot(x, wg_ref[...]) + bg_ref[0]
    u = _dot(x, wu_ref[...]) + bu_ref[0]
    g = jnp.minimum(g, SWIGLU_LIMIT)
    u = jnp.clip(u, -SWIGLU_LIMIT, SWIGLU_LIMIT)
    act = (u + 1.0) * (g * jax.nn.sigmoid(SWIGLU_ALPHA * g))
    y = _dot(act.astype(BF16), wd_ref[...]) + bd_ref[0]
    for c in range(ROW_TILE):
        ybuf_ref[pl.ds(ycur * blk + c, MOE_BLOCK, stride=ROW_TILE), :] = y[:, LANES * c:LANES * (c + 1)]


def _weight_schedule(blk_e):
    n = blk_e.shape[0]
    start = jnp.concatenate([jnp.ones((1,), jnp.int32), (blk_e[1:] != blk_e[:-1]).astype(jnp.int32)])
    parity = (jnp.cumsum(start) - 1) % 2
    idx = jnp.arange(n, dtype=jnp.int32)
    later_start = (idx[None, :] > idx[:, None]) & (start[None, :] == 1)
    j_next = jnp.min(jnp.where(later_start, idx[None, :], n), axis=1)
    next_e = jnp.where(j_next < n, blk_e[jnp.minimum(j_next, n - 1)], -1)
    return jnp.concatenate([start, next_e, parity]).astype(jnp.int32)


def _moe(blk_e, row_tok, row_dst, n_live, ht, n_groups, wg, wu, wd, bg, bu, bd):
    nb = blk_e.shape[0] // n_groups
    tg = ht.shape[0] // ROW_TILE // n_groups
    ew = lambda g, i, be, rt, rd, nl, wc: (be[g * nb + i], 0, 0)
    hbm = pl.BlockSpec(memory_space=pl.ANY)
    return pl.pallas_call(
        _moe_kernel,
        grid_spec=pltpu.PrefetchScalarGridSpec(
            num_scalar_prefetch=5,
            grid=(n_groups, nb),
            in_specs=[
                pl.BlockSpec((tg * ROW_TILE, LANES), lambda g, i, be, rt, rd, nl, wc: (g, 0),
                             pipeline_mode=pl.Buffered(1)),
                hbm, hbm, hbm,
                pl.BlockSpec((1, 1, D), ew),
                pl.BlockSpec((1, 1, D), ew),
                pl.BlockSpec((1, 1, D), ew),
            ],
            out_specs=pl.BlockSpec(memory_space=pl.ANY),
            scratch_shapes=[
                pltpu.VMEM((2 * MOE_BLOCK * ROW_TILE, LANES), F32),
                pltpu.VMEM((Y_SLOTS * MOE_BLOCK * ROW_TILE, LANES), F32),
                pltpu.VMEM((2, 3, D, D), BF16),
                pltpu.SemaphoreType.DMA((Y_SLOTS,)),
                pltpu.SemaphoreType.DMA((2, 3)),
            ],
        ),
        out_shape=jax.ShapeDtypeStruct(((n_groups * nb + Y_SLOTS) * MOE_BLOCK * ROW_TILE, LANES), F32),
        compiler_params=_cparams(("arbitrary", "arbitrary")),
        name="moe_experts",
    )(blk_e, row_tok, row_dst, n_live, _weight_schedule(blk_e), ht, wg, wu, wd, bg, bu, bd)


def _route(top_i, group, n_groups):
    tg = top_i.shape[0]
    n_assign = tg * TOP_K
    assert n_assign <= 1 << 16 and n_assign % MOE_BLOCK == 0
    flat_e = top_i.reshape(-1).astype(jnp.int32)
    experts = jnp.arange(N_EXPERTS, dtype=jnp.int32)
    counts = jnp.sum((flat_e[:, None] == experts[None, :]).astype(jnp.int32), axis=0)
    padded = (counts + MOE_BLOCK - 1) // MOE_BLOCK * MOE_BLOCK
    pad_end = jnp.cumsum(padded)
    n_blocks = n_assign // MOE_BLOCK + N_EXPERTS
    real_keys = (flat_e << 17) | jnp.arange(n_assign, dtype=jnp.int32)
    pad_id = experts[:, None] * MOE_BLOCK + jnp.arange(MOE_BLOCK, dtype=jnp.int32)[None, :]
    used = jnp.arange(MOE_BLOCK, dtype=jnp.int32)[None, :] < (padded - counts)[:, None]
    pad_keys = (jnp.where(used, experts[:, None], 63) << 17) | (1 << 16) | pad_id
    keys = jnp.sort(jnp.concatenate([real_keys, pad_keys.reshape(-1)]))
    is_pad = ((keys >> 16) & 1) == 1
    payload = keys & 0xFFFF
    tok = payload // TOP_K
    t_all = tg * n_groups
    row_tok = jnp.where(is_pad, 0, tok)
    row_dst = jnp.where(is_pad, TOP_K * t_all + group * (N_EXPERTS * MOE_BLOCK) + payload,
                        (payload % TOP_K) * t_all + group * tg + tok)
    blk_start = jnp.arange(n_blocks, dtype=jnp.int32) * MOE_BLOCK
    blk_e = jnp.sum((pad_end[None, :] <= blk_start[:, None]).astype(jnp.int32), axis=1)
    blk_e = jnp.minimum(blk_e, N_EXPERTS - 1)
    n_live = pad_end[-1] // MOE_BLOCK
    return blk_e, row_tok, row_dst, n_live


def _route_groups(top_i, n_groups):
    tg = top_i.shape[0] // n_groups
    parts = [_route(top_i[g * tg:(g + 1) * tg], g, n_groups) for g in range(n_groups)]
    blk_e = jnp.concatenate([p[0] for p in parts])
    row_tok = jnp.concatenate([p[1] for p in parts])
    spare = row_tok.shape[0] + jnp.arange(MOE_BLOCK, dtype=jnp.int32)
    row_dst = jnp.concatenate([spare] + [p[2] for p in parts])
    n_live = jnp.stack([p[3] for p in parts]).astype(jnp.int32)
    return blk_e, row_tok, row_dst, n_live


def _combine_kernel(final, y0_ref, y1_ref, y2_ref, y3_ref, gates_ref, x_ref, g2_ref, gf_ref, o_ref):
    g = gates_ref[...]
    tm = x_ref.shape[0]
    for c in range(ROW_TILE):
        rows = pl.ds(c, tm, stride=ROW_TILE)
        cols = slice(LANES * c, LANES * (c + 1))
        y = ((y0_ref[rows, :] * g[:, 0:1] + y1_ref[rows, :] * g[:, 1:2])
             + (y2_ref[rows, :] * g[:, 2:3] + y3_ref[rows, :] * g[:, 3:4]))
        o_ref[:, cols] = x_ref[:, cols] + g2_ref[0][:, cols] * y
    if final:
        x = o_ref[...]
        ms = jnp.mean(x * x, axis=-1, keepdims=True)
        o_ref[...] = x * lax.rsqrt(ms + EPS) * gf_ref[...]


def _combine(ys, gates, x, modr, tp, ts_per_batch, row0, n_rows, final_g=None):
    tm = 512
    nt_all = x.shape[0] // tm
    i0 = row0 // tm
    yspec = lambda k: pl.BlockSpec((tm * ROW_TILE, LANES), lambda i: (k * nt_all + i0 + i, 0))
    mod = pl.BlockSpec((1, 1, D), lambda i: (_mod_row(i0 + i, tm, tp, ts_per_batch) * 6 + 5, 0, 0))
    gf = jnp.ones((1, D), F32) if final_g is None else final_g
    return pl.pallas_call(
        functools.partial(_combine_kernel, final_g is not None),
        grid=(n_rows // tm,),
        in_specs=[yspec(0), yspec(1), yspec(2), yspec(3), pl.BlockSpec((tm, TOP_K), lambda i: (i0 + i, 0)),
                  pl.BlockSpec((tm, D), lambda i: (i0 + i, 0)), mod, pl.BlockSpec((1, D), lambda i: (0, 0))],
        out_specs=pl.BlockSpec((tm, D), lambda i: (i, 0)),
        out_shape=jax.ShapeDtypeStruct((n_rows, D), F32),
        compiler_params=_cparams(("parallel",)),
        name="moe_combine",
    )(ys, ys, ys, ys, gates, x, modr, gf)


def _na_bias(rpb, rows):
    kr = min(NA_WIN_R, rows)
    r = np.arange(rows)
    r0 = np.clip(r - kr // 2, 0, rows - kr)
    row_ok = (r[None, :] >= r0[:, None]) & (r[None, :] < r0[:, None] + kr)
    dr = np.clip(r[None, :] - r[:, None] + (NA_WIN_R - 1), 0, 2 * NA_WIN_R - 2)
    c = np.arange(GRID_W)
    w0 = np.clip(c - NA_WIN_C // 2, 0, GRID_W - NA_WIN_C)
    col_ok = (c[None, :] >= w0[:, None]) & (c[None, :] < w0[:, None] + NA_WIN_C)
    dc = np.clip(c[None, :] - c[:, None] + (NA_WIN_C - 1), 0, 2 * NA_WIN_C - 2)
    oh_r = np.eye(2 * NA_WIN_R - 1, dtype=np.float32)[dr]
    oh_c = np.eye(2 * NA_WIN_C - 1, dtype=np.float32)[dc]
    by_row = jnp.einsum('rkd,hde->hrke', oh_r, rpb.astype(F32), precision=HI)
    vals = jnp.einsum('hrke,qce->hrqkc', by_row, oh_c, precision=HI)
    ok = row_ok[:, None, :, None] & col_ok[None, :, None, :]
    n = rows * GRID_W
    return jnp.where(ok[None], vals, NEG_INF).reshape(rpb.shape[0], n, n)


def _rope_tables(n_tokens):
    t = np.arange(n_tokens)
    row = (t // GRID_W).astype(np.float32)
    col = (t % GRID_W).astype(np.float32)
    half = HEAD_DIM // 2
    inv = jnp.asarray(ROPE_THETA, F32) ** (-jnp.arange(0, half, 2, dtype=F32) / half)
    ang_r = jnp.asarray(row)[:, None] * inv
    ang_c = jnp.asarray(col)[:, None] * inv
    cr, sr, cc, sc = jnp.cos(ang_r), jnp.sin(ang_r), jnp.cos(ang_c), jnp.sin(ang_c)
    cos = jnp.concatenate([cr, cr, cc, cc] * 2, axis=-1)
    sin = jnp.concatenate([-sr, sr, -sc, sc] * 2, axis=-1)
    return cos, sin


def _to_heads(x, n_batch, seq, n_heads):
    return x.reshape(n_batch, seq, n_heads, HEAD_DIM).transpose(0, 2, 1, 3)


def _ctx_lanes(cache_l):
    b, h, p, dh = cache_l.shape
    return cache_l.transpose(0, 2, 1, 3).reshape(b, p, h * dh)


def kernel(x_prompt, x_sample, cache_na_k, cache_na_v, cache_gqa_k, cache_gqa_v, c, c_ctx, w_mod, b_mod, norm1_g, norm2_g, w_in, na_rpb, q_norm_g, k_norm_g, cm_ln_g, cm_ln_b, cm_ws, cm_bs, w_out, router_w, router_b, w_gate_up, b_gate_up, w_down, b_down, final_norm_g):
    nb, seq, _ = x_prompt.shape
    db, n_tok, _ = x_sample.shape
    past = cache_na_k.shape[3]
    tp, ts = nb * seq, db * n_tok
    assert db <= CTX_ROW and n_tok % GRID_W == 0
    assert tp == ts

    cond = jnp.zeros((MOD_ROWS, D), F32).at[:db].set(c).at[CTX_ROW].set(c_ctx)
    mod = _modulation(cond, w_mod, b_mod)
    cos, sin = _rope_tables(n_tok)
    xa, xb, xb_row0 = x_prompt.reshape(tp, D), x_sample.reshape(ts, D), 0

    w_in16 = w_in.astype(BF16)
    w_out16 = w_out.astype(BF16)
    wg16, wu16 = _prep_wgu(w_gate_up.reshape(DEPTH * N_EXPERTS, D, 2 * D))
    wd16 = w_down.astype(BF16).reshape(DEPTH * N_EXPERTS, D, D)
    bg = b_gate_up[..., 0::2].reshape(DEPTH * N_EXPERTS, 1, D)
    bu = b_gate_up[..., 1::2].reshape(DEPTH * N_EXPERTS, 1, D)
    bd = b_down.reshape(DEPTH * N_EXPERTS, 1, D)

    na_k, na_v, gqa_k, gqa_v = [], [], [], []
    for l in range(DEPTH):
        modr = mod[l].reshape(MOD_ROWS * 6, 1, D)
        qg2 = jnp.tile(q_norm_g[l], 2).reshape(1, LANES)
        kg2 = jnp.tile(k_norm_g[l], 2).reshape(1, LANES)
        lng = cm_ln_g[l].reshape(1, 256)
        lnb = cm_ln_b[l].reshape(1, 256)
        bsf = jnp.repeat(cm_bs[l].T, HEAD_DIM, axis=1)

        z = _inproj(xa, xb, xb_row0, tp + ts, modr, norm1_g[l].reshape(1, D), w_in16[l], tp, n_tok)
        mix_p, kbn = _ctx_mixer(z, nb, seq, qg2, kg2, lng, lnb, cm_ws[l], bsf)
        mix_s = _lat_mixer(z, tp, db, n_tok, past,
                           _ctx_lanes(cache_na_k[:, l]), _ctx_lanes(cache_na_v[:, l]),
                           _ctx_lanes(cache_gqa_k[:, l]), _ctx_lanes(cache_gqa_v[:, l]),
                           _na_bias(na_rpb[l], n_tok // GRID_W), cos, sin, qg2, kg2, lng, lnb, cm_ws[l], bsf)
        na_k.append(_to_heads(z[:tp, KA:KA + 256], nb, seq, 4))
        na_v.append(_to_heads(z[:tp, VA:VA + 256], nb, seq, 4))
        gqa_k.append(_to_heads(kbn, nb, seq, 2))
        gqa_v.append(_to_heads(z[:tp, VB:VB + LANES], nb, seq, 2))

        xn, ht, top_i, gates = _outproj(mix_p, mix_s, xa, xb, xb_row0, modr, norm2_g[l].reshape(1, D), w_out16[l],
                                  router_w[l].T, router_b[l].reshape(N_EXPERTS, 1), tp, n_tok)

        blk_e, row_tok, row_dst, n_live = _route_groups(top_i[:TOP_K].T, 2)
        gates = gates[:TOP_K].T
        ys = _moe(blk_e + l * N_EXPERTS, row_tok, row_dst, n_live, ht, 2, wg16, wu16, wd16, bg, bu, bd)
        if l + 1 < DEPTH:
            x = _combine(ys, gates, xn, modr, tp, n_tok, 0, tp + ts)
            xa, xb, xb_row0 = x, x, tp

    gf = final_norm_g.reshape(1, D)
    y_p = _combine(ys, gates, xn, modr, tp, n_tok, 0, tp, gf)
    y_s = _combine(ys, gates, xn, modr, tp, n_tok, tp, ts, gf)
    return (y_p.reshape(nb, seq, D), y_s.reshape(db, n_tok, D),
            jnp.stack(na_k, axis=1), jnp.stack(na_v, axis=1), jnp.stack(gqa_k, axis=1), jnp.stack(gqa_v, axis=1))
```

```python
import functools

import jax
import jax.numpy as jnp
import numpy as np
from jax import lax
from jax.experimental import pallas as pl
from jax.experimental.pallas import tpu as pltpu

D = 1024
DEPTH = 2
GRID_W = 64
HEAD_DIM = 64
NA_WIN_R = 8
NA_WIN_C = 16
CHUNK = 128
ROPE_THETA = 10000.0
N_EXPERTS = 32
TOP_K = 4
SWIGLU_LIMIT = 7.0
SWIGLU_ALPHA = 1.702
EPS = 1e-6
MOE_BLOCK = 128
FF_CHUNK = 512
Y_SLOTS = 3
NEG_INF = -1e30
IN_WIDTH = 2048

QA, KA, VA, QB, KB, VB, CU, CV = 0, 256, 512, 768, 1280, 1408, 1536, 1792
OA, OB, OC = 0, 256, 768

LANES = 128
ROW_TILE = D // LANES
MOD_ROWS = 16
CTX_ROW = 8
VMEM_LIMIT = 56 * 1024 * 1024

F32 = jnp.float32
BF16 = jnp.bfloat16
HI = lax.Precision.HIGHEST


def _cparams(sem):
    return pltpu.CompilerParams(dimension_semantics=sem, vmem_limit_bytes=VMEM_LIMIT)


def _dot(a, b):
    return jnp.dot(a, b, preferred_element_type=F32)


def _dot_nt(a, b):
    return lax.dot_general(a, b, (((1,), (1,)), ((), ())), preferred_element_type=F32)


def _lane_lo():
    return lax.broadcasted_iota(jnp.int32, (1, LANES), 1) < HEAD_DIM


def _mod_kernel(c_ref, w_ref, b_ref, o_ref):
    c = c_ref[...]
    s = c * jax.nn.sigmoid(c)
    o_ref[0] = jnp.dot(s, w_ref[0], preferred_element_type=F32, precision=HI) + b_ref[0]


def _modulation(cond, w_mod, b_mod):
    tn = 1536
    n = w_mod.shape[-1]
    return pl.pallas_call(
        _mod_kernel,
        grid=(DEPTH, n // tn),
        in_specs=[
            pl.BlockSpec((MOD_ROWS, D), lambda l, j: (0, 0)),
            pl.BlockSpec((1, D, tn), lambda l, j: (l, 0, j)),
            pl.BlockSpec((1, 1, tn), lambda l, j: (l, 0, j)),
        ],
        out_specs=pl.BlockSpec((1, MOD_ROWS, tn), lambda l, j: (l, 0, j)),
        out_shape=jax.ShapeDtypeStruct((DEPTH, MOD_ROWS, n), F32),
        compiler_params=_cparams(("arbitrary", "arbitrary")),
        name="modulation",
    )(cond, w_mod, b_mod.reshape(DEPTH, 1, n))


def _mod_row(i, tm, tp, ts_per_batch):
    start = i * tm
    return jnp.where(start < tp, CTX_ROW, (start - tp) // ts_per_batch)


def _mod_spec(k, tm, tp, ts_per_batch):
    return pl.BlockSpec((1, 1, D), lambda i: (_mod_row(i, tm, tp, ts_per_batch) * 6 + k, 0, 0))


def _x_specs(tm, n_top, bot_off, width=D):
    return [pl.BlockSpec((tm, width), lambda i: (jnp.minimum(i, n_top - 1), 0)),
            pl.BlockSpec((tm, width), lambda i: (jnp.maximum(i - n_top, 0) + bot_off, 0))]


def _inproj_kernel(n_top, xa_ref, xb_ref, shift_ref, scale_ref, g_ref, w_ref, z_ref):
    x = jnp.where(pl.program_id(0) < n_top, xa_ref[...], xb_ref[...])
    ms = jnp.mean(x * x, axis=-1, keepdims=True)
    y = x * lax.rsqrt(ms + EPS) * g_ref[...]
    h = y * (1.0 + scale_ref[0]) + shift_ref[0]
    z_ref[...] = _dot(h.astype(BF16), w_ref[...])


def _inproj(xa, xb, bot_off_rows, t, modr, g, w, tp, ts_per_batch):
    tm = 512
    n_top = tp // tm
    return pl.pallas_call(
        functools.partial(_inproj_kernel, n_top),
        grid=(t // tm,),
        in_specs=_x_specs(tm, n_top, bot_off_rows // tm) + [
            _mod_spec(0, tm, tp, ts_per_batch),
            _mod_spec(1, tm, tp, ts_per_batch),
            pl.BlockSpec((1, D), lambda i: (0, 0)),
            pl.BlockSpec((D, IN_WIDTH), lambda i: (0, 0)),
        ],
        out_specs=pl.BlockSpec((tm, IN_WIDTH), lambda i: (i, 0)),
        out_shape=jax.ShapeDtypeStruct((t, IN_WIDTH), F32),
        compiler_params=_cparams(("parallel",)),
        name="inproj",
    )(xa, xb, modr, modr, g, w)


def _head_rms(x, g2):
    lo = _lane_lo()
    x2 = x * x
    s_lo = jnp.sum(jnp.where(lo, x2, 0.0), axis=-1, keepdims=True)
    s_hi = jnp.sum(jnp.where(lo, 0.0, x2), axis=-1, keepdims=True)
    ms = jnp.where(lo, s_lo, s_hi) * (1.0 / HEAD_DIM)
    return x * lax.rsqrt(ms + EPS) * g2


def _softmax_pv(q16, ks, vs, biases):
    ss = []
    for k, b in zip(ks, biases):
        s = _dot_nt(q16, k)
        if b is not None:
            s = s + b
        ss.append(s)
    m = ss[0].max(axis=-1, keepdims=True)
    for s in ss[1:]:
        m = jnp.maximum(m, s.max(axis=-1, keepdims=True))
    den = None
    acc = None
    for s, v in zip(ss, vs):
        e = jnp.exp(s - m)
        d = e.sum(axis=-1, keepdims=True)
        o = _dot(e.astype(BF16), v)
        den = d if den is None else den + d
        acc = o if acc is None else acc + o
    return acc / den


def _attend_pair(q, ks_lo, vs_lo, ks_hi, vs_hi, b_lo, b_hi):
    lo = _lane_lo()
    o_lo = _softmax_pv(jnp.where(lo, q, 0.0).astype(BF16), ks_lo, vs_lo, b_lo)
    o_hi = _softmax_pv(jnp.where(lo, 0.0, q).astype(BF16), ks_hi, vs_hi, b_hi)
    return jnp.where(lo, o_lo, o_hi)


def _gelu(x):
    c = np.sqrt(2.0 / np.pi).astype(np.float32)
    return x * (0.5 * (1.0 + jnp.tanh(c * (x + 0.044715 * (x * x * x)))))


def _layer_norm(x, g, b):
    mu = jnp.mean(x, axis=-1, keepdims=True)
    xc = x - mu
    var = jnp.mean(xc * xc, axis=-1, keepdims=True)
    return xc * lax.rsqrt(var + EPS) * g + b


def _chunk_mlp(cu, cv, lng_ref, lnb_ref, ws_ref, bsf_ref, out_ref, col0):
    lo = _lane_lo()
    u = _gelu(cu)
    v = _layer_norm(_gelu(cv), lng_ref[...], lnb_ref[...]).astype(BF16)
    s = cu.shape[0]
    for n in range(s // CHUNK):
        rows = slice(n * CHUNK, (n + 1) * CHUNK)
        for jb in range(2):
            cols = slice(jb * LANES, (jb + 1) * LANES)
            vb = v[rows, cols]
            m_lo = _dot(ws_ref[2 * jb].astype(BF16), vb)
            m_hi = _dot(ws_ref[2 * jb + 1].astype(BF16), vb)
            mixed = jnp.where(lo, m_lo, m_hi) + bsf_ref[:, cols]
            out_ref[rows, col0 + jb * LANES:col0 + (jb + 1) * LANES] = (u[rows, cols] * mixed).astype(out_ref.dtype)


def _ctx_mixer_kernel(z_ref, qg_ref, kg_ref, lng_ref, lnb_ref, ws_ref, bsf_ref, mix_ref, kbn_ref):
    scale = HEAD_DIM ** -0.5
    for j in range(2):
        q = z_ref[:, QA + LANES * j:QA + LANES * (j + 1)] * scale
        k = [z_ref[:, KA + LANES * j:KA + LANES * (j + 1)].astype(BF16)]
        v = [z_ref[:, VA + LANES * j:VA + LANES * (j + 1)].astype(BF16)]
        o = _attend_pair(q, k, v, k, v, [None], [None])
        mix_ref[:, OA + LANES * j:OA + LANES * (j + 1)] = o.astype(mix_ref.dtype)

    kb = _head_rms(z_ref[:, KB:KB + LANES], kg_ref[...])
    kbn_ref[...] = kb
    vb = z_ref[:, VB:VB + LANES]
    k_same = [kb.astype(BF16)]
    k_swap = [pltpu.roll(kb, HEAD_DIM, 1).astype(BF16)]
    v_same = [vb.astype(BF16)]
    v_swap = [pltpu.roll(vb, HEAD_DIM, 1).astype(BF16)]
    for j in range(4):
        q = _head_rms(z_ref[:, QB + LANES * j:QB + LANES * (j + 1)], qg_ref[...]) * scale
        if j // 2 == 0:
            o = _attend_pair(q, k_same, v_same, k_swap, v_swap, [None], [None])
        else:
            o = _attend_pair(q, k_swap, v_swap, k_same, v_same, [None], [None])
        mix_ref[:, OB + LANES * j:OB + LANES * (j + 1)] = o.astype(mix_ref.dtype)

    _chunk_mlp(z_ref[:, CU:CU + 256], z_ref[:, CV:CV + 256], lng_ref, lnb_ref, ws_ref, bsf_ref, mix_ref, OC)


def _ctx_mixer(z, n_batch, seq, qg2, kg2, lng, lnb, ws, bsf):
    small = lambda shape: pl.BlockSpec(shape, lambda b: (0,) * len(shape))
    return pl.pallas_call(
        _ctx_mixer_kernel,
        grid=(n_batch,),
        in_specs=[
            pl.BlockSpec((seq, IN_WIDTH), lambda b: (b, 0)),
            small((1, LANES)), small((1, LANES)), small((1, 256)), small((1, 256)),
            small((4, CHUNK, CHUNK)), small((CHUNK, 256)),
        ],
        out_specs=[
            pl.BlockSpec((seq, D), lambda b: (b, 0)),
            pl.BlockSpec((seq, LANES), lambda b: (b, 0)),
        ],
        out_shape=[
            jax.ShapeDtypeStruct((n_batch * seq, D), BF16),
            jax.ShapeDtypeStruct((n_batch * seq, LANES), F32),
        ],
        compiler_params=_cparams(("parallel",)),
        name="ctx_mixer",
    )(z, qg2, kg2, lng, lnb, ws, bsf)


def _rope(x, cos, sin):
    first = (lax.broadcasted_iota(jnp.int32, (1, LANES), 1) % 32) < 16
    partner = jnp.where(first, pltpu.roll(x, LANES - 16, 1), pltpu.roll(x, 16, 1))
    return x * cos + partner * sin


def _lat_mixer_kernel(zq_ref, zkv_ref, nakc_ref, navc_ref, gkc_ref, gvc_ref, bias_ref,
                      cosq_ref, sinq_ref, cosk_ref, sink_ref,
                      qg_ref, kg_ref, lng_ref, lnb_ref, ws_ref, bsf_ref,
                      mix_ref,
                      kl_ref, kls_ref, vl_ref, vls_ref, kc_ref, kcs_ref, vc_ref, vcs_ref):
    scale = HEAD_DIM ** -0.5

    @pl.when(pl.program_id(1) == 0)
    def _():
        kb = _rope(_head_rms(zkv_ref[:, KB:KB + LANES], kg_ref[...]), cosk_ref[...], sink_ref[...])
        vb = zkv_ref[:, VB:VB + LANES]
        kl_ref[...] = kb.astype(BF16)
        kls_ref[...] = pltpu.roll(kb, HEAD_DIM, 1).astype(BF16)
        vl_ref[...] = vb.astype(BF16)
        vls_ref[...] = pltpu.roll(vb, HEAD_DIM, 1).astype(BF16)
        kc = gkc_ref[0]
        vc = gvc_ref[0]
        kc_ref[...] = kc.astype(BF16)
        kcs_ref[...] = pltpu.roll(kc, HEAD_DIM, 1).astype(BF16)
        vc_ref[...] = vc.astype(BF16)
        vcs_ref[...] = pltpu.roll(vc, HEAD_DIM, 1).astype(BF16)

    for j in range(2):
        cols = slice(LANES * j, LANES * (j + 1))
        q = zq_ref[:, QA + LANES * j:QA + LANES * (j + 1)] * scale
        ks = [zkv_ref[:, KA + LANES * j:KA + LANES * (j + 1)].astype(BF16), nakc_ref[0, :, cols].astype(BF16)]
        vs = [zkv_ref[:, VA + LANES * j:VA + LANES * (j + 1)].astype(BF16), navc_ref[0, :, cols].astype(BF16)]
        o = _attend_pair(q, ks, vs, ks, vs, [bias_ref[2 * j], None], [bias_ref[2 * j + 1], None])
        mix_ref[:, OA + LANES * j:OA + LANES * (j + 1)] = o.astype(mix_ref.dtype)

    same = ([kl_ref[...], kc_ref[...]], [vl_ref[...], vc_ref[...]])
    swap = ([kls_ref[...], kcs_ref[...]], [vls_ref[...], vcs_ref[...]])
    for j in range(4):
        q = _head_rms(zq_ref[:, QB + LANES * j:QB + LANES * (j + 1)], qg_ref[...])
        q = _rope(q, cosq_ref[...], sinq_ref[...]) * scale
        lo_kv, hi_kv = (same, swap) if j // 2 == 0 else (swap, same)
        o = _attend_pair(q, lo_kv[0], lo_kv[1], hi_kv[0], hi_kv[1], [None, None], [None, None])
        mix_ref[:, OB + LANES * j:OB + LANES * (j + 1)] = o.astype(mix_ref.dtype)

    _chunk_mlp(zq_ref[:, CU:CU + 256], zq_ref[:, CV:CV + 256], lng_ref, lnb_ref, ws_ref, bsf_ref, mix_ref, OC)


def _lat_mixer(z, row0, n_batch, n_tok, past, nakc, navc, gkc, gvc, bias, cos, sin, qg2, kg2, lng, lnb, ws, bsf):
    tq = 256
    nq = n_tok // tq
    small = lambda shape: pl.BlockSpec(shape, lambda b, t: (0,) * len(shape))
    qblk0 = row0 // tq
    kvblk0 = row0 // n_tok
    return pl.pallas_call(
        _lat_mixer_kernel,
        grid=(n_batch, nq),
        in_specs=[
            pl.BlockSpec((tq, IN_WIDTH), lambda b, t: (qblk0 + b * nq + t, 0)),
            pl.BlockSpec((n_tok, IN_WIDTH), lambda b, t: (kvblk0 + b, 0)),
            pl.BlockSpec((1, past, 256), lambda b, t: (b, 0, 0)),
            pl.BlockSpec((1, past, 256), lambda b, t: (b, 0, 0)),
            pl.BlockSpec((1, past, LANES), lambda b, t: (b, 0, 0)),
            pl.BlockSpec((1, past, LANES), lambda b, t: (b, 0, 0)),
            pl.BlockSpec((4, tq, n_tok), lambda b, t: (0, t, 0)),
            pl.BlockSpec((tq, LANES), lambda b, t: (t, 0)),
            pl.BlockSpec((tq, LANES), lambda b, t: (t, 0)),
            small((n_tok, LANES)), small((n_tok, LANES)),
            small((1, LANES)), small((1, LANES)), small((1, 256)), small((1, 256)),
            small((4, CHUNK, CHUNK)), small((CHUNK, 256)),
        ],
        out_specs=pl.BlockSpec((tq, D), lambda b, t: (b * nq + t, 0)),
        out_shape=jax.ShapeDtypeStruct((n_batch * n_tok, D), BF16),
        scratch_shapes=[pltpu.VMEM((n_tok, LANES), BF16)] * 4 + [pltpu.VMEM((past, LANES), BF16)] * 4,
        compiler_params=_cparams(("parallel", "arbitrary")),
        name="lat_mixer",
    )(z, z, nakc, navc, gkc, gvc, bias, cos, sin, cos, sin, qg2, kg2, lng, lnb, ws, bsf)


def _split_bf16(x):
    hi = x.astype(BF16)
    return hi, (x - hi.astype(F32)).astype(BF16)


def _outproj_kernel(n_top, mixa_ref, mixb_ref, xa_ref, xb_ref, g1_ref, shift_ref, scale_ref, g_ref, w_ref, rwt_ref, rb_ref,
                    xn_ref, ht_ref, ti_ref, gt_ref):
    top = pl.program_id(0) < n_top
    x = jnp.where(top, xa_ref[...], xb_ref[...])
    x = x + g1_ref[0] * _dot(jnp.where(top, mixa_ref[...], mixb_ref[...]), w_ref[...])
    xn_ref[...] = x
    ms = jnp.mean(x * x, axis=-1, keepdims=True)
    h = x * lax.rsqrt(ms + EPS) * g_ref[...]
    h = h * (1.0 + scale_ref[0]) + shift_ref[0]
    h_hi, h_lo = _split_bf16(h)
    rw_hi, rw_lo = _split_bf16(rwt_ref[...])
    logits = (_dot_nt(rw_hi, h_hi) + _dot_nt(rw_hi, h_lo)) + _dot_nt(rw_lo, h_hi) + rb_ref[...]

    expert = lax.broadcasted_iota(jnp.int32, logits.shape, 0)
    out_row = lax.broadcasted_iota(jnp.int32, ti_ref.shape, 0)
    top_i = jnp.zeros(ti_ref.shape, jnp.int32)
    top_v = []
    for k in range(TOP_K):
        m = jnp.max(logits, axis=0, keepdims=True)
        idx = jnp.min(jnp.where(logits == m, expert, N_EXPERTS), axis=0, keepdims=True)
        logits = jnp.where(expert == idx, -jnp.inf, logits)
        top_i = jnp.where(out_row == k, idx, top_i)
        top_v.append(m)
    es = [jnp.exp(v - top_v[0]) for v in top_v]
    den = (es[0] + es[1]) + (es[2] + es[3])
    gates = jnp.zeros(gt_ref.shape, F32)
    for k in range(TOP_K):
        gates = jnp.where(out_row == k, es[k] / den, gates)
    ti_ref[...] = top_i
    gt_ref[...] = gates

    for c in range(ROW_TILE):
        ht_ref[pl.ds(c, h.shape[0], stride=ROW_TILE), :] = h[:, LANES * c:LANES * (c + 1)]


def _outproj(mix_a, mix_b, xa, xb, bot_off_rows, modr, g, w, rwt, rb, tp, ts_per_batch):
    tm = 512
    t = mix_a.shape[0] + mix_b.shape[0]
    n_top = tp // tm
    return pl.pallas_call(
        functools.partial(_outproj_kernel, n_top),
        grid=(t // tm,),
        in_specs=_x_specs(tm, n_top, 0) + _x_specs(tm, n_top, bot_off_rows // tm) + [
            _mod_spec(2, tm, tp, ts_per_batch),
            _mod_spec(3, tm, tp, ts_per_batch),
            _mod_spec(4, tm, tp, ts_per_batch),
            pl.BlockSpec((1, D), lambda i: (0, 0)),
            pl.BlockSpec((D, D), lambda i: (0, 0)),
            pl.BlockSpec((N_EXPERTS, D), lambda i: (0, 0)),
            pl.BlockSpec((N_EXPERTS, 1), lambda i: (0, 0)),
        ],
        out_specs=[
            pl.BlockSpec((tm, D), lambda i: (i, 0)),
            pl.BlockSpec((tm * ROW_TILE, LANES), lambda i: (i, 0)),
            pl.BlockSpec((8, tm), lambda i: (0, i)),
            pl.BlockSpec((8, tm), lambda i: (0, i)),
        ],
        out_shape=[
            jax.ShapeDtypeStruct((t, D), F32),
            jax.ShapeDtypeStruct((t * ROW_TILE, LANES), F32),
            jax.ShapeDtypeStruct((8, t), jnp.int32),
            jax.ShapeDtypeStruct((8, t), F32),
        ],
        compiler_params=_cparams(("parallel",)),
        name="outproj",
    )(mix_a, mix_b, xa, xb, modr, modr, modr, g, w, rwt, rb)


def _prep_wgu_kernel(w_ref, p_ref, wg_ref, wu_ref):
    for j in range(w_ref.shape[2] // 256):
        w = w_ref[0, :, 256 * j:256 * (j + 1)].astype(BF16)
        sel = _dot(w, p_ref[...])
        wg_ref[0, :, LANES * j:LANES * (j + 1)] = sel[:, :LANES].astype(BF16)
        wu_ref[0, :, LANES * j:LANES * (j + 1)] = sel[:, LANES:].astype(BF16)


def _prep_wgu(w):
    e, d, f2 = w.shape
    perm = np.zeros((256, 256), np.float32)
    perm[2 * np.arange(LANES), np.arange(LANES)] = 1.0
    perm[2 * np.arange(LANES) + 1, LANES + np.arange(LANES)] = 1.0
    out = jax.ShapeDtypeStruct((e, d, f2 // 2), BF16)
    return pl.pallas_call(
        _prep_wgu_kernel,
        grid=(e,),
        in_specs=[pl.BlockSpec((1, d, f2), lambda i: (i, 0, 0)), pl.BlockSpec((256, 256), lambda i: (0, 0))],
        out_specs=[pl.BlockSpec((1, d, f2 // 2), lambda i: (i, 0, 0))] * 2,
        out_shape=[out, out],
        compiler_params=_cparams(("parallel",)),
        name="prep_wgu",
    )(w, jnp.asarray(perm, BF16))


def _moe_kernel(blk_e_ref, row_tok_ref, row_dst_ref, n_live_ref, wctl_ref, ht_ref, wg_hbm, wu_hbm, wd_hbm,
                bg_ref, bu_ref, bd_ref, out_ref, xg_ref, ybuf_ref, wbuf_ref, sem, wsem):
    i = pl.program_id(1)
    nb = pl.num_programs(1)
    step = pl.program_id(0) * nb + i
    n = pl.num_programs(0) * nb
    cur = step % 2
    nxt = 1 - cur
    ycur = step % Y_SLOTS
    yprev = (step + Y_SLOTS - 1) % Y_SLOTS
    blk = MOE_BLOCK * ROW_TILE

    def slot_copy(s):
        return pltpu.make_async_copy(ybuf_ref.at[pl.ds(s * blk, blk)], out_ref.at[pl.ds(0, blk)], sem.at[s])

    def gather_row(base, r, s):
        tok = row_tok_ref[base + r]
        xg_ref[pl.ds(pl.multiple_of(s * blk + r * ROW_TILE, ROW_TILE), ROW_TILE), :] = (
            ht_ref[pl.ds(pl.multiple_of(tok * ROW_TILE, ROW_TILE), ROW_TILE), :])

    def send_row(base, r, s, priority=0):
        dst = row_dst_ref[base + r]
        pltpu.make_async_copy(
            ybuf_ref.at[pl.ds(pl.multiple_of(s * blk + r * ROW_TILE, ROW_TILE), ROW_TILE)],
            out_ref.at[pl.ds(pl.multiple_of(dst * ROW_TILE, ROW_TILE), ROW_TILE)], sem.at[s]).start(priority)

    @pl.when(step == 0)
    def _():
        ybuf_ref[...] = jnp.zeros(ybuf_ref.shape, F32)
        for s in range(Y_SLOTS - 1):
            pltpu.make_async_copy(ybuf_ref.at[pl.ds(s * blk, blk)], out_ref.at[pl.ds((n + 1 + s) * blk, blk)],
                                  sem.at[s]).start()

    @pl.when(i == 0)
    def _():
        def first(r, carry):
            gather_row(step * MOE_BLOCK, r, cur)
            return carry

        lax.fori_loop(0, MOE_BLOCK, first, 0, unroll=8)

    wslot = wctl_ref[2 * n + step]

    def weight_copies(e, s):
        return [pltpu.make_async_copy(w.at[e], wbuf_ref.at[s, j], wsem.at[s, j])
                for j, w in enumerate((wg_hbm, wu_hbm, wd_hbm))]

    @pl.when(step == 0)
    def _():
        for cp in weight_copies(blk_e_ref[0], 0):
            cp.start()

    @pl.when(wctl_ref[step] == 1)
    def _():
        for cp in weight_copies(blk_e_ref[step], wslot):
            cp.wait()
        next_e = wctl_ref[n + step]

        @pl.when(next_e >= 0)
        def _():
            for cp in weight_copies(next_e, 1 - wslot):
                cp.start()

    slot_copy(ycur).wait()
    live = i < n_live_ref[pl.program_id(0)]

    @pl.when(jnp.logical_not(live))
    def _():
        def prev(r, carry):
            send_row(step * MOE_BLOCK, r, yprev)
            return carry

        lax.fori_loop(0, MOE_BLOCK, prev, 0, unroll=8)

    @pl.when(live)
    def _():
        _moe_block(step, i, nb, cur, nxt, ycur, yprev, blk, gather_row, send_row, xg_ref, ybuf_ref,
                   wbuf_ref.at[wslot, 0], wbuf_ref.at[wslot, 1], wbuf_ref.at[wslot, 2], bg_ref, bu_ref, bd_ref)

    @pl.when(step == n - 1)
    def _():
        def last(r, carry):
            send_row(n * MOE_BLOCK, r, ycur)
            return carry

        lax.fori_loop(0, MOE_BLOCK, last, 0, unroll=8)
        for s in range(Y_SLOTS):
            slot_copy(s).wait()


def _moe_block(step, i, nb, cur, nxt, ycur, yprev, blk, gather_row, send_row,
               xg_ref, ybuf_ref, wg_ref, wu_ref, wd_ref, bg_ref, bu_ref, bd_ref):
    x = jnp.concatenate(
        [xg_ref[pl.ds(cur * blk + c, MOE_BLOCK, stride=ROW_TILE), :].astype(BF16) for c in range(ROW_TILE)], axis=1)
    next_blk = jnp.where(i + 1 < nb, step + 1, step)
    for r in range(MOE_BLOCK):
        gather_row(next_blk * MOE_BLOCK, r, nxt)
    for r in range(MOE_BLOCK):
        send_row(step * MOE_BLOCK, r, yprev, priority=r % 2)
    y = bd_ref[0]
    for f in range(0, D, FF_CHUNK):
        cols = slice(f, f + FF_CHUNK)
        g = _dot(x, wg_ref[:, cols]) + bg_ref[0][:, cols]
        u = _dot(x, wu_ref[:, cols]) + bu_ref[0][:, cols]
        g = jnp.minimum(g, SWIGLU_LIMIT)
        u = jnp.clip(u, -SWIGLU_LIMIT, SWIGLU_LIMIT)
        act = (u + 1.0) * (g * jax.nn.sigmoid(SWIGLU_ALPHA * g))
        y = y + _dot(act.astype(BF16), wd_ref[cols, :])
    for c in range(ROW_TILE):
        ybuf_ref[pl.ds(ycur * blk + c, MOE_BLOCK, stride=ROW_TILE), :] = y[:, LANES * c:LANES * (c + 1)]


def _weight_schedule(blk_e):
    n = blk_e.shape[0]
    start = jnp.concatenate([jnp.ones((1,), jnp.int32), (blk_e[1:] != blk_e[:-1]).astype(jnp.int32)])
    parity = (jnp.cumsum(start) - 1) % 2
    idx = jnp.arange(n, dtype=jnp.int32)
    later_start = (idx[None, :] > idx[:, None]) & (start[None, :] == 1)
    j_next = jnp.min(jnp.where(later_start, idx[None, :], n), axis=1)
    next_e = jnp.where(j_next < n, blk_e[jnp.minimum(j_next, n - 1)], -1)
    return jnp.concatenate([start, next_e, parity]).astype(jnp.int32)


def _moe(blk_e, row_tok, row_dst, n_live, ht, n_groups, wg, wu, wd, bg, bu, bd):
    nb = blk_e.shape[0] // n_groups
    tg = ht.shape[0] // ROW_TILE // n_groups
    ew = lambda g, i, be, rt, rd, nl, wc: (be[g * nb + i], 0, 0)
    hbm = pl.BlockSpec(memory_space=pl.ANY)
    return pl.pallas_call(
        _moe_kernel,
        grid_spec=pltpu.PrefetchScalarGridSpec(
            num_scalar_prefetch=5,
            grid=(n_groups, nb),
            in_specs=[
                pl.BlockSpec((tg * ROW_TILE, LANES), lambda g, i, be, rt, rd, nl, wc: (g, 0),
                             pipeline_mode=pl.Buffered(1)),
                hbm, hbm, hbm,
                pl.BlockSpec((1, 1, D), ew),
                pl.BlockSpec((1, 1, D), ew),
                pl.BlockSpec((1, 1, D), ew),
            ],
            out_specs=pl.BlockSpec(memory_space=pl.ANY),
            scratch_shapes=[
                pltpu.VMEM((2 * MOE_BLOCK * ROW_TILE, LANES), F32),
                pltpu.VMEM((Y_SLOTS * MOE_BLOCK * ROW_TILE, LANES), F32),
                pltpu.VMEM((2, 3, D, D), BF16),
                pltpu.SemaphoreType.DMA((Y_SLOTS,)),
                pltpu.SemaphoreType.DMA((2, 3)),
            ],
        ),
        out_shape=jax.ShapeDtypeStruct(((n_groups * nb + Y_SLOTS) * MOE_BLOCK * ROW_TILE, LANES), F32),
        compiler_params=_cparams(("arbitrary", "arbitrary")),
        name="moe_experts",
    )(blk_e, row_tok, row_dst, n_live, _weight_schedule(blk_e), ht, wg, wu, wd, bg, bu, bd)


def _route(top_i, group, n_groups):
    tg = top_i.shape[0]
    n_assign = tg * TOP_K
    assert n_assign <= 1 << 16 and n_assign % MOE_BLOCK == 0
    flat_e = top_i.reshape(-1).astype(jnp.int32)
    experts = jnp.arange(N_EXPERTS, dtype=jnp.int32)
    counts = jnp.sum((flat_e[:, None] == experts[None, :]).astype(jnp.int32), axis=0)
    padded = (counts + MOE_BLOCK - 1) // MOE_BLOCK * MOE_BLOCK
    pad_end = jnp.cumsum(padded)
    n_blocks = n_assign // MOE_BLOCK + N_EXPERTS
    real_keys = (flat_e << 17) | jnp.arange(n_assign, dtype=jnp.int32)
    pad_id = experts[:, None] * MOE_BLOCK + jnp.arange(MOE_BLOCK, dtype=jnp.int32)[None, :]
    used = jnp.arange(MOE_BLOCK, dtype=jnp.int32)[None, :] < (padded - counts)[:, None]
    pad_keys = (jnp.where(used, experts[:, None], 63) << 17) | (1 << 16) | pad_id
    keys = jnp.sort(jnp.concatenate([real_keys, pad_keys.reshape(-1)]))
    is_pad = ((keys >> 16) & 1) == 1
    payload = keys & 0xFFFF
    tok = payload // TOP_K
    t_all = tg * n_groups
    row_tok = jnp.where(is_pad, 0, tok)
    row_dst = jnp.where(is_pad, TOP_K * t_all + group * (N_EXPERTS * MOE_BLOCK) + payload,
                        (payload % TOP_K) * t_all + group * tg + tok)
    blk_start = jnp.arange(n_blocks, dtype=jnp.int32) * MOE_BLOCK
    blk_e = jnp.sum((pad_end[None, :] <= blk_start[:, None]).astype(jnp.int32), axis=1)
    blk_e = jnp.minimum(blk_e, N_EXPERTS - 1)
    n_live = pad_end[-1] // MOE_BLOCK
    return blk_e, row_tok, row_dst, n_live


def _route_groups(top_i, n_groups):
    tg = top_i.shape[0] // n_groups
    parts = [_route(top_i[g * tg:(g + 1) * tg], g, n_groups) for g in range(n_groups)]
    blk_e = jnp.concatenate([p[0] for p in parts])
    row_tok = jnp.concatenate([p[1] for p in parts])
    spare = row_tok.shape[0] + jnp.arange(MOE_BLOCK, dtype=jnp.int32)
    row_dst = jnp.concatenate([spare] + [p[2] for p in parts])
    n_live = jnp.stack([p[3] for p in parts]).astype(jnp.int32)
    return blk_e, row_tok, row_dst, n_live


def _combine_kernel(final, y0_ref, y1_ref, y2_ref, y3_ref, gates_ref, x_ref, g2_ref, gf_ref, o_ref):
    g = gates_ref[...]
    tm = x_ref.shape[0]
    for c in range(ROW_TILE):
        rows = pl.ds(c, tm, stride=ROW_TILE)
        cols = slice(LANES * c, LANES * (c + 1))
        y = ((y0_ref[rows, :] * g[:, 0:1] + y1_ref[rows, :] * g[:, 1:2])
             + (y2_ref[rows, :] * g[:, 2:3] + y3_ref[rows, :] * g[:, 3:4]))
        o_ref[:, cols] = x_ref[:, cols] + g2_ref[0][:, cols] * y
    if final:
        x = o_ref[...]
        ms = jnp.mean(x * x, axis=-1, keepdims=True)
        o_ref[...] = x * lax.rsqrt(ms + EPS) * gf_ref[...]


def _combine(ys, gates, x, modr, tp, ts_per_batch, row0, n_rows, final_g=None):
    tm = 512
    nt_all = x.shape[0] // tm
    i0 = row0 // tm
    yspec = lambda k: pl.BlockSpec((tm * ROW_TILE, LANES), lambda i: (k * nt_all + i0 + i, 0))
    mod = pl.BlockSpec((1, 1, D), lambda i: (_mod_row(i0 + i, tm, tp, ts_per_batch) * 6 + 5, 0, 0))
    gf = jnp.ones((1, D), F32) if final_g is None else final_g
    return pl.pallas_call(
        functools.partial(_combine_kernel, final_g is not None),
        grid=(n_rows // tm,),
        in_specs=[yspec(0), yspec(1), yspec(2), yspec(3), pl.BlockSpec((tm, TOP_K), lambda i: (i0 + i, 0)),
                  pl.BlockSpec((tm, D), lambda i: (i0 + i, 0)), mod, pl.BlockSpec((1, D), lambda i: (0, 0))],
        out_specs=pl.BlockSpec((tm, D), lambda i: (i, 0)),
        out_shape=jax.ShapeDtypeStruct((n_rows, D), F32),
        compiler_params=_cparams(("parallel",)),
        name="moe_combine",
    )(ys, ys, ys, ys, gates, x, modr, gf)


def _na_bias(rpb, rows):
    kr = min(NA_WIN_R, rows)
    r = np.arange(rows)
    r0 = np.clip(r - kr // 2, 0, rows - kr)
    row_ok = (r[None, :] >= r0[:, None]) & (r[None, :] < r0[:, None] + kr)
    dr = np.clip(r[None, :] - r[:, None] + (NA_WIN_R - 1), 0, 2 * NA_WIN_R - 2)
    c = np.arange(GRID_W)
    w0 = np.clip(c - NA_WIN_C // 2, 0, GRID_W - NA_WIN_C)
    col_ok = (c[None, :] >= w0[:, None]) & (c[None, :] < w0[:, None] + NA_WIN_C)
    dc = np.clip(c[None, :] - c[:, None] + (NA_WIN_C - 1), 0, 2 * NA_WIN_C - 2)
    oh_r = np.eye(2 * NA_WIN_R - 1, dtype=np.float32)[dr]
    oh_c = np.eye(2 * NA_WIN_C - 1, dtype=np.float32)[dc]
    by_row = jnp.einsum('rkd,hde->hrke', oh_r, rpb.astype(F32), precision=HI)
    vals = jnp.einsum('hrke,qce->hrqkc', by_row, oh_c, precision=HI)
    ok = row_ok[:, None, :, None] & col_ok[None, :, None, :]
    n = rows * GRID_W
    return jnp.where(ok[None], vals, NEG_INF).reshape(rpb.shape[0], n, n)


def _rope_tables(n_tokens):
    t = np.arange(n_tokens)
    row = (t // GRID_W).astype(np.float32)
    col = (t % GRID_W).astype(np.float32)
    half = HEAD_DIM // 2
    inv = jnp.asarray(ROPE_THETA, F32) ** (-jnp.arange(0, half, 2, dtype=F32) / half)
    ang_r = jnp.asarray(row)[:, None] * inv
    ang_c = jnp.asarray(col)[:, None] * inv
    cr, sr, cc, sc = jnp.cos(ang_r), jnp.sin(ang_r), jnp.cos(ang_c), jnp.sin(ang_c)
    cos = jnp.concatenate([cr, cr, cc, cc] * 2, axis=-1)
    sin = jnp.concatenate([-sr, sr, -sc, sc] * 2, axis=-1)
    return cos, sin


def _to_heads(x, n_batch, seq, n_heads):
    return x.reshape(n_batch, seq, n_heads, HEAD_DIM).transpose(0, 2, 1, 3)


def _ctx_lanes(cache_l):
    b, h, p, dh = cache_l.shape
    return cache_l.transpose(0, 2, 1, 3).reshape(b, p, h * dh)


def kernel(x_prompt, x_sample, cache_na_k, cache_na_v, cache_gqa_k, cache_gqa_v, c, c_ctx, w_mod, b_mod, norm1_g, norm2_g, w_in, na_rpb, q_norm_g, k_norm_g, cm_ln_g, cm_ln_b, cm_ws, cm_bs, w_out, router_w, router_b, w_gate_up, b_gate_up, w_down, b_down, final_norm_g):
    nb, seq, _ = x_prompt.shape
    db, n_tok, _ = x_sample.shape
    past = cache_na_k.shape[3]
    tp, ts = nb * seq, db * n_tok
    assert db <= CTX_ROW and n_tok % GRID_W == 0
    assert tp == ts

    cond = jnp.zeros((MOD_ROWS, D), F32).at[:db].set(c).at[CTX_ROW].set(c_ctx)
    mod = _modulation(cond, w_mod, b_mod)
    cos, sin = _rope_tables(n_tok)
    xa, xb, xb_row0 = x_prompt.reshape(tp, D), x_sample.reshape(ts, D), 0

    w_in16 = w_in.astype(BF16)
    w_out16 = w_out.astype(BF16)
    wg16, wu16 = _prep_wgu(w_gate_up.reshape(DEPTH * N_EXPERTS, D, 2 * D))
    wd16 = w_down.astype(BF16).reshape(DEPTH * N_EXPERTS, D, D)
    bg = b_gate_up[..., 0::2].reshape(DEPTH * N_EXPERTS, 1, D)
    bu = b_gate_up[..., 1::2].reshape(DEPTH * N_EXPERTS, 1, D)
    bd = b_down.reshape(DEPTH * N_EXPERTS, 1, D)

    na_k, na_v, gqa_k, gqa_v = [], [], [], []
    for l in range(DEPTH):
        modr = mod[l].reshape(MOD_ROWS * 6, 1, D)
        qg2 = jnp.tile(q_norm_g[l], 2).reshape(1, LANES)
        kg2 = jnp.tile(k_norm_g[l], 2).reshape(1, LANES)
        lng = cm_ln_g[l].reshape(1, 256)
        lnb = cm_ln_b[l].reshape(1, 256)
        bsf = jnp.repeat(cm_bs[l].T, HEAD_DIM, axis=1)

        z = _inproj(xa, xb, xb_row0, tp + ts, modr, norm1_g[l].reshape(1, D), w_in16[l], tp, n_tok)
        mix_p, kbn = _ctx_mixer(z, nb, seq, qg2, kg2, lng, lnb, cm_ws[l], bsf)
        mix_s = _lat_mixer(z, tp, db, n_tok, past,
                           _ctx_lanes(cache_na_k[:, l]), _ctx_lanes(cache_na_v[:, l]),
                           _ctx_lanes(cache_gqa_k[:, l]), _ctx_lanes(cache_gqa_v[:, l]),
                           _na_bias(na_rpb[l], n_tok // GRID_W), cos, sin, qg2, kg2, lng, lnb, cm_ws[l], bsf)
        na_k.append(_to_heads(z[:tp, KA:KA + 256], nb, seq, 4))
        na_v.append(_to_heads(z[:tp, VA:VA + 256], nb, seq, 4))
        gqa_k.append(_to_heads(kbn, nb, seq, 2))
        gqa_v.append(_to_heads(z[:tp, VB:VB + LANES], nb, seq, 2))

        xn, ht, top_i, gates = _outproj(mix_p, mix_s, xa, xb, xb_row0, modr, norm2_g[l].reshape(1, D), w_out16[l],
                                  router_w[l].T, router_b[l].reshape(N_EXPERTS, 1), tp, n_tok)

        blk_e, row_tok, row_dst, n_live = _route_groups(top_i[:TOP_K].T, 2)
        gates = gates[:TOP_K].T
        ys = _moe(blk_e + l * N_EXPERTS, row_tok, row_dst, n_live, ht, 2, wg16, wu16, wd16, bg, bu, bd)
        if l + 1 < DEPTH:
            x = _combine(ys, gates, xn, modr, tp, n_tok, 0, tp + ts)
            xa, xb, xb_row0 = x, x, tp

    gf = final_norm_g.reshape(1, D)
    y_p = _combine(ys, gates, xn, modr, tp, n_tok, 0, tp, gf)
    y_s = _combine(ys, gates, xn, modr, tp, n_tok, tp, ts, gf)
    return (y_p.reshape(nb, seq, D), y_s.reshape(db, n_tok, D),
            jnp.stack(na_k, axis=1), jnp.stack(na_v, axis=1), jnp.stack(gqa_k, axis=1), jnp.stack(gqa_v, axis=1))
```

```python
import functools

import jax
import jax.numpy as jnp
import numpy as np
from jax import lax
from jax.experimental import pallas as pl
from jax.experimental.pallas import tpu as pltpu

D = 1024
DEPTH = 2
GRID_W = 64
HEAD_DIM = 64
NA_WIN_R = 8
NA_WIN_C = 16
CHUNK = 128
ROPE_THETA = 10000.0
N_EXPERTS = 32
TOP_K = 4
SWIGLU_LIMIT = 7.0
SWIGLU_ALPHA = 1.702
EPS = 1e-6
MOE_BLOCK = 128
Y_SLOTS = 3
NEG_INF = -1e30
IN_WIDTH = 2048

QA, KA, VA, QB, KB, VB, CU, CV = 0, 256, 512, 768, 1280, 1408, 1536, 1792
OA, OB, OC = 0, 256, 768

LANES = 128
ROW_TILE = D // LANES
MOD_ROWS = 16
CTX_ROW = 8
VMEM_LIMIT = 56 * 1024 * 1024

F32 = jnp.float32
BF16 = jnp.bfloat16
HI = lax.Precision.HIGHEST


def _cparams(sem):
    return pltpu.CompilerParams(dimension_semantics=sem, vmem_limit_bytes=VMEM_LIMIT)


def _dot(a, b):
    return jnp.dot(a, b, preferred_element_type=F32)


def _dot_nt(a, b):
    return lax.dot_general(a, b, (((1,), (1,)), ((), ())), preferred_element_type=F32)


def _lane_lo():
    return lax.broadcasted_iota(jnp.int32, (1, LANES), 1) < HEAD_DIM


def _mod_kernel(c_ref, w_ref, b_ref, o_ref):
    c = c_ref[...]
    s = c * jax.nn.sigmoid(c)
    o_ref[0] = jnp.dot(s, w_ref[0], preferred_element_type=F32, precision=HI) + b_ref[0]


def _modulation(cond, w_mod, b_mod):
    tn = 1536
    n = w_mod.shape[-1]
    return pl.pallas_call(
        _mod_kernel,
        grid=(DEPTH, n // tn),
        in_specs=[
            pl.BlockSpec((MOD_ROWS, D), lambda l, j: (0, 0)),
            pl.BlockSpec((1, D, tn), lambda l, j: (l, 0, j)),
            pl.BlockSpec((1, 1, tn), lambda l, j: (l, 0, j)),
        ],
        out_specs=pl.BlockSpec((1, MOD_ROWS, tn), lambda l, j: (l, 0, j)),
        out_shape=jax.ShapeDtypeStruct((DEPTH, MOD_ROWS, n), F32),
        compiler_params=_cparams(("arbitrary", "arbitrary")),
        name="modulation",
    )(cond, w_mod, b_mod.reshape(DEPTH, 1, n))


def _mod_row(i, tm, tp, ts_per_batch):
    start = i * tm
    return jnp.where(start < tp, CTX_ROW, (start - tp) // ts_per_batch)


def _mod_spec(k, tm, tp, ts_per_batch):
    return pl.BlockSpec((1, 1, D), lambda i: (_mod_row(i, tm, tp, ts_per_batch) * 6 + k, 0, 0))


def _x_specs(tm, n_top, bot_off, width=D):
    return [pl.BlockSpec((tm, width), lambda i: (jnp.minimum(i, n_top - 1), 0)),
            pl.BlockSpec((tm, width), lambda i: (jnp.maximum(i - n_top, 0) + bot_off, 0))]


def _inproj_kernel(n_top, xa_ref, xb_ref, shift_ref, scale_ref, g_ref, w_ref, z_ref):
    x = jnp.where(pl.program_id(0) < n_top, xa_ref[...], xb_ref[...])
    ms = jnp.mean(x * x, axis=-1, keepdims=True)
    y = x * lax.rsqrt(ms + EPS) * g_ref[...]
    h = y * (1.0 + scale_ref[0]) + shift_ref[0]
    z_ref[...] = _dot(h.astype(BF16), w_ref[...])


def _inproj(xa, xb, bot_off_rows, t, modr, g, w, tp, ts_per_batch):
    tm = 512
    n_top = tp // tm
    return pl.pallas_call(
        functools.partial(_inproj_kernel, n_top),
        grid=(t // tm,),
        in_specs=_x_specs(tm, n_top, bot_off_rows // tm) + [
            _mod_spec(0, tm, tp, ts_per_batch),
            _mod_spec(1, tm, tp, ts_per_batch),
            pl.BlockSpec((1, D), lambda i: (0, 0)),
            pl.BlockSpec((D, IN_WIDTH), lambda i: (0, 0)),
        ],
        out_specs=pl.BlockSpec((tm, IN_WIDTH), lambda i: (i, 0)),
        out_shape=jax.ShapeDtypeStruct((t, IN_WIDTH), F32),
        compiler_params=_cparams(("parallel",)),
        name="inproj",
    )(xa, xb, modr, modr, g, w)


def _head_rms(x, g2):
    lo = _lane_lo()
    x2 = x * x
    s_lo = jnp.sum(jnp.where(lo, x2, 0.0), axis=-1, keepdims=True)
    s_hi = jnp.sum(jnp.where(lo, 0.0, x2), axis=-1, keepdims=True)
    ms = jnp.where(lo, s_lo, s_hi) * (1.0 / HEAD_DIM)
    return x * lax.rsqrt(ms + EPS) * g2


def _softmax_pv(q16, ks, vs, biases):
    ss = []
    for k, b in zip(ks, biases):
        s = _dot_nt(q16, k)
        if b is not None:
            s = s + b
        ss.append(s)
    m = ss[0].max(axis=-1, keepdims=True)
    for s in ss[1:]:
        m = jnp.maximum(m, s.max(axis=-1, keepdims=True))
    den = None
    acc = None
    for s, v in zip(ss, vs):
        e = jnp.exp(s - m)
        d = e.sum(axis=-1, keepdims=True)
        o = _dot(e.astype(BF16), v)
        den = d if den is None else den + d
        acc = o if acc is None else acc + o
    return acc / den


def _attend_pair(q, ks_lo, vs_lo, ks_hi, vs_hi, b_lo, b_hi):
    lo = _lane_lo()
    o_lo = _softmax_pv(jnp.where(lo, q, 0.0).astype(BF16), ks_lo, vs_lo, b_lo)
    o_hi = _softmax_pv(jnp.where(lo, 0.0, q).astype(BF16), ks_hi, vs_hi, b_hi)
    return jnp.where(lo, o_lo, o_hi)


def _gelu(x):
    c = np.sqrt(2.0 / np.pi).astype(np.float32)
    return x * (0.5 * (1.0 + jnp.tanh(c * (x + 0.044715 * (x * x * x)))))


def _layer_norm(x, g, b):
    mu = jnp.mean(x, axis=-1, keepdims=True)
    xc = x - mu
    var = jnp.mean(xc * xc, axis=-1, keepdims=True)
    return xc * lax.rsqrt(var + EPS) * g + b


def _chunk_mlp(cu, cv, lng_ref, lnb_ref, ws_ref, bsf_ref, out_ref, col0):
    lo = _lane_lo()
    u = _gelu(cu)
    v = _layer_norm(_gelu(cv), lng_ref[...], lnb_ref[...]).astype(BF16)
    s = cu.shape[0]
    for n in range(s // CHUNK):
        rows = slice(n * CHUNK, (n + 1) * CHUNK)
        for jb in range(2):
            cols = slice(jb * LANES, (jb + 1) * LANES)
            vb = v[rows, cols]
            m_lo = _dot(ws_ref[2 * jb].astype(BF16), vb)
            m_hi = _dot(ws_ref[2 * jb + 1].astype(BF16), vb)
            mixed = jnp.where(lo, m_lo, m_hi) + bsf_ref[:, cols]
            out_ref[rows, col0 + jb * LANES:col0 + (jb + 1) * LANES] = (u[rows, cols] * mixed).astype(out_ref.dtype)


def _ctx_mixer_kernel(z_ref, qg_ref, kg_ref, lng_ref, lnb_ref, ws_ref, bsf_ref, mix_ref, kbn_ref):
    scale = HEAD_DIM ** -0.5
    for j in range(2):
        q = z_ref[:, QA + LANES * j:QA + LANES * (j + 1)] * scale
        k = [z_ref[:, KA + LANES * j:KA + LANES * (j + 1)].astype(BF16)]
        v = [z_ref[:, VA + LANES * j:VA + LANES * (j + 1)].astype(BF16)]
        o = _attend_pair(q, k, v, k, v, [None], [None])
        mix_ref[:, OA + LANES * j:OA + LANES * (j + 1)] = o.astype(mix_ref.dtype)

    kb = _head_rms(z_ref[:, KB:KB + LANES], kg_ref[...])
    kbn_ref[...] = kb
    vb = z_ref[:, VB:VB + LANES]
    k_same = [kb.astype(BF16)]
    k_swap = [pltpu.roll(kb, HEAD_DIM, 1).astype(BF16)]
    v_same = [vb.astype(BF16)]
    v_swap = [pltpu.roll(vb, HEAD_DIM, 1).astype(BF16)]
    for j in range(4):
        q = _head_rms(z_ref[:, QB + LANES * j:QB + LANES * (j + 1)], qg_ref[...]) * scale
        if j // 2 == 0:
            o = _attend_pair(q, k_same, v_same, k_swap, v_swap, [None], [None])
        else:
            o = _attend_pair(q, k_swap, v_swap, k_same, v_same, [None], [None])
        mix_ref[:, OB + LANES * j:OB + LANES * (j + 1)] = o.astype(mix_ref.dtype)

    _chunk_mlp(z_ref[:, CU:CU + 256], z_ref[:, CV:CV + 256], lng_ref, lnb_ref, ws_ref, bsf_ref, mix_ref, OC)


def _ctx_mixer(z, n_batch, seq, qg2, kg2, lng, lnb, ws, bsf):
    small = lambda shape: pl.BlockSpec(shape, lambda b: (0,) * len(shape))
    return pl.pallas_call(
        _ctx_mixer_kernel,
        grid=(n_batch,),
        in_specs=[
            pl.BlockSpec((seq, IN_WIDTH), lambda b: (b, 0)),
            small((1, LANES)), small((1, LANES)), small((1, 256)), small((1, 256)),
            small((4, CHUNK, CHUNK)), small((CHUNK, 256)),
        ],
        out_specs=[
            pl.BlockSpec((seq, D), lambda b: (b, 0)),
            pl.BlockSpec((seq, LANES), lambda b: (b, 0)),
        ],
        out_shape=[
            jax.ShapeDtypeStruct((n_batch * seq, D), BF16),
            jax.ShapeDtypeStruct((n_batch * seq, LANES), F32),
        ],
        compiler_params=_cparams(("parallel",)),
        name="ctx_mixer",
    )(z, qg2, kg2, lng, lnb, ws, bsf)


def _rope(x, cos, sin):
    first = (lax.broadcasted_iota(jnp.int32, (1, LANES), 1) % 32) < 16
    partner = jnp.where(first, pltpu.roll(x, LANES - 16, 1), pltpu.roll(x, 16, 1))
    return x * cos + partner * sin


def _lat_mixer_kernel(zq_ref, zkv_ref, nakc_ref, navc_ref, gkc_ref, gvc_ref, bias_ref,
                      cosq_ref, sinq_ref, cosk_ref, sink_ref,
                      qg_ref, kg_ref, lng_ref, lnb_ref, ws_ref, bsf_ref,
                      mix_ref,
                      kl_ref, kls_ref, vl_ref, vls_ref, kc_ref, kcs_ref, vc_ref, vcs_ref):
    scale = HEAD_DIM ** -0.5

    @pl.when(pl.program_id(1) == 0)
    def _():
        kb = _rope(_head_rms(zkv_ref[:, KB:KB + LANES], kg_ref[...]), cosk_ref[...], sink_ref[...])
        vb = zkv_ref[:, VB:VB + LANES]
        kl_ref[...] = kb.astype(BF16)
        kls_ref[...] = pltpu.roll(kb, HEAD_DIM, 1).astype(BF16)
        vl_ref[...] = vb.astype(BF16)
        vls_ref[...] = pltpu.roll(vb, HEAD_DIM, 1).astype(BF16)
        kc = gkc_ref[0]
        vc = gvc_ref[0]
        kc_ref[...] = kc.astype(BF16)
        kcs_ref[...] = pltpu.roll(kc, HEAD_DIM, 1).astype(BF16)
        vc_ref[...] = vc.astype(BF16)
        vcs_ref[...] = pltpu.roll(vc, HEAD_DIM, 1).astype(BF16)

    for j in range(2):
        cols = slice(LANES * j, LANES * (j + 1))
        q = zq_ref[:, QA + LANES * j:QA + LANES * (j + 1)] * scale
        ks = [zkv_ref[:, KA + LANES * j:KA + LANES * (j + 1)].astype(BF16), nakc_ref[0, :, cols].astype(BF16)]
        vs = [zkv_ref[:, VA + LANES * j:VA + LANES * (j + 1)].astype(BF16), navc_ref[0, :, cols].astype(BF16)]
        o = _attend_pair(q, ks, vs, ks, vs, [bias_ref[2 * j], None], [bias_ref[2 * j + 1], None])
        mix_ref[:, OA + LANES * j:OA + LANES * (j + 1)] = o.astype(mix_ref.dtype)

    same = ([kl_ref[...], kc_ref[...]], [vl_ref[...], vc_ref[...]])
    swap = ([kls_ref[...], kcs_ref[...]], [vls_ref[...], vcs_ref[...]])
    for j in range(4):
        q = _head_rms(zq_ref[:, QB + LANES * j:QB + LANES * (j + 1)], qg_ref[...])
        q = _rope(q, cosq_ref[...], sinq_ref[...]) * scale
        lo_kv, hi_kv = (same, swap) if j // 2 == 0 else (swap, same)
        o = _attend_pair(q, lo_kv[0], lo_kv[1], hi_kv[0], hi_kv[1], [None, None], [None, None])
        mix_ref[:, OB + LANES * j:OB + LANES * (j + 1)] = o.astype(mix_ref.dtype)

    _chunk_mlp(zq_ref[:, CU:CU + 256], zq_ref[:, CV:CV + 256], lng_ref, lnb_ref, ws_ref, bsf_ref, mix_ref, OC)


def _lat_mixer(z, row0, n_batch, n_tok, past, nakc, navc, gkc, gvc, bias, cos, sin, qg2, kg2, lng, lnb, ws, bsf):
    tq = 256
    nq = n_tok // tq
    small = lambda shape: pl.BlockSpec(shape, lambda b, t: (0,) * len(shape))
    qblk0 = row0 // tq
    kvblk0 = row0 // n_tok
    return pl.pallas_call(
        _lat_mixer_kernel,
        grid=(n_batch, nq),
        in_specs=[
            pl.BlockSpec((tq, IN_WIDTH), lambda b, t: (qblk0 + b * nq + t, 0)),
            pl.BlockSpec((n_tok, IN_WIDTH), lambda b, t: (kvblk0 + b, 0)),
            pl.BlockSpec((1, past, 256), lambda b, t: (b, 0, 0)),
            pl.BlockSpec((1, past, 256), lambda b, t: (b, 0, 0)),
            pl.BlockSpec((1, past, LANES), lambda b, t: (b, 0, 0)),
            pl.BlockSpec((1, past, LANES), lambda b, t: (b, 0, 0)),
            pl.BlockSpec((4, tq, n_tok), lambda b, t: (0, t, 0)),
            pl.BlockSpec((tq, LANES), lambda b, t: (t, 0)),
            pl.BlockSpec((tq, LANES), lambda b, t: (t, 0)),
            small((n_tok, LANES)), small((n_tok, LANES)),
            small((1, LANES)), small((1, LANES)), small((1, 256)), small((1, 256)),
            small((4, CHUNK, CHUNK)), small((CHUNK, 256)),
        ],
        out_specs=pl.BlockSpec((tq, D), lambda b, t: (b * nq + t, 0)),
        out_shape=jax.ShapeDtypeStruct((n_batch * n_tok, D), BF16),
        scratch_shapes=[pltpu.VMEM((n_tok, LANES), BF16)] * 4 + [pltpu.VMEM((past, LANES), BF16)] * 4,
        compiler_params=_cparams(("parallel", "arbitrary")),
        name="lat_mixer",
    )(z, z, nakc, navc, gkc, gvc, bias, cos, sin, cos, sin, qg2, kg2, lng, lnb, ws, bsf)


def _split_bf16(x):
    hi = x.astype(BF16)
    return hi, (x - hi.astype(F32)).astype(BF16)


def _outproj_kernel(n_top, mixa_ref, mixb_ref, xa_ref, xb_ref, g1_ref, shift_ref, scale_ref, g_ref, w_ref, rwt_ref, rb_ref,
                    xn_ref, ht_ref, ti_ref, gt_ref):
    top = pl.program_id(0) < n_top
    x = jnp.where(top, xa_ref[...], xb_ref[...])
    x = x + g1_ref[0] * _dot(jnp.where(top, mixa_ref[...], mixb_ref[...]), w_ref[...])
    xn_ref[...] = x
    ms = jnp.mean(x * x, axis=-1, keepdims=True)
    h = x * lax.rsqrt(ms + EPS) * g_ref[...]
    h = h * (1.0 + scale_ref[0]) + shift_ref[0]
    h_hi, h_lo = _split_bf16(h)
    rw_hi, rw_lo = _split_bf16(rwt_ref[...])
    logits = (_dot_nt(rw_hi, h_hi) + _dot_nt(rw_hi, h_lo)) + _dot_nt(rw_lo, h_hi) + rb_ref[...]

    expert = lax.broadcasted_iota(jnp.int32, logits.shape, 0)
    out_row = lax.broadcasted_iota(jnp.int32, ti_ref.shape, 0)
    top_i = jnp.zeros(ti_ref.shape, jnp.int32)
    top_v = []
    for k in range(TOP_K):
        m = jnp.max(logits, axis=0, keepdims=True)
        idx = jnp.min(jnp.where(logits == m, expert, N_EXPERTS), axis=0, keepdims=True)
        logits = jnp.where(expert == idx, -jnp.inf, logits)
        top_i = jnp.where(out_row == k, idx, top_i)
        top_v.append(m)
    es = [jnp.exp(v - top_v[0]) for v in top_v]
    den = (es[0] + es[1]) + (es[2] + es[3])
    gates = jnp.zeros(gt_ref.shape, F32)
    for k in range(TOP_K):
        gates = jnp.where(out_row == k, es[k] / den, gates)
    ti_ref[...] = top_i
    gt_ref[...] = gates

    for c in range(ROW_TILE):
        ht_ref[pl.ds(c, h.shape[0], stride=ROW_TILE), :] = h[:, LANES * c:LANES * (c + 1)]


def _outproj(mix_a, mix_b, xa, xb, bot_off_rows, modr, g, w, rwt, rb, tp, ts_per_batch):
    tm = 512
    t = mix_a.shape[0] + mix_b.shape[0]
    n_top = tp // tm
    return pl.pallas_call(
        functools.partial(_outproj_kernel, n_top),
        grid=(t // tm,),
        in_specs=_x_specs(tm, n_top, 0) + _x_specs(tm, n_top, bot_off_rows // tm) + [
            _mod_spec(2, tm, tp, ts_per_batch),
            _mod_spec(3, tm, tp, ts_per_batch),
            _mod_spec(4, tm, tp, ts_per_batch),
            pl.BlockSpec((1, D), lambda i: (0, 0)),
            pl.BlockSpec((D, D), lambda i: (0, 0)),
            pl.BlockSpec((N_EXPERTS, D), lambda i: (0, 0)),
            pl.BlockSpec((N_EXPERTS, 1), lambda i: (0, 0)),
        ],
        out_specs=[
            pl.BlockSpec((tm, D), lambda i: (i, 0)),
            pl.BlockSpec((tm * ROW_TILE, LANES), lambda i: (i, 0)),
            pl.BlockSpec((8, tm), lambda i: (0, i)),
            pl.BlockSpec((8, tm), lambda i: (0, i)),
        ],
        out_shape=[
            jax.ShapeDtypeStruct((t, D), F32),
            jax.ShapeDtypeStruct((t * ROW_TILE, LANES), F32),
            jax.ShapeDtypeStruct((8, t), jnp.int32),
            jax.ShapeDtypeStruct((8, t), F32),
        ],
        compiler_params=_cparams(("parallel",)),
        name="outproj",
    )(mix_a, mix_b, xa, xb, modr, modr, modr, g, w, rwt, rb)


def _prep_wgu_kernel(w_ref, p_ref, wg_ref, wu_ref):
    for j in range(w_ref.shape[2] // 256):
        w = w_ref[0, :, 256 * j:256 * (j + 1)].astype(BF16)
        sel = _dot(w, p_ref[...])
        wg_ref[0, :, LANES * j:LANES * (j + 1)] = sel[:, :LANES].astype(BF16)
        wu_ref[0, :, LANES * j:LANES * (j + 1)] = sel[:, LANES:].astype(BF16)


def _prep_wgu(w):
    e, d, f2 = w.shape
    perm = np.zeros((256, 256), np.float32)
    perm[2 * np.arange(LANES), np.arange(LANES)] = 1.0
    perm[2 * np.arange(LANES) + 1, LANES + np.arange(LANES)] = 1.0
    out = jax.ShapeDtypeStruct((e, d, f2 // 2), BF16)
    return pl.pallas_call(
        _prep_wgu_kernel,
        grid=(e,),
        in_specs=[pl.BlockSpec((1, d, f2), lambda i: (i, 0, 0)), pl.BlockSpec((256, 256), lambda i: (0, 0))],
        out_specs=[pl.BlockSpec((1, d, f2 // 2), lambda i: (i, 0, 0))] * 2,
        out_shape=[out, out],
        compiler_params=_cparams(("parallel",)),
        name="prep_wgu",
    )(w, jnp.asarray(perm, BF16))


def _moe_kernel(blk_e_ref, row_tok_ref, row_dst_ref, n_live_ref, wctl_ref, ht_ref, wg_hbm, wu_hbm, wd_hbm,
                bg_ref, bu_ref, bd_ref, out_ref, xg_ref, ybuf_ref, wbuf_ref, sem, wsem):
    i = pl.program_id(1)
    nb = pl.num_programs(1)
    step = pl.program_id(0) * nb + i
    n = pl.num_programs(0) * nb
    cur = step % 2
    nxt = 1 - cur
    ycur = step % Y_SLOTS
    yprev = (step + Y_SLOTS - 1) % Y_SLOTS
    blk = MOE_BLOCK * ROW_TILE

    def slot_copy(s):
        return pltpu.make_async_copy(ybuf_ref.at[pl.ds(s * blk, blk)], out_ref.at[pl.ds(0, blk)], sem.at[s])

    def gather_row(base, r, s):
        tok = row_tok_ref[base + r]
        xg_ref[pl.ds(pl.multiple_of(s * blk + r * ROW_TILE, ROW_TILE), ROW_TILE), :] = (
            ht_ref[pl.ds(pl.multiple_of(tok * ROW_TILE, ROW_TILE), ROW_TILE), :])

    def send_row(base, r, s, priority=0):
        dst = row_dst_ref[base + r]
        pltpu.make_async_copy(
            ybuf_ref.at[pl.ds(pl.multiple_of(s * blk + r * ROW_TILE, ROW_TILE), ROW_TILE)],
            out_ref.at[pl.ds(pl.multiple_of(dst * ROW_TILE, ROW_TILE), ROW_TILE)], sem.at[s]).start(priority)

    @pl.when(step == 0)
    def _():
        ybuf_ref[...] = jnp.zeros(ybuf_ref.shape, F32)
        for s in range(Y_SLOTS - 1):
            pltpu.make_async_copy(ybuf_ref.at[pl.ds(s * blk, blk)], out_ref.at[pl.ds((n + 1 + s) * blk, blk)],
                                  sem.at[s]).start()

    @pl.when(i == 0)
    def _():
        def first(r, carry):
            gather_row(step * MOE_BLOCK, r, cur)
            return carry

        lax.fori_loop(0, MOE_BLOCK, first, 0, unroll=8)

    wslot = wctl_ref[2 * n + step]

    def weight_copies(e, s):
        return [pltpu.make_async_copy(w.at[e], wbuf_ref.at[s, j], wsem.at[s, j])
                for j, w in enumerate((wg_hbm, wu_hbm, wd_hbm))]

    @pl.when(step == 0)
    def _():
        for cp in weight_copies(blk_e_ref[0], 0):
            cp.start()

    @pl.when(wctl_ref[step] == 1)
    def _():
        for cp in weight_copies(blk_e_ref[step], wslot):
            cp.wait()
        next_e = wctl_ref[n + step]

        @pl.when(next_e >= 0)
        def _():
            for cp in weight_copies(next_e, 1 - wslot):
                cp.start()

    slot_copy(ycur).wait()
    live = i < n_live_ref[pl.program_id(0)]

    @pl.when(jnp.logical_not(live))
    def _():
        def prev(r, carry):
            send_row(step * MOE_BLOCK, r, yprev)
            return carry

        lax.fori_loop(0, MOE_BLOCK, prev, 0, unroll=8)

    @pl.when(live)
    def _():
        _moe_block(step, i, nb, cur, nxt, ycur, yprev, blk, gather_row, send_row, xg_ref, ybuf_ref,
                   wbuf_ref.at[wslot, 0], wbuf_ref.at[wslot, 1], wbuf_ref.at[wslot, 2], bg_ref, bu_ref, bd_ref)

    @pl.when(step == n - 1)
    def _():
        def last(r, carry):
            send_row(n * MOE_BLOCK, r, ycur)
            return carry

        lax.fori_loop(0, MOE_BLOCK, last, 0, unroll=8)
        for s in range(Y_SLOTS):
            slot_copy(s).wait()


def _moe_block(step, i, nb, cur, nxt, ycur, yprev, blk, gather_row, send_row,
               xg_ref, ybuf_ref, wg_ref, wu_ref, wd_ref, bg_ref, bu_ref, bd_ref):
    x = jnp.concatenate(
        [xg_ref[pl.ds(cur * blk + c, MOE_BLOCK, stride=ROW_TILE), :].astype(BF16) for c in range(ROW_TILE)], axis=1)
    next_blk = jnp.where(i + 1 < nb, step + 1, step)
    for r in range(MOE_BLOCK):
        gather_row(next_blk * MOE_BLOCK, r, nxt)
    for r in range(MOE_BLOCK):
        send_row(step * MOE_BLOCK, r, yprev, priority=r % 2)
    g = _dot(x, wg_ref[...]) + bg_ref[0]
    u = _dot(x, wu_ref[...]) + bu_ref[0]
    g = jnp.minimum(g, SWIGLU_LIMIT)
    u = jnp.clip(u, -SWIGLU_LIMIT, SWIGLU_LIMIT)
    act = (u + 1.0) * (g * jax.nn.sigmoid(SWIGLU_ALPHA * g))
    y = _dot(act.astype(BF16), wd_ref[...]) + bd_ref[0]
    for c in range(ROW_TILE):
        ybuf_ref[pl.ds(ycur * blk + c, MOE_BLOCK, stride=ROW_TILE), :] = y[:, LANES * c:LANES * (c + 1)]


def _weight_schedule(blk_e):
    n = blk_e.shape[0]
    start = jnp.concatenate([jnp.ones((1,), jnp.int32), (blk_e[1:] != blk_e[:-1]).astype(jnp.int32)])
    parity = (jnp.cumsum(start) - 1) % 2
    idx = jnp.arange(n, dtype=jnp.int32)
    later_start = (idx[None, :] > idx[:, None]) & (start[None, :] == 1)
    j_next = jnp.min(jnp.where(later_start, idx[None, :], n), axis=1)
    next_e = jnp.where(j_next < n, blk_e[jnp.minimum(j_next, n - 1)], -1)
    return jnp.concatenate([start, next_e, parity]).astype(jnp.int32)


def _moe(blk_e, row_tok, row_dst, n_live, ht, n_groups, wg, wu, wd, bg, bu, bd):
    nb = blk_e.shape[0] // n_groups
    tg = ht.shape[0] // ROW_TILE // n_groups
    ew = lambda g, i, be, rt, rd, nl, wc: (be[g * nb + i], 0, 0)
    hbm = pl.BlockSpec(memory_space=pl.ANY)
    return pl.pallas_call(
        _moe_kernel,
        grid_spec=pltpu.PrefetchScalarGridSpec(
            num_scalar_prefetch=5,
            grid=(n_groups, nb),
            in_specs=[
                pl.BlockSpec((tg * ROW_TILE, LANES), lambda g, i, be, rt, rd, nl, wc: (g, 0),
                             pipeline_mode=pl.Buffered(1)),
                hbm, hbm, hbm,
                pl.BlockSpec((1, 1, D), ew),
                pl.BlockSpec((1, 1, D), ew),
                pl.BlockSpec((1, 1, D), ew),
            ],
            out_specs=pl.BlockSpec(memory_space=pl.ANY),
            scratch_shapes=[
                pltpu.VMEM((2 * MOE_BLOCK * ROW_TILE, LANES), F32),
                pltpu.VMEM((Y_SLOTS * MOE_BLOCK * ROW_TILE, LANES), F32),
                pltpu.VMEM((2, 3, D, D), BF16),
                pltpu.SemaphoreType.DMA((Y_SLOTS,)),
                pltpu.SemaphoreType.DMA((2, 3)),
            ],
        ),
        out_shape=jax.ShapeDtypeStruct(((n_groups * nb + Y_SLOTS) * MOE_BLOCK * ROW_TILE, LANES), F32),
        compiler_params=_cparams(("arbitrary", "arbitrary")),
        name="moe_experts",
    )(blk_e, row_tok, row_dst, n_live, _weight_schedule(blk_e), ht, wg, wu, wd, bg, bu, bd)


def _route(top_i, group, n_groups):
    tg = top_i.shape[0]
    n_assign = tg * TOP_K
    assert n_assign <= 1 << 16 and n_assign % MOE_BLOCK == 0
    flat_e = top_i.reshape(-1).astype(jnp.int32)
    experts = jnp.arange(N_EXPERTS, dtype=jnp.int32)
    counts = jnp.sum((flat_e[:, None] == experts[None, :]).astype(jnp.int32), axis=0)
    padded = (counts + MOE_BLOCK - 1) // MOE_BLOCK * MOE_BLOCK
    pad_end = jnp.cumsum(padded)
    n_blocks = n_assign // MOE_BLOCK + N_EXPERTS
    real_keys = (flat_e << 17) | jnp.arange(n_assign, dtype=jnp.int32)
    pad_id = experts[:, None] * MOE_BLOCK + jnp.arange(MOE_BLOCK, dtype=jnp.int32)[None, :]
    used = jnp.arange(MOE_BLOCK, dtype=jnp.int32)[None, :] < (padded - counts)[:, None]
    pad_keys = (jnp.where(used, experts[:, None], 63) << 17) | (1 << 16) | pad_id
    keys = jnp.sort(jnp.concatenate([real_keys, pad_keys.reshape(-1)]))
    is_pad = ((keys >> 16) & 1) == 1
    payload = keys & 0xFFFF
    tok = payload // TOP_K
    t_all = tg * n_groups
    row_tok = jnp.where(is_pad, 0, tok)
    row_dst = jnp.where(is_pad, TOP_K * t_all + group * (N_EXPERTS * MOE_BLOCK) + payload,
                        (payload % TOP_K) * t_all + group * tg + tok)
    blk_start = jnp.arange(n_blocks, dtype=jnp.int32) * MOE_BLOCK
    blk_e = jnp.sum((pad_end[None, :] <= blk_start[:, None]).astype(jnp.int32), axis=1)
    blk_e = jnp.minimum(blk_e, N_EXPERTS - 1)
    n_live = pad_end[-1] // MOE_BLOCK
    return blk_e, row_tok, row_dst, n_live


def _route_groups(top_i, n_groups):
    tg = top_i.shape[0] // n_groups
    parts = [_route(top_i[g * tg:(g + 1) * tg], g, n_groups) for g in range(n_groups)]
    blk_e = jnp.concatenate([p[0] for p in parts])
    row_tok = jnp.concatenate([p[1] for p in parts])
    spare = row_tok.shape[0] + jnp.arange(MOE_BLOCK, dtype=jnp.int32)
    row_dst = jnp.concatenate([spare] + [p[2] for p in parts])
    n_live = jnp.stack([p[3] for p in parts]).astype(jnp.int32)
    return blk_e, row_tok, row_dst, n_live


def _combine_kernel(final, y0_ref, y1_ref, y2_ref, y3_ref, gates_ref, x_ref, g2_ref, gf_ref, o_ref):
    g = gates_ref[...]
    tm = x_ref.shape[0]
    for c in range(ROW_TILE):
        rows = pl.ds(c, tm, stride=ROW_TILE)
        cols = slice(LANES * c, LANES * (c + 1))
        y = ((y0_ref[rows, :] * g[:, 0:1] + y1_ref[rows, :] * g[:, 1:2])
             + (y2_ref[rows, :] * g[:, 2:3] + y3_ref[rows, :] * g[:, 3:4]))
        o_ref[:, cols] = x_ref[:, cols] + g2_ref[0][:, cols] * y
    if final:
        x = o_ref[...]
        ms = jnp.mean(x * x, axis=-1, keepdims=True)
        o_ref[...] = x * lax.rsqrt(ms + EPS) * gf_ref[...]


def _combine(ys, gates, x, modr, tp, ts_per_batch, row0, n_rows, final_g=None):
    tm = 512
    nt_all = x.shape[0] // tm
    i0 = row0 // tm
    yspec = lambda k: pl.BlockSpec((tm * ROW_TILE, LANES), lambda i: (k * nt_all + i0 + i, 0))
    mod = pl.BlockSpec((1, 1, D), lambda i: (_mod_row(i0 + i, tm, tp, ts_per_batch) * 6 + 5, 0, 0))
    gf = jnp.ones((1, D), F32) if final_g is None else final_g
    return pl.pallas_call(
        functools.partial(_combine_kernel, final_g is not None),
        grid=(n_rows // tm,),
        in_specs=[yspec(0), yspec(1), yspec(2), yspec(3), pl.BlockSpec((tm, TOP_K), lambda i: (i0 + i, 0)),
                  pl.BlockSpec((tm, D), lambda i: (i0 + i, 0)), mod, pl.BlockSpec((1, D), lambda i: (0, 0))],
        out_specs=pl.BlockSpec((tm, D), lambda i: (i, 0)),
        out_shape=jax.ShapeDtypeStruct((n_rows, D), F32),
        compiler_params=_cparams(("parallel",)),
        name="moe_combine",
    )(ys, ys, ys, ys, gates, x, modr, gf)


def _na_bias_kernel(rows, t_ref, o_ref):
    kr = min(NA_WIN_R, rows)
    lo = _lane_lo()
    for r in range(rows):
        r0 = min(max(r - kr // 2, 0), rows - kr)
        for p in range(rows // 2):
            ok0 = r0 <= 2 * p < r0 + kr
            ok1 = r0 <= 2 * p + 1 < r0 + kr
            if ok0 or ok1:
                d = 2 * p - r + (NA_WIN_R - 1)
                assert 0 <= d < t_ref.shape[1]
                tile = t_ref[0, d]
                if not ok0:
                    tile = jnp.where(lo, NEG_INF, tile)
                if not ok1:
                    tile = jnp.where(lo, tile, NEG_INF)
            else:
                tile = jnp.full((GRID_W, LANES), NEG_INF, F32)
            o_ref[0, r * GRID_W:(r + 1) * GRID_W, p * LANES:(p + 1) * LANES] = tile


def _na_bias(rpb, rows):
    assert rows % 2 == 0 and 2 * GRID_W == LANES
    heads = rpb.shape[0]
    nd = 2 * NA_WIN_R
    c = np.arange(GRID_W)
    w0 = np.clip(c - NA_WIN_C // 2, 0, GRID_W - NA_WIN_C)
    col_ok = (c[None, :] >= w0[:, None]) & (c[None, :] < w0[:, None] + NA_WIN_C)
    dc = np.clip(c[None, :] - c[:, None] + (NA_WIN_C - 1), 0, 2 * NA_WIN_C - 2)
    oh_c = np.eye(2 * NA_WIN_C - 1, dtype=np.float32)[dc]
    vals = jnp.einsum('hde,qce->hdqc', rpb.astype(F32), oh_c, precision=HI)
    per_row = jnp.where(col_ok[None, None], vals, NEG_INF)
    per_row = jnp.pad(per_row, ((0, 0), (0, nd + 1 - per_row.shape[1]), (0, 0), (0, 0)), constant_values=NEG_INF)
    tables = jnp.concatenate([per_row[:, :nd], per_row[:, 1:nd + 1]], axis=-1)
    n = rows * GRID_W
    return pl.pallas_call(
        functools.partial(_na_bias_kernel, rows),
        grid=(heads,),
        in_specs=[pl.BlockSpec((1, nd, GRID_W, LANES), lambda h: (h, 0, 0, 0))],
        out_specs=pl.BlockSpec((1, n, n), lambda h: (h, 0, 0)),
        out_shape=jax.ShapeDtypeStruct((heads, n, n), F32),
        compiler_params=_cparams(("parallel",)),
        name="na_bias",
    )(tables)


def _rope_tables(n_tokens):
    t = np.arange(n_tokens)
    row = (t // GRID_W).astype(np.float32)
    col = (t % GRID_W).astype(np.float32)
    half = HEAD_DIM // 2
    inv = jnp.asarray(ROPE_THETA, F32) ** (-jnp.arange(0, half, 2, dtype=F32) / half)
    ang_r = jnp.asarray(row)[:, None] * inv
    ang_c = jnp.asarray(col)[:, None] * inv
    cr, sr, cc, sc = jnp.cos(ang_r), jnp.sin(ang_r), jnp.cos(ang_c), jnp.sin(ang_c)
    cos = jnp.concatenate([cr, cr, cc, cc] * 2, axis=-1)
    sin = jnp.concatenate([-sr, sr, -sc, sc] * 2, axis=-1)
    return cos, sin


def _to_heads(x, n_batch, seq, n_heads):
    return x.reshape(n_batch, seq, n_heads, HEAD_DIM).transpose(0, 2, 1, 3)


def _ctx_lanes(cache_l):
    b, h, p, dh = cache_l.shape
    return cache_l.transpose(0, 2, 1, 3).reshape(b, p, h * dh)


def kernel(x_prompt, x_sample, cache_na_k, cache_na_v, cache_gqa_k, cache_gqa_v, c, c_ctx, w_mod, b_mod, norm1_g, norm2_g, w_in, na_rpb, q_norm_g, k_norm_g, cm_ln_g, cm_ln_b, cm_ws, cm_bs, w_out, router_w, router_b, w_gate_up, b_gate_up, w_down, b_down, final_norm_g):
    nb, seq, _ = x_prompt.shape
    db, n_tok, _ = x_sample.shape
    past = cache_na_k.shape[3]
    tp, ts = nb * seq, db * n_tok
    assert db <= CTX_ROW and n_tok % GRID_W == 0
    assert tp == ts

    cond = jnp.zeros((MOD_ROWS, D), F32).at[:db].set(c).at[CTX_ROW].set(c_ctx)
    mod = _modulation(cond, w_mod, b_mod)
    cos, sin = _rope_tables(n_tok)
    xa, xb, xb_row0 = x_prompt.reshape(tp, D), x_sample.reshape(ts, D), 0

    w_in16 = w_in.astype(BF16)
    w_out16 = w_out.astype(BF16)
    wg16, wu16 = _prep_wgu(w_gate_up.reshape(DEPTH * N_EXPERTS, D, 2 * D))
    wd16 = w_down.astype(BF16).reshape(DEPTH * N_EXPERTS, D, D)
    bg = b_gate_up[..., 0::2].reshape(DEPTH * N_EXPERTS, 1, D)
    bu = b_gate_up[..., 1::2].reshape(DEPTH * N_EXPERTS, 1, D)
    bd = b_down.reshape(DEPTH * N_EXPERTS, 1, D)

    na_k, na_v, gqa_k, gqa_v = [], [], [], []
    for l in range(DEPTH):
        modr = mod[l].reshape(MOD_ROWS * 6, 1, D)
        qg2 = jnp.tile(q_norm_g[l], 2).reshape(1, LANES)
        kg2 = jnp.tile(k_norm_g[l], 2).reshape(1, LANES)
        lng = cm_ln_g[l].reshape(1, 256)
        lnb = cm_ln_b[l].reshape(1, 256)
        bsf = jnp.repeat(cm_bs[l].T, HEAD_DIM, axis=1)

        z = _inproj(xa, xb, xb_row0, tp + ts, modr, norm1_g[l].reshape(1, D), w_in16[l], tp, n_tok)
        mix_p, kbn = _ctx_mixer(z, nb, seq, qg2, kg2, lng, lnb, cm_ws[l], bsf)
        mix_s = _lat_mixer(z, tp, db, n_tok, past,
                           _ctx_lanes(cache_na_k[:, l]), _ctx_lanes(cache_na_v[:, l]),
                           _ctx_lanes(cache_gqa_k[:, l]), _ctx_lanes(cache_gqa_v[:, l]),
                           _na_bias(na_rpb[l], n_tok // GRID_W), cos, sin, qg2, kg2, lng, lnb, cm_ws[l], bsf)
        na_k.append(_to_heads(z[:tp, KA:KA + 256], nb, seq, 4))
        na_v.append(_to_heads(z[:tp, VA:VA + 256], nb, seq, 4))
        gqa_k.append(_to_heads(kbn, nb, seq, 2))
        gqa_v.append(_to_heads(z[:tp, VB:VB + LANES], nb, seq, 2))

        xn, ht, top_i, gates = _outproj(mix_p, mix_s, xa, xb, xb_row0, modr, norm2_g[l].reshape(1, D), w_out16[l],
                                  router_w[l].T, router_b[l].reshape(N_EXPERTS, 1), tp, n_tok)

        blk_e, row_tok, row_dst, n_live = _route_groups(top_i[:TOP_K].T, 2)
        gates = gates[:TOP_K].T
        ys = _moe(blk_e + l * N_EXPERTS, row_tok, row_dst, n_live, ht, 2, wg16, wu16, wd16, bg, bu, bd)
        if l + 1 < DEPTH:
            x = _combine(ys, gates, xn, modr, tp, n_tok, 0, tp + ts)
            xa, xb, xb_row0 = x, x, tp

    gf = final_norm_g.reshape(1, D)
    y_p = _combine(ys, gates, xn, modr, tp, n_tok, 0, tp, gf)
    y_s = _combine(ys, gates, xn, modr, tp, n_tok, tp, ts, gf)
    return (y_p.reshape(nb, seq, D), y_s.reshape(db, n_tok, D),
            jnp.stack(na_k, axis=1), jnp.stack(na_v, axis=1), jnp.stack(gqa_k, axis=1), jnp.stack(gqa_v, axis=1))
```

```python
import functools

import jax
import jax.numpy as jnp
import numpy as np
from jax import lax
from jax.experimental import pallas as pl
from jax.experimental.pallas import tpu as pltpu

D = 1024
DEPTH = 2
GRID_W = 64
HEAD_DIM = 64
NA_WIN_R = 8
NA_WIN_C = 16
CHUNK = 128
ROPE_THETA = 10000.0
N_EXPERTS = 32
TOP_K = 4
SWIGLU_LIMIT = 7.0
SWIGLU_ALPHA = 1.702
EPS = 1e-6
MOE_BLOCK = 128
Y_SLOTS = 3
NEG_INF = -1e30
IN_WIDTH = 2048

QA, KA, VA, QB, KB, VB, CU, CV = 0, 256, 512, 768, 1280, 1408, 1536, 1792
OA, OB, OC = 0, 256, 768

LANES = 128
ROW_TILE = D // LANES
MOD_ROWS = 16
CTX_ROW = 8
VMEM_LIMIT = 56 * 1024 * 1024

F32 = jnp.float32
BF16 = jnp.bfloat16
HI = lax.Precision.HIGHEST


def _cparams(sem):
    return pltpu.CompilerParams(dimension_semantics=sem, vmem_limit_bytes=VMEM_LIMIT)


def _dot(a, b):
    return jnp.dot(a, b, preferred_element_type=F32)


def _dot_nt(a, b):
    return lax.dot_general(a, b, (((1,), (1,)), ((), ())), preferred_element_type=F32)


def _lane_lo():
    return lax.broadcasted_iota(jnp.int32, (1, LANES), 1) < HEAD_DIM


def _mod_kernel(c_ref, w_ref, b_ref, o_ref):
    c = c_ref[...]
    s = c * jax.nn.sigmoid(c)
    o_ref[0] = jnp.dot(s, w_ref[0], preferred_element_type=F32, precision=HI) + b_ref[0]


def _modulation(cond, w_mod, b_mod):
    tn = 1536
    n = w_mod.shape[-1]
    return pl.pallas_call(
        _mod_kernel,
        grid=(DEPTH, n // tn),
        in_specs=[
            pl.BlockSpec((MOD_ROWS, D), lambda l, j: (0, 0)),
            pl.BlockSpec((1, D, tn), lambda l, j: (l, 0, j)),
            pl.BlockSpec((1, 1, tn), lambda l, j: (l, 0, j)),
        ],
        out_specs=pl.BlockSpec((1, MOD_ROWS, tn), lambda l, j: (l, 0, j)),
        out_shape=jax.ShapeDtypeStruct((DEPTH, MOD_ROWS, n), F32),
        compiler_params=_cparams(("arbitrary", "arbitrary")),
        name="modulation",
    )(cond, w_mod, b_mod.reshape(DEPTH, 1, n))


def _mod_row(i, tm, tp, ts_per_batch):
    start = i * tm
    return jnp.where(start < tp, CTX_ROW, (start - tp) // ts_per_batch)


def _mod_spec(k, tm, tp, ts_per_batch):
    return pl.BlockSpec((1, 1, D), lambda i: (_mod_row(i, tm, tp, ts_per_batch) * 6 + k, 0, 0))


def _x_specs(tm, n_top, bot_off, width=D):
    return [pl.BlockSpec((tm, width), lambda i: (jnp.minimum(i, n_top - 1), 0)),
            pl.BlockSpec((tm, width), lambda i: (jnp.maximum(i - n_top, 0) + bot_off, 0))]


def _inproj_kernel(n_top, xa_ref, xb_ref, shift_ref, scale_ref, g_ref, w_ref, z_ref):
    x = jnp.where(pl.program_id(0) < n_top, xa_ref[...], xb_ref[...])
    ms = jnp.mean(x * x, axis=-1, keepdims=True)
    y = x * lax.rsqrt(ms + EPS) * g_ref[...]
    h = y * (1.0 + scale_ref[0]) + shift_ref[0]
    z_ref[...] = _dot(h.astype(BF16), w_ref[...])


def _inproj(xa, xb, bot_off_rows, t, modr, g, w, tp, ts_per_batch):
    tm = 512
    n_top = tp // tm
    return pl.pallas_call(
        functools.partial(_inproj_kernel, n_top),
        grid=(t // tm,),
        in_specs=_x_specs(tm, n_top, bot_off_rows // tm) + [
            _mod_spec(0, tm, tp, ts_per_batch),
            _mod_spec(1, tm, tp, ts_per_batch),
            pl.BlockSpec((1, D), lambda i: (0, 0)),
            pl.BlockSpec((D, IN_WIDTH), lambda i: (0, 0)),
        ],
        out_specs=pl.BlockSpec((tm, IN_WIDTH), lambda i: (i, 0)),
        out_shape=jax.ShapeDtypeStruct((t, IN_WIDTH), F32),
        compiler_params=_cparams(("parallel",)),
        name="inproj",
    )(xa, xb, modr, modr, g, w)


def _head_rms(x, g2):
    lo = _lane_lo()
    x2 = x * x
    s_lo = jnp.sum(jnp.where(lo, x2, 0.0), axis=-1, keepdims=True)
    s_hi = jnp.sum(jnp.where(lo, 0.0, x2), axis=-1, keepdims=True)
    ms = jnp.where(lo, s_lo, s_hi) * (1.0 / HEAD_DIM)
    return x * lax.rsqrt(ms + EPS) * g2


def _softmax_pv(q16, ks, vs, biases):
    ss = []
    for k, b in zip(ks, biases):
        s = _dot_nt(q16, k)
        if b is not None:
            s = s + b
        ss.append(s)
    m = ss[0].max(axis=-1, keepdims=True)
    for s in ss[1:]:
        m = jnp.maximum(m, s.max(axis=-1, keepdims=True))
    den = None
    acc = None
    for s, v in zip(ss, vs):
        e = jnp.exp(s - m)
        d = e.sum(axis=-1, keepdims=True)
        o = _dot(e.astype(BF16), v)
        den = d if den is None else den + d
        acc = o if acc is None else acc + o
    return acc / den


def _attend_pair(q, ks_lo, vs_lo, ks_hi, vs_hi, b_lo, b_hi):
    lo = _lane_lo()
    o_lo = _softmax_pv(jnp.where(lo, q, 0.0).astype(BF16), ks_lo, vs_lo, b_lo)
    o_hi = _softmax_pv(jnp.where(lo, 0.0, q).astype(BF16), ks_hi, vs_hi, b_hi)
    return jnp.where(lo, o_lo, o_hi)


def _gelu(x):
    c = np.sqrt(2.0 / np.pi).astype(np.float32)
    return x * (0.5 * (1.0 + jnp.tanh(c * (x + 0.044715 * (x * x * x)))))


def _layer_norm(x, g, b):
    mu = jnp.mean(x, axis=-1, keepdims=True)
    xc = x - mu
    var = jnp.mean(xc * xc, axis=-1, keepdims=True)
    return xc * lax.rsqrt(var + EPS) * g + b


def _chunk_mlp(cu, cv, lng_ref, lnb_ref, ws_ref, bsf_ref, out_ref, col0):
    lo = _lane_lo()
    u = _gelu(cu)
    v = _layer_norm(_gelu(cv), lng_ref[...], lnb_ref[...]).astype(BF16)
    s = cu.shape[0]
    for n in range(s // CHUNK):
        rows = slice(n * CHUNK, (n + 1) * CHUNK)
        for jb in range(2):
            cols = slice(jb * LANES, (jb + 1) * LANES)
            vb = v[rows, cols]
            m_lo = _dot(ws_ref[2 * jb].astype(BF16), vb)
            m_hi = _dot(ws_ref[2 * jb + 1].astype(BF16), vb)
            mixed = jnp.where(lo, m_lo, m_hi) + bsf_ref[:, cols]
            out_ref[rows, col0 + jb * LANES:col0 + (jb + 1) * LANES] = (u[rows, cols] * mixed).astype(out_ref.dtype)


def _ctx_mixer_kernel(z_ref, qg_ref, kg_ref, lng_ref, lnb_ref, ws_ref, bsf_ref, mix_ref, kbn_ref):
    scale = HEAD_DIM ** -0.5
    for j in range(2):
        q = z_ref[:, QA + LANES * j:QA + LANES * (j + 1)] * scale
        k = [z_ref[:, KA + LANES * j:KA + LANES * (j + 1)].astype(BF16)]
        v = [z_ref[:, VA + LANES * j:VA + LANES * (j + 1)].astype(BF16)]
        o = _attend_pair(q, k, v, k, v, [None], [None])
        mix_ref[:, OA + LANES * j:OA + LANES * (j + 1)] = o.astype(mix_ref.dtype)

    kb = _head_rms(z_ref[:, KB:KB + LANES], kg_ref[...])
    kbn_ref[...] = kb
    vb = z_ref[:, VB:VB + LANES]
    k_same = [kb.astype(BF16)]
    k_swap = [pltpu.roll(kb, HEAD_DIM, 1).astype(BF16)]
    v_same = [vb.astype(BF16)]
    v_swap = [pltpu.roll(vb, HEAD_DIM, 1).astype(BF16)]
    for j in range(4):
        q = _head_rms(z_ref[:, QB + LANES * j:QB + LANES * (j + 1)], qg_ref[...]) * scale
        if j // 2 == 0:
            o = _attend_pair(q, k_same, v_same, k_swap, v_swap, [None], [None])
        else:
            o = _attend_pair(q, k_swap, v_swap, k_same, v_same, [None], [None])
        mix_ref[:, OB + LANES * j:OB + LANES * (j + 1)] = o.astype(mix_ref.dtype)

    _chunk_mlp(z_ref[:, CU:CU + 256], z_ref[:, CV:CV + 256], lng_ref, lnb_ref, ws_ref, bsf_ref, mix_ref, OC)


def _ctx_mixer(z, n_batch, seq, qg2, kg2, lng, lnb, ws, bsf):
    small = lambda shape: pl.BlockSpec(shape, lambda b: (0,) * len(shape))
    return pl.pallas_call(
        _ctx_mixer_kernel,
        grid=(n_batch,),
        in_specs=[
            pl.BlockSpec((seq, IN_WIDTH), lambda b: (b, 0)),
            small((1, LANES)), small((1, LANES)), small((1, 256)), small((1, 256)),
            small((4, CHUNK, CHUNK)), small((CHUNK, 256)),
        ],
        out_specs=[
            pl.BlockSpec((seq, D), lambda b: (b, 0)),
            pl.BlockSpec((seq, LANES), lambda b: (b, 0)),
        ],
        out_shape=[
            jax.ShapeDtypeStruct((n_batch * seq, D), BF16),
            jax.ShapeDtypeStruct((n_batch * seq, LANES), F32),
        ],
        compiler_params=_cparams(("parallel",)),
        name="ctx_mixer",
    )(z, qg2, kg2, lng, lnb, ws, bsf)


def _rope(x, cos, sin):
    first = (lax.broadcasted_iota(jnp.int32, (1, LANES), 1) % 32) < 16
    partner = jnp.where(first, pltpu.roll(x, LANES - 16, 1), pltpu.roll(x, 16, 1))
    return x * cos + partner * sin


def _lat_mixer_kernel(zq_ref, zkv_ref, nakc_ref, navc_ref, gkc_ref, gvc_ref, bias_ref,
                      cosq_ref, sinq_ref, cosk_ref, sink_ref,
                      qg_ref, kg_ref, lng_ref, lnb_ref, ws_ref, bsf_ref,
                      mix_ref,
                      kl_ref, kls_ref, vl_ref, vls_ref, kc_ref, kcs_ref, vc_ref, vcs_ref):
    scale = HEAD_DIM ** -0.5

    @pl.when(pl.program_id(1) == 0)
    def _():
        kb = _rope(_head_rms(zkv_ref[:, KB:KB + LANES], kg_ref[...]), cosk_ref[...], sink_ref[...])
        vb = zkv_ref[:, VB:VB + LANES]
        kl_ref[...] = kb.astype(BF16)
        kls_ref[...] = pltpu.roll(kb, HEAD_DIM, 1).astype(BF16)
        vl_ref[...] = vb.astype(BF16)
        vls_ref[...] = pltpu.roll(vb, HEAD_DIM, 1).astype(BF16)
        kc = gkc_ref[0]
        vc = gvc_ref[0]
        kc_ref[...] = kc.astype(BF16)
        kcs_ref[...] = pltpu.roll(kc, HEAD_DIM, 1).astype(BF16)
        vc_ref[...] = vc.astype(BF16)
        vcs_ref[...] = pltpu.roll(vc, HEAD_DIM, 1).astype(BF16)

    for j in range(2):
        cols = slice(LANES * j, LANES * (j + 1))
        q = zq_ref[:, QA + LANES * j:QA + LANES * (j + 1)] * scale
        ks = [zkv_ref[:, KA + LANES * j:KA + LANES * (j + 1)].astype(BF16), nakc_ref[0, :, cols].astype(BF16)]
        vs = [zkv_ref[:, VA + LANES * j:VA + LANES * (j + 1)].astype(BF16), navc_ref[0, :, cols].astype(BF16)]
        o = _attend_pair(q, ks, vs, ks, vs, [bias_ref[2 * j], None], [bias_ref[2 * j + 1], None])
        mix_ref[:, OA + LANES * j:OA + LANES * (j + 1)] = o.astype(mix_ref.dtype)

    same = ([kl_ref[...], kc_ref[...]], [vl_ref[...], vc_ref[...]])
    swap = ([kls_ref[...], kcs_ref[...]], [vls_ref[...], vcs_ref[...]])
    for j in range(4):
        q = _head_rms(zq_ref[:, QB + LANES * j:QB + LANES * (j + 1)], qg_ref[...])
        q = _rope(q, cosq_ref[...], sinq_ref[...]) * scale
        lo_kv, hi_kv = (same, swap) if j // 2 == 0 else (swap, same)
        o = _attend_pair(q, lo_kv[0], lo_kv[1], hi_kv[0], hi_kv[1], [None, None], [None, None])
        mix_ref[:, OB + LANES * j:OB + LANES * (j + 1)] = o.astype(mix_ref.dtype)

    _chunk_mlp(zq_ref[:, CU:CU + 256], zq_ref[:, CV:CV + 256], lng_ref, lnb_ref, ws_ref, bsf_ref, mix_ref, OC)


def _lat_mixer(z, row0, n_batch, n_tok, past, nakc, navc, gkc, gvc, bias, cos, sin, qg2, kg2, lng, lnb, ws, bsf):
    tq = 256
    nq = n_tok // tq
    small = lambda shape: pl.BlockSpec(shape, lambda b, t: (0,) * len(shape))
    qblk0 = row0 // tq
    kvblk0 = row0 // n_tok
    return pl.pallas_call(
        _lat_mixer_kernel,
        grid=(n_batch, nq),
        in_specs=[
            pl.BlockSpec((tq, IN_WIDTH), lambda b, t: (qblk0 + b * nq + t, 0)),
            pl.BlockSpec((n_tok, IN_WIDTH), lambda b, t: (kvblk0 + b, 0)),
            pl.BlockSpec((1, past, 256), lambda b, t: (b, 0, 0)),
            pl.BlockSpec((1, past, 256), lambda b, t: (b, 0, 0)),
            pl.BlockSpec((1, past, LANES), lambda b, t: (b, 0, 0)),
            pl.BlockSpec((1, past, LANES), lambda b, t: (b, 0, 0)),
            pl.BlockSpec((4, tq, n_tok), lambda b, t: (0, t, 0)),
            pl.BlockSpec((tq, LANES), lambda b, t: (t, 0)),
            pl.BlockSpec((tq, LANES), lambda b, t: (t, 0)),
            small((n_tok, LANES)), small((n_tok, LANES)),
            small((1, LANES)), small((1, LANES)), small((1, 256)), small((1, 256)),
            small((4, CHUNK, CHUNK)), small((CHUNK, 256)),
        ],
        out_specs=pl.BlockSpec((tq, D), lambda b, t: (b * nq + t, 0)),
        out_shape=jax.ShapeDtypeStruct((n_batch * n_tok, D), BF16),
        scratch_shapes=[pltpu.VMEM((n_tok, LANES), BF16)] * 4 + [pltpu.VMEM((past, LANES), BF16)] * 4,
        compiler_params=_cparams(("parallel", "arbitrary")),
        name="lat_mixer",
    )(z, z, nakc, navc, gkc, gvc, bias, cos, sin, cos, sin, qg2, kg2, lng, lnb, ws, bsf)


def _split_bf16(x):
    hi = x.astype(BF16)
    return hi, (x - hi.astype(F32)).astype(BF16)


def _outproj_kernel(n_top, mixa_ref, mixb_ref, xa_ref, xb_ref, g1_ref, shift_ref, scale_ref, g_ref, w_ref, rwt_ref, rb_ref,
                    xn_ref, ht_ref, ti_ref, gt_ref):
    top = pl.program_id(0) < n_top
    x = jnp.where(top, xa_ref[...], xb_ref[...])
    x = x + g1_ref[0] * _dot(jnp.where(top, mixa_ref[...], mixb_ref[...]), w_ref[...])
    xn_ref[...] = x
    ms = jnp.mean(x * x, axis=-1, keepdims=True)
    h = x * lax.rsqrt(ms + EPS) * g_ref[...]
    h = h * (1.0 + scale_ref[0]) + shift_ref[0]
    h_hi, h_lo = _split_bf16(h)
    rw_hi, rw_lo = _split_bf16(rwt_ref[...])
    logits = (_dot_nt(rw_hi, h_hi) + _dot_nt(rw_hi, h_lo)) + _dot_nt(rw_lo, h_hi) + rb_ref[...]

    expert = lax.broadcasted_iota(jnp.int32, logits.shape, 0)
    out_row = lax.broadcasted_iota(jnp.int32, ti_ref.shape, 0)
    top_i = jnp.zeros(ti_ref.shape, jnp.int32)
    top_v = []
    for k in range(TOP_K):
        m = jnp.max(logits, axis=0, keepdims=True)
        idx = jnp.min(jnp.where(logits == m, expert, N_EXPERTS), axis=0, keepdims=True)
        logits = jnp.where(expert == idx, -jnp.inf, logits)
        top_i = jnp.where(out_row == k, idx, top_i)
        top_v.append(m)
    es = [jnp.exp(v - top_v[0]) for v in top_v]
    den = (es[0] + es[1]) + (es[2] + es[3])
    gates = jnp.zeros(gt_ref.shape, F32)
    for k in range(TOP_K):
        gates = jnp.where(out_row == k, es[k] / den, gates)
    ti_ref[...] = top_i
    gt_ref[...] = gates

    for c in range(ROW_TILE):
        ht_ref[pl.ds(c, h.shape[0], stride=ROW_TILE), :] = h[:, LANES * c:LANES * (c + 1)]


def _outproj(mix_a, mix_b, xa, xb, bot_off_rows, modr, g, w, rwt, rb, tp, ts_per_batch):
    tm = 512
    t = mix_a.shape[0] + mix_b.shape[0]
    n_top = tp // tm
    return pl.pallas_call(
        functools.partial(_outproj_kernel, n_top),
        grid=(t // tm,),
        in_specs=_x_specs(tm, n_top, 0) + _x_specs(tm, n_top, bot_off_rows // tm) + [
            _mod_spec(2, tm, tp, ts_per_batch),
            _mod_spec(3, tm, tp, ts_per_batch),
            _mod_spec(4, tm, tp, ts_per_batch),
            pl.BlockSpec((1, D), lambda i: (0, 0)),
            pl.BlockSpec((D, D), lambda i: (0, 0)),
            pl.BlockSpec((N_EXPERTS, D), lambda i: (0, 0)),
            pl.BlockSpec((N_EXPERTS, 1), lambda i: (0, 0)),
        ],
        out_specs=[
            pl.BlockSpec((tm, D), lambda i: (i, 0)),
            pl.BlockSpec((tm * ROW_TILE, LANES), lambda i: (i, 0)),
            pl.BlockSpec((8, tm), lambda i: (0, i)),
            pl.BlockSpec((8, tm), lambda i: (0, i)),
        ],
        out_shape=[
            jax.ShapeDtypeStruct((t, D), F32),
            jax.ShapeDtypeStruct((t * ROW_TILE, LANES), F32),
            jax.ShapeDtypeStruct((8, t), jnp.int32),
            jax.ShapeDtypeStruct((8, t), F32),
        ],
        compiler_params=_cparams(("parallel",)),
        name="outproj",
    )(mix_a, mix_b, xa, xb, modr, modr, modr, g, w, rwt, rb)


def _prep_wgu_kernel(w_ref, p_ref, wg_ref, wu_ref):
    for j in range(w_ref.shape[2] // 256):
        w = w_ref[0, :, 256 * j:256 * (j + 1)].astype(BF16)
        sel = _dot(w, p_ref[...])
        wg_ref[0, :, LANES * j:LANES * (j + 1)] = sel[:, :LANES].astype(BF16)
        wu_ref[0, :, LANES * j:LANES * (j + 1)] = sel[:, LANES:].astype(BF16)


def _prep_wgu(w):
    e, d, f2 = w.shape
    perm = np.zeros((256, 256), np.float32)
    perm[2 * np.arange(LANES), np.arange(LANES)] = 1.0
    perm[2 * np.arange(LANES) + 1, LANES + np.arange(LANES)] = 1.0
    out = jax.ShapeDtypeStruct((e, d, f2 // 2), BF16)
    return pl.pallas_call(
        _prep_wgu_kernel,
        grid=(e,),
        in_specs=[pl.BlockSpec((1, d, f2), lambda i: (i, 0, 0)), pl.BlockSpec((256, 256), lambda i: (0, 0))],
        out_specs=[pl.BlockSpec((1, d, f2 // 2), lambda i: (i, 0, 0))] * 2,
        out_shape=[out, out],
        compiler_params=_cparams(("parallel",)),
        name="prep_wgu",
    )(w, jnp.asarray(perm, BF16))


def _moe_kernel(blk_e_ref, row_tok_ref, row_dst_ref, n_live_ref, wctl_ref, ht_ref, wg_hbm, wu_hbm, wd_hbm,
                bg_ref, bu_ref, bd_ref, out_ref, xg_ref, ybuf_ref, wbuf_ref, wd32_ref, sem, wsem):
    i = pl.program_id(1)
    nb = pl.num_programs(1)
    step = pl.program_id(0) * nb + i
    n = pl.num_programs(0) * nb
    cur = step % 2
    nxt = 1 - cur
    ycur = step % Y_SLOTS
    yprev = (step + Y_SLOTS - 1) % Y_SLOTS
    blk = MOE_BLOCK * ROW_TILE

    def slot_copy(s):
        return pltpu.make_async_copy(ybuf_ref.at[pl.ds(s * blk, blk)], out_ref.at[pl.ds(0, blk)], sem.at[s])

    def gather_row(base, r, s):
        tok = row_tok_ref[base + r]
        xg_ref[pl.ds(pl.multiple_of(s * blk + r * ROW_TILE, ROW_TILE), ROW_TILE), :] = (
            ht_ref[pl.ds(pl.multiple_of(tok * ROW_TILE, ROW_TILE), ROW_TILE), :])

    def send_row(base, r, s, priority=0):
        dst = row_dst_ref[base + r]
        pltpu.make_async_copy(
            ybuf_ref.at[pl.ds(pl.multiple_of(s * blk + r * ROW_TILE, ROW_TILE), ROW_TILE)],
            out_ref.at[pl.ds(pl.multiple_of(dst * ROW_TILE, ROW_TILE), ROW_TILE)], sem.at[s]).start(priority)

    @pl.when(step == 0)
    def _():
        ybuf_ref[...] = jnp.zeros(ybuf_ref.shape, F32)
        for s in range(Y_SLOTS - 1):
            pltpu.make_async_copy(ybuf_ref.at[pl.ds(s * blk, blk)], out_ref.at[pl.ds((n + 1 + s) * blk, blk)],
                                  sem.at[s]).start()

    @pl.when(i == 0)
    def _():
        def first(r, carry):
            gather_row(step * MOE_BLOCK, r, cur)
            return carry

        lax.fori_loop(0, MOE_BLOCK, first, 0, unroll=8)

    wslot = wctl_ref[2 * n + step]

    def weight_copies(e, s):
        return [pltpu.make_async_copy(wg_hbm.at[e], wbuf_ref.at[s, 0], wsem.at[s, 0]),
                pltpu.make_async_copy(wu_hbm.at[e], wbuf_ref.at[s, 1], wsem.at[s, 1]),
                pltpu.make_async_copy(wd_hbm.at[e], wd32_ref, wsem.at[s, 2])]

    @pl.when(step == 0)
    def _():
        for cp in weight_copies(blk_e_ref[0], 0):
            cp.start()

    @pl.when(wctl_ref[step] == 1)
    def _():
        for cp in weight_copies(blk_e_ref[step], wslot):
            cp.wait()

        def cast_rows(j, carry):
            rows = pl.ds(pl.multiple_of(j * LANES, LANES), LANES)
            wbuf_ref[wslot, 2, rows, :] = wd32_ref[rows, :].astype(BF16)
            return carry

        lax.fori_loop(0, D // LANES, cast_rows, 0)
        next_e = wctl_ref[n + step]

        @pl.when(next_e >= 0)
        def _():
            for cp in weight_copies(next_e, 1 - wslot):
                cp.start()

    slot_copy(ycur).wait()
    live = i < n_live_ref[pl.program_id(0)]

    @pl.when(jnp.logical_not(live))
    def _():
        def prev(r, carry):
            send_row(step * MOE_BLOCK, r, yprev)
            return carry

        lax.fori_loop(0, MOE_BLOCK, prev, 0, unroll=8)

    @pl.when(live)
    def _():
        _moe_block(step, i, nb, cur, nxt, ycur, yprev, blk, gather_row, send_row, xg_ref, ybuf_ref,
                   wbuf_ref.at[wslot, 0], wbuf_ref.at[wslot, 1], wbuf_ref.at[wslot, 2], bg_ref, bu_ref, bd_ref)

    @pl.when(step == n - 1)
    def _():
        def last(r, carry):
            send_row(n * MOE_BLOCK, r, ycur)
            return carry

        lax.fori_loop(0, MOE_BLOCK, last, 0, unroll=8)
        for s in range(Y_SLOTS):
            slot_copy(s).wait()


def _moe_block(step, i, nb, cur, nxt, ycur, yprev, blk, gather_row, send_row,
               xg_ref, ybuf_ref, wg_ref, wu_ref, wd_ref, bg_ref, bu_ref, bd_ref):
    x = jnp.concatenate(
        [xg_ref[pl.ds(cur * blk + c, MOE_BLOCK, stride=ROW_TILE), :].astype(BF16) for c in range(ROW_TILE)], axis=1)
    next_blk = jnp.where(i + 1 < nb, step + 1, step)
    for r in range(MOE_BLOCK):
        gather_row(next_blk * MOE_BLOCK, r, nxt)
    for r in range(MOE_BLOCK):
        send_row(step * MOE_BLOCK, r, yprev, priority=r % 2)
    g = _dot(x, wg_ref[...]) + bg_ref[0]
    u = _dot(x, wu_ref[...]) + bu_ref[0]
    g = jnp.minimum(g, SWIGLU_LIMIT)
    u = jnp.clip(u, -SWIGLU_LIMIT, SWIGLU_LIMIT)
    act = (u + 1.0) * (g * jax.nn.sigmoid(SWIGLU_ALPHA * g))
    y = _dot(act.astype(BF16), wd_ref[...]) + bd_ref[0]
    for c in range(ROW_TILE):
        ybuf_ref[pl.ds(ycur * blk + c, MOE_BLOCK, stride=ROW_TILE), :] = y[:, LANES * c:LANES * (c + 1)]


def _weight_schedule(blk_e):
    n = blk_e.shape[0]
    start = jnp.concatenate([jnp.ones((1,), jnp.int32), (blk_e[1:] != blk_e[:-1]).astype(jnp.int32)])
    parity = (jnp.cumsum(start) - 1) % 2
    idx = jnp.arange(n, dtype=jnp.int32)
    later_start = (idx[None, :] > idx[:, None]) & (start[None, :] == 1)
    j_next = jnp.min(jnp.where(later_start, idx[None, :], n), axis=1)
    next_e = jnp.where(j_next < n, blk_e[jnp.minimum(j_next, n - 1)], -1)
    return jnp.concatenate([start, next_e, parity]).astype(jnp.int32)


def _moe(blk_e, row_tok, row_dst, n_live, ht, n_groups, wg, wu, wd, bg, bu, bd):
    nb = blk_e.shape[0] // n_groups
    tg = ht.shape[0] // ROW_TILE // n_groups
    ew = lambda g, i, be, rt, rd, nl, wc: (be[g * nb + i], 0, 0)
    hbm = pl.BlockSpec(memory_space=pl.ANY)
    return pl.pallas_call(
        _moe_kernel,
        grid_spec=pltpu.PrefetchScalarGridSpec(
            num_scalar_prefetch=5,
            grid=(n_groups, nb),
            in_specs=[
                pl.BlockSpec((tg * ROW_TILE, LANES), lambda g, i, be, rt, rd, nl, wc: (g, 0),
                             pipeline_mode=pl.Buffered(1)),
                hbm, hbm, hbm,
                pl.BlockSpec((1, 1, D), ew),
                pl.BlockSpec((1, 1, D), ew),
                pl.BlockSpec((1, 1, D), ew),
            ],
            out_specs=pl.BlockSpec(memory_space=pl.ANY),
            scratch_shapes=[
                pltpu.VMEM((2 * MOE_BLOCK * ROW_TILE, LANES), F32),
                pltpu.VMEM((Y_SLOTS * MOE_BLOCK * ROW_TILE, LANES), F32),
                pltpu.VMEM((2, 3, D, D), BF16),
                pltpu.VMEM((D, D), F32),
                pltpu.SemaphoreType.DMA((Y_SLOTS,)),
                pltpu.SemaphoreType.DMA((2, 3)),
            ],
        ),
        out_shape=jax.ShapeDtypeStruct(((n_groups * nb + Y_SLOTS) * MOE_BLOCK * ROW_TILE, LANES), F32),
        compiler_params=_cparams(("arbitrary", "arbitrary")),
        name="moe_experts",
    )(blk_e, row_tok, row_dst, n_live, _weight_schedule(blk_e), ht, wg, wu, wd, bg, bu, bd)


def _route(top_i, group, n_groups):
    tg = top_i.shape[0]
    n_assign = tg * TOP_K
    assert n_assign <= 1 << 16 and n_assign % MOE_BLOCK == 0
    flat_e = top_i.reshape(-1).astype(jnp.int32)
    experts = jnp.arange(N_EXPERTS, dtype=jnp.int32)
    counts = jnp.sum((flat_e[:, None] == experts[None, :]).astype(jnp.int32), axis=0)
    padded = (counts + MOE_BLOCK - 1) // MOE_BLOCK * MOE_BLOCK
    pad_end = jnp.cumsum(padded)
    n_blocks = n_assign // MOE_BLOCK + N_EXPERTS
    real_keys = (flat_e << 17) | jnp.arange(n_assign, dtype=jnp.int32)
    pad_id = experts[:, None] * MOE_BLOCK + jnp.arange(MOE_BLOCK, dtype=jnp.int32)[None, :]
    used = jnp.arange(MOE_BLOCK, dtype=jnp.int32)[None, :] < (padded - counts)[:, None]
    pad_keys = (jnp.where(used, experts[:, None], 63) << 17) | (1 << 16) | pad_id
    keys = jnp.sort(jnp.concatenate([real_keys, pad_keys.reshape(-1)]))
    is_pad = ((keys >> 16) & 1) == 1
    payload = keys & 0xFFFF
    tok = payload // TOP_K
    t_all = tg * n_groups
    row_tok = jnp.where(is_pad, 0, tok)
    row_dst = jnp.where(is_pad, TOP_K * t_all + group * (N_EXPERTS * MOE_BLOCK) + payload,
                        (payload % TOP_K) * t_all + group * tg + tok)
    blk_start = jnp.arange(n_blocks, dtype=jnp.int32) * MOE_BLOCK
    blk_e = jnp.sum((pad_end[None, :] <= blk_start[:, None]).astype(jnp.int32), axis=1)
    blk_e = jnp.minimum(blk_e, N_EXPERTS - 1)
    n_live = pad_end[-1] // MOE_BLOCK
    return blk_e, row_tok, row_dst, n_live


def _route_groups(top_i, n_groups):
    tg = top_i.shape[0] // n_groups
    parts = [_route(top_i[g * tg:(g + 1) * tg], g, n_groups) for g in range(n_groups)]
    blk_e = jnp.concatenate([p[0] for p in parts])
    row_tok = jnp.concatenate([p[1] for p in parts])
    spare = row_tok.shape[0] + jnp.arange(MOE_BLOCK, dtype=jnp.int32)
    row_dst = jnp.concatenate([spare] + [p[2] for p in parts])
    n_live = jnp.stack([p[3] for p in parts]).astype(jnp.int32)
    return blk_e, row_tok, row_dst, n_live


def _combine_kernel(final, y0_ref, y1_ref, y2_ref, y3_ref, gates_ref, x_ref, g2_ref, gf_ref, o_ref):
    g = gates_ref[...]
    tm = x_ref.shape[0]
    for c in range(ROW_TILE):
        rows = pl.ds(c, tm, stride=ROW_TILE)
        cols = slice(LANES * c, LANES * (c + 1))
        y = ((y0_ref[rows, :] * g[:, 0:1] + y1_ref[rows, :] * g[:, 1:2])
             + (y2_ref[rows, :] * g[:, 2:3] + y3_ref[rows, :] * g[:, 3:4]))
        o_ref[:, cols] = x_ref[:, cols] + g2_ref[0][:, cols] * y
    if final:
        x = o_ref[...]
        ms = jnp.mean(x * x, axis=-1, keepdims=True)
        o_ref[...] = x * lax.rsqrt(ms + EPS) * gf_ref[...]


def _combine(ys, gates, x, modr, tp, ts_per_batch, row0, n_rows, final_g=None):
    tm = 512
    nt_all = x.shape[0] // tm
    i0 = row0 // tm
    yspec = lambda k: pl.BlockSpec((tm * ROW_TILE, LANES), lambda i: (k * nt_all + i0 + i, 0))
    mod = pl.BlockSpec((1, 1, D), lambda i: (_mod_row(i0 + i, tm, tp, ts_per_batch) * 6 + 5, 0, 0))
    gf = jnp.ones((1, D), F32) if final_g is None else final_g
    return pl.pallas_call(
        functools.partial(_combine_kernel, final_g is not None),
        grid=(n_rows // tm,),
        in_specs=[yspec(0), yspec(1), yspec(2), yspec(3), pl.BlockSpec((tm, TOP_K), lambda i: (i0 + i, 0)),
                  pl.BlockSpec((tm, D), lambda i: (i0 + i, 0)), mod, pl.BlockSpec((1, D), lambda i: (0, 0))],
        out_specs=pl.BlockSpec((tm, D), lambda i: (i, 0)),
        out_shape=jax.ShapeDtypeStruct((n_rows, D), F32),
        compiler_params=_cparams(("parallel",)),
        name="moe_combine",
    )(ys, ys, ys, ys, gates, x, modr, gf)


def _na_bias_kernel(rows, t_ref, o_ref):
    kr = min(NA_WIN_R, rows)
    lo = _lane_lo()
    for r in range(rows):
        r0 = min(max(r - kr // 2, 0), rows - kr)
        for p in range(rows // 2):
            ok0 = r0 <= 2 * p < r0 + kr
            ok1 = r0 <= 2 * p + 1 < r0 + kr
            if ok0 or ok1:
                d = 2 * p - r + (NA_WIN_R - 1)
                assert 0 <= d < t_ref.shape[1]
                tile = t_ref[0, d]
                if not ok0:
                    tile = jnp.where(lo, NEG_INF, tile)
                if not ok1:
                    tile = jnp.where(lo, tile, NEG_INF)
            else:
                tile = jnp.full((GRID_W, LANES), NEG_INF, F32)
            o_ref[0, r * GRID_W:(r + 1) * GRID_W, p * LANES:(p + 1) * LANES] = tile


def _na_bias(rpb, rows):
    assert rows % 2 == 0 and 2 * GRID_W == LANES
    heads = rpb.shape[0]
    nd = 2 * NA_WIN_R
    c = np.arange(GRID_W)
    w0 = np.clip(c - NA_WIN_C // 2, 0, GRID_W - NA_WIN_C)
    col_ok = (c[None, :] >= w0[:, None]) & (c[None, :] < w0[:, None] + NA_WIN_C)
    dc = np.clip(c[None, :] - c[:, None] + (NA_WIN_C - 1), 0, 2 * NA_WIN_C - 2)
    oh_c = np.eye(2 * NA_WIN_C - 1, dtype=np.float32)[dc]
    vals = jnp.einsum('hde,qce->hdqc', rpb.astype(F32), oh_c, precision=HI)
    per_row = jnp.where(col_ok[None, None], vals, NEG_INF)
    per_row = jnp.pad(per_row, ((0, 0), (0, nd + 1 - per_row.shape[1]), (0, 0), (0, 0)), constant_values=NEG_INF)
    tables = jnp.concatenate([per_row[:, :nd], per_row[:, 1:nd + 1]], axis=-1)
    n = rows * GRID_W
    return pl.pallas_call(
        functools.partial(_na_bias_kernel, rows),
        grid=(heads,),
        in_specs=[pl.BlockSpec((1, nd, GRID_W, LANES), lambda h: (h, 0, 0, 0))],
        out_specs=pl.BlockSpec((1, n, n), lambda h: (h, 0, 0)),
        out_shape=jax.ShapeDtypeStruct((heads, n, n), F32),
        compiler_params=_cparams(("parallel",)),
        name="na_bias",
    )(tables)


def _rope_tables(n_tokens):
    t = np.arange(n_tokens)
    row = (t // GRID_W).astype(np.float32)
    col = (t % GRID_W).astype(np.float32)
    half = HEAD_DIM // 2
    inv = jnp.asarray(ROPE_THETA, F32) ** (-jnp.arange(0, half, 2, dtype=F32) / half)
    ang_r = jnp.asarray(row)[:, None] * inv
    ang_c = jnp.asarray(col)[:, None] * inv
    cr, sr, cc, sc = jnp.cos(ang_r), jnp.sin(ang_r), jnp.cos(ang_c), jnp.sin(ang_c)
    cos = jnp.concatenate([cr, cr, cc, cc] * 2, axis=-1)
    sin = jnp.concatenate([-sr, sr, -sc, sc] * 2, axis=-1)
    return cos, sin


def _to_heads(x, n_batch, seq, n_heads):
    return x.reshape(n_batch, seq, n_heads, HEAD_DIM).transpose(0, 2, 1, 3)


def _ctx_lanes(cache_l):
    b, h, p, dh = cache_l.shape
    return cache_l.transpose(0, 2, 1, 3).reshape(b, p, h * dh)


def kernel(x_prompt, x_sample, cache_na_k, cache_na_v, cache_gqa_k, cache_gqa_v, c, c_ctx, w_mod, b_mod, norm1_g, norm2_g, w_in, na_rpb, q_norm_g, k_norm_g, cm_ln_g, cm_ln_b, cm_ws, cm_bs, w_out, router_w, router_b, w_gate_up, b_gate_up, w_down, b_down, final_norm_g):
    nb, seq, _ = x_prompt.shape
    db, n_tok, _ = x_sample.shape
    past = cache_na_k.shape[3]
    tp, ts = nb * seq, db * n_tok
    assert db <= CTX_ROW and n_tok % GRID_W == 0
    assert tp == ts

    cond = jnp.zeros((MOD_ROWS, D), F32).at[:db].set(c).at[CTX_ROW].set(c_ctx)
    mod = _modulation(cond, w_mod, b_mod)
    cos, sin = _rope_tables(n_tok)
    xa, xb, xb_row0 = x_prompt.reshape(tp, D), x_sample.reshape(ts, D), 0

    w_in16 = w_in.astype(BF16)
    w_out16 = w_out.astype(BF16)
    wg16, wu16 = _prep_wgu(w_gate_up.reshape(DEPTH * N_EXPERTS, D, 2 * D))
    wd32 = w_down.reshape(DEPTH * N_EXPERTS, D, D)
    bg = b_gate_up[..., 0::2].reshape(DEPTH * N_EXPERTS, 1, D)
    bu = b_gate_up[..., 1::2].reshape(DEPTH * N_EXPERTS, 1, D)
    bd = b_down.reshape(DEPTH * N_EXPERTS, 1, D)

    na_k, na_v, gqa_k, gqa_v = [], [], [], []
    for l in range(DEPTH):
        modr = mod[l].reshape(MOD_ROWS * 6, 1, D)
        qg2 = jnp.tile(q_norm_g[l], 2).reshape(1, LANES)
        kg2 = jnp.tile(k_norm_g[l], 2).reshape(1, LANES)
        lng = cm_ln_g[l].reshape(1, 256)
        lnb = cm_ln_b[l].reshape(1, 256)
        bsf = jnp.repeat(cm_bs[l].T, HEAD_DIM, axis=1)

        z = _inproj(xa, xb, xb_row0, tp + ts, modr, norm1_g[l].reshape(1, D), w_in16[l], tp, n_tok)
        mix_p, kbn = _ctx_mixer(z, nb, seq, qg2, kg2, lng, lnb, cm_ws[l], bsf)
        mix_s = _lat_mixer(z, tp, db, n_tok, past,
                           _ctx_lanes(cache_na_k[:, l]), _ctx_lanes(cache_na_v[:, l]),
                           _ctx_lanes(cache_gqa_k[:, l]), _ctx_lanes(cache_gqa_v[:, l]),
                           _na_bias(na_rpb[l], n_tok // GRID_W), cos, sin, qg2, kg2, lng, lnb, cm_ws[l], bsf)
        na_k.append(_to_heads(z[:tp, KA:KA + 256], nb, seq, 4))
        na_v.append(_to_heads(z[:tp, VA:VA + 256], nb, seq, 4))
        gqa_k.append(_to_heads(kbn, nb, seq, 2))
        gqa_v.append(_to_heads(z[:tp, VB:VB + LANES], nb, seq, 2))

        xn, ht, top_i, gates = _outproj(mix_p, mix_s, xa, xb, xb_row0, modr, norm2_g[l].reshape(1, D), w_out16[l],
                                  router_w[l].T, router_b[l].reshape(N_EXPERTS, 1), tp, n_tok)

        blk_e, row_tok, row_dst, n_live = _route_groups(top_i[:TOP_K].T, 2)
        gates = gates[:TOP_K].T
        ys = _moe(blk_e + l * N_EXPERTS, row_tok, row_dst, n_live, ht, 2, wg16, wu16, wd32, bg, bu, bd)
        if l + 1 < DEPTH:
            x = _combine(ys, gates, xn, modr, tp, n_tok, 0, tp + ts)
            xa, xb, xb_row0 = x, x, tp

    gf = final_norm_g.reshape(1, D)
    y_p = _combine(ys, gates, xn, modr, tp, n_tok, 0, tp, gf)
    y_s = _combine(ys, gates, xn, modr, tp, n_tok, tp, ts, gf)
    return (y_p.reshape(nb, seq, D), y_s.reshape(db, n_tok, D),
            jnp.stack(na_k, axis=1), jnp.stack(na_v, axis=1), jnp.stack(gqa_k, axis=1), jnp.stack(gqa_v, axis=1))
```

```python
import functools

import jax
import jax.numpy as jnp
import numpy as np
from jax import lax
from jax.experimental import pallas as pl
from jax.experimental.pallas import tpu as pltpu

D = 1024
DEPTH = 2
GRID_W = 64
HEAD_DIM = 64
NA_WIN_R = 8
NA_WIN_C = 16
CHUNK = 128
ROPE_THETA = 10000.0
N_EXPERTS = 32
TOP_K = 4
SWIGLU_LIMIT = 7.0
SWIGLU_ALPHA = 1.702
EPS = 1e-6
MOE_BLOCK = 128
Y_SLOTS = 3
NEG_INF = -1e30
IN_WIDTH = 2048

QA, KA, VA, QB, KB, VB, CU, CV = 0, 256, 512, 768, 1280, 1408, 1536, 1792
OA, OB, OC = 0, 256, 768

LANES = 128
ROW_TILE = D // LANES
MOD_ROWS = 16
CTX_ROW = 8
VMEM_LIMIT = 56 * 1024 * 1024

F32 = jnp.float32
BF16 = jnp.bfloat16
HI = lax.Precision.HIGHEST


def _cparams(sem):
    return pltpu.CompilerParams(dimension_semantics=sem, vmem_limit_bytes=VMEM_LIMIT)


def _dot(a, b):
    return jnp.dot(a, b, preferred_element_type=F32)


def _dot_nt(a, b):
    return lax.dot_general(a, b, (((1,), (1,)), ((), ())), preferred_element_type=F32)


def _lane_lo():
    return lax.broadcasted_iota(jnp.int32, (1, LANES), 1) < HEAD_DIM


def _mod_kernel(c_ref, w_ref, b_ref, o_ref):
    c = c_ref[...]
    s = c * jax.nn.sigmoid(c)
    o_ref[0] = jnp.dot(s, w_ref[0], preferred_element_type=F32, precision=HI) + b_ref[0]


def _modulation(cond, w_mod, b_mod):
    tn = 1536
    n = w_mod.shape[-1]
    return pl.pallas_call(
        _mod_kernel,
        grid=(DEPTH, n // tn),
        in_specs=[
            pl.BlockSpec((MOD_ROWS, D), lambda l, j: (0, 0)),
            pl.BlockSpec((1, D, tn), lambda l, j: (l, 0, j)),
            pl.BlockSpec((1, 1, tn), lambda l, j: (l, 0, j)),
        ],
        out_specs=pl.BlockSpec((1, MOD_ROWS, tn), lambda l, j: (l, 0, j)),
        out_shape=jax.ShapeDtypeStruct((DEPTH, MOD_ROWS, n), F32),
        compiler_params=_cparams(("arbitrary", "arbitrary")),
        name="modulation",
    )(cond, w_mod, b_mod.reshape(DEPTH, 1, n))


def _mod_row(i, tm, tp, ts_per_batch):
    start = i * tm
    return jnp.where(start < tp, CTX_ROW, (start - tp) // ts_per_batch)


def _mod_spec(k, tm, tp, ts_per_batch):
    return pl.BlockSpec((1, 1, D), lambda i: (_mod_row(i, tm, tp, ts_per_batch) * 6 + k, 0, 0))


def _x_specs(tm, n_top, bot_off, width=D):
    return [pl.BlockSpec((tm, width), lambda i: (jnp.minimum(i, n_top - 1), 0)),
            pl.BlockSpec((tm, width), lambda i: (jnp.maximum(i - n_top, 0) + bot_off, 0))]


def _inproj_kernel(n_top, xa_ref, xb_ref, shift_ref, scale_ref, g_ref, w_ref, z_ref):
    x = jnp.where(pl.program_id(0) < n_top, xa_ref[...], xb_ref[...])
    ms = jnp.mean(x * x, axis=-1, keepdims=True)
    y = x * lax.rsqrt(ms + EPS) * g_ref[...]
    h = y * (1.0 + scale_ref[0]) + shift_ref[0]
    z_ref[...] = _dot(h.astype(BF16), w_ref[...])


def _inproj(xa, xb, bot_off_rows, t, modr, g, w, tp, ts_per_batch):
    tm = 512
    n_top = tp // tm
    return pl.pallas_call(
        functools.partial(_inproj_kernel, n_top),
        grid=(t // tm,),
        in_specs=_x_specs(tm, n_top, bot_off_rows // tm) + [
            _mod_spec(0, tm, tp, ts_per_batch),
            _mod_spec(1, tm, tp, ts_per_batch),
            pl.BlockSpec((1, D), lambda i: (0, 0)),
            pl.BlockSpec((D, IN_WIDTH), lambda i: (0, 0)),
        ],
        out_specs=pl.BlockSpec((tm, IN_WIDTH), lambda i: (i, 0)),
        out_shape=jax.ShapeDtypeStruct((t, IN_WIDTH), F32),
        compiler_params=_cparams(("parallel",)),
        name="inproj",
    )(xa, xb, modr, modr, g, w)


def _head_rms(x, g2):
    lo = _lane_lo()
    x2 = x * x
    s_lo = jnp.sum(jnp.where(lo, x2, 0.0), axis=-1, keepdims=True)
    s_hi = jnp.sum(jnp.where(lo, 0.0, x2), axis=-1, keepdims=True)
    ms = jnp.where(lo, s_lo, s_hi) * (1.0 / HEAD_DIM)
    return x * lax.rsqrt(ms + EPS) * g2


def _softmax_pv(q16, ks, vs, biases):
    ss = []
    for k, b in zip(ks, biases):
        s = _dot_nt(q16, k)
        if b is not None:
            s = s + b
        ss.append(s)
    m = ss[0].max(axis=-1, keepdims=True)
    for s in ss[1:]:
        m = jnp.maximum(m, s.max(axis=-1, keepdims=True))
    den = None
    acc = None
    for s, v in zip(ss, vs):
        e = jnp.exp(s - m)
        d = e.sum(axis=-1, keepdims=True)
        o = _dot(e.astype(BF16), v)
        den = d if den is None else den + d
        acc = o if acc is None else acc + o
    return acc / den


def _attend_pair(q, ks_lo, vs_lo, ks_hi, vs_hi, b_lo, b_hi):
    lo = _lane_lo()
    o_lo = _softmax_pv(jnp.where(lo, q, 0.0).astype(BF16), ks_lo, vs_lo, b_lo)
    o_hi = _softmax_pv(jnp.where(lo, 0.0, q).astype(BF16), ks_hi, vs_hi, b_hi)
    return jnp.where(lo, o_lo, o_hi)


def _gelu(x):
    c = np.sqrt(2.0 / np.pi).astype(np.float32)
    return x * (0.5 * (1.0 + jnp.tanh(c * (x + 0.044715 * (x * x * x)))))


def _layer_norm(x, g, b):
    mu = jnp.mean(x, axis=-1, keepdims=True)
    xc = x - mu
    var = jnp.mean(xc * xc, axis=-1, keepdims=True)
    return xc * lax.rsqrt(var + EPS) * g + b


def _chunk_mlp(cu, cv, lng_ref, lnb_ref, ws_ref, bsf_ref, out_ref, col0):
    lo = _lane_lo()
    u = _gelu(cu)
    v = _layer_norm(_gelu(cv), lng_ref[...], lnb_ref[...]).astype(BF16)
    s = cu.shape[0]
    for n in range(s // CHUNK):
        rows = slice(n * CHUNK, (n + 1) * CHUNK)
        for jb in range(2):
            cols = slice(jb * LANES, (jb + 1) * LANES)
            vb = v[rows, cols]
            m_lo = _dot(ws_ref[2 * jb].astype(BF16), vb)
            m_hi = _dot(ws_ref[2 * jb + 1].astype(BF16), vb)
            mixed = jnp.where(lo, m_lo, m_hi) + bsf_ref[:, cols]
            out_ref[rows, col0 + jb * LANES:col0 + (jb + 1) * LANES] = (u[rows, cols] * mixed).astype(out_ref.dtype)


def _store_heads(out_ref, first_head, x):
    out_ref[0, first_head] = x[:, :HEAD_DIM]
    out_ref[0, first_head + 1] = x[:, HEAD_DIM:]


def _ctx_mixer_kernel(z_ref, qg_ref, kg_ref, lng_ref, lnb_ref, ws_ref, bsf_ref,
                      mix_ref, nak_ref, nav_ref, gk_ref, gv_ref):
    scale = HEAD_DIM ** -0.5
    for j in range(2):
        q = z_ref[:, QA + LANES * j:QA + LANES * (j + 1)] * scale
        ka = z_ref[:, KA + LANES * j:KA + LANES * (j + 1)]
        va = z_ref[:, VA + LANES * j:VA + LANES * (j + 1)]
        _store_heads(nak_ref, 2 * j, ka)
        _store_heads(nav_ref, 2 * j, va)
        k = [ka.astype(BF16)]
        v = [va.astype(BF16)]
        o = _attend_pair(q, k, v, k, v, [None], [None])
        mix_ref[:, OA + LANES * j:OA + LANES * (j + 1)] = o.astype(mix_ref.dtype)

    kb = _head_rms(z_ref[:, KB:KB + LANES], kg_ref[...])
    vb = z_ref[:, VB:VB + LANES]
    _store_heads(gk_ref, 0, kb)
    _store_heads(gv_ref, 0, vb)
    k_same = [kb.astype(BF16)]
    k_swap = [pltpu.roll(kb, HEAD_DIM, 1).astype(BF16)]
    v_same = [vb.astype(BF16)]
    v_swap = [pltpu.roll(vb, HEAD_DIM, 1).astype(BF16)]
    for j in range(4):
        q = _head_rms(z_ref[:, QB + LANES * j:QB + LANES * (j + 1)], qg_ref[...]) * scale
        if j // 2 == 0:
            o = _attend_pair(q, k_same, v_same, k_swap, v_swap, [None], [None])
        else:
            o = _attend_pair(q, k_swap, v_swap, k_same, v_same, [None], [None])
        mix_ref[:, OB + LANES * j:OB + LANES * (j + 1)] = o.astype(mix_ref.dtype)

    _chunk_mlp(z_ref[:, CU:CU + 256], z_ref[:, CV:CV + 256], lng_ref, lnb_ref, ws_ref, bsf_ref, mix_ref, OC)


def _ctx_mixer(z, n_batch, seq, qg2, kg2, lng, lnb, ws, bsf):
    small = lambda shape: pl.BlockSpec(shape, lambda b: (0,) * len(shape))
    return pl.pallas_call(
        _ctx_mixer_kernel,
        grid=(n_batch,),
        in_specs=[
            pl.BlockSpec((seq, IN_WIDTH), lambda b: (b, 0)),
            small((1, LANES)), small((1, LANES)), small((1, 256)), small((1, 256)),
            small((4, CHUNK, CHUNK)), small((CHUNK, 256)),
        ],
        out_specs=[pl.BlockSpec((seq, D), lambda b: (b, 0))] + [
            pl.BlockSpec((1, h, seq, HEAD_DIM), lambda b: (b, 0, 0, 0)) for h in (4, 4, 2, 2)],
        out_shape=[jax.ShapeDtypeStruct((n_batch * seq, D), BF16)] + [
            jax.ShapeDtypeStruct((n_batch, h, seq, HEAD_DIM), F32) for h in (4, 4, 2, 2)],
        compiler_params=_cparams(("parallel",)),
        name="ctx_mixer",
    )(z, qg2, kg2, lng, lnb, ws, bsf)


def _rope(x, cos, sin):
    first = (lax.broadcasted_iota(jnp.int32, (1, LANES), 1) % 32) < 16
    partner = jnp.where(first, pltpu.roll(x, LANES - 16, 1), pltpu.roll(x, 16, 1))
    return x * cos + partner * sin


def _lat_mixer_kernel(zq_ref, zkv_ref, nakc_ref, navc_ref, gkc_ref, gvc_ref, bias_ref,
                      cosq_ref, sinq_ref, cosk_ref, sink_ref,
                      qg_ref, kg_ref, lng_ref, lnb_ref, ws_ref, bsf_ref,
                      mix_ref,
                      kl_ref, kls_ref, vl_ref, vls_ref, kc_ref, kcs_ref, vc_ref, vcs_ref):
    scale = HEAD_DIM ** -0.5

    @pl.when(pl.program_id(1) == 0)
    def _():
        kb = _rope(_head_rms(zkv_ref[:, KB:KB + LANES], kg_ref[...]), cosk_ref[...], sink_ref[...])
        vb = zkv_ref[:, VB:VB + LANES]
        kl_ref[...] = kb.astype(BF16)
        kls_ref[...] = pltpu.roll(kb, HEAD_DIM, 1).astype(BF16)
        vl_ref[...] = vb.astype(BF16)
        vls_ref[...] = pltpu.roll(vb, HEAD_DIM, 1).astype(BF16)
        kc = gkc_ref[0]
        vc = gvc_ref[0]
        kc_ref[...] = kc.astype(BF16)
        kcs_ref[...] = pltpu.roll(kc, HEAD_DIM, 1).astype(BF16)
        vc_ref[...] = vc.astype(BF16)
        vcs_ref[...] = pltpu.roll(vc, HEAD_DIM, 1).astype(BF16)

    for j in range(2):
        cols = slice(LANES * j, LANES * (j + 1))
        q = zq_ref[:, QA + LANES * j:QA + LANES * (j + 1)] * scale
        ks = [zkv_ref[:, KA + LANES * j:KA + LANES * (j + 1)].astype(BF16), nakc_ref[0, :, cols].astype(BF16)]
        vs = [zkv_ref[:, VA + LANES * j:VA + LANES * (j + 1)].astype(BF16), navc_ref[0, :, cols].astype(BF16)]
        o = _attend_pair(q, ks, vs, ks, vs, [bias_ref[2 * j], None], [bias_ref[2 * j + 1], None])
        mix_ref[:, OA + LANES * j:OA + LANES * (j + 1)] = o.astype(mix_ref.dtype)

    same = ([kl_ref[...], kc_ref[...]], [vl_ref[...], vc_ref[...]])
    swap = ([kls_ref[...], kcs_ref[...]], [vls_ref[...], vcs_ref[...]])
    for j in range(4):
        q = _head_rms(zq_ref[:, QB + LANES * j:QB + LANES * (j + 1)], qg_ref[...])
        q = _rope(q, cosq_ref[...], sinq_ref[...]) * scale
        lo_kv, hi_kv = (same, swap) if j // 2 == 0 else (swap, same)
        o = _attend_pair(q, lo_kv[0], lo_kv[1], hi_kv[0], hi_kv[1], [None, None], [None, None])
        mix_ref[:, OB + LANES * j:OB + LANES * (j + 1)] = o.astype(mix_ref.dtype)

    _chunk_mlp(zq_ref[:, CU:CU + 256], zq_ref[:, CV:CV + 256], lng_ref, lnb_ref, ws_ref, bsf_ref, mix_ref, OC)


def _lat_mixer(z, row0, n_batch, n_tok, past, nakc, navc, gkc, gvc, bias, cos, sin, qg2, kg2, lng, lnb, ws, bsf):
    tq = 256
    nq = n_tok // tq
    small = lambda shape: pl.BlockSpec(shape, lambda b, t: (0,) * len(shape))
    qblk0 = row0 // tq
    kvblk0 = row0 // n_tok
    return pl.pallas_call(
        _lat_mixer_kernel,
        grid=(n_batch, nq),
        in_specs=[
            pl.BlockSpec((tq, IN_WIDTH), lambda b, t: (qblk0 + b * nq + t, 0)),
            pl.BlockSpec((n_tok, IN_WIDTH), lambda b, t: (kvblk0 + b, 0)),
            pl.BlockSpec((1, past, 256), lambda b, t: (b, 0, 0)),
            pl.BlockSpec((1, past, 256), lambda b, t: (b, 0, 0)),
            pl.BlockSpec((1, past, LANES), lambda b, t: (b, 0, 0)),
            pl.BlockSpec((1, past, LANES), lambda b, t: (b, 0, 0)),
            pl.BlockSpec((4, tq, n_tok), lambda b, t: (0, t, 0)),
            pl.BlockSpec((tq, LANES), lambda b, t: (t, 0)),
            pl.BlockSpec((tq, LANES), lambda b, t: (t, 0)),
            small((n_tok, LANES)), small((n_tok, LANES)),
            small((1, LANES)), small((1, LANES)), small((1, 256)), small((1, 256)),
            small((4, CHUNK, CHUNK)), small((CHUNK, 256)),
        ],
        out_specs=pl.BlockSpec((tq, D), lambda b, t: (b * nq + t, 0)),
        out_shape=jax.ShapeDtypeStruct((n_batch * n_tok, D), BF16),
        scratch_shapes=[pltpu.VMEM((n_tok, LANES), BF16)] * 4 + [pltpu.VMEM((past, LANES), BF16)] * 4,
        compiler_params=_cparams(("parallel", "arbitrary")),
        name="lat_mixer",
    )(z, z, nakc, navc, gkc, gvc, bias, cos, sin, cos, sin, qg2, kg2, lng, lnb, ws, bsf)


def _split_bf16(x):
    hi = x.astype(BF16)
    return hi, (x - hi.astype(F32)).astype(BF16)


def _outproj_kernel(n_top, mixa_ref, mixb_ref, xa_ref, xb_ref, g1_ref, shift_ref, scale_ref, g_ref, w_ref, rwt_ref, rb_ref,
                    xn_ref, ht_ref, ti_ref, gt_ref):
    top = pl.program_id(0) < n_top
    x = jnp.where(top, xa_ref[...], xb_ref[...])
    x = x + g1_ref[0] * _dot(jnp.where(top, mixa_ref[...], mixb_ref[...]), w_ref[...])
    xn_ref[...] = x
    ms = jnp.mean(x * x, axis=-1, keepdims=True)
    h = x * lax.rsqrt(ms + EPS) * g_ref[...]
    h = h * (1.0 + scale_ref[0]) + shift_ref[0]
    h_hi, h_lo = _split_bf16(h)
    rw_hi, rw_lo = _split_bf16(rwt_ref[...])
    logits = (_dot_nt(rw_hi, h_hi) + _dot_nt(rw_hi, h_lo)) + _dot_nt(rw_lo, h_hi) + rb_ref[...]

    expert = lax.broadcasted_iota(jnp.int32, logits.shape, 0)
    out_row = lax.broadcasted_iota(jnp.int32, ti_ref.shape, 0)
    top_i = jnp.zeros(ti_ref.shape, jnp.int32)
    top_v = []
    for k in range(TOP_K):
        m = jnp.max(logits, axis=0, keepdims=True)
        idx = jnp.min(jnp.where(logits == m, expert, N_EXPERTS), axis=0, keepdims=True)
        logits = jnp.where(expert == idx, -jnp.inf, logits)
        top_i = jnp.where(out_row == k, idx, top_i)
        top_v.append(m)
    es = [jnp.exp(v - top_v[0]) for v in top_v]
    den = (es[0] + es[1]) + (es[2] + es[3])
    gates = jnp.zeros(gt_ref.shape, F32)
    for k in range(TOP_K):
        gates = jnp.where(out_row == k, es[k] / den, gates)
    ti_ref[...] = top_i
    gt_ref[...] = gates

    for c in range(ROW_TILE):
        ht_ref[pl.ds(c, h.shape[0], stride=ROW_TILE), :] = h[:, LANES * c:LANES * (c + 1)]


def _outproj(mix_a, mix_b, xa, xb, bot_off_rows, modr, g, w, rwt, rb, tp, ts_per_batch):
    tm = 512
    t = mix_a.shape[0] + mix_b.shape[0]
    n_top = tp // tm
    return pl.pallas_call(
        functools.partial(_outproj_kernel, n_top),
        grid=(t // tm,),
        in_specs=_x_specs(tm, n_top, 0) + _x_specs(tm, n_top, bot_off_rows // tm) + [
            _mod_spec(2, tm, tp, ts_per_batch),
            _mod_spec(3, tm, tp, ts_per_batch),
            _mod_spec(4, tm, tp, ts_per_batch),
            pl.BlockSpec((1, D), lambda i: (0, 0)),
            pl.BlockSpec((D, D), lambda i: (0, 0)),
            pl.BlockSpec((N_EXPERTS, D), lambda i: (0, 0)),
            pl.BlockSpec((N_EXPERTS, 1), lambda i: (0, 0)),
        ],
        out_specs=[
            pl.BlockSpec((tm, D), lambda i: (i, 0)),
            pl.BlockSpec((tm * ROW_TILE, LANES), lambda i: (i, 0)),
            pl.BlockSpec((8, tm), lambda i: (0, i)),
            pl.BlockSpec((8, tm), lambda i: (0, i)),
        ],
        out_shape=[
            jax.ShapeDtypeStruct((t, D), F32),
            jax.ShapeDtypeStruct((t * ROW_TILE, LANES), F32),
            jax.ShapeDtypeStruct((8, t), jnp.int32),
            jax.ShapeDtypeStruct((8, t), F32),
        ],
        compiler_params=_cparams(("parallel",)),
        name="outproj",
    )(mix_a, mix_b, xa, xb, modr, modr, modr, g, w, rwt, rb)


def _prep_wgu_kernel(w_ref, p_ref, wg_ref, wu_ref):
    for j in range(w_ref.shape[2] // 256):
        w = w_ref[0, :, 256 * j:256 * (j + 1)].astype(BF16)
        sel = _dot(w, p_ref[...])
        wg_ref[0, :, LANES * j:LANES * (j + 1)] = sel[:, :LANES].astype(BF16)
        wu_ref[0, :, LANES * j:LANES * (j + 1)] = sel[:, LANES:].astype(BF16)


def _prep_wgu(w):
    e, d, f2 = w.shape
    perm = np.zeros((256, 256), np.float32)
    perm[2 * np.arange(LANES), np.arange(LANES)] = 1.0
    perm[2 * np.arange(LANES) + 1, LANES + np.arange(LANES)] = 1.0
    out = jax.ShapeDtypeStruct((e, d, f2 // 2), BF16)
    return pl.pallas_call(
        _prep_wgu_kernel,
        grid=(e,),
        in_specs=[pl.BlockSpec((1, d, f2), lambda i: (i, 0, 0)), pl.BlockSpec((256, 256), lambda i: (0, 0))],
        out_specs=[pl.BlockSpec((1, d, f2 // 2), lambda i: (i, 0, 0))] * 2,
        out_shape=[out, out],
        compiler_params=_cparams(("parallel",)),
        name="prep_wgu",
    )(w, jnp.asarray(perm, BF16))


def _moe_kernel(blk_e_ref, row_tok_ref, row_dst_ref, n_live_ref, wctl_ref, ht_ref, wg_hbm, wu_hbm, wd_hbm,
                bg_ref, bu_ref, bd_ref, out_ref, xg_ref, ybuf_ref, wbuf_ref, wd32_ref, sem, wsem):
    i = pl.program_id(1)
    nb = pl.num_programs(1)
    step = pl.program_id(0) * nb + i
    n = pl.num_programs(0) * nb
    cur = step % 2
    nxt = 1 - cur
    ycur = step % Y_SLOTS
    yprev = (step + Y_SLOTS - 1) % Y_SLOTS
    blk = MOE_BLOCK * ROW_TILE

    def slot_copy(s):
        return pltpu.make_async_copy(ybuf_ref.at[pl.ds(s * blk, blk)], out_ref.at[pl.ds(0, blk)], sem.at[s])

    def gather_row(base, r, s):
        tok = row_tok_ref[base + r]
        xg_ref[pl.ds(pl.multiple_of(s * blk + r * ROW_TILE, ROW_TILE), ROW_TILE), :] = (
            ht_ref[pl.ds(pl.multiple_of(tok * ROW_TILE, ROW_TILE), ROW_TILE), :])

    def send_row(base, r, s, priority=0):
        dst = row_dst_ref[base + r]
        pltpu.make_async_copy(
            ybuf_ref.at[pl.ds(pl.multiple_of(s * blk + r * ROW_TILE, ROW_TILE), ROW_TILE)],
            out_ref.at[pl.ds(pl.multiple_of(dst * ROW_TILE, ROW_TILE), ROW_TILE)], sem.at[s]).start(priority)

    @pl.when(step == 0)
    def _():
        ybuf_ref[...] = jnp.zeros(ybuf_ref.shape, F32)
        for s in range(Y_SLOTS - 1):
            pltpu.make_async_copy(ybuf_ref.at[pl.ds(s * blk, blk)], out_ref.at[pl.ds((n + 1 + s) * blk, blk)],
                                  sem.at[s]).start()

    @pl.when(i == 0)
    def _():
        def first(r, carry):
            gather_row(step * MOE_BLOCK, r, cur)
            return carry

        lax.fori_loop(0, MOE_BLOCK, first, 0, unroll=8)

    wslot = wctl_ref[2 * n + step]

    def weight_copies(e, s):
        return [pltpu.make_async_copy(wg_hbm.at[e], wbuf_ref.at[s, 0], wsem.at[s, 0]),
                pltpu.make_async_copy(wu_hbm.at[e], wbuf_ref.at[s, 1], wsem.at[s, 1]),
                pltpu.make_async_copy(wd_hbm.at[e], wd32_ref, wsem.at[s, 2])]

    @pl.when(step == 0)
    def _():
        for cp in weight_copies(blk_e_ref[0], 0):
            cp.start()

    @pl.when(wctl_ref[step] == 1)
    def _():
        for cp in weight_copies(blk_e_ref[step], wslot):
            cp.wait()

        def cast_rows(j, carry):
            rows = pl.ds(pl.multiple_of(j * LANES, LANES), LANES)
            wbuf_ref[wslot, 2, rows, :] = wd32_ref[rows, :].astype(BF16)
            return carry

        lax.fori_loop(0, D // LANES, cast_rows, 0)
        next_e = wctl_ref[n + step]

        @pl.when(next_e >= 0)
        def _():
            for cp in weight_copies(next_e, 1 - wslot):
                cp.start()

    slot_copy(ycur).wait()
    live = i < n_live_ref[pl.program_id(0)]

    @pl.when(jnp.logical_not(live))
    def _():
        def prev(r, carry):
            send_row(step * MOE_BLOCK, r, yprev)
            return carry

        lax.fori_loop(0, MOE_BLOCK, prev, 0, unroll=8)

    @pl.when(live)
    def _():
        _moe_block(step, i, nb, cur, nxt, ycur, yprev, blk, gather_row, send_row, xg_ref, ybuf_ref,
                   wbuf_ref.at[wslot, 0], wbuf_ref.at[wslot, 1], wbuf_ref.at[wslot, 2], bg_ref, bu_ref, bd_ref)

    @pl.when(step == n - 1)
    def _():
        def last(r, carry):
            send_row(n * MOE_BLOCK, r, ycur)
            return carry

        lax.fori_loop(0, MOE_BLOCK, last, 0, unroll=8)
        for s in range(Y_SLOTS):
            slot_copy(s).wait()


def _moe_block(step, i, nb, cur, nxt, ycur, yprev, blk, gather_row, send_row,
               xg_ref, ybuf_ref, wg_ref, wu_ref, wd_ref, bg_ref, bu_ref, bd_ref):
    x = jnp.concatenate(
        [xg_ref[pl.ds(cur * blk + c, MOE_BLOCK, stride=ROW_TILE), :].astype(BF16) for c in range(ROW_TILE)], axis=1)
    next_blk = jnp.where(i + 1 < nb, step + 1, step)
    for r in range(MOE_BLOCK):
        gather_row(next_blk * MOE_BLOCK, r, nxt)
    for r in range(MOE_BLOCK):
        send_row(step * MOE_BLOCK, r, yprev, priority=r % 2)
    g = _dot(x, wg_ref[...]) + bg_ref[0]
    u = _dot(x, wu_ref[...]) + bu_ref[0]
    g = jnp.minimum(g, SWIGLU_LIMIT)
    u = jnp.clip(u, -SWIGLU_LIMIT, SWIGLU_LIMIT)
    act = (u + 1.0) * (g * jax.nn.sigmoid(SWIGLU_ALPHA * g))
    y = _dot(act.astype(BF16), wd_ref[...]) + bd_ref[0]
    for c in range(ROW_TILE):
        ybuf_ref[pl.ds(ycur * blk + c, MOE_BLOCK, stride=ROW_TILE), :] = y[:, LANES * c:LANES * (c + 1)]


def _weight_schedule(blk_e):
    n = blk_e.shape[0]
    start = jnp.concatenate([jnp.ones((1,), jnp.int32), (blk_e[1:] != blk_e[:-1]).astype(jnp.int32)])
    parity = (jnp.cumsum(start) - 1) % 2
    idx = jnp.arange(n, dtype=jnp.int32)
    later_start = (idx[None, :] > idx[:, None]) & (start[None, :] == 1)
    j_next = jnp.min(jnp.where(later_start, idx[None, :], n), axis=1)
    next_e = jnp.where(j_next < n, blk_e[jnp.minimum(j_next, n - 1)], -1)
    return jnp.concatenate([start, next_e, parity]).astype(jnp.int32)


def _moe(blk_e, row_tok, row_dst, n_live, ht, n_groups, wg, wu, wd, bg, bu, bd):
    nb = blk_e.shape[0] // n_groups
    tg = ht.shape[0] // ROW_TILE // n_groups
    ew = lambda g, i, be, rt, rd, nl, wc: (be[g * nb + i], 0, 0)
    hbm = pl.BlockSpec(memory_space=pl.ANY)
    return pl.pallas_call(
        _moe_kernel,
        grid_spec=pltpu.PrefetchScalarGridSpec(
            num_scalar_prefetch=5,
            grid=(n_groups, nb),
            in_specs=[
                pl.BlockSpec((tg * ROW_TILE, LANES), lambda g, i, be, rt, rd, nl, wc: (g, 0),
                             pipeline_mode=pl.Buffered(1)),
                hbm, hbm, hbm,
                pl.BlockSpec((1, 1, D), ew),
                pl.BlockSpec((1, 1, D), ew),
                pl.BlockSpec((1, 1, D), ew),
            ],
            out_specs=pl.BlockSpec(memory_space=pl.ANY),
            scratch_shapes=[
                pltpu.VMEM((2 * MOE_BLOCK * ROW_TILE, LANES), F32),
                pltpu.VMEM((Y_SLOTS * MOE_BLOCK * ROW_TILE, LANES), F32),
                pltpu.VMEM((2, 3, D, D), BF16),
                pltpu.VMEM((D, D), F32),
                pltpu.SemaphoreType.DMA((Y_SLOTS,)),
                pltpu.SemaphoreType.DMA((2, 3)),
            ],
        ),
        out_shape=jax.ShapeDtypeStruct(((n_groups * nb + Y_SLOTS) * MOE_BLOCK * ROW_TILE, LANES), F32),
        compiler_params=_cparams(("arbitrary", "arbitrary")),
        name="moe_experts",
    )(blk_e, row_tok, row_dst, n_live, _weight_schedule(blk_e), ht, wg, wu, wd, bg, bu, bd)


def _route(top_i, group, n_groups):
    tg = top_i.shape[0]
    n_assign = tg * TOP_K
    assert n_assign <= 1 << 16 and n_assign % MOE_BLOCK == 0
    flat_e = top_i.reshape(-1).astype(jnp.int32)
    experts = jnp.arange(N_EXPERTS, dtype=jnp.int32)
    counts = jnp.sum((flat_e[:, None] == experts[None, :]).astype(jnp.int32), axis=0)
    padded = (counts + MOE_BLOCK - 1) // MOE_BLOCK * MOE_BLOCK
    pad_end = jnp.cumsum(padded)
    n_blocks = n_assign // MOE_BLOCK + N_EXPERTS
    real_keys = (flat_e << 17) | jnp.arange(n_assign, dtype=jnp.int32)
    pad_id = experts[:, None] * MOE_BLOCK + jnp.arange(MOE_BLOCK, dtype=jnp.int32)[None, :]
    used = jnp.arange(MOE_BLOCK, dtype=jnp.int32)[None, :] < (padded - counts)[:, None]
    pad_keys = (jnp.where(used, experts[:, None], 63) << 17) | (1 << 16) | pad_id
    keys = jnp.sort(jnp.concatenate([real_keys, pad_keys.reshape(-1)]))
    is_pad = ((keys >> 16) & 1) == 1
    payload = keys & 0xFFFF
    tok = payload // TOP_K
    t_all = tg * n_groups
    row_tok = jnp.where(is_pad, 0, tok)
    row_dst = jnp.where(is_pad, TOP_K * t_all + group * (N_EXPERTS * MOE_BLOCK) + payload,
                        (payload % TOP_K) * t_all + group * tg + tok)
    blk_start = jnp.arange(n_blocks, dtype=jnp.int32) * MOE_BLOCK
    blk_e = jnp.sum((pad_end[None, :] <= blk_start[:, None]).astype(jnp.int32), axis=1)
    blk_e = jnp.minimum(blk_e, N_EXPERTS - 1)
    n_live = pad_end[-1] // MOE_BLOCK
    return blk_e, row_tok, row_dst, n_live


def _route_groups(top_i, n_groups):
    tg = top_i.shape[0] // n_groups
    parts = [_route(top_i[g * tg:(g + 1) * tg], g, n_groups) for g in range(n_groups)]
    blk_e = jnp.concatenate([p[0] for p in parts])
    row_tok = jnp.concatenate([p[1] for p in parts])
    spare = row_tok.shape[0] + jnp.arange(MOE_BLOCK, dtype=jnp.int32)
    row_dst = jnp.concatenate([spare] + [p[2] for p in parts])
    n_live = jnp.stack([p[3] for p in parts]).astype(jnp.int32)
    return blk_e, row_tok, row_dst, n_live


def _combine_kernel(final, y0_ref, y1_ref, y2_ref, y3_ref, gates_ref, x_ref, g2_ref, gf_ref, o_ref):
    g = gates_ref[...]
    tm = x_ref.shape[0]
    for c in range(ROW_TILE):
        rows = pl.ds(c, tm, stride=ROW_TILE)
        cols = slice(LANES * c, LANES * (c + 1))
        y = ((y0_ref[rows, :] * g[:, 0:1] + y1_ref[rows, :] * g[:, 1:2])
             + (y2_ref[rows, :] * g[:, 2:3] + y3_ref[rows, :] * g[:, 3:4]))
        o_ref[:, cols] = x_ref[:, cols] + g2_ref[0][:, cols] * y
    if final:
        x = o_ref[...]
        ms = jnp.mean(x * x, axis=-1, keepdims=True)
        o_ref[...] = x * lax.rsqrt(ms + EPS) * gf_ref[...]


def _combine(ys, gates, x, modr, tp, ts_per_batch, row0, n_rows, final_g=None):
    tm = 512
    nt_all = x.shape[0] // tm
    i0 = row0 // tm
    yspec = lambda k: pl.BlockSpec((tm * ROW_TILE, LANES), lambda i: (k * nt_all + i0 + i, 0))
    mod = pl.BlockSpec((1, 1, D), lambda i: (_mod_row(i0 + i, tm, tp, ts_per_batch) * 6 + 5, 0, 0))
    gf = jnp.ones((1, D), F32) if final_g is None else final_g
    return pl.pallas_call(
        functools.partial(_combine_kernel, final_g is not None),
        grid=(n_rows // tm,),
        in_specs=[yspec(0), yspec(1), yspec(2), yspec(3), pl.BlockSpec((tm, TOP_K), lambda i: (i0 + i, 0)),
                  pl.BlockSpec((tm, D), lambda i: (i0 + i, 0)), mod, pl.BlockSpec((1, D), lambda i: (0, 0))],
        out_specs=pl.BlockSpec((tm, D), lambda i: (i, 0)),
        out_shape=jax.ShapeDtypeStruct((n_rows, D), F32),
        compiler_params=_cparams(("parallel",)),
        name="moe_combine",
    )(ys, ys, ys, ys, gates, x, modr, gf)


def _na_bias_kernel(rows, t_ref, o_ref):
    kr = min(NA_WIN_R, rows)
    lo = _lane_lo()
    for r in range(rows):
        r0 = min(max(r - kr // 2, 0), rows - kr)
        for p in range(rows // 2):
            ok0 = r0 <= 2 * p < r0 + kr
            ok1 = r0 <= 2 * p + 1 < r0 + kr
            if ok0 or ok1:
                d = 2 * p - r + (NA_WIN_R - 1)
                assert 0 <= d < t_ref.shape[1]
                tile = t_ref[0, d]
                if not ok0:
                    tile = jnp.where(lo, NEG_INF, tile)
                if not ok1:
                    tile = jnp.where(lo, tile, NEG_INF)
            else:
                tile = jnp.full((GRID_W, LANES), NEG_INF, F32)
            o_ref[0, r * GRID_W:(r + 1) * GRID_W, p * LANES:(p + 1) * LANES] = tile


def _na_bias(rpb, rows):
    assert rows % 2 == 0 and 2 * GRID_W == LANES
    heads = rpb.shape[0]
    nd = 2 * NA_WIN_R
    c = np.arange(GRID_W)
    w0 = np.clip(c - NA_WIN_C // 2, 0, GRID_W - NA_WIN_C)
    col_ok = (c[None, :] >= w0[:, None]) & (c[None, :] < w0[:, None] + NA_WIN_C)
    dc = np.clip(c[None, :] - c[:, None] + (NA_WIN_C - 1), 0, 2 * NA_WIN_C - 2)
    oh_c = np.eye(2 * NA_WIN_C - 1, dtype=np.float32)[dc]
    vals = jnp.einsum('hde,qce->hdqc', rpb.astype(F32), oh_c, precision=HI)
    per_row = jnp.where(col_ok[None, None], vals, NEG_INF)
    per_row = jnp.pad(per_row, ((0, 0), (0, nd + 1 - per_row.shape[1]), (0, 0), (0, 0)), constant_values=NEG_INF)
    tables = jnp.concatenate([per_row[:, :nd], per_row[:, 1:nd + 1]], axis=-1)
    n = rows * GRID_W
    return pl.pallas_call(
        functools.partial(_na_bias_kernel, rows),
        grid=(heads,),
        in_specs=[pl.BlockSpec((1, nd, GRID_W, LANES), lambda h: (h, 0, 0, 0))],
        out_specs=pl.BlockSpec((1, n, n), lambda h: (h, 0, 0)),
        out_shape=jax.ShapeDtypeStruct((heads, n, n), F32),
        compiler_params=_cparams(("parallel",)),
        name="na_bias",
    )(tables)


def _rope_tables(n_tokens):
    t = np.arange(n_tokens)
    row = (t // GRID_W).astype(np.float32)
    col = (t % GRID_W).astype(np.float32)
    half = HEAD_DIM // 2
    inv = jnp.asarray(ROPE_THETA, F32) ** (-jnp.arange(0, half, 2, dtype=F32) / half)
    ang_r = jnp.asarray(row)[:, None] * inv
    ang_c = jnp.asarray(col)[:, None] * inv
    cr, sr, cc, sc = jnp.cos(ang_r), jnp.sin(ang_r), jnp.cos(ang_c), jnp.sin(ang_c)
    cos = jnp.concatenate([cr, cr, cc, cc] * 2, axis=-1)
    sin = jnp.concatenate([-sr, sr, -sc, sc] * 2, axis=-1)
    return cos, sin


def _ctx_lanes(cache_l):
    b, h, p, dh = cache_l.shape
    return cache_l.transpose(0, 2, 1, 3).reshape(b, p, h * dh)


def kernel(x_prompt, x_sample, cache_na_k, cache_na_v, cache_gqa_k, cache_gqa_v, c, c_ctx, w_mod, b_mod, norm1_g, norm2_g, w_in, na_rpb, q_norm_g, k_norm_g, cm_ln_g, cm_ln_b, cm_ws, cm_bs, w_out, router_w, router_b, w_gate_up, b_gate_up, w_down, b_down, final_norm_g):
    nb, seq, _ = x_prompt.shape
    db, n_tok, _ = x_sample.shape
    past = cache_na_k.shape[3]
    tp, ts = nb * seq, db * n_tok
    assert db <= CTX_ROW and n_tok % GRID_W == 0
    assert tp == ts

    cond = jnp.zeros((MOD_ROWS, D), F32).at[:db].set(c).at[CTX_ROW].set(c_ctx)
    mod = _modulation(cond, w_mod, b_mod)
    cos, sin = _rope_tables(n_tok)
    xa, xb, xb_row0 = x_prompt.reshape(tp, D), x_sample.reshape(ts, D), 0

    w_in16 = w_in.astype(BF16)
    w_out16 = w_out.astype(BF16)
    wg16, wu16 = _prep_wgu(w_gate_up.reshape(DEPTH * N_EXPERTS, D, 2 * D))
    wd32 = w_down.reshape(DEPTH * N_EXPERTS, D, D)
    bg = b_gate_up[..., 0::2].reshape(DEPTH * N_EXPERTS, 1, D)
    bu = b_gate_up[..., 1::2].reshape(DEPTH * N_EXPERTS, 1, D)
    bd = b_down.reshape(DEPTH * N_EXPERTS, 1, D)

    na_k, na_v, gqa_k, gqa_v = [], [], [], []
    for l in range(DEPTH):
        modr = mod[l].reshape(MOD_ROWS * 6, 1, D)
        qg2 = jnp.tile(q_norm_g[l], 2).reshape(1, LANES)
        kg2 = jnp.tile(k_norm_g[l], 2).reshape(1, LANES)
        lng = cm_ln_g[l].reshape(1, 256)
        lnb = cm_ln_b[l].reshape(1, 256)
        bsf = jnp.repeat(cm_bs[l].T, HEAD_DIM, axis=1)

        z = _inproj(xa, xb, xb_row0, tp + ts, modr, norm1_g[l].reshape(1, D), w_in16[l], tp, n_tok)
        mix_p, nak, nav, gk, gv = _ctx_mixer(z, nb, seq, qg2, kg2, lng, lnb, cm_ws[l], bsf)
        mix_s = _lat_mixer(z, tp, db, n_tok, past,
                           _ctx_lanes(cache_na_k[:, l]), _ctx_lanes(cache_na_v[:, l]),
                           _ctx_lanes(cache_gqa_k[:, l]), _ctx_lanes(cache_gqa_v[:, l]),
                           _na_bias(na_rpb[l], n_tok // GRID_W), cos, sin, qg2, kg2, lng, lnb, cm_ws[l], bsf)
        na_k.append(nak)
        na_v.append(nav)
        gqa_k.append(gk)
        gqa_v.append(gv)

        xn, ht, top_i, gates = _outproj(mix_p, mix_s, xa, xb, xb_row0, modr, norm2_g[l].reshape(1, D), w_out16[l],
                                  router_w[l].T, router_b[l].reshape(N_EXPERTS, 1), tp, n_tok)

        blk_e, row_tok, row_dst, n_live = _route_groups(top_i[:TOP_K].T, 2)
        gates = gates[:TOP_K].T
        ys = _moe(blk_e + l * N_EXPERTS, row_tok, row_dst, n_live, ht, 2, wg16, wu16, wd32, bg, bu, bd)
        if l + 1 < DEPTH:
            x = _combine(ys, gates, xn, modr, tp, n_tok, 0, tp + ts)
            xa, xb, xb_row0 = x, x, tp

    gf = final_norm_g.reshape(1, D)
    y_p = _combine(ys, gates, xn, modr, tp, n_tok, 0, tp, gf)
    y_s = _combine(ys, gates, xn, modr, tp, n_tok, tp, ts, gf)
    return (y_p.reshape(nb, seq, D), y_s.reshape(db, n_tok, D),
            jnp.stack(na_k, axis=1), jnp.stack(na_v, axis=1), jnp.stack(gqa_k, axis=1), jnp.stack(gqa_v, axis=1))
```

```python
import functools

import jax
import jax.numpy as jnp
import numpy as np
from jax import lax
from jax.experimental import pallas as pl
from jax.experimental.pallas import tpu as pltpu

D = 1024
DEPTH = 2
GRID_W = 64
HEAD_DIM = 64
NA_WIN_R = 8
NA_WIN_C = 16
CHUNK = 128
ROPE_THETA = 10000.0
N_EXPERTS = 32
TOP_K = 4
SWIGLU_LIMIT = 7.0
SWIGLU_ALPHA = 1.702
EPS = 1e-6
MOE_BLOCK = 128
Y_SLOTS = 3
NEG_INF = -1e30
IN_WIDTH = 2048

QA, KA, VA, QB, KB, VB, CU, CV = 0, 256, 512, 768, 1280, 1408, 1536, 1792
OA, OB, OC = 0, 256, 768

LANES = 128
ROW_TILE = D // LANES
MOD_ROWS = 16
CTX_ROW = 8
VMEM_LIMIT = 56 * 1024 * 1024

F32 = jnp.float32
BF16 = jnp.bfloat16
HI = lax.Precision.HIGHEST


def _cparams(sem):
    return pltpu.CompilerParams(dimension_semantics=sem, vmem_limit_bytes=VMEM_LIMIT)


def _dot(a, b):
    return jnp.dot(a, b, preferred_element_type=F32)


def _dot_nt(a, b):
    return lax.dot_general(a, b, (((1,), (1,)), ((), ())), preferred_element_type=F32)


def _lane_lo():
    return lax.broadcasted_iota(jnp.int32, (1, LANES), 1) < HEAD_DIM


def _mod_kernel(c_ref, w_ref, b_ref, o_ref):
    c = c_ref[...]
    s = c * jax.nn.sigmoid(c)
    o_ref[0] = jnp.dot(s, w_ref[0], preferred_element_type=F32, precision=HI) + b_ref[0]


def _modulation(cond, w_mod, b_mod):
    tn = 1536
    n = w_mod.shape[-1]
    return pl.pallas_call(
        _mod_kernel,
        grid=(DEPTH, n // tn),
        in_specs=[
            pl.BlockSpec((MOD_ROWS, D), lambda l, j: (0, 0)),
            pl.BlockSpec((1, D, tn), lambda l, j: (l, 0, j)),
            pl.BlockSpec((1, 1, tn), lambda l, j: (l, 0, j)),
        ],
        out_specs=pl.BlockSpec((1, MOD_ROWS, tn), lambda l, j: (l, 0, j)),
        out_shape=jax.ShapeDtypeStruct((DEPTH, MOD_ROWS, n), F32),
        compiler_params=_cparams(("arbitrary", "arbitrary")),
        name="modulation",
    )(cond, w_mod, b_mod.reshape(DEPTH, 1, n))


def _mod_row(i, tm, tp, ts_per_batch):
    start = i * tm
    return jnp.where(start < tp, CTX_ROW, (start - tp) // ts_per_batch)


def _mod_spec(k, tm, tp, ts_per_batch):
    return pl.BlockSpec((1, 1, D), lambda i: (_mod_row(i, tm, tp, ts_per_batch) * 6 + k, 0, 0))


def _x_specs(tm, n_top, bot_off, width=D):
    return [pl.BlockSpec((tm, width), lambda i: (jnp.minimum(i, n_top - 1), 0)),
            pl.BlockSpec((tm, width), lambda i: (jnp.maximum(i - n_top, 0) + bot_off, 0))]


def _inproj_kernel(n_top, xa_ref, xb_ref, shift_ref, scale_ref, g_ref, w_ref, z_ref):
    x = jnp.where(pl.program_id(0) < n_top, xa_ref[...], xb_ref[...])
    ms = jnp.mean(x * x, axis=-1, keepdims=True)
    y = x * lax.rsqrt(ms + EPS) * g_ref[...]
    h = y * (1.0 + scale_ref[0]) + shift_ref[0]
    z_ref[...] = _dot(h.astype(BF16), w_ref[...])


def _inproj(xa, xb, bot_off_rows, t, modr, g, w, tp, ts_per_batch):
    tm = 512
    n_top = tp // tm
    return pl.pallas_call(
        functools.partial(_inproj_kernel, n_top),
        grid=(t // tm,),
        in_specs=_x_specs(tm, n_top, bot_off_rows // tm) + [
            _mod_spec(0, tm, tp, ts_per_batch),
            _mod_spec(1, tm, tp, ts_per_batch),
            pl.BlockSpec((1, D), lambda i: (0, 0)),
            pl.BlockSpec((D, IN_WIDTH), lambda i: (0, 0)),
        ],
        out_specs=pl.BlockSpec((tm, IN_WIDTH), lambda i: (i, 0)),
        out_shape=jax.ShapeDtypeStruct((t, IN_WIDTH), F32),
        compiler_params=_cparams(("parallel",)),
        name="inproj",
    )(xa, xb, modr, modr, g, w)


def _head_rms(x, g2):
    lo = _lane_lo()
    x2 = x * x
    s_lo = jnp.sum(jnp.where(lo, x2, 0.0), axis=-1, keepdims=True)
    s_hi = jnp.sum(jnp.where(lo, 0.0, x2), axis=-1, keepdims=True)
    ms = jnp.where(lo, s_lo, s_hi) * (1.0 / HEAD_DIM)
    return x * lax.rsqrt(ms + EPS) * g2


def _softmax_pv(q16, ks, vs, biases):
    ss = []
    for k, b in zip(ks, biases):
        s = _dot_nt(q16, k)
        if b is not None:
            s = s + b
        ss.append(s)
    m = ss[0].max(axis=-1, keepdims=True)
    for s in ss[1:]:
        m = jnp.maximum(m, s.max(axis=-1, keepdims=True))
    den = None
    acc = None
    for s, v in zip(ss, vs):
        e = jnp.exp(s - m)
        d = e.sum(axis=-1, keepdims=True)
        o = _dot(e.astype(BF16), v)
        den = d if den is None else den + d
        acc = o if acc is None else acc + o
    return acc / den


def _attend_pair(q, ks_lo, vs_lo, ks_hi, vs_hi, b_lo, b_hi):
    lo = _lane_lo()
    o_lo = _softmax_pv(jnp.where(lo, q, 0.0).astype(BF16), ks_lo, vs_lo, b_lo)
    o_hi = _softmax_pv(jnp.where(lo, 0.0, q).astype(BF16), ks_hi, vs_hi, b_hi)
    return jnp.where(lo, o_lo, o_hi)


def _gelu(x):
    c = np.sqrt(2.0 / np.pi).astype(np.float32)
    return x * (0.5 * (1.0 + jnp.tanh(c * (x + 0.044715 * (x * x * x)))))


def _layer_norm(x, g, b):
    mu = jnp.mean(x, axis=-1, keepdims=True)
    xc = x - mu
    var = jnp.mean(xc * xc, axis=-1, keepdims=True)
    return xc * lax.rsqrt(var + EPS) * g + b


def _chunk_mlp(cu, cv, lng_ref, lnb_ref, ws_ref, bsf_ref, out_ref, col0):
    lo = _lane_lo()
    u = _gelu(cu)
    v = _layer_norm(_gelu(cv), lng_ref[...], lnb_ref[...]).astype(BF16)
    s = cu.shape[0]
    for n in range(s // CHUNK):
        rows = slice(n * CHUNK, (n + 1) * CHUNK)
        for jb in range(2):
            cols = slice(jb * LANES, (jb + 1) * LANES)
            vb = v[rows, cols]
            m_lo = _dot(ws_ref[2 * jb].astype(BF16), vb)
            m_hi = _dot(ws_ref[2 * jb + 1].astype(BF16), vb)
            mixed = jnp.where(lo, m_lo, m_hi) + bsf_ref[:, cols]
            out_ref[rows, col0 + jb * LANES:col0 + (jb + 1) * LANES] = (u[rows, cols] * mixed).astype(out_ref.dtype)


def _store_heads(out_ref, first_head, x):
    out_ref[0, first_head] = x[:, :HEAD_DIM]
    out_ref[0, first_head + 1] = x[:, HEAD_DIM:]


def _ctx_mixer_kernel(z_ref, qg_ref, kg_ref, lng_ref, lnb_ref, ws_ref, bsf_ref,
                      mix_ref, nak_ref, nav_ref, gk_ref, gv_ref):
    scale = HEAD_DIM ** -0.5
    for j in range(2):
        q = z_ref[:, QA + LANES * j:QA + LANES * (j + 1)] * scale
        ka = z_ref[:, KA + LANES * j:KA + LANES * (j + 1)]
        va = z_ref[:, VA + LANES * j:VA + LANES * (j + 1)]
        _store_heads(nak_ref, 2 * j, ka)
        _store_heads(nav_ref, 2 * j, va)
        k = [ka.astype(BF16)]
        v = [va.astype(BF16)]
        o = _attend_pair(q, k, v, k, v, [None], [None])
        mix_ref[:, OA + LANES * j:OA + LANES * (j + 1)] = o.astype(mix_ref.dtype)

    kb = _head_rms(z_ref[:, KB:KB + LANES], kg_ref[...])
    vb = z_ref[:, VB:VB + LANES]
    _store_heads(gk_ref, 0, kb)
    _store_heads(gv_ref, 0, vb)
    k_same = [kb.astype(BF16)]
    k_swap = [pltpu.roll(kb, HEAD_DIM, 1).astype(BF16)]
    v_same = [vb.astype(BF16)]
    v_swap = [pltpu.roll(vb, HEAD_DIM, 1).astype(BF16)]
    for j in range(4):
        q = _head_rms(z_ref[:, QB + LANES * j:QB + LANES * (j + 1)], qg_ref[...]) * scale
        if j // 2 == 0:
            o = _attend_pair(q, k_same, v_same, k_swap, v_swap, [None], [None])
        else:
            o = _attend_pair(q, k_swap, v_swap, k_same, v_same, [None], [None])
        mix_ref[:, OB + LANES * j:OB + LANES * (j + 1)] = o.astype(mix_ref.dtype)

    _chunk_mlp(z_ref[:, CU:CU + 256], z_ref[:, CV:CV + 256], lng_ref, lnb_ref, ws_ref, bsf_ref, mix_ref, OC)


def _ctx_mixer(z, n_batch, seq, qg2, kg2, lng, lnb, ws, bsf):
    small = lambda shape: pl.BlockSpec(shape, lambda b: (0,) * len(shape))
    return pl.pallas_call(
        _ctx_mixer_kernel,
        grid=(n_batch,),
        in_specs=[
            pl.BlockSpec((seq, IN_WIDTH), lambda b: (b, 0)),
            small((1, LANES)), small((1, LANES)), small((1, 256)), small((1, 256)),
            small((4, CHUNK, CHUNK)), small((CHUNK, 256)),
        ],
        out_specs=[pl.BlockSpec((seq, D), lambda b: (b, 0))] + [
            pl.BlockSpec((1, h, seq, HEAD_DIM), lambda b: (b, 0, 0, 0)) for h in (4, 4, 2, 2)],
        out_shape=[jax.ShapeDtypeStruct((n_batch * seq, D), BF16)] + [
            jax.ShapeDtypeStruct((n_batch, h, seq, HEAD_DIM), F32) for h in (4, 4, 2, 2)],
        compiler_params=_cparams(("parallel",)),
        name="ctx_mixer",
    )(z, qg2, kg2, lng, lnb, ws, bsf)


def _rope(x, cos, sin):
    first = (lax.broadcasted_iota(jnp.int32, (1, LANES), 1) % 32) < 16
    partner = jnp.where(first, pltpu.roll(x, LANES - 16, 1), pltpu.roll(x, 16, 1))
    return x * cos + partner * sin


def _lat_mixer_kernel(zq_ref, zkv_ref, nakc_ref, navc_ref, gkc_ref, gvc_ref, bias_ref,
                      cosq_ref, sinq_ref, cosk_ref, sink_ref,
                      qg_ref, kg_ref, lng_ref, lnb_ref, ws_ref, bsf_ref,
                      mix_ref,
                      kl_ref, kls_ref, vl_ref, vls_ref, kc_ref, kcs_ref, vc_ref, vcs_ref):
    scale = HEAD_DIM ** -0.5

    @pl.when(pl.program_id(1) == 0)
    def _():
        kb = _rope(_head_rms(zkv_ref[:, KB:KB + LANES], kg_ref[...]), cosk_ref[...], sink_ref[...])
        vb = zkv_ref[:, VB:VB + LANES]
        kl_ref[...] = kb.astype(BF16)
        kls_ref[...] = pltpu.roll(kb, HEAD_DIM, 1).astype(BF16)
        vl_ref[...] = vb.astype(BF16)
        vls_ref[...] = pltpu.roll(vb, HEAD_DIM, 1).astype(BF16)
        kc = gkc_ref[0]
        vc = gvc_ref[0]
        kc_ref[...] = kc.astype(BF16)
        kcs_ref[...] = pltpu.roll(kc, HEAD_DIM, 1).astype(BF16)
        vc_ref[...] = vc.astype(BF16)
        vcs_ref[...] = pltpu.roll(vc, HEAD_DIM, 1).astype(BF16)

    for j in range(2):
        cols = slice(LANES * j, LANES * (j + 1))
        q = zq_ref[:, QA + LANES * j:QA + LANES * (j + 1)] * scale
        ks = [zkv_ref[:, KA + LANES * j:KA + LANES * (j + 1)].astype(BF16), nakc_ref[0, :, cols].astype(BF16)]
        vs = [zkv_ref[:, VA + LANES * j:VA + LANES * (j + 1)].astype(BF16), navc_ref[0, :, cols].astype(BF16)]
        o = _attend_pair(q, ks, vs, ks, vs, [bias_ref[2 * j], None], [bias_ref[2 * j + 1], None])
        mix_ref[:, OA + LANES * j:OA + LANES * (j + 1)] = o.astype(mix_ref.dtype)

    same = ([kl_ref[...], kc_ref[...]], [vl_ref[...], vc_ref[...]])
    swap = ([kls_ref[...], kcs_ref[...]], [vls_ref[...], vcs_ref[...]])
    for j in range(4):
        q = _head_rms(zq_ref[:, QB + LANES * j:QB + LANES * (j + 1)], qg_ref[...])
        q = _rope(q, cosq_ref[...], sinq_ref[...]) * scale
        lo_kv, hi_kv = (same, swap) if j // 2 == 0 else (swap, same)
        o = _attend_pair(q, lo_kv[0], lo_kv[1], hi_kv[0], hi_kv[1], [None, None], [None, None])
        mix_ref[:, OB + LANES * j:OB + LANES * (j + 1)] = o.astype(mix_ref.dtype)

    _chunk_mlp(zq_ref[:, CU:CU + 256], zq_ref[:, CV:CV + 256], lng_ref, lnb_ref, ws_ref, bsf_ref, mix_ref, OC)


def _lat_mixer(z, row0, n_batch, n_tok, past, nakc, navc, gkc, gvc, bias, cos, sin, qg2, kg2, lng, lnb, ws, bsf):
    tq = 256
    nq = n_tok // tq
    small = lambda shape: pl.BlockSpec(shape, lambda b, t: (0,) * len(shape))
    qblk0 = row0 // tq
    kvblk0 = row0 // n_tok
    return pl.pallas_call(
        _lat_mixer_kernel,
        grid=(n_batch, nq),
        in_specs=[
            pl.BlockSpec((tq, IN_WIDTH), lambda b, t: (qblk0 + b * nq + t, 0)),
            pl.BlockSpec((n_tok, IN_WIDTH), lambda b, t: (kvblk0 + b, 0)),
            pl.BlockSpec((1, past, 256), lambda b, t: (b, 0, 0)),
            pl.BlockSpec((1, past, 256), lambda b, t: (b, 0, 0)),
            pl.BlockSpec((1, past, LANES), lambda b, t: (b, 0, 0)),
            pl.BlockSpec((1, past, LANES), lambda b, t: (b, 0, 0)),
            pl.BlockSpec((4, tq, n_tok), lambda b, t: (0, t, 0)),
            pl.BlockSpec((tq, LANES), lambda b, t: (t, 0)),
            pl.BlockSpec((tq, LANES), lambda b, t: (t, 0)),
            small((n_tok, LANES)), small((n_tok, LANES)),
            small((1, LANES)), small((1, LANES)), small((1, 256)), small((1, 256)),
            small((4, CHUNK, CHUNK)), small((CHUNK, 256)),
        ],
        out_specs=pl.BlockSpec((tq, D), lambda b, t: (b * nq + t, 0)),
        out_shape=jax.ShapeDtypeStruct((n_batch * n_tok, D), BF16),
        scratch_shapes=[pltpu.VMEM((n_tok, LANES), BF16)] * 4 + [pltpu.VMEM((past, LANES), BF16)] * 4,
        compiler_params=_cparams(("parallel", "arbitrary")),
        name="lat_mixer",
    )(z, z, nakc, navc, gkc, gvc, bias, cos, sin, cos, sin, qg2, kg2, lng, lnb, ws, bsf)


def _split_bf16(x):
    hi = x.astype(BF16)
    return hi, (x - hi.astype(F32)).astype(BF16)


def _outproj_kernel(n_top, mixa_ref, mixb_ref, xa_ref, xb_ref, g1_ref, shift_ref, scale_ref, g_ref, w_ref, rwt_ref, rb_ref,
                    xn_ref, ht_ref, ti_ref, gt_ref):
    top = pl.program_id(0) < n_top
    x = jnp.where(top, xa_ref[...], xb_ref[...])
    x = x + g1_ref[0] * _dot(jnp.where(top, mixa_ref[...], mixb_ref[...]), w_ref[...])
    xn_ref[...] = x
    ms = jnp.mean(x * x, axis=-1, keepdims=True)
    h = x * lax.rsqrt(ms + EPS) * g_ref[...]
    h = h * (1.0 + scale_ref[0]) + shift_ref[0]
    h_hi, h_lo = _split_bf16(h)
    rw_hi, rw_lo = _split_bf16(rwt_ref[...])
    logits = (_dot_nt(rw_hi, h_hi) + _dot_nt(rw_hi, h_lo)) + _dot_nt(rw_lo, h_hi) + rb_ref[...]

    expert = lax.broadcasted_iota(jnp.int32, logits.shape, 0)
    out_row = lax.broadcasted_iota(jnp.int32, ti_ref.shape, 0)
    top_i = jnp.zeros(ti_ref.shape, jnp.int32)
    top_v = []
    for k in range(TOP_K):
        m = jnp.max(logits, axis=0, keepdims=True)
        idx = jnp.min(jnp.where(logits == m, expert, N_EXPERTS), axis=0, keepdims=True)
        logits = jnp.where(expert == idx, -jnp.inf, logits)
        top_i = jnp.where(out_row == k, idx, top_i)
        top_v.append(m)
    es = [jnp.exp(v - top_v[0]) for v in top_v]
    den = (es[0] + es[1]) + (es[2] + es[3])
    gates = jnp.zeros(gt_ref.shape, F32)
    for k in range(TOP_K):
        gates = jnp.where(out_row == k, es[k] / den, gates)
    ti_ref[...] = top_i
    gt_ref[...] = gates

    for c in range(ROW_TILE):
        ht_ref[pl.ds(c, h.shape[0], stride=ROW_TILE), :] = h[:, LANES * c:LANES * (c + 1)]


def _outproj(mix_a, mix_b, xa, xb, bot_off_rows, modr, g, w, rwt, rb, tp, ts_per_batch):
    tm = 512
    t = mix_a.shape[0] + mix_b.shape[0]
    n_top = tp // tm
    return pl.pallas_call(
        functools.partial(_outproj_kernel, n_top),
        grid=(t // tm,),
        in_specs=_x_specs(tm, n_top, 0) + _x_specs(tm, n_top, bot_off_rows // tm) + [
            _mod_spec(2, tm, tp, ts_per_batch),
            _mod_spec(3, tm, tp, ts_per_batch),
            _mod_spec(4, tm, tp, ts_per_batch),
            pl.BlockSpec((1, D), lambda i: (0, 0)),
            pl.BlockSpec((D, D), lambda i: (0, 0)),
            pl.BlockSpec((N_EXPERTS, D), lambda i: (0, 0)),
            pl.BlockSpec((N_EXPERTS, 1), lambda i: (0, 0)),
        ],
        out_specs=[
            pl.BlockSpec((tm, D), lambda i: (i, 0)),
            pl.BlockSpec((tm * ROW_TILE, LANES), lambda i: (i, 0)),
            pl.BlockSpec((8, tm), lambda i: (0, i)),
            pl.BlockSpec((8, tm), lambda i: (0, i)),
        ],
        out_shape=[
            jax.ShapeDtypeStruct((t, D), F32),
            jax.ShapeDtypeStruct((t * ROW_TILE, LANES), F32),
            jax.ShapeDtypeStruct((8, t), jnp.int32),
            jax.ShapeDtypeStruct((8, t), F32),
        ],
        compiler_params=_cparams(("parallel",)),
        name="outproj",
    )(mix_a, mix_b, xa, xb, modr, modr, modr, g, w, rwt, rb)


def _prep_wgu_kernel(w_ref, p_ref, wg_ref, wu_ref):
    for j in range(w_ref.shape[2] // 256):
        w = w_ref[0, :, 256 * j:256 * (j + 1)].astype(BF16)
        sel = _dot(w, p_ref[...])
        wg_ref[0, :, LANES * j:LANES * (j + 1)] = sel[:, :LANES].astype(BF16)
        wu_ref[0, :, LANES * j:LANES * (j + 1)] = sel[:, LANES:].astype(BF16)


def _prep_wgu(w):
    e, d, f2 = w.shape
    perm = np.zeros((256, 256), np.float32)
    perm[2 * np.arange(LANES), np.arange(LANES)] = 1.0
    perm[2 * np.arange(LANES) + 1, LANES + np.arange(LANES)] = 1.0
    out = jax.ShapeDtypeStruct((e, d, f2 // 2), BF16)
    return pl.pallas_call(
        _prep_wgu_kernel,
        grid=(e,),
        in_specs=[pl.BlockSpec((1, d, f2), lambda i: (i, 0, 0)), pl.BlockSpec((256, 256), lambda i: (0, 0))],
        out_specs=[pl.BlockSpec((1, d, f2 // 2), lambda i: (i, 0, 0))] * 2,
        out_shape=[out, out],
        compiler_params=_cparams(("parallel",)),
        name="prep_wgu",
    )(w, jnp.asarray(perm, BF16))


def _moe_kernel(blk_e_ref, row_tok_ref, row_dst_ref, n_live_ref, wctl_ref, ht_ref, wg_hbm, wu_hbm, wd_hbm,
                bg_ref, bu_ref, bd_ref, out_ref, xg_ref, ybuf_ref, wbuf_ref, wd32_ref, sem, wsem):
    i = pl.program_id(1)
    nb = pl.num_programs(1)
    step = pl.program_id(0) * nb + i
    n = pl.num_programs(0) * nb
    cur = step % 2
    nxt = 1 - cur
    ycur = step % Y_SLOTS
    yprev = (step + Y_SLOTS - 1) % Y_SLOTS
    blk = MOE_BLOCK * ROW_TILE

    def slot_copy(s):
        return pltpu.make_async_copy(ybuf_ref.at[pl.ds(s * blk, blk)], out_ref.at[pl.ds(0, blk)], sem.at[s])

    def gather_row(base, r, s):
        tok = row_tok_ref[base + r]
        xg_ref[pl.ds(pl.multiple_of(s * blk + r * ROW_TILE, ROW_TILE), ROW_TILE), :] = (
            ht_ref[pl.ds(pl.multiple_of(tok * ROW_TILE, ROW_TILE), ROW_TILE), :])

    def send_row(base, r, s, priority=0):
        dst = row_dst_ref[base + r]
        pltpu.make_async_copy(
            ybuf_ref.at[pl.ds(pl.multiple_of(s * blk + r * ROW_TILE, ROW_TILE), ROW_TILE)],
            out_ref.at[pl.ds(pl.multiple_of(dst * ROW_TILE, ROW_TILE), ROW_TILE)], sem.at[s]).start(priority)

    @pl.when(step == 0)
    def _():
        ybuf_ref[...] = jnp.zeros(ybuf_ref.shape, F32)
        for s in range(Y_SLOTS - 1):
            pltpu.make_async_copy(ybuf_ref.at[pl.ds(s * blk, blk)], out_ref.at[pl.ds((n + 1 + s) * blk, blk)],
                                  sem.at[s]).start()

    @pl.when(i == 0)
    def _():
        def first(r, carry):
            gather_row(step * MOE_BLOCK, r, cur)
            return carry

        lax.fori_loop(0, MOE_BLOCK, first, 0, unroll=8)

    wslot = wctl_ref[2 * n + step]

    def weight_copies(e, s):
        return [pltpu.make_async_copy(wg_hbm.at[e], wbuf_ref.at[s, 0], wsem.at[s, 0]),
                pltpu.make_async_copy(wu_hbm.at[e], wbuf_ref.at[s, 1], wsem.at[s, 1]),
                pltpu.make_async_copy(wd_hbm.at[e], wd32_ref, wsem.at[s, 2])]

    @pl.when(step == 0)
    def _():
        for cp in weight_copies(blk_e_ref[0], 0):
            cp.start()

    @pl.when(wctl_ref[step] == 1)
    def _():
        for cp in weight_copies(blk_e_ref[step], wslot):
            cp.wait()

        def cast_rows(j, carry):
            rows = pl.ds(pl.multiple_of(j * LANES, LANES), LANES)
            wbuf_ref[wslot, 2, rows, :] = wd32_ref[rows, :].astype(BF16)
            return carry

        lax.fori_loop(0, D // LANES, cast_rows, 0)
        next_e = wctl_ref[n + step]

        @pl.when(next_e >= 0)
        def _():
            for cp in weight_copies(next_e, 1 - wslot):
                cp.start()

    slot_copy(ycur).wait()
    live = i < n_live_ref[pl.program_id(0)]

    @pl.when(jnp.logical_not(live))
    def _():
        def prev(r, carry):
            send_row(step * MOE_BLOCK, r, yprev)
            return carry

        lax.fori_loop(0, MOE_BLOCK, prev, 0, unroll=8)

    @pl.when(live)
    def _():
        _moe_block(step, i, nb, cur, nxt, ycur, yprev, blk, gather_row, send_row, xg_ref, ybuf_ref,
                   wbuf_ref.at[wslot, 0], wbuf_ref.at[wslot, 1], wbuf_ref.at[wslot, 2], bg_ref, bu_ref, bd_ref)

    @pl.when(step == n - 1)
    def _():
        def last(r, carry):
            send_row(n * MOE_BLOCK, r, ycur)
            return carry

        lax.fori_loop(0, MOE_BLOCK, last, 0, unroll=8)
        for s in range(Y_SLOTS):
            slot_copy(s).wait()


def _moe_block(step, i, nb, cur, nxt, ycur, yprev, blk, gather_row, send_row,
               xg_ref, ybuf_ref, wg_ref, wu_ref, wd_ref, bg_ref, bu_ref, bd_ref):
    x = jnp.concatenate(
        [xg_ref[pl.ds(cur * blk + c, MOE_BLOCK, stride=ROW_TILE), :].astype(BF16) for c in range(ROW_TILE)], axis=1)
    next_blk = jnp.where(i + 1 < nb, step + 1, step)
    for r in range(MOE_BLOCK):
        gather_row(next_blk * MOE_BLOCK, r, nxt)
    for r in range(MOE_BLOCK):
        send_row(step * MOE_BLOCK, r, yprev, priority=r % 2)
    g = _dot(x, wg_ref[...]) + bg_ref[0]
    u = _dot(x, wu_ref[...]) + bu_ref[0]
    g = jnp.minimum(g, SWIGLU_LIMIT)
    u = jnp.clip(u, -SWIGLU_LIMIT, SWIGLU_LIMIT)
    act = (u + 1.0) * (g * jax.nn.sigmoid(SWIGLU_ALPHA * g))
    y = _dot(act.astype(BF16), wd_ref[...]) + bd_ref[0]
    for c in range(ROW_TILE):
        ybuf_ref[pl.ds(ycur * blk + c, MOE_BLOCK, stride=ROW_TILE), :] = y[:, LANES * c:LANES * (c + 1)]


def _weight_schedule(blk_e):
    n = blk_e.shape[0]
    start = jnp.concatenate([jnp.ones((1,), jnp.int32), (blk_e[1:] != blk_e[:-1]).astype(jnp.int32)])
    parity = (jnp.cumsum(start) - 1) % 2
    idx = jnp.arange(n, dtype=jnp.int32)
    later_start = (idx[None, :] > idx[:, None]) & (start[None, :] == 1)
    j_next = jnp.min(jnp.where(later_start, idx[None, :], n), axis=1)
    next_e = jnp.where(j_next < n, blk_e[jnp.minimum(j_next, n - 1)], -1)
    return jnp.concatenate([start, next_e, parity]).astype(jnp.int32)


def _moe(blk_e, row_tok, row_dst, n_live, ht, n_groups, wg, wu, wd, bg, bu, bd):
    nb = blk_e.shape[0] // n_groups
    tg = ht.shape[0] // ROW_TILE // n_groups
    ew = lambda g, i, be, rt, rd, nl, wc: (be[g * nb + i], 0, 0)
    hbm = pl.BlockSpec(memory_space=pl.ANY)
    return pl.pallas_call(
        _moe_kernel,
        grid_spec=pltpu.PrefetchScalarGridSpec(
            num_scalar_prefetch=5,
            grid=(n_groups, nb),
            in_specs=[
                pl.BlockSpec((tg * ROW_TILE, LANES), lambda g, i, be, rt, rd, nl, wc: (g, 0),
                             pipeline_mode=pl.Buffered(1)),
                hbm, hbm, hbm,
                pl.BlockSpec((1, 1, D), ew),
                pl.BlockSpec((1, 1, D), ew),
                pl.BlockSpec((1, 1, D), ew),
            ],
            out_specs=pl.BlockSpec(memory_space=pl.ANY),
            scratch_shapes=[
                pltpu.VMEM((2 * MOE_BLOCK * ROW_TILE, LANES), F32),
                pltpu.VMEM((Y_SLOTS * MOE_BLOCK * ROW_TILE, LANES), F32),
                pltpu.VMEM((2, 3, D, D), BF16),
                pltpu.VMEM((D, D), F32),
                pltpu.SemaphoreType.DMA((Y_SLOTS,)),
                pltpu.SemaphoreType.DMA((2, 3)),
            ],
        ),
        out_shape=jax.ShapeDtypeStruct(((n_groups * nb + Y_SLOTS) * MOE_BLOCK * ROW_TILE, LANES), F32),
        compiler_params=_cparams(("arbitrary", "arbitrary")),
        name="moe_experts",
    )(blk_e, row_tok, row_dst, n_live, _weight_schedule(blk_e), ht, wg, wu, wd, bg, bu, bd)


GROUP_BIT = 23


def _route_keys(top_i):
    tg = top_i.shape[0]
    n_assign = tg * TOP_K
    assert n_assign <= 1 << 16 and n_assign % MOE_BLOCK == 0
    flat_e = top_i.reshape(-1).astype(jnp.int32)
    experts = jnp.arange(N_EXPERTS, dtype=jnp.int32)
    counts = jnp.sum((flat_e[:, None] == experts[None, :]).astype(jnp.int32), axis=0)
    padded = (counts + MOE_BLOCK - 1) // MOE_BLOCK * MOE_BLOCK
    pad_end = jnp.cumsum(padded)
    n_blocks = n_assign // MOE_BLOCK + N_EXPERTS
    real_keys = (flat_e << 17) | jnp.arange(n_assign, dtype=jnp.int32)
    pad_id = experts[:, None] * MOE_BLOCK + jnp.arange(MOE_BLOCK, dtype=jnp.int32)[None, :]
    used = jnp.arange(MOE_BLOCK, dtype=jnp.int32)[None, :] < (padded - counts)[:, None]
    pad_keys = (jnp.where(used, experts[:, None], 63) << 17) | (1 << 16) | pad_id
    blk_start = jnp.arange(n_blocks, dtype=jnp.int32) * MOE_BLOCK
    blk_e = jnp.sum((pad_end[None, :] <= blk_start[:, None]).astype(jnp.int32), axis=1)
    blk_e = jnp.minimum(blk_e, N_EXPERTS - 1)
    return jnp.concatenate([real_keys, pad_keys.reshape(-1)]), blk_e, pad_end[-1] // MOE_BLOCK


def _route_rows(keys, group, tg, n_groups):
    is_pad = ((keys >> 16) & 1) == 1
    payload = keys & 0xFFFF
    tok = payload // TOP_K
    t_all = tg * n_groups
    row_tok = jnp.where(is_pad, 0, tok)
    row_dst = jnp.where(is_pad, TOP_K * t_all + group * (N_EXPERTS * MOE_BLOCK) + payload,
                        (payload % TOP_K) * t_all + group * tg + tok)
    return row_tok, row_dst


def _route_groups(top_i, n_groups):
    tg = top_i.shape[0] // n_groups
    parts = [_route_keys(top_i[g * tg:(g + 1) * tg]) for g in range(n_groups)]
    rows_g = parts[0][0].shape[0]
    keys = jnp.sort(jnp.concatenate([p[0] | (g << GROUP_BIT) for g, p in enumerate(parts)]))
    keys = keys & ((1 << GROUP_BIT) - 1)
    rows = [_route_rows(keys[g * rows_g:(g + 1) * rows_g], g, tg, n_groups) for g in range(n_groups)]
    blk_e = jnp.concatenate([p[1] for p in parts])
    row_tok = jnp.concatenate([r[0] for r in rows])
    spare = row_tok.shape[0] + jnp.arange(MOE_BLOCK, dtype=jnp.int32)
    row_dst = jnp.concatenate([spare] + [r[1] for r in rows])
    n_live = jnp.stack([p[2] for p in parts]).astype(jnp.int32)
    return blk_e, row_tok, row_dst, n_live


def _combine_kernel(final, y0_ref, y1_ref, y2_ref, y3_ref, gates_ref, x_ref, g2_ref, gf_ref, o_ref):
    g = gates_ref[...]
    tm = x_ref.shape[0]
    for c in range(ROW_TILE):
        rows = pl.ds(c, tm, stride=ROW_TILE)
        cols = slice(LANES * c, LANES * (c + 1))
        y = ((y0_ref[rows, :] * g[:, 0:1] + y1_ref[rows, :] * g[:, 1:2])
             + (y2_ref[rows, :] * g[:, 2:3] + y3_ref[rows, :] * g[:, 3:4]))
        o_ref[:, cols] = x_ref[:, cols] + g2_ref[0][:, cols] * y
    if final:
        x = o_ref[...]
        ms = jnp.mean(x * x, axis=-1, keepdims=True)
        o_ref[...] = x * lax.rsqrt(ms + EPS) * gf_ref[...]


def _combine(ys, gates, x, modr, tp, ts_per_batch, row0, n_rows, final_g=None):
    tm = 512
    nt_all = x.shape[0] // tm
    i0 = row0 // tm
    yspec = lambda k: pl.BlockSpec((tm * ROW_TILE, LANES), lambda i: (k * nt_all + i0 + i, 0))
    mod = pl.BlockSpec((1, 1, D), lambda i: (_mod_row(i0 + i, tm, tp, ts_per_batch) * 6 + 5, 0, 0))
    gf = jnp.ones((1, D), F32) if final_g is None else final_g
    return pl.pallas_call(
        functools.partial(_combine_kernel, final_g is not None),
        grid=(n_rows // tm,),
        in_specs=[yspec(0), yspec(1), yspec(2), yspec(3), pl.BlockSpec((tm, TOP_K), lambda i: (i0 + i, 0)),
                  pl.BlockSpec((tm, D), lambda i: (i0 + i, 0)), mod, pl.BlockSpec((1, D), lambda i: (0, 0))],
        out_specs=pl.BlockSpec((tm, D), lambda i: (i, 0)),
        out_shape=jax.ShapeDtypeStruct((n_rows, D), F32),
        compiler_params=_cparams(("parallel",)),
        name="moe_combine",
    )(ys, ys, ys, ys, gates, x, modr, gf)


def _na_bias_kernel(rows, t_ref, o_ref):
    kr = min(NA_WIN_R, rows)
    lo = _lane_lo()
    for r in range(rows):
        r0 = min(max(r - kr // 2, 0), rows - kr)
        for p in range(rows // 2):
            ok0 = r0 <= 2 * p < r0 + kr
            ok1 = r0 <= 2 * p + 1 < r0 + kr
            if ok0 or ok1:
                d = 2 * p - r + (NA_WIN_R - 1)
                assert 0 <= d < t_ref.shape[1]
                tile = t_ref[0, d]
                if not ok0:
                    tile = jnp.where(lo, NEG_INF, tile)
                if not ok1:
                    tile = jnp.where(lo, tile, NEG_INF)
            else:
                tile = jnp.full((GRID_W, LANES), NEG_INF, F32)
            o_ref[0, r * GRID_W:(r + 1) * GRID_W, p * LANES:(p + 1) * LANES] = tile


def _na_bias(rpb, rows):
    assert rows % 2 == 0 and 2 * GRID_W == LANES
    heads = rpb.shape[0]
    nd = 2 * NA_WIN_R
    c = np.arange(GRID_W)
    w0 = np.clip(c - NA_WIN_C // 2, 0, GRID_W - NA_WIN_C)
    col_ok = (c[None, :] >= w0[:, None]) & (c[None, :] < w0[:, None] + NA_WIN_C)
    dc = np.clip(c[None, :] - c[:, None] + (NA_WIN_C - 1), 0, 2 * NA_WIN_C - 2)
    oh_c = np.eye(2 * NA_WIN_C - 1, dtype=np.float32)[dc]
    vals = jnp.einsum('hde,qce->hdqc', rpb.astype(F32), oh_c, precision=HI)
    per_row = jnp.where(col_ok[None, None], vals, NEG_INF)
    per_row = jnp.pad(per_row, ((0, 0), (0, nd + 1 - per_row.shape[1]), (0, 0), (0, 0)), constant_values=NEG_INF)
    tables = jnp.concatenate([per_row[:, :nd], per_row[:, 1:nd + 1]], axis=-1)
    n = rows * GRID_W
    return pl.pallas_call(
        functools.partial(_na_bias_kernel, rows),
        grid=(heads,),
        in_specs=[pl.BlockSpec((1, nd, GRID_W, LANES), lambda h: (h, 0, 0, 0))],
        out_specs=pl.BlockSpec((1, n, n), lambda h: (h, 0, 0)),
        out_shape=jax.ShapeDtypeStruct((heads, n, n), F32),
        compiler_params=_cparams(("parallel",)),
        name="na_bias",
    )(tables)


def _rope_tables(n_tokens):
    t = np.arange(n_tokens)
    row = (t // GRID_W).astype(np.float32)
    col = (t % GRID_W).astype(np.float32)
    half = HEAD_DIM // 2
    inv = jnp.asarray(ROPE_THETA, F32) ** (-jnp.arange(0, half, 2, dtype=F32) / half)
    ang_r = jnp.asarray(row)[:, None] * inv
    ang_c = jnp.asarray(col)[:, None] * inv
    cr, sr, cc, sc = jnp.cos(ang_r), jnp.sin(ang_r), jnp.cos(ang_c), jnp.sin(ang_c)
    cos = jnp.concatenate([cr, cr, cc, cc] * 2, axis=-1)
    sin = jnp.concatenate([-sr, sr, -sc, sc] * 2, axis=-1)
    return cos, sin


def _ctx_lanes(cache_l):
    b, h, p, dh = cache_l.shape
    return cache_l.transpose(0, 2, 1, 3).reshape(b, p, h * dh)


def kernel(x_prompt, x_sample, cache_na_k, cache_na_v, cache_gqa_k, cache_gqa_v, c, c_ctx, w_mod, b_mod, norm1_g, norm2_g, w_in, na_rpb, q_norm_g, k_norm_g, cm_ln_g, cm_ln_b, cm_ws, cm_bs, w_out, router_w, router_b, w_gate_up, b_gate_up, w_down, b_down, final_norm_g):
    nb, seq, _ = x_prompt.shape
    db, n_tok, _ = x_sample.shape
    past = cache_na_k.shape[3]
    tp, ts = nb * seq, db * n_tok
    assert db <= CTX_ROW and n_tok % GRID_W == 0
    assert tp == ts

    cond = jnp.zeros((MOD_ROWS, D), F32).at[:db].set(c).at[CTX_ROW].set(c_ctx)
    mod = _modulation(cond, w_mod, b_mod)
    cos, sin = _rope_tables(n_tok)
    xa, xb, xb_row0 = x_prompt.reshape(tp, D), x_sample.reshape(ts, D), 0

    w_in16 = w_in.astype(BF16)
    w_out16 = w_out.astype(BF16)
    wg16, wu16 = _prep_wgu(w_gate_up.reshape(DEPTH * N_EXPERTS, D, 2 * D))
    wd32 = w_down.reshape(DEPTH * N_EXPERTS, D, D)
    bg = b_gate_up[..., 0::2].reshape(DEPTH * N_EXPERTS, 1, D)
    bu = b_gate_up[..., 1::2].reshape(DEPTH * N_EXPERTS, 1, D)
    bd = b_down.reshape(DEPTH * N_EXPERTS, 1, D)

    na_k, na_v, gqa_k, gqa_v = [], [], [], []
    for l in range(DEPTH):
        modr = mod[l].reshape(MOD_ROWS * 6, 1, D)
        qg2 = jnp.tile(q_norm_g[l], 2).reshape(1, LANES)
        kg2 = jnp.tile(k_norm_g[l], 2).reshape(1, LANES)
        lng = cm_ln_g[l].reshape(1, 256)
        lnb = cm_ln_b[l].reshape(1, 256)
        bsf = jnp.repeat(cm_bs[l].T, HEAD_DIM, axis=1)

        z = _inproj(xa, xb, xb_row0, tp + ts, modr, norm1_g[l].reshape(1, D), w_in16[l], tp, n_tok)
        mix_p, nak, nav, gk, gv = _ctx_mixer(z, nb, seq, qg2, kg2, lng, lnb, cm_ws[l], bsf)
        mix_s = _lat_mixer(z, tp, db, n_tok, past,
                           _ctx_lanes(cache_na_k[:, l]), _ctx_lanes(cache_na_v[:, l]),
                           _ctx_lanes(cache_gqa_k[:, l]), _ctx_lanes(cache_gqa_v[:, l]),
                           _na_bias(na_rpb[l], n_tok // GRID_W), cos, sin, qg2, kg2, lng, lnb, cm_ws[l], bsf)
        na_k.append(nak)
        na_v.append(nav)
        gqa_k.append(gk)
        gqa_v.append(gv)

        xn, ht, top_i, gates = _outproj(mix_p, mix_s, xa, xb, xb_row0, modr, norm2_g[l].reshape(1, D), w_out16[l],
                                  router_w[l].T, router_b[l].reshape(N_EXPERTS, 1), tp, n_tok)

        blk_e, row_tok, row_dst, n_live = _route_groups(top_i[:TOP_K].T, 2)
        gates = gates[:TOP_K].T
        ys = _moe(blk_e + l * N_EXPERTS, row_tok, row_dst, n_live, ht, 2, wg16, wu16, wd32, bg, bu, bd)
        if l + 1 < DEPTH:
            x = _combine(ys, gates, xn, modr, tp, n_tok, 0, tp + ts)
            xa, xb, xb_row0 = x, x, tp

    gf = final_norm_g.reshape(1, D)
    y_p = _combine(ys, gates, xn, modr, tp, n_tok, 0, tp, gf)
    y_s = _combine(ys, gates, xn, modr, tp, n_tok, tp, ts, gf)
    return (y_p.reshape(nb, seq, D), y_s.reshape(db, n_tok, D),
            jnp.stack(na_k, axis=1), jnp.stack(na_v, axis=1), jnp.stack(gqa_k, axis=1), jnp.stack(gqa_v, axis=1))
```

```python
import functools

import jax
import jax.numpy as jnp
import numpy as np
from jax import lax
from jax.experimental import pallas as pl
from jax.experimental.pallas import tpu as pltpu

D = 1024
DEPTH = 2
GRID_W = 64
HEAD_DIM = 64
NA_WIN_R = 8
NA_WIN_C = 16
CHUNK = 128
ROPE_THETA = 10000.0
N_EXPERTS = 32
TOP_K = 4
SWIGLU_LIMIT = 7.0
SWIGLU_ALPHA = 1.702
EPS = 1e-6
MOE_BLOCK = 128
Y_SLOTS = 3
NEG_INF = -1e30
IN_WIDTH = 2048

QA, KA, VA, QB, KB, VB, CU, CV = 0, 256, 512, 768, 1280, 1408, 1536, 1792
OA, OB, OC = 0, 256, 768

LANES = 128
ROW_TILE = D // LANES
MOD_ROWS = 16
CTX_ROW = 8
VMEM_LIMIT = 56 * 1024 * 1024

F32 = jnp.float32
BF16 = jnp.bfloat16
HI = lax.Precision.HIGHEST


def _cparams(sem):
    return pltpu.CompilerParams(dimension_semantics=sem, vmem_limit_bytes=VMEM_LIMIT)


def _dot(a, b):
    return jnp.dot(a, b, preferred_element_type=F32)


def _dot_nt(a, b):
    return lax.dot_general(a, b, (((1,), (1,)), ((), ())), preferred_element_type=F32)


def _lane_lo():
    return lax.broadcasted_iota(jnp.int32, (1, LANES), 1) < HEAD_DIM


def _mod_kernel(c_ref, w_ref, b_ref, o_ref):
    c = c_ref[...]
    s = c * jax.nn.sigmoid(c)
    o_ref[0] = jnp.dot(s, w_ref[0], preferred_element_type=F32, precision=HI) + b_ref[0]


def _modulation(cond, w_mod, b_mod):
    tn = 1536
    n = w_mod.shape[-1]
    return pl.pallas_call(
        _mod_kernel,
        grid=(DEPTH, n // tn),
        in_specs=[
            pl.BlockSpec((MOD_ROWS, D), lambda l, j: (0, 0)),
            pl.BlockSpec((1, D, tn), lambda l, j: (l, 0, j)),
            pl.BlockSpec((1, 1, tn), lambda l, j: (l, 0, j)),
        ],
        out_specs=pl.BlockSpec((1, MOD_ROWS, tn), lambda l, j: (l, 0, j)),
        out_shape=jax.ShapeDtypeStruct((DEPTH, MOD_ROWS, n), F32),
        compiler_params=_cparams(("arbitrary", "arbitrary")),
        name="modulation",
    )(cond, w_mod, b_mod.reshape(DEPTH, 1, n))


def _mod_row(i, tm, tp, ts_per_batch):
    start = i * tm
    return jnp.where(start < tp, CTX_ROW, (start - tp) // ts_per_batch)


def _mod_spec(k, tm, tp, ts_per_batch):
    return pl.BlockSpec((1, 1, D), lambda i: (_mod_row(i, tm, tp, ts_per_batch) * 6 + k, 0, 0))


def _x_specs(tm, n_top, bot_off, width=D):
    return [pl.BlockSpec((tm, width), lambda i: (jnp.minimum(i, n_top - 1), 0)),
            pl.BlockSpec((tm, width), lambda i: (jnp.maximum(i - n_top, 0) + bot_off, 0))]


def _inproj_kernel(n_top, xa_ref, xb_ref, shift_ref, scale_ref, g_ref, w_ref, z_ref):
    x = jnp.where(pl.program_id(0) < n_top, xa_ref[...], xb_ref[...])
    ms = jnp.mean(x * x, axis=-1, keepdims=True)
    y = x * lax.rsqrt(ms + EPS) * g_ref[...]
    h = y * (1.0 + scale_ref[0]) + shift_ref[0]
    z_ref[...] = _dot(h.astype(BF16), w_ref[...])


def _inproj(xa, xb, bot_off_rows, t, modr, g, w, tp, ts_per_batch):
    tm = 1024
    n_top = tp // tm
    return pl.pallas_call(
        functools.partial(_inproj_kernel, n_top),
        grid=(t // tm,),
        in_specs=_x_specs(tm, n_top, bot_off_rows // tm) + [
            _mod_spec(0, tm, tp, ts_per_batch),
            _mod_spec(1, tm, tp, ts_per_batch),
            pl.BlockSpec((1, D), lambda i: (0, 0)),
            pl.BlockSpec((D, IN_WIDTH), lambda i: (0, 0)),
        ],
        out_specs=pl.BlockSpec((tm, IN_WIDTH), lambda i: (i, 0)),
        out_shape=jax.ShapeDtypeStruct((t, IN_WIDTH), F32),
        compiler_params=_cparams(("parallel",)),
        name="inproj",
    )(xa, xb, modr, modr, g, w)


def _head_rms(x, g2):
    lo = _lane_lo()
    x2 = x * x
    s_lo = jnp.sum(jnp.where(lo, x2, 0.0), axis=-1, keepdims=True)
    s_hi = jnp.sum(jnp.where(lo, 0.0, x2), axis=-1, keepdims=True)
    ms = jnp.where(lo, s_lo, s_hi) * (1.0 / HEAD_DIM)
    return x * lax.rsqrt(ms + EPS) * g2


def _softmax_pv(q16, ks, vs, biases):
    ss = []
    for k, b in zip(ks, biases):
        s = _dot_nt(q16, k)
        if b is not None:
            s = s + b
        ss.append(s)
    m = ss[0].max(axis=-1, keepdims=True)
    for s in ss[1:]:
        m = jnp.maximum(m, s.max(axis=-1, keepdims=True))
    den = None
    acc = None
    for s, v in zip(ss, vs):
        e = jnp.exp(s - m)
        d = e.sum(axis=-1, keepdims=True)
        o = _dot(e.astype(BF16), v)
        den = d if den is None else den + d
        acc = o if acc is None else acc + o
    return acc / den


def _attend_pair(q, ks_lo, vs_lo, ks_hi, vs_hi, b_lo, b_hi):
    lo = _lane_lo()
    o_lo = _softmax_pv(jnp.where(lo, q, 0.0).astype(BF16), ks_lo, vs_lo, b_lo)
    o_hi = _softmax_pv(jnp.where(lo, 0.0, q).astype(BF16), ks_hi, vs_hi, b_hi)
    return jnp.where(lo, o_lo, o_hi)


def _gelu(x):
    c = np.sqrt(2.0 / np.pi).astype(np.float32)
    return x * (0.5 * (1.0 + jnp.tanh(c * (x + 0.044715 * (x * x * x)))))


def _layer_norm(x, g, b):
    mu = jnp.mean(x, axis=-1, keepdims=True)
    xc = x - mu
    var = jnp.mean(xc * xc, axis=-1, keepdims=True)
    return xc * lax.rsqrt(var + EPS) * g + b


def _chunk_mlp(cu, cv, lng_ref, lnb_ref, ws_ref, bsf_ref, out_ref, col0):
    lo = _lane_lo()
    u = _gelu(cu)
    v = _layer_norm(_gelu(cv), lng_ref[...], lnb_ref[...]).astype(BF16)
    s = cu.shape[0]
    for n in range(s // CHUNK):
        rows = slice(n * CHUNK, (n + 1) * CHUNK)
        for jb in range(2):
            cols = slice(jb * LANES, (jb + 1) * LANES)
            vb = v[rows, cols]
            m_lo = _dot(ws_ref[2 * jb].astype(BF16), vb)
            m_hi = _dot(ws_ref[2 * jb + 1].astype(BF16), vb)
            mixed = jnp.where(lo, m_lo, m_hi) + bsf_ref[:, cols]
            out_ref[rows, col0 + jb * LANES:col0 + (jb + 1) * LANES] = (u[rows, cols] * mixed).astype(out_ref.dtype)


def _store_heads(out_ref, first_head, x):
    out_ref[0, first_head] = x[:, :HEAD_DIM]
    out_ref[0, first_head + 1] = x[:, HEAD_DIM:]


def _ctx_mixer_kernel(z_ref, qg_ref, kg_ref, lng_ref, lnb_ref, ws_ref, bsf_ref,
                      mix_ref, nak_ref, nav_ref, gk_ref, gv_ref):
    scale = HEAD_DIM ** -0.5
    for j in range(2):
        q = z_ref[:, QA + LANES * j:QA + LANES * (j + 1)] * scale
        ka = z_ref[:, KA + LANES * j:KA + LANES * (j + 1)]
        va = z_ref[:, VA + LANES * j:VA + LANES * (j + 1)]
        _store_heads(nak_ref, 2 * j, ka)
        _store_heads(nav_ref, 2 * j, va)
        k = [ka.astype(BF16)]
        v = [va.astype(BF16)]
        o = _attend_pair(q, k, v, k, v, [None], [None])
        mix_ref[:, OA + LANES * j:OA + LANES * (j + 1)] = o.astype(mix_ref.dtype)

    kb = _head_rms(z_ref[:, KB:KB + LANES], kg_ref[...])
    vb = z_ref[:, VB:VB + LANES]
    _store_heads(gk_ref, 0, kb)
    _store_heads(gv_ref, 0, vb)
    k_same = [kb.astype(BF16)]
    k_swap = [pltpu.roll(kb, HEAD_DIM, 1).astype(BF16)]
    v_same = [vb.astype(BF16)]
    v_swap = [pltpu.roll(vb, HEAD_DIM, 1).astype(BF16)]
    for j in range(4):
        q = _head_rms(z_ref[:, QB + LANES * j:QB + LANES * (j + 1)], qg_ref[...]) * scale
        if j // 2 == 0:
            o = _attend_pair(q, k_same, v_same, k_swap, v_swap, [None], [None])
        else:
            o = _attend_pair(q, k_swap, v_swap, k_same, v_same, [None], [None])
        mix_ref[:, OB + LANES * j:OB + LANES * (j + 1)] = o.astype(mix_ref.dtype)

    _chunk_mlp(z_ref[:, CU:CU + 256], z_ref[:, CV:CV + 256], lng_ref, lnb_ref, ws_ref, bsf_ref, mix_ref, OC)


def _ctx_mixer(z, n_batch, seq, qg2, kg2, lng, lnb, ws, bsf):
    small = lambda shape: pl.BlockSpec(shape, lambda b: (0,) * len(shape))
    return pl.pallas_call(
        _ctx_mixer_kernel,
        grid=(n_batch,),
        in_specs=[
            pl.BlockSpec((seq, IN_WIDTH), lambda b: (b, 0)),
            small((1, LANES)), small((1, LANES)), small((1, 256)), small((1, 256)),
            small((4, CHUNK, CHUNK)), small((CHUNK, 256)),
        ],
        out_specs=[pl.BlockSpec((seq, D), lambda b: (b, 0))] + [
            pl.BlockSpec((1, h, seq, HEAD_DIM), lambda b: (b, 0, 0, 0)) for h in (4, 4, 2, 2)],
        out_shape=[jax.ShapeDtypeStruct((n_batch * seq, D), BF16)] + [
            jax.ShapeDtypeStruct((n_batch, h, seq, HEAD_DIM), F32) for h in (4, 4, 2, 2)],
        compiler_params=_cparams(("parallel",)),
        name="ctx_mixer",
    )(z, qg2, kg2, lng, lnb, ws, bsf)


def _rope(x, cos, sin):
    first = (lax.broadcasted_iota(jnp.int32, (1, LANES), 1) % 32) < 16
    partner = jnp.where(first, pltpu.roll(x, LANES - 16, 1), pltpu.roll(x, 16, 1))
    return x * cos + partner * sin


def _lat_mixer_kernel(zq_ref, zkv_ref, nakc_ref, navc_ref, gkc_ref, gvc_ref, bias_ref,
                      cosq_ref, sinq_ref, cosk_ref, sink_ref,
                      qg_ref, kg_ref, lng_ref, lnb_ref, ws_ref, bsf_ref,
                      mix_ref,
                      kl_ref, kls_ref, vl_ref, vls_ref, kc_ref, kcs_ref, vc_ref, vcs_ref):
    scale = HEAD_DIM ** -0.5

    @pl.when(pl.program_id(1) == 0)
    def _():
        kb = _rope(_head_rms(zkv_ref[:, KB:KB + LANES], kg_ref[...]), cosk_ref[...], sink_ref[...])
        vb = zkv_ref[:, VB:VB + LANES]
        kl_ref[...] = kb.astype(BF16)
        kls_ref[...] = pltpu.roll(kb, HEAD_DIM, 1).astype(BF16)
        vl_ref[...] = vb.astype(BF16)
        vls_ref[...] = pltpu.roll(vb, HEAD_DIM, 1).astype(BF16)
        kc = gkc_ref[0]
        vc = gvc_ref[0]
        kc_ref[...] = kc.astype(BF16)
        kcs_ref[...] = pltpu.roll(kc, HEAD_DIM, 1).astype(BF16)
        vc_ref[...] = vc.astype(BF16)
        vcs_ref[...] = pltpu.roll(vc, HEAD_DIM, 1).astype(BF16)

    for j in range(2):
        cols = slice(LANES * j, LANES * (j + 1))
        q = zq_ref[:, QA + LANES * j:QA + LANES * (j + 1)] * scale
        ks = [zkv_ref[:, KA + LANES * j:KA + LANES * (j + 1)].astype(BF16), nakc_ref[0, :, cols].astype(BF16)]
        vs = [zkv_ref[:, VA + LANES * j:VA + LANES * (j + 1)].astype(BF16), navc_ref[0, :, cols].astype(BF16)]
        o = _attend_pair(q, ks, vs, ks, vs, [bias_ref[2 * j], None], [bias_ref[2 * j + 1], None])
        mix_ref[:, OA + LANES * j:OA + LANES * (j + 1)] = o.astype(mix_ref.dtype)

    same = ([kl_ref[...], kc_ref[...]], [vl_ref[...], vc_ref[...]])
    swap = ([kls_ref[...], kcs_ref[...]], [vls_ref[...], vcs_ref[...]])
    for j in range(4):
        q = _head_rms(zq_ref[:, QB + LANES * j:QB + LANES * (j + 1)], qg_ref[...])
        q = _rope(q, cosq_ref[...], sinq_ref[...]) * scale
        lo_kv, hi_kv = (same, swap) if j // 2 == 0 else (swap, same)
        o = _attend_pair(q, lo_kv[0], lo_kv[1], hi_kv[0], hi_kv[1], [None, None], [None, None])
        mix_ref[:, OB + LANES * j:OB + LANES * (j + 1)] = o.astype(mix_ref.dtype)

    _chunk_mlp(zq_ref[:, CU:CU + 256], zq_ref[:, CV:CV + 256], lng_ref, lnb_ref, ws_ref, bsf_ref, mix_ref, OC)


def _lat_mixer(z, row0, n_batch, n_tok, past, nakc, navc, gkc, gvc, bias, cos, sin, qg2, kg2, lng, lnb, ws, bsf):
    tq = 256
    nq = n_tok // tq
    small = lambda shape: pl.BlockSpec(shape, lambda b, t: (0,) * len(shape))
    qblk0 = row0 // tq
    kvblk0 = row0 // n_tok
    return pl.pallas_call(
        _lat_mixer_kernel,
        grid=(n_batch, nq),
        in_specs=[
            pl.BlockSpec((tq, IN_WIDTH), lambda b, t: (qblk0 + b * nq + t, 0)),
            pl.BlockSpec((n_tok, IN_WIDTH), lambda b, t: (kvblk0 + b, 0)),
            pl.BlockSpec((1, past, 256), lambda b, t: (b, 0, 0)),
            pl.BlockSpec((1, past, 256), lambda b, t: (b, 0, 0)),
            pl.BlockSpec((1, past, LANES), lambda b, t: (b, 0, 0)),
            pl.BlockSpec((1, past, LANES), lambda b, t: (b, 0, 0)),
            pl.BlockSpec((4, tq, n_tok), lambda b, t: (0, t, 0)),
            pl.BlockSpec((tq, LANES), lambda b, t: (t, 0)),
            pl.BlockSpec((tq, LANES), lambda b, t: (t, 0)),
            small((n_tok, LANES)), small((n_tok, LANES)),
            small((1, LANES)), small((1, LANES)), small((1, 256)), small((1, 256)),
            small((4, CHUNK, CHUNK)), small((CHUNK, 256)),
        ],
        out_specs=pl.BlockSpec((tq, D), lambda b, t: (b * nq + t, 0)),
        out_shape=jax.ShapeDtypeStruct((n_batch * n_tok, D), BF16),
        scratch_shapes=[pltpu.VMEM((n_tok, LANES), BF16)] * 4 + [pltpu.VMEM((past, LANES), BF16)] * 4,
        compiler_params=_cparams(("parallel", "arbitrary")),
        name="lat_mixer",
    )(z, z, nakc, navc, gkc, gvc, bias, cos, sin, cos, sin, qg2, kg2, lng, lnb, ws, bsf)


def _split_bf16(x):
    hi = x.astype(BF16)
    return hi, (x - hi.astype(F32)).astype(BF16)


def _outproj_kernel(n_top, mixa_ref, mixb_ref, xa_ref, xb_ref, g1_ref, shift_ref, scale_ref, g_ref, w_ref, rwt_ref, rb_ref,
                    xn_ref, ht_ref, ti_ref, gt_ref):
    top = pl.program_id(0) < n_top
    x = jnp.where(top, xa_ref[...], xb_ref[...])
    x = x + g1_ref[0] * _dot(jnp.where(top, mixa_ref[...], mixb_ref[...]), w_ref[...])
    xn_ref[...] = x
    ms = jnp.mean(x * x, axis=-1, keepdims=True)
    h = x * lax.rsqrt(ms + EPS) * g_ref[...]
    h = h * (1.0 + scale_ref[0]) + shift_ref[0]
    h_hi, h_lo = _split_bf16(h)
    rw_hi, rw_lo = _split_bf16(rwt_ref[...])
    logits = (_dot_nt(rw_hi, h_hi) + _dot_nt(rw_hi, h_lo)) + _dot_nt(rw_lo, h_hi) + rb_ref[...]

    expert = lax.broadcasted_iota(jnp.int32, logits.shape, 0)
    out_row = lax.broadcasted_iota(jnp.int32, ti_ref.shape, 0)
    top_i = jnp.zeros(ti_ref.shape, jnp.int32)
    top_v = []
    for k in range(TOP_K):
        m = jnp.max(logits, axis=0, keepdims=True)
        idx = jnp.min(jnp.where(logits == m, expert, N_EXPERTS), axis=0, keepdims=True)
        logits = jnp.where(expert == idx, -jnp.inf, logits)
        top_i = jnp.where(out_row == k, idx, top_i)
        top_v.append(m)
    es = [jnp.exp(v - top_v[0]) for v in top_v]
    den = (es[0] + es[1]) + (es[2] + es[3])
    gates = jnp.zeros(gt_ref.shape, F32)
    for k in range(TOP_K):
        gates = jnp.where(out_row == k, es[k] / den, gates)
    ti_ref[...] = top_i
    gt_ref[...] = gates

    for c in range(ROW_TILE):
        ht_ref[pl.ds(c, h.shape[0], stride=ROW_TILE), :] = h[:, LANES * c:LANES * (c + 1)]


def _outproj(mix_a, mix_b, xa, xb, bot_off_rows, modr, g, w, rwt, rb, tp, ts_per_batch):
    tm = 512
    t = mix_a.shape[0] + mix_b.shape[0]
    n_top = tp // tm
    return pl.pallas_call(
        functools.partial(_outproj_kernel, n_top),
        grid=(t // tm,),
        in_specs=_x_specs(tm, n_top, 0) + _x_specs(tm, n_top, bot_off_rows // tm) + [
            _mod_spec(2, tm, tp, ts_per_batch),
            _mod_spec(3, tm, tp, ts_per_batch),
            _mod_spec(4, tm, tp, ts_per_batch),
            pl.BlockSpec((1, D), lambda i: (0, 0)),
            pl.BlockSpec((D, D), lambda i: (0, 0)),
            pl.BlockSpec((N_EXPERTS, D), lambda i: (0, 0)),
            pl.BlockSpec((N_EXPERTS, 1), lambda i: (0, 0)),
        ],
        out_specs=[
            pl.BlockSpec((tm, D), lambda i: (i, 0)),
            pl.BlockSpec((tm * ROW_TILE, LANES), lambda i: (i, 0)),
            pl.BlockSpec((8, tm), lambda i: (0, i)),
            pl.BlockSpec((8, tm), lambda i: (0, i)),
        ],
        out_shape=[
            jax.ShapeDtypeStruct((t, D), F32),
            jax.ShapeDtypeStruct((t * ROW_TILE, LANES), F32),
            jax.ShapeDtypeStruct((8, t), jnp.int32),
            jax.ShapeDtypeStruct((8, t), F32),
        ],
        compiler_params=_cparams(("parallel",)),
        name="outproj",
    )(mix_a, mix_b, xa, xb, modr, modr, modr, g, w, rwt, rb)


def _prep_wgu_kernel(w_ref, p_ref, wg_ref, wu_ref):
    for j in range(w_ref.shape[2] // 256):
        w = w_ref[0, :, 256 * j:256 * (j + 1)].astype(BF16)
        sel = _dot(w, p_ref[...])
        wg_ref[0, :, LANES * j:LANES * (j + 1)] = sel[:, :LANES].astype(BF16)
        wu_ref[0, :, LANES * j:LANES * (j + 1)] = sel[:, LANES:].astype(BF16)


def _prep_wgu(w):
    e, d, f2 = w.shape
    perm = np.zeros((256, 256), np.float32)
    perm[2 * np.arange(LANES), np.arange(LANES)] = 1.0
    perm[2 * np.arange(LANES) + 1, LANES + np.arange(LANES)] = 1.0
    out = jax.ShapeDtypeStruct((e, d, f2 // 2), BF16)
    return pl.pallas_call(
        _prep_wgu_kernel,
        grid=(e,),
        in_specs=[pl.BlockSpec((1, d, f2), lambda i: (i, 0, 0)), pl.BlockSpec((256, 256), lambda i: (0, 0))],
        out_specs=[pl.BlockSpec((1, d, f2 // 2), lambda i: (i, 0, 0))] * 2,
        out_shape=[out, out],
        compiler_params=_cparams(("parallel",)),
        name="prep_wgu",
    )(w, jnp.asarray(perm, BF16))


def _moe_kernel(blk_e_ref, row_tok_ref, row_dst_ref, n_live_ref, wctl_ref, ht_ref, wg_hbm, wu_hbm, wd_hbm,
                bg_ref, bu_ref, bd_ref, out_ref, xg_ref, ybuf_ref, wbuf_ref, wd32_ref, sem, wsem):
    i = pl.program_id(1)
    nb = pl.num_programs(1)
    step = pl.program_id(0) * nb + i
    n = pl.num_programs(0) * nb
    cur = step % 2
    nxt = 1 - cur
    ycur = step % Y_SLOTS
    yprev = (step + Y_SLOTS - 1) % Y_SLOTS
    blk = MOE_BLOCK * ROW_TILE

    def slot_copy(s):
        return pltpu.make_async_copy(ybuf_ref.at[pl.ds(s * blk, blk)], out_ref.at[pl.ds(0, blk)], sem.at[s])

    def gather_row(base, r, s):
        tok = row_tok_ref[base + r]
        xg_ref[pl.ds(pl.multiple_of(s * blk + r * ROW_TILE, ROW_TILE), ROW_TILE), :] = (
            ht_ref[pl.ds(pl.multiple_of(tok * ROW_TILE, ROW_TILE), ROW_TILE), :])

    def send_row(base, r, s, priority=0):
        dst = row_dst_ref[base + r]
        pltpu.make_async_copy(
            ybuf_ref.at[pl.ds(pl.multiple_of(s * blk + r * ROW_TILE, ROW_TILE), ROW_TILE)],
            out_ref.at[pl.ds(pl.multiple_of(dst * ROW_TILE, ROW_TILE), ROW_TILE)], sem.at[s]).start(priority)

    @pl.when(step == 0)
    def _():
        ybuf_ref[...] = jnp.zeros(ybuf_ref.shape, F32)
        for s in range(Y_SLOTS - 1):
            pltpu.make_async_copy(ybuf_ref.at[pl.ds(s * blk, blk)], out_ref.at[pl.ds((n + 1 + s) * blk, blk)],
                                  sem.at[s]).start()

    @pl.when(i == 0)
    def _():
        def first(r, carry):
            gather_row(step * MOE_BLOCK, r, cur)
            return carry

        lax.fori_loop(0, MOE_BLOCK, first, 0, unroll=8)

    wslot = wctl_ref[2 * n + step]

    def weight_copies(e, s):
        return [pltpu.make_async_copy(wg_hbm.at[e], wbuf_ref.at[s, 0], wsem.at[s, 0]),
                pltpu.make_async_copy(wu_hbm.at[e], wbuf_ref.at[s, 1], wsem.at[s, 1]),
                pltpu.make_async_copy(wd_hbm.at[e], wd32_ref, wsem.at[s, 2])]

    @pl.when(step == 0)
    def _():
        for cp in weight_copies(blk_e_ref[0], 0):
            cp.start()

    @pl.when(wctl_ref[step] == 1)
    def _():
        for cp in weight_copies(blk_e_ref[step], wslot):
            cp.wait()

        def cast_rows(j, carry):
            rows = pl.ds(pl.multiple_of(j * LANES, LANES), LANES)
            wbuf_ref[wslot, 2, rows, :] = wd32_ref[rows, :].astype(BF16)
            return carry

        lax.fori_loop(0, D // LANES, cast_rows, 0)
        next_e = wctl_ref[n + step]

        @pl.when(next_e >= 0)
        def _():
            for cp in weight_copies(next_e, 1 - wslot):
                cp.start()

    slot_copy(ycur).wait()
    live = i < n_live_ref[pl.program_id(0)]

    @pl.when(jnp.logical_not(live))
    def _():
        def prev(r, carry):
            send_row(step * MOE_BLOCK, r, yprev)
            return carry

        lax.fori_loop(0, MOE_BLOCK, prev, 0, unroll=8)

    @pl.when(live)
    def _():
        _moe_block(step, i, nb, cur, nxt, ycur, yprev, blk, gather_row, send_row, xg_ref, ybuf_ref,
                   wbuf_ref.at[wslot, 0], wbuf_ref.at[wslot, 1], wbuf_ref.at[wslot, 2], bg_ref, bu_ref, bd_ref)

    @pl.when(step == n - 1)
    def _():
        def last(r, carry):
            send_row(n * MOE_BLOCK, r, ycur)
            return carry

        lax.fori_loop(0, MOE_BLOCK, last, 0, unroll=8)
        for s in range(Y_SLOTS):
            slot_copy(s).wait()


def _moe_block(step, i, nb, cur, nxt, ycur, yprev, blk, gather_row, send_row,
               xg_ref, ybuf_ref, wg_ref, wu_ref, wd_ref, bg_ref, bu_ref, bd_ref):
    x = jnp.concatenate(
        [xg_ref[pl.ds(cur * blk + c, MOE_BLOCK, stride=ROW_TILE), :].astype(BF16) for c in range(ROW_TILE)], axis=1)
    next_blk = jnp.where(i + 1 < nb, step + 1, step)
    for r in range(MOE_BLOCK):
        gather_row(next_blk * MOE_BLOCK, r, nxt)
    for r in range(MOE_BLOCK):
        send_row(step * MOE_BLOCK, r, yprev, priority=r % 2)
    g = _dot(x, wg_ref[...]) + bg_ref[0]
    u = _dot(x, wu_ref[...]) + bu_ref[0]
    g = jnp.minimum(g, SWIGLU_LIMIT)
    u = jnp.clip(u, -SWIGLU_LIMIT, SWIGLU_LIMIT)
    act = (u + 1.0) * (g * jax.nn.sigmoid(SWIGLU_ALPHA * g))
    y = _dot(act.astype(BF16), wd_ref[...]) + bd_ref[0]
    for c in range(ROW_TILE):
        ybuf_ref[pl.ds(ycur * blk + c, MOE_BLOCK, stride=ROW_TILE), :] = y[:, LANES * c:LANES * (c + 1)]


def _weight_schedule(blk_e):
    n = blk_e.shape[0]
    start = jnp.concatenate([jnp.ones((1,), jnp.int32), (blk_e[1:] != blk_e[:-1]).astype(jnp.int32)])
    parity = (jnp.cumsum(start) - 1) % 2
    idx = jnp.arange(n, dtype=jnp.int32)
    later_start = (idx[None, :] > idx[:, None]) & (start[None, :] == 1)
    j_next = jnp.min(jnp.where(later_start, idx[None, :], n), axis=1)
    next_e = jnp.where(j_next < n, blk_e[jnp.minimum(j_next, n - 1)], -1)
    return jnp.concatenate([start, next_e, parity]).astype(jnp.int32)


def _moe(blk_e, row_tok, row_dst, n_live, ht, n_groups, wg, wu, wd, bg, bu, bd):
    nb = blk_e.shape[0] // n_groups
    tg = ht.shape[0] // ROW_TILE // n_groups
    ew = lambda g, i, be, rt, rd, nl, wc: (be[g * nb + i], 0, 0)
    hbm = pl.BlockSpec(memory_space=pl.ANY)
    return pl.pallas_call(
        _moe_kernel,
        grid_spec=pltpu.PrefetchScalarGridSpec(
            num_scalar_prefetch=5,
            grid=(n_groups, nb),
            in_specs=[
                pl.BlockSpec((tg * ROW_TILE, LANES), lambda g, i, be, rt, rd, nl, wc: (g, 0),
                             pipeline_mode=pl.Buffered(1)),
                hbm, hbm, hbm,
                pl.BlockSpec((1, 1, D), ew),
                pl.BlockSpec((1, 1, D), ew),
                pl.BlockSpec((1, 1, D), ew),
            ],
            out_specs=pl.BlockSpec(memory_space=pl.ANY),
            scratch_shapes=[
                pltpu.VMEM((2 * MOE_BLOCK * ROW_TILE, LANES), F32),
                pltpu.VMEM((Y_SLOTS * MOE_BLOCK * ROW_TILE, LANES), F32),
                pltpu.VMEM((2, 3, D, D), BF16),
                pltpu.VMEM((D, D), F32),
                pltpu.SemaphoreType.DMA((Y_SLOTS,)),
                pltpu.SemaphoreType.DMA((2, 3)),
            ],
        ),
        out_shape=jax.ShapeDtypeStruct(((n_groups * nb + Y_SLOTS) * MOE_BLOCK * ROW_TILE, LANES), F32),
        compiler_params=_cparams(("arbitrary", "arbitrary")),
        name="moe_experts",
    )(blk_e, row_tok, row_dst, n_live, _weight_schedule(blk_e), ht, wg, wu, wd, bg, bu, bd)


def _route(top_i, group, n_groups):
    tg = top_i.shape[0]
    n_assign = tg * TOP_K
    assert n_assign <= 1 << 16 and n_assign % MOE_BLOCK == 0
    flat_e = top_i.reshape(-1).astype(jnp.int32)
    experts = jnp.arange(N_EXPERTS, dtype=jnp.int32)
    counts = jnp.sum((flat_e[:, None] == experts[None, :]).astype(jnp.int32), axis=0)
    padded = (counts + MOE_BLOCK - 1) // MOE_BLOCK * MOE_BLOCK
    pad_end = jnp.cumsum(padded)
    n_blocks = n_assign // MOE_BLOCK + N_EXPERTS
    real_keys = (flat_e << 17) | jnp.arange(n_assign, dtype=jnp.int32)
    pad_id = experts[:, None] * MOE_BLOCK + jnp.arange(MOE_BLOCK, dtype=jnp.int32)[None, :]
    used = jnp.arange(MOE_BLOCK, dtype=jnp.int32)[None, :] < (padded - counts)[:, None]
    pad_keys = (jnp.where(used, experts[:, None], 63) << 17) | (1 << 16) | pad_id
    keys = jnp.sort(jnp.concatenate([real_keys, pad_keys.reshape(-1)]))
    is_pad = ((keys >> 16) & 1) == 1
    payload = keys & 0xFFFF
    tok = payload // TOP_K
    t_all = tg * n_groups
    row_tok = jnp.where(is_pad, 0, tok)
    row_dst = jnp.where(is_pad, TOP_K * t_all + group * (N_EXPERTS * MOE_BLOCK) + payload,
                        (payload % TOP_K) * t_all + group * tg + tok)
    blk_start = jnp.arange(n_blocks, dtype=jnp.int32) * MOE_BLOCK
    blk_e = jnp.sum((pad_end[None, :] <= blk_start[:, None]).astype(jnp.int32), axis=1)
    blk_e = jnp.minimum(blk_e, N_EXPERTS - 1)
    n_live = pad_end[-1] // MOE_BLOCK
    return blk_e, row_tok, row_dst, n_live


def _route_groups(top_i, n_groups):
    tg = top_i.shape[0] // n_groups
    parts = [_route(top_i[g * tg:(g + 1) * tg], g, n_groups) for g in range(n_groups)]
    blk_e = jnp.concatenate([p[0] for p in parts])
    row_tok = jnp.concatenate([p[1] for p in parts])
    spare = row_tok.shape[0] + jnp.arange(MOE_BLOCK, dtype=jnp.int32)
    row_dst = jnp.concatenate([spare] + [p[2] for p in parts])
    n_live = jnp.stack([p[3] for p in parts]).astype(jnp.int32)
    return blk_e, row_tok, row_dst, n_live


def _combine_kernel(final, y0_ref, y1_ref, y2_ref, y3_ref, gates_ref, x_ref, g2_ref, gf_ref, o_ref):
    g = gates_ref[...]
    tm = x_ref.shape[0]
    for c in range(ROW_TILE):
        rows = pl.ds(c, tm, stride=ROW_TILE)
        cols = slice(LANES * c, LANES * (c + 1))
        y = ((y0_ref[rows, :] * g[:, 0:1] + y1_ref[rows, :] * g[:, 1:2])
             + (y2_ref[rows, :] * g[:, 2:3] + y3_ref[rows, :] * g[:, 3:4]))
        o_ref[:, cols] = x_ref[:, cols] + g2_ref[0][:, cols] * y
    if final:
        x = o_ref[...]
        ms = jnp.mean(x * x, axis=-1, keepdims=True)
        o_ref[...] = x * lax.rsqrt(ms + EPS) * gf_ref[...]


def _combine(ys, gates, x, modr, tp, ts_per_batch, row0, n_rows, final_g=None):
    tm = 512
    nt_all = x.shape[0] // tm
    i0 = row0 // tm
    yspec = lambda k: pl.BlockSpec((tm * ROW_TILE, LANES), lambda i: (k * nt_all + i0 + i, 0))
    mod = pl.BlockSpec((1, 1, D), lambda i: (_mod_row(i0 + i, tm, tp, ts_per_batch) * 6 + 5, 0, 0))
    gf = jnp.ones((1, D), F32) if final_g is None else final_g
    return pl.pallas_call(
        functools.partial(_combine_kernel, final_g is not None),
        grid=(n_rows // tm,),
        in_specs=[yspec(0), yspec(1), yspec(2), yspec(3), pl.BlockSpec((tm, TOP_K), lambda i: (i0 + i, 0)),
                  pl.BlockSpec((tm, D), lambda i: (i0 + i, 0)), mod, pl.BlockSpec((1, D), lambda i: (0, 0))],
        out_specs=pl.BlockSpec((tm, D), lambda i: (i, 0)),
        out_shape=jax.ShapeDtypeStruct((n_rows, D), F32),
        compiler_params=_cparams(("parallel",)),
        name="moe_combine",
    )(ys, ys, ys, ys, gates, x, modr, gf)


def _na_bias_kernel(rows, t_ref, o_ref):
    kr = min(NA_WIN_R, rows)
    lo = _lane_lo()
    for r in range(rows):
        r0 = min(max(r - kr // 2, 0), rows - kr)
        for p in range(rows // 2):
            ok0 = r0 <= 2 * p < r0 + kr
            ok1 = r0 <= 2 * p + 1 < r0 + kr
            if ok0 or ok1:
                d = 2 * p - r + (NA_WIN_R - 1)
                assert 0 <= d < t_ref.shape[1]
                tile = t_ref[0, d]
                if not ok0:
                    tile = jnp.where(lo, NEG_INF, tile)
                if not ok1:
                    tile = jnp.where(lo, tile, NEG_INF)
            else:
                tile = jnp.full((GRID_W, LANES), NEG_INF, F32)
            o_ref[0, r * GRID_W:(r + 1) * GRID_W, p * LANES:(p + 1) * LANES] = tile


def _na_bias(rpb, rows):
    assert rows % 2 == 0 and 2 * GRID_W == LANES
    heads = rpb.shape[0]
    nd = 2 * NA_WIN_R
    c = np.arange(GRID_W)
    w0 = np.clip(c - NA_WIN_C // 2, 0, GRID_W - NA_WIN_C)
    col_ok = (c[None, :] >= w0[:, None]) & (c[None, :] < w0[:, None] + NA_WIN_C)
    dc = np.clip(c[None, :] - c[:, None] + (NA_WIN_C - 1), 0, 2 * NA_WIN_C - 2)
    oh_c = np.eye(2 * NA_WIN_C - 1, dtype=np.float32)[dc]
    vals = jnp.einsum('hde,qce->hdqc', rpb.astype(F32), oh_c, precision=HI)
    per_row = jnp.where(col_ok[None, None], vals, NEG_INF)
    per_row = jnp.pad(per_row, ((0, 0), (0, nd + 1 - per_row.shape[1]), (0, 0), (0, 0)), constant_values=NEG_INF)
    tables = jnp.concatenate([per_row[:, :nd], per_row[:, 1:nd + 1]], axis=-1)
    n = rows * GRID_W
    return pl.pallas_call(
        functools.partial(_na_bias_kernel, rows),
        grid=(heads,),
        in_specs=[pl.BlockSpec((1, nd, GRID_W, LANES), lambda h: (h, 0, 0, 0))],
        out_specs=pl.BlockSpec((1, n, n), lambda h: (h, 0, 0)),
        out_shape=jax.ShapeDtypeStruct((heads, n, n), F32),
        compiler_params=_cparams(("parallel",)),
        name="na_bias",
    )(tables)


def _rope_tables(n_tokens):
    t = np.arange(n_tokens)
    row = (t // GRID_W).astype(np.float32)
    col = (t % GRID_W).astype(np.float32)
    half = HEAD_DIM // 2
    inv = jnp.asarray(ROPE_THETA, F32) ** (-jnp.arange(0, half, 2, dtype=F32) / half)
    ang_r = jnp.asarray(row)[:, None] * inv
    ang_c = jnp.asarray(col)[:, None] * inv
    cr, sr, cc, sc = jnp.cos(ang_r), jnp.sin(ang_r), jnp.cos(ang_c), jnp.sin(ang_c)
    cos = jnp.concatenate([cr, cr, cc, cc] * 2, axis=-1)
    sin = jnp.concatenate([-sr, sr, -sc, sc] * 2, axis=-1)
    return cos, sin


def _ctx_lanes(cache_l):
    b, h, p, dh = cache_l.shape
    return cache_l.transpose(0, 2, 1, 3).reshape(b, p, h * dh)


def kernel(x_prompt, x_sample, cache_na_k, cache_na_v, cache_gqa_k, cache_gqa_v, c, c_ctx, w_mod, b_mod, norm1_g, norm2_g, w_in, na_rpb, q_norm_g, k_norm_g, cm_ln_g, cm_ln_b, cm_ws, cm_bs, w_out, router_w, router_b, w_gate_up, b_gate_up, w_down, b_down, final_norm_g):
    nb, seq, _ = x_prompt.shape
    db, n_tok, _ = x_sample.shape
    past = cache_na_k.shape[3]
    tp, ts = nb * seq, db * n_tok
    assert db <= CTX_ROW and n_tok % GRID_W == 0
    assert tp == ts

    cond = jnp.zeros((MOD_ROWS, D), F32).at[:db].set(c).at[CTX_ROW].set(c_ctx)
    mod = _modulation(cond, w_mod, b_mod)
    cos, sin = _rope_tables(n_tok)
    xa, xb, xb_row0 = x_prompt.reshape(tp, D), x_sample.reshape(ts, D), 0

    w_in16 = w_in.astype(BF16)
    w_out16 = w_out.astype(BF16)
    wg16, wu16 = _prep_wgu(w_gate_up.reshape(DEPTH * N_EXPERTS, D, 2 * D))
    wd32 = w_down.reshape(DEPTH * N_EXPERTS, D, D)
    bg = b_gate_up[..., 0::2].reshape(DEPTH * N_EXPERTS, 1, D)
    bu = b_gate_up[..., 1::2].reshape(DEPTH * N_EXPERTS, 1, D)
    bd = b_down.reshape(DEPTH * N_EXPERTS, 1, D)

    na_k, na_v, gqa_k, gqa_v = [], [], [], []
    for l in range(DEPTH):
        modr = mod[l].reshape(MOD_ROWS * 6, 1, D)
        qg2 = jnp.tile(q_norm_g[l], 2).reshape(1, LANES)
        kg2 = jnp.tile(k_norm_g[l], 2).reshape(1, LANES)
        lng = cm_ln_g[l].reshape(1, 256)
        lnb = cm_ln_b[l].reshape(1, 256)
        bsf = jnp.repeat(cm_bs[l].T, HEAD_DIM, axis=1)

        z = _inproj(xa, xb, xb_row0, tp + ts, modr, norm1_g[l].reshape(1, D), w_in16[l], tp, n_tok)
        mix_p, nak, nav, gk, gv = _ctx_mixer(z, nb, seq, qg2, kg2, lng, lnb, cm_ws[l], bsf)
        mix_s = _lat_mixer(z, tp, db, n_tok, past,
                           _ctx_lanes(cache_na_k[:, l]), _ctx_lanes(cache_na_v[:, l]),
                           _ctx_lanes(cache_gqa_k[:, l]), _ctx_lanes(cache_gqa_v[:, l]),
                           _na_bias(na_rpb[l], n_tok // GRID_W), cos, sin, qg2, kg2, lng, lnb, cm_ws[l], bsf)
        na_k.append(nak)
        na_v.append(nav)
        gqa_k.append(gk)
        gqa_v.append(gv)

        xn, ht, top_i, gates = _outproj(mix_p, mix_s, xa, xb, xb_row0, modr, norm2_g[l].reshape(1, D), w_out16[l],
                                  router_w[l].T, router_b[l].reshape(N_EXPERTS, 1), tp, n_tok)

        blk_e, row_tok, row_dst, n_live = _route_groups(top_i[:TOP_K].T, 2)
        gates = gates[:TOP_K].T
        ys = _moe(blk_e + l * N_EXPERTS, row_tok, row_dst, n_live, ht, 2, wg16, wu16, wd32, bg, bu, bd)
        if l + 1 < DEPTH:
            x = _combine(ys, gates, xn, modr, tp, n_tok, 0, tp + ts)
            xa, xb, xb_row0 = x, x, tp

    gf = final_norm_g.reshape(1, D)
    y_p = _combine(ys, gates, xn, modr, tp, n_tok, 0, tp, gf)
    y_s = _combine(ys, gates, xn, modr, tp, n_tok, tp, ts, gf)
    return (y_p.reshape(nb, seq, D), y_s.reshape(db, n_tok, D),
            jnp.stack(na_k, axis=1), jnp.stack(na_v, axis=1), jnp.stack(gqa_k, axis=1), jnp.stack(gqa_v, axis=1))
```
